```python
import jax, jax.numpy as jnp
from jax import lax
import numpy as np

D_MODEL = 1024
BATCH = 32
SEQ = 2048
DEPTH = 1

GRID_W = 64
CTX_LEN = 256
HEAD_DIM = 64
N_Q_HEADS = (D_MODEL // 2) // HEAD_DIM
N_KV_HEADS = N_Q_HEADS // 4
Q_PER_KV = N_Q_HEADS // N_KV_HEADS
WINDOW = 128
BLOCK = 128
ROPE_FREQS = HEAD_DIM // 4
ROPE_BASE = 10000.0
GLA_HEADS = 4
GLA_DV = (D_MODEL // 2) // GLA_HEADS
GLA_DK = GLA_DV // 2
GATE_RANK = 16
GATE_NORMALIZER = 16.0
CHUNK = 64
N_EXPERTS = 16
CAPACITY_FACTOR = 2
D_EXPERT = D_MODEL
ATTN_WIDTH = N_Q_HEADS * HEAD_DIM
KV_WIDTH = N_KV_HEADS * HEAD_DIM
GLA_QK_WIDTH = GLA_HEADS * GLA_DK
GLA_WIDTH = GLA_HEADS * GLA_DV
MIX_WIDTH = ATTN_WIDTH + GLA_WIDTH
IN_PROJ_DIM = ATTN_WIDTH + 2 * KV_WIDTH + 2 * GLA_QK_WIDTH + 2 * GLA_WIDTH + 2 * GATE_RANK
EPS = 1e-6
NEG_INF = -1e30

kernel_name = "hybrid_swa_gla_ecmoe_diffusion_layer"


def rms_norm(t, g):
    tf = t.astype(jnp.float32)
    y = tf * lax.rsqrt(jnp.mean(tf * tf, axis=-1, keepdims=True) + EPS)
    return (y * g.astype(jnp.float32)).astype(t.dtype)


def modulate(h, shift, scale):
    return h * (1.0 + scale) + shift


def rev(t):
    return jnp.flip(t, axis=1)


def split_in_proj(t):
    sizes = (ATTN_WIDTH, KV_WIDTH, KV_WIDTH, GLA_QK_WIDTH, GLA_QK_WIDTH,
             GLA_WIDTH, GLA_WIDTH, GATE_RANK, GATE_RANK)
    return jnp.split(t, np.cumsum(sizes)[:-1].tolist(), axis=-1)


def heads(t, n, d):
    return t.reshape(t.shape[:2] + (n, d))


def axial_rope_tables(n_rows):
    inv = ROPE_BASE ** (-jnp.arange(ROPE_FREQS, dtype=jnp.float32) / ROPE_FREQS)
    row = jnp.repeat(jnp.arange(n_rows, dtype=jnp.float32), GRID_W)
    col = jnp.tile(jnp.arange(GRID_W, dtype=jnp.float32), n_rows)
    ang = jnp.stack([row[:, None] * inv, col[:, None] * inv], axis=1)
    return jnp.cos(ang), jnp.sin(ang)


def apply_axial_rope(t, cos, sin):
    tt = t.reshape(t.shape[:-1] + (2, 2, ROPE_FREQS))
    t1, t2 = tt[..., 0, :], tt[..., 1, :]
    cs, sn = cos[:, None], sin[:, None]
    out = jnp.stack([t1 * cs - t2 * sn, t2 * cs + t1 * sn], axis=-2)
    return out.reshape(t.shape).astype(t.dtype)


def attn_q(t, g, rope):
    B, T = t.shape[:2]
    q = rms_norm(heads(t, N_Q_HEADS, HEAD_DIM), g)
    if rope is not None:
        q = apply_axial_rope(q, *rope)
    return (q * HEAD_DIM ** -0.5).reshape(B, T, N_KV_HEADS, Q_PER_KV, HEAD_DIM)


def attn_kv(tk, tv, g, rope):
    k = rms_norm(heads(tk, N_KV_HEADS, HEAD_DIM), g)
    if rope is not None:
        k = apply_axial_rope(k, *rope)
    return k, heads(tv, N_KV_HEADS, HEAD_DIM)


def attn_with_sink(q, k, v, sink, mask=None):
    s = jnp.einsum('bqhgd,bkhd->bhgqk', q, k).astype(jnp.float32)
    if mask is not None:
        s = jnp.where(mask, s, NEG_INF)
    sink_logit = jnp.broadcast_to(sink.astype(jnp.float32)[None, :, :, None, None], s.shape[:-1] + (1,))
    p = jax.nn.softmax(jnp.concatenate([sink_logit, s], axis=-1), axis=-1)[..., 1:]
    return jnp.einsum('bhgqk,bkhd->bqhgd', p.astype(v.dtype), v)


def band(t):
    B, L = t.shape[:2]
    tb = t.reshape((B, L // BLOCK, BLOCK) + t.shape[2:])
    tp = jnp.pad(tb, ((0, 0), (1, 1)) + ((0, 0),) * (tb.ndim - 2))
    return jnp.concatenate([tp[:, :-2], tp[:, 1:-1], tp[:, 2:]], axis=2)


def window_attention(q, k, v, kc, vc, sink):
    B, L = q.shape[:2]
    nb = L // BLOCK
    n_ctx = kc.shape[1]
    qb = q.reshape(B, nb, BLOCK, N_KV_HEADS, Q_PER_KV, HEAD_DIM)
    kb, vb = band(k), band(v)
    qi = jnp.arange(BLOCK)
    kj = jnp.arange(3 * BLOCK)
    rel = kj[None, :] - BLOCK - qi[:, None]
    ctx_ok = jnp.ones((BLOCK, n_ctx), dtype=bool)

    def one_block(args):
        n, qn, kn, vn = args
        kpos = (n - 1) * BLOCK + kj
        win = (jnp.abs(rel) <= WINDOW) & ((kpos >= 0) & (kpos < L))[None, :]
        mask = jnp.concatenate([ctx_ok, win], axis=-1)
        return attn_with_sink(qn, jnp.concatenate([kc, kn], axis=1),
                              jnp.concatenate([vc, vn], axis=1), sink, mask)

    o = lax.map(one_block, (jnp.arange(nb), jnp.moveaxis(qb, 1, 0),
                            jnp.moveaxis(kb, 1, 0), jnp.moveaxis(vb, 1, 0)))
    return jnp.moveaxis(o, 0, 1).reshape(B, L, ATTN_WIDTH)


def gla_log_decay(lr, w, b):
    z = (lr @ w + b).astype(jnp.float32)
    return heads(jax.nn.log_sigmoid(z) / GATE_NORMALIZER, GLA_HEADS, GLA_DK)


def gla_chunked(q, k, v, log_a, s0):
    B, T, H, dk = q.shape
    dv = v.shape[-1]
    N = T // CHUNK
    qc = q.astype(jnp.float32).reshape(B, N, CHUNK, H, dk)
    kc = k.astype(jnp.float32).reshape(B, N, CHUNK, H, dk)
    vc = v.astype(jnp.float32).reshape(B, N, CHUNK, H, dv)
    cum = jnp.cumsum(log_a.reshape(B, N, CHUNK, H, dk), axis=2)
    ref = cum[:, :, CHUNK // 2 - 1:CHUNK // 2]
    A = jnp.einsum('bnihk,bnjhk->bnhij', qc * jnp.exp(cum - ref), kc * jnp.exp(ref - cum))
    tril = jnp.tril(jnp.ones((CHUNK, CHUNK), dtype=bool))
    o_intra = jnp.einsum('bnhij,bnjhv->bnihv', jnp.where(tril, A, 0.0), vc)
    total = cum[:, :, -1]
    kv = jnp.einsum('bnjhk,bnjhv->bnhkv', kc * jnp.exp(total[:, :, None] - cum), vc)

    def step(S, inp):
        decay, kv_n = inp
        return decay[..., None] * S + kv_n, S

    s_final, s_enter = lax.scan(step, s0, (jnp.moveaxis(jnp.exp(total), 1, 0), jnp.moveaxis(kv, 1, 0)))
    s_enter = jnp.moveaxis(s_enter, 0, 1)
    o_inter = jnp.einsum('bnihk,bnhkv->bnihv', qc * jnp.exp(cum), s_enter)
    return (o_intra + o_inter).reshape(B, T, H, dv).astype(v.dtype), s_final


def gla_final_state(k, v, log_a):
    cum = jnp.cumsum(log_a, axis=1)
    w = k.astype(jnp.float32) * jnp.exp(cum[:, -1:] - cum)
    return jnp.einsum('bthk,bthv->bhkv', w, v.astype(jnp.float32))


def gla_bidirectional(q, k, v, la_f, la_b, s_f0, s_b0):
    o_f, s_f = gla_chunked(q, k, v, la_f, s_f0)
    o_b, s_b = gla_chunked(rev(q), rev(k), rev(v), rev(la_b), s_b0)
    return o_f + rev(o_b), s_f, s_b


def gla_output(o, gate, g):
    B, T = o.shape[:2]
    return rms_norm(o, g).reshape(B, T, GLA_WIDTH) * jax.nn.silu(gate)


def expert_choice_ffn(h, w_router, w_gate, w_up, w_down):
    B, n, _ = h.shape
    cap = CAPACITY_FACTOR * n // N_EXPERTS
    aff = jax.nn.softmax((h @ w_router).astype(jnp.float32), axis=-1)
    g, idx = lax.top_k(jnp.swapaxes(aff, 1, 2), cap)
    bidx = jnp.arange(B)[:, None, None]
    xs = h[bidx, idx]
    hid = jax.nn.silu(jnp.einsum('becd,edf->becf', xs, w_gate)) * jnp.einsum('becd,edf->becf', xs, w_up)
    y = jnp.einsum('becf,efd->becd', hid, w_down) * g[..., None].astype(h.dtype)
    return jnp.zeros_like(h).at[bidx, idx].add(y)


def setup_inputs(seed: int = 0) -> dict:
    key = jax.random.key(seed)
    ks = jax.random.split(key, 22)
    D = D_MODEL

    def nrm(k, shape, scale):
        return scale * jax.random.normal(k, shape, jnp.float32)

    return {
        "x": nrm(ks[0], (BATCH, SEQ, D), 1.0),
        "c": nrm(ks[1], (BATCH, D), 1.0),
        "ctx": nrm(ks[2], (BATCH, CTX_LEN, D), 1.0),
        "c_ctx": nrm(ks[3], (D,), 1.0),
        "w_mod": nrm(ks[4], (DEPTH, D, 6 * D), 0.5 * D ** -0.5),
        "b_mod": nrm(ks[5], (DEPTH, 6 * D), 0.01),
        "norm1_g": 1.0 + nrm(ks[6], (DEPTH, D), 0.02),
        "w_in": nrm(ks[7], (DEPTH, D, IN_PROJ_DIM), D ** -0.5),
        "q_norm_g": 1.0 + nrm(ks[8], (DEPTH, HEAD_DIM), 0.02),
        "k_norm_g": 1.0 + nrm(ks[9], (DEPTH, HEAD_DIM), 0.02),
        "attn_sink": nrm(ks[10], (DEPTH, N_Q_HEADS), 0.5),
        "w_decay_fwd": nrm(ks[11], (DEPTH, GATE_RANK, GLA_QK_WIDTH), GATE_RANK ** -0.5),
        "b_decay_fwd": nrm(ks[12], (DEPTH, GLA_QK_WIDTH), 0.1),
        "w_decay_bwd": nrm(ks[13], (DEPTH, GATE_RANK, GLA_QK_WIDTH), GATE_RANK ** -0.5),
        "b_decay_bwd": nrm(ks[14], (DEPTH, GLA_QK_WIDTH), 0.1),
        "gla_norm_g": 1.0 + nrm(ks[15], (DEPTH, GLA_DV), 0.02),
        "w_out": nrm(ks[16], (DEPTH, MIX_WIDTH, D), MIX_WIDTH ** -0.5),
        "norm2_g": 1.0 + nrm(ks[17], (DEPTH, D), 0.02),
        "w_router": nrm(ks[18], (DEPTH, D, N_EXPERTS), D ** -0.5),
        "w_e_gate": nrm(ks[19], (DEPTH, N_EXPERTS, D, D_EXPERT), D ** -0.5),
        "w_e_up": nrm(ks[20], (DEPTH, N_EXPERTS, D, D_EXPERT), D ** -0.5),
        "w_e_down": nrm(ks[21], (DEPTH, N_EXPERTS, D_EXPERT, D), D_EXPERT ** -0.5),
    }


def reference(x, c, ctx, c_ctx, w_mod, b_mod, norm1_g, w_in, q_norm_g, k_norm_g, attn_sink,
              w_decay_fwd, b_decay_fwd, w_decay_bwd, b_decay_bwd, gla_norm_g, w_out, norm2_g,
              w_router, w_e_gate, w_e_up, w_e_down):
    B, L, _ = x.shape
    n_ctx = ctx.shape[1]
    n_rows = L // GRID_W
    rope = axial_rope_tables(n_rows)
    for layer in range(DEPTH):
        last = layer == DEPTH - 1
        mod = jax.nn.silu(c) @ w_mod[layer] + b_mod[layer]
        sh1, sc1, g1, sh2, sc2, g2 = jnp.split(mod[:, None, :], 6, axis=-1)
        mod_c = jax.nn.silu(c_ctx) @ w_mod[layer] + b_mod[layer]
        sh1c, sc1c, g1c, sh2c, sc2c, g2c = jnp.split(mod_c, 6, axis=-1)
        w_aq, w_ak, w_av, w_gq, w_gk, w_gv, w_gg, w_lf, w_lb = split_in_proj(w_in[layer])
        sink = attn_sink[layer].reshape(N_KV_HEADS, Q_PER_KV)

        hc = modulate(rms_norm(ctx, norm1_g[layer]), sh1c, sc1c)
        kc, vc = attn_kv(hc @ w_ak, hc @ w_av, k_norm_g[layer], None)
        gkc = heads(hc @ w_gk, GLA_HEADS, GLA_DK)
        gvc = heads(hc @ w_gv, GLA_HEADS, GLA_DV)
        la_fc = gla_log_decay(hc @ w_lf, w_decay_fwd[layer], b_decay_fwd[layer])
        la_bc = gla_log_decay(hc @ w_lb, w_decay_bwd[layer], b_decay_bwd[layer])
        if last:
            s_f = gla_final_state(gkc, gvc, la_fc)
            s_b = gla_final_state(rev(gkc), rev(gvc), rev(la_bc))
        else:
            zero_state = jnp.zeros((B, GLA_HEADS, GLA_DK, GLA_DV), jnp.float32)
            gqc = heads(hc @ w_gq, GLA_HEADS, GLA_DK) * GLA_DK ** -0.5
            gla_c, s_f, s_b = gla_bidirectional(gqc, gkc, gvc, la_fc, la_bc, zero_state, zero_state)
            qc = attn_q(hc @ w_aq, q_norm_g[layer], None)
            attn_c = attn_with_sink(qc, kc, vc, sink).reshape(B, n_ctx, ATTN_WIDTH)
            mix_c = jnp.concatenate([attn_c, gla_output(gla_c, hc @ w_gg, gla_norm_g[layer])], axis=-1)

        h = modulate(rms_norm(x, norm1_g[layer]), sh1, sc1)
        aq, ak, av, gq, gk, gv, gg, lf, lb = split_in_proj(h @ w_in[layer])
        q = attn_q(aq, q_norm_g[layer], rope)
        k, v = attn_kv(ak, av, k_norm_g[layer], rope)
        attn_lat = window_attention(q, k, v, kc, vc, sink)
        la_f = gla_log_decay(lf, w_decay_fwd[layer], b_decay_fwd[layer])
        la_b = gla_log_decay(lb, w_decay_bwd[layer], b_decay_bwd[layer])
        gla_lat, _, _ = gla_bidirectional(heads(gq, GLA_HEADS, GLA_DK) * GLA_DK ** -0.5,
                                          heads(gk, GLA_HEADS, GLA_DK), heads(gv, GLA_HEADS, GLA_DV),
                                          la_f, la_b, s_f, s_b)
        mix = jnp.concatenate([attn_lat, gla_output(gla_lat, gg, gla_norm_g[layer])], axis=-1)
        x = x + g1 * (mix @ w_out[layer])
        h2 = modulate(rms_norm(x, norm2_g[layer]), sh2, sc2)
        x = x + g2 * expert_choice_ffn(h2, w_router[layer], w_e_gate[layer], w_e_up[layer], w_e_down[layer])

        if not last:
            ctx = ctx + g1c * (mix_c @ w_out[layer])
            h2c = modulate(rms_norm(ctx, norm2_g[layer]), sh2c, sc2c)
            ctx = ctx + g2c * expert_choice_ffn(h2c, w_router[layer], w_e_gate[layer], w_e_up[layer], w_e_down[layer])
    return x
```

```python
import functools

import jax
import jax.numpy as jnp
import numpy as np
from jax import lax
from jax.experimental import pallas as pl
from jax.experimental.pallas import tpu as pltpu

D_MODEL = 1024
GRID_W = 64
HEAD_DIM = 64
N_Q_HEADS = 8
N_KV_HEADS = 2
BLOCK = 128
ROPE_FREQS = 16
ROPE_BASE = 10000.0
GLA_HEADS = 4
GLA_DV = 128
GLA_DK = 64
GATE_RANK = 16
GATE_NORMALIZER = 16.0
CHUNK = 64
N_EXPERTS = 16
CAPACITY_FACTOR = 2
ATTN_WIDTH = 512
KV_WIDTH = 128
GLA_QK_WIDTH = 256
GLA_WIDTH = 512
EPS = 1e-6
NEG_INF = -1e30

LANES = 128
VMEM_LIMIT = 56 * 1024 * 1024

F32 = jnp.float32
BF16 = jnp.bfloat16
HI = lax.Precision.HIGHEST


def _cparams(sem):
    return pltpu.CompilerParams(dimension_semantics=sem, vmem_limit_bytes=VMEM_LIMIT)


def _dot(a, b):
    return jnp.dot(a, b, preferred_element_type=F32)


def _dot_hi(a, b):
    return jnp.dot(a, b, preferred_element_type=F32, precision=HI)


def _dot_nt(a, b):
    return lax.dot_general(a, b, (((1,), (1,)), ((), ())), preferred_element_type=F32)


def _dot_tn(a, b, precision=None):
    return lax.dot_general(a, b, (((0,), (0,)), ((), ())), preferred_element_type=F32,
                           precision=precision)


def _split2(t):
    hi = t.astype(BF16)
    lo = (t - hi.astype(F32)).astype(BF16)
    return hi, lo


def _rms_mod(t, g, shift, scale):
    y = t * lax.rsqrt(jnp.mean(t * t, axis=-1, keepdims=True) + EPS)
    return (y * g) * (1.0 + scale) + shift


def _lane_lo(shape):
    return (lax.broadcasted_iota(jnp.int32, shape, len(shape) - 1) % LANES) < HEAD_DIM


def _mod_kernel(c_ref, w_ref, b_ref, o_ref):
    c = c_ref[...]
    s = c * jax.nn.sigmoid(c)
    o_ref[...] = _dot_hi(s, w_ref[...]) + b_ref[...]


def _modulation(cc, w_mod, b_mod):
    m = cc.shape[0]
    n = w_mod.shape[1]
    tn = 1024
    return pl.pallas_call(
        _mod_kernel,
        grid=(n // tn,),
        in_specs=[pl.BlockSpec((m, D_MODEL), lambda j: (0, 0)),
                  pl.BlockSpec((D_MODEL, tn), lambda j: (0, j)),
                  pl.BlockSpec((1, tn), lambda j: (0, j))],
        out_specs=pl.BlockSpec((m, tn), lambda j: (0, j)),
        out_shape=jax.ShapeDtypeStruct((m, n), F32),
        compiler_params=_cparams(("arbitrary",)),
        name="adaln_mod",
    )(cc, w_mod, b_mod.reshape(1, n))


def _ctx_kernel(ctx_ref, mod_ref, g1_ref, w_ref, kg_ref, bd_ref, wd_ref, bdec_ref,
                kc_ref, kcs_ref, vc_ref, vcs_ref, sf_ref, sb_ref):
    n = ctx_ref.shape[0]
    h = _rms_mod(ctx_ref[...], g1_ref[...], mod_ref[0:1, :], mod_ref[1:2, :]).astype(BF16)
    pc = _dot(h, w_ref[...])
    ak = pc[:, 0:128]
    av = pc[:, 128:256]
    gk = pc[:, 256:512]
    gv = pc[:, 512:1024].astype(BF16)
    lr = pc[:, 1024:1152]
    sq_hi, sq_lo = _split2(ak * ak)
    ms = _dot(sq_hi, bd_ref[...]) + _dot(sq_lo, bd_ref[...])
    kn = ak * lax.rsqrt(ms + EPS) * kg_ref[...]
    kc_ref[...] = kn.astype(BF16)
    kcs_ref[...] = pltpu.roll(kn, HEAD_DIM, 1).astype(BF16)
    vc_ref[...] = av.astype(BF16)
    vcs_ref[...] = pltpu.roll(av, HEAD_DIM, 1).astype(BF16)
    z = _dot_hi(lr, wd_ref[...]) + bdec_ref[...]
    la = (jnp.minimum(z, 0.0) - jnp.log(1.0 + jnp.exp(-jnp.abs(z)))) * (1.0 / GATE_NORMALIZER)
    r = lax.broadcasted_iota(jnp.int32, (n, n), 0)
    cidx = lax.broadcasted_iota(jnp.int32, (n, n), 1)
    after = (cidx > r).astype(F32)
    before = (cidx < r).astype(F32)
    w_f = jnp.exp(_dot_hi(after, la[:, 0:256]))
    w_b = jnp.exp(_dot_hi(before, la[:, 256:512]))
    lo = _lane_lo((n, LANES))
    for w, out in ((w_f, sf_ref), (w_b, sb_ref)):
        kw = gk * w
        for c in range(2):
            kwc = kw[:, c * LANES:(c + 1) * LANES]
            k_lo = jnp.where(lo, kwc, 0.0).astype(BF16)
            k_hi = jnp.where(lo, 0.0, kwc).astype(BF16)
            v0 = gv[:, (2 * c) * GLA_DV:(2 * c + 1) * GLA_DV]
            v1 = gv[:, (2 * c + 1) * GLA_DV:(2 * c + 2) * GLA_DV]
            out[c] = _dot_tn(k_lo, v0) + _dot_tn(k_hi, v1)


def _context_side(ctx, modc, g1, w_ctx, kg2, bd128, wd, bdec):
    B, n, _ = ctx.shape
    full = lambda shape: pl.BlockSpec(shape, lambda b: (0,) * len(shape))
    kv_spec = pl.BlockSpec((None, n, KV_WIDTH), lambda b: (b, 0, 0))
    st_spec = pl.BlockSpec((None, 2, LANES, GLA_DV), lambda b: (b, 0, 0, 0))
    kv_shape = jax.ShapeDtypeStruct((B, n, KV_WIDTH), BF16)
    st_shape = jax.ShapeDtypeStruct((B, 2, LANES, GLA_DV), F32)
    return pl.pallas_call(
        _ctx_kernel,
        grid=(B,),
        in_specs=[pl.BlockSpec((None, n, D_MODEL), lambda b: (b, 0, 0)),
                  full(modc.shape), full(g1.shape), full(w_ctx.shape), full(kg2.shape),
                  full(bd128.shape), full(wd.shape), full(bdec.shape)],
        out_specs=[kv_spec, kv_spec, kv_spec, kv_spec, st_spec, st_spec],
        out_shape=[kv_shape, kv_shape, kv_shape, kv_shape, st_shape, st_shape],
        compiler_params=_cparams(("arbitrary",)),
        name="context_side",
    )(ctx, modc, g1, w_ctx, kg2, bd128, wd, bdec)


def _swap16(t):
    n = t.shape[1]
    first = (lax.broadcasted_iota(jnp.int32, t.shape, 1) % 32) < ROPE_FREQS
    return jnp.where(first, pltpu.roll(t, n - ROPE_FREQS, 1), pltpu.roll(t, ROPE_FREQS, 1))


def _inproj_kernel(x_ref, mod_ref, g1_ref, w_ref, qg_ref, kg_ref, bd_ref, cos_ref, sin_ref,
                   q_ref, k_ref, ks_ref, v_ref, vs_ref, gq_ref, gk_ref, gv_ref, gg_ref, lr_ref):
    h = _rms_mod(x_ref[...], g1_ref[...], mod_ref[0:1, :], mod_ref[1:2, :]).astype(BF16)
    cos = cos_ref[...]
    sin = sin_ref[...]

    def head_norm_rope(t, g, bd, reps):
        sq_hi, sq_lo = _split2(t * t)
        ms = _dot(sq_hi, bd) + _dot(sq_lo, bd)
        tn = t * lax.rsqrt(ms + EPS) * g
        c = jnp.concatenate([cos] * reps, axis=1) if reps > 1 else cos
        s = jnp.concatenate([sin] * reps, axis=1) if reps > 1 else sin
        return tn * c + _swap16(tn) * s

    aq = _dot(h, w_ref[:, 0:512])
    q = head_norm_rope(aq, qg_ref[...], bd_ref[...], 4) * (HEAD_DIM ** -0.5)
    q_ref[...] = q.astype(BF16)
    akv = _dot(h, w_ref[:, 512:768])
    k = head_norm_rope(akv[:, 0:128], kg_ref[...], bd_ref[0:128, 0:128], 1)
    k_ref[...] = k.astype(BF16)
    ks_ref[...] = pltpu.roll(k, HEAD_DIM, 1).astype(BF16)
    v = akv[:, 128:256]
    v_ref[...] = v.astype(BF16)
    vs_ref[...] = pltpu.roll(v, HEAD_DIM, 1).astype(BF16)
    gqk = _dot(h, w_ref[:, 768:1280])
    gq_ref[...] = (gqk[:, 0:256] * (GLA_DK ** -0.5)).astype(BF16)
    gk_ref[...] = gqk[:, 256:512].astype(BF16)
    gv_ref[...] = _dot(h, w_ref[:, 1280:1792]).astype(BF16)
    gg_ref[...] = _dot(h, w_ref[:, 1792:2304]).astype(BF16)
    lr_ref[...] = _dot(h, w_ref[:, 2304:2432])


def _input_projection(x, mod3, g1, w_in_r, qg, kg2, bd512, cos_t, sin_t, tm):
    B, L, _ = x.shape
    full = lambda shape: pl.BlockSpec(shape, lambda b, i: (0,) * len(shape))
    tok = lambda w: pl.BlockSpec((None, tm, w), lambda b, i: (b, i, 0))
    widths = (ATTN_WIDTH, KV_WIDTH, KV_WIDTH, KV_WIDTH, KV_WIDTH,
              GLA_QK_WIDTH, GLA_QK_WIDTH, GLA_WIDTH, GLA_WIDTH, LANES)
    dtypes = (BF16,) * 9 + (F32,)
    return pl.pallas_call(
        _inproj_kernel,
        grid=(B, L // tm),
        in_specs=[tok(D_MODEL),
                  pl.BlockSpec((None, 6, D_MODEL), lambda b, i: (b, 0, 0)),
                  full(g1.shape), full(w_in_r.shape), full(qg.shape), full(kg2.shape),
                  full(bd512.shape),
                  pl.BlockSpec((tm, LANES), lambda b, i: (i, 0)),
                  pl.BlockSpec((tm, LANES), lambda b, i: (i, 0))],
        out_specs=[tok(w) for w in widths],
        out_shape=[jax.ShapeDtypeStruct((B, L, w), dt) for w, dt in zip(widths, dtypes)],
        compiler_params=_cparams(("arbitrary", "arbitrary")),
        name="input_projection",
    )(x, mod3, g1, w_in_r, qg, kg2, bd512, cos_t, sin_t)


def _attn_kernel(sink_ref, q_ref, kp_ref, ko_ref, kn_ref, ksp_ref, kso_ref, ksn_ref,
                 vp_ref, vo_ref, vn_ref, vsp_ref, vso_ref, vsn_ref,
                 kc_ref, kcs_ref, vc_ref, vcs_ref, o_ref):
    n = pl.program_id(1)
    nb = pl.num_programs(1)
    n_ctx = kc_ref.shape[0]
    nk = n_ctx + 3 * BLOCK
    k_all = jnp.concatenate([kc_ref[...], kp_ref[...], ko_ref[...], kn_ref[...]], axis=0)
    ks_all = jnp.concatenate([kcs_ref[...], ksp_ref[...], kso_ref[...], ksn_ref[...]], axis=0)
    v_all = jnp.concatenate([vc_ref[...], vp_ref[...], vo_ref[...], vn_ref[...]], axis=0)
    vs_all = jnp.concatenate([vcs_ref[...], vsp_ref[...], vso_ref[...], vsn_ref[...]], axis=0)
    lo_k = _lane_lo((nk, LANES))
    zero = jnp.zeros((), BF16)
    qi = lax.broadcasted_iota(jnp.int32, (2 * BLOCK, nk), 0) % BLOCK
    kj = lax.broadcasted_iota(jnp.int32, (2 * BLOCK, nk), 1) - n_ctx
    has_prev = (n > 0).astype(jnp.int32)
    has_next = (n < nb - 1).astype(jnp.int32)
    first_ok = qi * has_prev + BLOCK * (1 - has_prev)
    last_ok = 2 * BLOCK - 1 + (qi + 1) * has_next
    valid = (kj < 0) | ((kj >= first_ok) & (kj <= last_ok))
    row_first = lax.broadcasted_iota(jnp.int32, (2 * BLOCK, 1), 0) < BLOCK
    lo_o = _lane_lo((2 * BLOCK, LANES))
    for h in range(N_KV_HEADS):
        kx, ky = (k_all, ks_all) if h == 0 else (ks_all, k_all)
        vx, vy = (v_all, vs_all) if h == 0 else (vs_all, v_all)
        qs = jnp.concatenate([q_ref[:, (2 * h) * LANES:(2 * h + 1) * LANES],
                              q_ref[:, (2 * h + 1) * LANES:(2 * h + 2) * LANES]], axis=0)
        outs = []
        for half, (kk, vv) in enumerate(((jnp.where(lo_k, kx, zero), vx),
                                         (jnp.where(lo_k, zero, ky), vy))):
            s = _dot_nt(qs, kk)
            s = jnp.where(valid, s, NEG_INF)
            sink = jnp.where(row_first, sink_ref[4 * h + half], sink_ref[4 * h + 2 + half])
            m = jnp.maximum(jnp.max(s, axis=-1, keepdims=True), sink)
            e = jnp.exp(s - m)
            den = jnp.sum(e, axis=-1, keepdims=True) + jnp.exp(sink - m)
            p = (e * (1.0 / den)).astype(BF16)
            outs.append(_dot(p, vv))
        o = jnp.where(lo_o, outs[0], outs[1]).astype(BF16)
        o_ref[:, (2 * h) * LANES:(2 * h + 1) * LANES] = o[0:BLOCK]
        o_ref[:, (2 * h + 1) * LANES:(2 * h + 2) * LANES] = o[BLOCK:2 * BLOCK]


def _window_attention(sink, q, k, ks, v, vs, kc, kcs, vc, vcs):
    B, L, _ = q.shape
    nb = L // BLOCK
    n_ctx = kc.shape[1]
    prev = pl.BlockSpec((None, BLOCK, KV_WIDTH), lambda b, n: (b, jnp.maximum(n - 1, 0), 0))
    own = pl.BlockSpec((None, BLOCK, KV_WIDTH), lambda b, n: (b, n, 0))
    nxt = pl.BlockSpec((None, BLOCK, KV_WIDTH), lambda b, n: (b, jnp.minimum(n + 1, nb - 1), 0))
    cspec = pl.BlockSpec((None, n_ctx, KV_WIDTH), lambda b, n: (b, 0, 0))
    return pl.pallas_call(
        _attn_kernel,
        grid=(B, nb),
        in_specs=[pl.BlockSpec(memory_space=pltpu.SMEM),
                  pl.BlockSpec((None, BLOCK, ATTN_WIDTH), lambda b, n: (b, n, 0)),
                  prev, own, nxt, prev, own, nxt, prev, own, nxt, prev, own, nxt,
                  cspec, cspec, cspec, cspec],
        out_specs=pl.BlockSpec((None, BLOCK, ATTN_WIDTH), lambda b, n: (b, n, 0)),
        out_shape=jax.ShapeDtypeStruct((B, L, ATTN_WIDTH), BF16),
        compiler_params=_cparams(("arbitrary", "arbitrary")),
        name="window_attention",
    )(sink, q, k, k, k, ks, ks, ks, v, v, v, vs, vs, vs, kc, kcs, vc, vcs)


def _gla_kernel(gq_ref, gk_ref, gv_ref, gg_ref, lr_ref, wd_ref, bdec_ref, gn_ref, sf_ref, sb_ref,
                o_ref, la_ref, of_ref, ob_ref, st_ref):
    L = gq_ref.shape[0]
    nchunk = L // CHUNK
    z = _dot_hi(lr_ref[...], wd_ref[...]) + bdec_ref[...]
    la_ref[...] = (jnp.minimum(z, 0.0) - jnp.log(1.0 + jnp.exp(-jnp.abs(z)))) * (1.0 / GATE_NORMALIZER)
    st_ref[0] = sf_ref[0]
    st_ref[1] = sf_ref[1]
    st_ref[2] = sb_ref[0]
    st_ref[3] = sb_ref[1]

    ri = lax.broadcasted_iota(jnp.int32, (CHUNK, CHUNK), 0)
    ci = lax.broadcasted_iota(jnp.int32, (CHUNK, CHUNK), 1)
    tri_f = ci <= ri
    tri_b = ci >= ri
    ones = jnp.ones((CHUNK, LANES), F32)
    lo = _lane_lo((CHUNK, LANES))

    def one_direction(d, r0, tri, ref_row, tot_row, out_ref):
        rows = pl.ds(r0, CHUNK)
        la = la_ref[rows, d * GLA_QK_WIDTH:(d + 1) * GLA_QK_WIDTH]
        cum = _dot_hi(tri.astype(F32), la)
        ref = cum[ref_row:ref_row + 1]
        tot = cum[tot_row:tot_row + 1]
        q = gq_ref[rows, :].astype(F32)
        k = gk_ref[rows, :].astype(F32)
        qe = q * jnp.exp(cum - ref)
        ke = k * jnp.exp(ref - cum)
        qg = q * jnp.exp(cum)
        kd = k * jnp.exp(tot - cum)
        for c in range(2):
            cl = slice(c * LANES, (c + 1) * LANES)
            state = st_ref[2 * d + c]
            state_b = state.astype(BF16)
            kv = None
            for hh in range(2):
                hd = 2 * c + hh
                pick = lo if hh == 0 else jnp.logical_not(lo)
                v = gv_ref[rows, hd * GLA_DV:(hd + 1) * GLA_DV]
                a = _dot_nt(qe[:, cl].astype(BF16), jnp.where(pick, ke[:, cl], 0.0).astype(BF16))
                a = jnp.where(tri, a, 0.0).astype(BF16)
                o = _dot(a, v) + _dot(jnp.where(pick, qg[:, cl], 0.0).astype(BF16), state_b)
                out_ref[rows, hd * GLA_DV:(hd + 1) * GLA_DV] = o
                kvh = _dot_tn(jnp.where(pick, kd[:, cl], 0.0).astype(BF16), v)
                kv = kvh if kv is None else kv + kvh
            decay = jnp.exp(_dot_tn(la[:, cl], ones, precision=HI))
            st_ref[2 * d + c] = decay * state + kv

    def body(n, carry):
        one_direction(0, pl.multiple_of(n * CHUNK, CHUNK), tri_f, CHUNK // 2 - 1, CHUNK - 1, of_ref)
        one_direction(1, pl.multiple_of((nchunk - 1 - n) * CHUNK, CHUNK), tri_b, CHUNK // 2, 0, ob_ref)
        return carry

    lax.fori_loop(0, nchunk, body, 0)

    for hd in range(GLA_HEADS):
        cl = slice(hd * GLA_DV, (hd + 1) * GLA_DV)
        o = of_ref[:, cl] + ob_ref[:, cl]
        y = o * lax.rsqrt(jnp.mean(o * o, axis=-1, keepdims=True) + EPS) * gn_ref[:, cl]
        g = gg_ref[:, cl].astype(F32)
        o_ref[:, cl] = (y * (g * jax.nn.sigmoid(g))).astype(BF16)


def _gla(gq, gk, gv, gg, lr, wd, bdec, gn, s_f, s_b):
    B, L, _ = gq.shape
    full = lambda shape: pl.BlockSpec(shape, lambda b: (0,) * len(shape))
    tok = lambda w: pl.BlockSpec((None, L, w), lambda b: (b, 0, 0))
    st_spec = pl.BlockSpec((None, 2, LANES, GLA_DV), lambda b: (b, 0, 0, 0))
    return pl.pallas_call(
        _gla_kernel,
        grid=(B,),
        in_specs=[tok(GLA_QK_WIDTH), tok(GLA_QK_WIDTH), tok(GLA_WIDTH), tok(GLA_WIDTH), tok(LANES),
                  full(wd.shape), full(bdec.shape), full(gn.shape), st_spec, st_spec],
        out_specs=tok(GLA_WIDTH),
        out_shape=jax.ShapeDtypeStruct((B, L, GLA_WIDTH), BF16),
        scratch_shapes=[pltpu.VMEM((L, 2 * GLA_QK_WIDTH), F32),
                        pltpu.VMEM((L, GLA_WIDTH), F32),
                        pltpu.VMEM((L, GLA_WIDTH), F32),
                        pltpu.VMEM((4, LANES, GLA_DV), F32)],
        compiler_params=_cparams(("arbitrary",)),
        name="gla_bidirectional",
    )(gq, gk, gv, gg, lr, wd, bdec, gn, s_f, s_b)


def _outproj_kernel(attn_ref, gla_ref, x_ref, mod_ref, w_ref, g2_ref, wr_ref,
                    x1_ref, h2_ref, afft_ref):
    y = _dot(attn_ref[...], w_ref[0:ATTN_WIDTH, :]) + _dot(gla_ref[...], w_ref[ATTN_WIDTH:, :])
    x1 = x_ref[...] + mod_ref[2:3, :] * y
    x1_ref[...] = x1
    h2 = _rms_mod(x1, g2_ref[...], mod_ref[3:4, :], mod_ref[4:5, :]).astype(BF16)
    h2_ref[...] = h2
    logits = _dot(h2, wr_ref[...])
    live = lax.broadcasted_iota(jnp.int32, logits.shape, 1) < N_EXPERTS
    logits = jnp.where(live, logits, NEG_INF)
    e = jnp.exp(logits - jnp.max(logits, axis=-1, keepdims=True))
    aff = e / jnp.sum(e, axis=-1, keepdims=True)
    afft_ref[...] = aff.T[0:N_EXPERTS, :]


def _output_projection(attn, gla, x, mod3, w_out, g2, w_router, tm):
    B, L, _ = x.shape
    full = lambda shape: pl.BlockSpec(shape, lambda b, i: (0,) * len(shape))
    tok = lambda w: pl.BlockSpec((None, tm, w), lambda b, i: (b, i, 0))
    return pl.pallas_call(
        _outproj_kernel,
        grid=(B, L // tm),
        in_specs=[tok(ATTN_WIDTH), tok(GLA_WIDTH), tok(D_MODEL),
                  pl.BlockSpec((None, 6, D_MODEL), lambda b, i: (b, 0, 0)),
                  full(w_out.shape), full(g2.shape), full(w_router.shape)],
        out_specs=[tok(D_MODEL), tok(D_MODEL),
                   pl.BlockSpec((None, N_EXPERTS, tm), lambda b, i: (b, 0, i))],
        out_shape=[jax.ShapeDtypeStruct((B, L, D_MODEL), F32),
                   jax.ShapeDtypeStruct((B, L, D_MODEL), BF16),
                   jax.ShapeDtypeStruct((B, N_EXPERTS, L), F32)],
        compiler_params=_cparams(("arbitrary", "arbitrary")),
        name="output_projection_router",
    )(attn, gla, x, mod3, w_out, g2, w_router)


def _topk_kernel(afft_ref, post_ref, pos_ref, gsel_ref, *, cap):
    aff = afft_ref[...]
    E, L = aff.shape
    bits = pltpu.bitcast(aff, jnp.int32)

    def search(i, thr):
        cand = thr | jnp.left_shift(jnp.int32(1), 30 - i)
        cnt = jnp.sum(jnp.where(bits >= cand, 1.0, 0.0), axis=-1, keepdims=True)
        return jnp.where(cnt >= cap, cand, thr)

    thr = lax.fori_loop(0, 31, search, jnp.zeros((E, 1), jnp.int32))
    above = bits > thr
    tie = bits == thr
    need = cap - jnp.sum(jnp.where(above, 1.0, 0.0), axis=-1, keepdims=True)

    upper = (lax.broadcasted_iota(jnp.int32, (LANES, LANES), 0)
             <= lax.broadcasted_iota(jnp.int32, (LANES, LANES), 1)).astype(BF16)

    def prefix(mask):
        parts = []
        run = jnp.zeros((E, 1), F32)
        for j in range(L // LANES):
            blk = jnp.where(mask[:, j * LANES:(j + 1) * LANES], 1.0, 0.0).astype(BF16)
            loc = _dot(blk, upper) + run
            parts.append(loc)
            run = loc[:, LANES - 1:LANES]
        return jnp.concatenate(parts, axis=1)

    tie_rank = prefix(tie)
    sel = above | (tie & (tie_rank <= need))
    slot = prefix(sel).astype(jnp.int32) - 1
    post = jnp.where(sel, slot, -1)
    post_ref[...] = post
    gsel = jnp.where(sel, aff, 0.0)
    pad_i = jnp.full((LANES - E, L), -1, jnp.int32)
    pad_f = jnp.zeros((LANES - E, L), F32)
    pos_ref[...] = jnp.concatenate([post, pad_i], axis=0).T
    gsel_ref[...] = jnp.concatenate([gsel, pad_f], axis=0).T


def _expert_choice(afft, cap):
    B, E, L = afft.shape
    return pl.pallas_call(
        functools.partial(_topk_kernel, cap=cap),
        grid=(B,),
        in_specs=[pl.BlockSpec((None, E, L), lambda b: (b, 0, 0))],
        out_specs=[pl.BlockSpec((None, E, L), lambda b: (b, 0, 0)),
                   pl.BlockSpec((None, L, LANES), lambda b: (b, 0, 0)),
                   pl.BlockSpec((None, L, LANES), lambda b: (b, 0, 0))],
        out_shape=[jax.ShapeDtypeStruct((B, E, L), jnp.int32),
                   jax.ShapeDtypeStruct((B, L, LANES), jnp.int32),
                   jax.ShapeDtypeStruct((B, L, LANES), F32)],
        compiler_params=_cparams(("arbitrary",)),
        name="expert_choice_topk",
    )(afft)


def _dispatch_kernel(post_ref, h2_ref, xs_ref, *, cap):
    L = h2_ref.shape[0]
    slot = lax.broadcasted_iota(jnp.int32, (cap, L), 0)
    onehot = jnp.where(post_ref[...] == slot, 1.0, 0.0).astype(BF16)
    xs_ref[...] = _dot(onehot, h2_ref[...]).astype(BF16)


def _dispatch(post4, h2, cap):
    B, E, _, L = post4.shape
    return pl.pallas_call(
        functools.partial(_dispatch_kernel, cap=cap),
        grid=(B, E),
        in_specs=[pl.BlockSpec((None, None, 1, L), lambda b, e: (b, e, 0, 0)),
                  pl.BlockSpec((None, L, D_MODEL), lambda b, e: (b, 0, 0))],
        out_specs=pl.BlockSpec((None, None, cap, D_MODEL), lambda b, e: (b, e, 0, 0)),
        out_shape=jax.ShapeDtypeStruct((B, E, cap, D_MODEL), BF16),
        compiler_params=_cparams(("arbitrary", "arbitrary")),
        name="moe_dispatch",
    )(post4, h2)


def _ffn_kernel(xs_ref, wg_ref, wu_ref, wd_ref, y_ref):
    nbatch, cap, d = xs_ref.shape
    xs = xs_ref[...].reshape(nbatch * cap, d)
    f = wg_ref.shape[1]
    half = f // 2
    acc = None
    for j in range(2):
        cols = slice(j * half, (j + 1) * half)
        g = _dot(xs, wg_ref[:, cols])
        u = _dot(xs, wu_ref[:, cols])
        hid = (g * jax.nn.sigmoid(g) * u).astype(BF16)
        part = _dot(hid, wd_ref[cols, :])
        acc = part if acc is None else acc + part
    y_ref[...] = acc.astype(BF16).reshape(nbatch, cap, d)


def _expert_ffn(xs, w_gate, w_up, w_down, nbatch):
    B, E, cap, d = xs.shape
    f = w_gate.shape[2]
    tok = pl.BlockSpec((nbatch, None, cap, d), lambda e, b: (b, e, 0, 0))
    return pl.pallas_call(
        _ffn_kernel,
        grid=(E, B // nbatch),
        in_specs=[tok,
                  pl.BlockSpec((None, d, f), lambda e, b: (e, 0, 0)),
                  pl.BlockSpec((None, d, f), lambda e, b: (e, 0, 0)),
                  pl.BlockSpec((None, f, d), lambda e, b: (e, 0, 0))],
        out_specs=tok,
        out_shape=jax.ShapeDtypeStruct((B, E, cap, d), BF16),
        compiler_params=_cparams(("arbitrary", "arbitrary")),
        name="expert_swiglu",
    )(xs, w_gate, w_up, w_down)


def _combine_kernel(pos_ref, gsel_ref, y_ref, x1_ref, mod_ref, o_ref, acc_ref, *, cap):
    tt = pos_ref.shape[0]
    slot = lax.broadcasted_iota(jnp.int32, (tt, cap), 1)
    for e in range(N_EXPERTS):
        onehot = jnp.where(pos_ref[:, e:e + 1] == slot, 1.0, 0.0).astype(BF16)
        part = gsel_ref[:, e:e + 1] * _dot(onehot, y_ref[e])
        if e == 0:
            acc_ref[...] = part
        else:
            acc_ref[...] += part
    o_ref[...] = x1_ref[...] + mod_ref[5:6, :] * acc_ref[...]


def _combine(pos, gsel, y, x1, mod3, cap, tt):
    B, L, _ = x1.shape
    tok = lambda w: pl.BlockSpec((None, tt, w), lambda b, i: (b, i, 0))
    return pl.pallas_call(
        functools.partial(_combine_kernel, cap=cap),
        grid=(B, L // tt),
        in_specs=[tok(LANES), tok(LANES),
                  pl.BlockSpec((None, N_EXPERTS, cap, D_MODEL), lambda b, i: (b, 0, 0, 0)),
                  tok(D_MODEL),
                  pl.BlockSpec((None, 6, D_MODEL), lambda b, i: (b, 0, 0))],
        out_specs=tok(D_MODEL),
        out_shape=jax.ShapeDtypeStruct((B, L, D_MODEL), F32),
        scratch_shapes=[pltpu.VMEM((tt, D_MODEL), F32)],
        compiler_params=_cparams(("arbitrary", "arbitrary")),
        name="moe_combine",
    )(pos, gsel, y, x1, mod3)


def _rope_tables(L):
    inv = ROPE_BASE ** (-jnp.arange(ROPE_FREQS, dtype=F32) / ROPE_FREQS)
    pos = jnp.arange(L)
    row = (pos // GRID_W).astype(F32)[:, None] * inv
    col = (pos % GRID_W).astype(F32)[:, None] * inv
    cos = jnp.concatenate([jnp.cos(row), jnp.cos(row), jnp.cos(col), jnp.cos(col)], axis=1)
    sin = jnp.concatenate([-jnp.sin(row), jnp.sin(row), -jnp.sin(col), jnp.sin(col)], axis=1)
    return jnp.tile(cos, (1, 2)), jnp.tile(sin, (1, 2))


def _head_mean_matrix(n):
    idx = np.arange(n) // HEAD_DIM
    return jnp.asarray((idx[:, None] == idx[None, :]).astype(np.float32) / HEAD_DIM, dtype=BF16)


def kernel(x, c, ctx, c_ctx, w_mod, b_mod, norm1_g, w_in, q_norm_g, k_norm_g, attn_sink,
           w_decay_fwd, b_decay_fwd, w_decay_bwd, b_decay_bwd, gla_norm_g, w_out, norm2_g,
           w_router, w_e_gate, w_e_up, w_e_down):
    B, L, D = x.shape
    cap = CAPACITY_FACTOR * L // N_EXPERTS
    layer = 0

    rows = ((B + 1 + 7) // 8) * 8
    cc = jnp.concatenate([c, c_ctx[None, :], jnp.zeros((rows - B - 1, D), F32)], axis=0)
    mod_all = _modulation(cc, w_mod[layer], b_mod[layer])
    mod3 = mod_all[:B].reshape(B, 6, D)
    modc = mod_all[B].reshape(6, D)

    w = w_in[layer]
    o = np.cumsum([0, ATTN_WIDTH, KV_WIDTH, KV_WIDTH, GLA_QK_WIDTH, GLA_QK_WIDTH,
                   GLA_WIDTH, GLA_WIDTH, GATE_RANK, GATE_RANK])
    w_lr = jnp.concatenate([w[:, o[7]:o[9]], jnp.zeros((D, LANES - 2 * GATE_RANK), F32)], axis=1)
    w_in_r = jnp.concatenate([w[:, :o[7]], w_lr], axis=1).astype(BF16)
    w_ctx = jnp.concatenate([w[:, o[1]:o[3]], w[:, o[4]:o[6]], w_lr], axis=1).astype(BF16)
    wd = jnp.zeros((LANES, 2 * GLA_QK_WIDTH), F32)
    wd = wd.at[0:GATE_RANK, 0:GLA_QK_WIDTH].set(w_decay_fwd[layer])
    wd = wd.at[GATE_RANK:2 * GATE_RANK, GLA_QK_WIDTH:].set(w_decay_bwd[layer])
    bdec = jnp.concatenate([b_decay_fwd[layer], b_decay_bwd[layer]])[None, :]
    g1 = norm1_g[layer][None, :]
    g2 = norm2_g[layer][None, :]
    qg = jnp.tile(q_norm_g[layer], N_Q_HEADS)[None, :]
    kg2 = jnp.tile(k_norm_g[layer], N_KV_HEADS)[None, :]
    gn = jnp.tile(gla_norm_g[layer], GLA_HEADS)[None, :]
    bd512 = _head_mean_matrix(ATTN_WIDTH)
    bd128 = _head_mean_matrix(KV_WIDTH)
    cos_t, sin_t = _rope_tables(L)
    w_router_p = jnp.concatenate([w_router[layer], jnp.zeros((D, LANES - N_EXPERTS), F32)],
                                 axis=1).astype(BF16)

    kc, kcs, vc, vcs, s_f, s_b = _context_side(ctx, modc, g1, w_ctx, kg2, bd128, wd, bdec)
    q, k, ks, v, vs, gq, gk, gv, gg, lr = _input_projection(
        x, mod3, g1, w_in_r, qg, kg2, bd512, cos_t, sin_t, tm=512)
    attn = _window_attention(attn_sink[layer], q, k, ks, v, vs, kc, kcs, vc, vcs)
    gla = _gla(gq, gk, gv, gg, lr, wd, bdec, gn, s_f, s_b)
    x1, h2, afft = _output_projection(attn, gla, x, mod3, w_out[layer].astype(BF16), g2,
                                      w_router_p, tm=512)
    post, pos, gsel = _expert_choice(afft, cap)
    xs = _dispatch(post.reshape(B, N_EXPERTS, 1, L), h2, cap)
    y = _expert_ffn(xs, w_e_gate[layer].astype(BF16), w_e_up[layer].astype(BF16),
                    w_e_down[layer].astype(BF16), nbatch=4)
    return _combine(pos, gsel, y, x1, mod3, cap, tt=512)
```

```python
import functools

import jax
import jax.numpy as jnp
import numpy as np
from jax import lax
from jax.experimental import pallas as pl
from jax.experimental.pallas import tpu as pltpu

D_MODEL = 1024
GRID_W = 64
HEAD_DIM = 64
N_Q_HEADS = 8
N_KV_HEADS = 2
BLOCK = 128
ROPE_FREQS = 16
ROPE_BASE = 10000.0
GLA_HEADS = 4
GLA_DV = 128
GLA_DK = 64
GATE_RANK = 16
GATE_NORMALIZER = 16.0
CHUNK = 64
N_EXPERTS = 16
CAPACITY_FACTOR = 2
ATTN_WIDTH = 512
KV_WIDTH = 128
GLA_QK_WIDTH = 256
GLA_WIDTH = 512
EPS = 1e-6
NEG_INF = -1e30

LANES = 128
VMEM_LIMIT = 56 * 1024 * 1024

F32 = jnp.float32
BF16 = jnp.bfloat16
HI = lax.Precision.HIGHEST


def _cparams(sem):
    return pltpu.CompilerParams(dimension_semantics=sem, vmem_limit_bytes=VMEM_LIMIT)


def _dot(a, b):
    return jnp.dot(a, b, preferred_element_type=F32)


def _dot_hi(a, b):
    return jnp.dot(a, b, preferred_element_type=F32, precision=HI)


def _dot_nt(a, b):
    return lax.dot_general(a, b, (((1,), (1,)), ((), ())), preferred_element_type=F32)


def _dot_tn(a, b, precision=None):
    return lax.dot_general(a, b, (((0,), (0,)), ((), ())), preferred_element_type=F32,
                           precision=precision)


def _split2(t):
    hi = t.astype(BF16)
    lo = (t - hi.astype(F32)).astype(BF16)
    return hi, lo


def _rms_mod(t, g, shift, scale):
    y = t * lax.rsqrt(jnp.mean(t * t, axis=-1, keepdims=True) + EPS)
    return (y * g) * (1.0 + scale) + shift


def _lane_lo(shape):
    return (lax.broadcasted_iota(jnp.int32, shape, len(shape) - 1) % LANES) < HEAD_DIM


def _mod_kernel(c_ref, w_ref, b_ref, o_ref):
    c = c_ref[...]
    s = c * jax.nn.sigmoid(c)
    o_ref[...] = _dot_hi(s, w_ref[...]) + b_ref[...]


def _modulation(cc, w_mod, b_mod):
    m = cc.shape[0]
    n = w_mod.shape[1]
    tn = 1024
    return pl.pallas_call(
        _mod_kernel,
        grid=(n // tn,),
        in_specs=[pl.BlockSpec((m, D_MODEL), lambda j: (0, 0)),
                  pl.BlockSpec((D_MODEL, tn), lambda j: (0, j)),
                  pl.BlockSpec((1, tn), lambda j: (0, j))],
        out_specs=pl.BlockSpec((m, tn), lambda j: (0, j)),
        out_shape=jax.ShapeDtypeStruct((m, n), F32),
        compiler_params=_cparams(("arbitrary",)),
        name="adaln_mod",
    )(cc, w_mod, b_mod.reshape(1, n))


def _ctx_kernel(ctx_ref, mod_ref, g1_ref, w_ref, kg_ref, bd_ref, wd_ref, bdec_ref,
                kc_ref, kcs_ref, vc_ref, vcs_ref, sf_ref, sb_ref):
    n = ctx_ref.shape[0]
    h = _rms_mod(ctx_ref[...], g1_ref[...], mod_ref[0:1, :], mod_ref[1:2, :]).astype(BF16)
    pc = _dot(h, w_ref[...])
    ak = pc[:, 0:128]
    av = pc[:, 128:256]
    gk = pc[:, 256:512]
    gv = pc[:, 512:1024].astype(BF16)
    lr = pc[:, 1024:1152]
    sq_hi, sq_lo = _split2(ak * ak)
    ms = _dot(sq_hi, bd_ref[...]) + _dot(sq_lo, bd_ref[...])
    kn = ak * lax.rsqrt(ms + EPS) * kg_ref[...]
    kc_ref[...] = kn.astype(BF16)
    kcs_ref[...] = pltpu.roll(kn, HEAD_DIM, 1).astype(BF16)
    vc_ref[...] = av.astype(BF16)
    vcs_ref[...] = pltpu.roll(av, HEAD_DIM, 1).astype(BF16)
    z = _dot_hi(lr, wd_ref[...]) + bdec_ref[...]
    la = (jnp.minimum(z, 0.0) - jnp.log(1.0 + jnp.exp(-jnp.abs(z)))) * (1.0 / GATE_NORMALIZER)
    r = lax.broadcasted_iota(jnp.int32, (n, n), 0)
    cidx = lax.broadcasted_iota(jnp.int32, (n, n), 1)
    after = (cidx > r).astype(F32)
    before = (cidx < r).astype(F32)
    w_f = jnp.exp(_dot_hi(after, la[:, 0:256]))
    w_b = jnp.exp(_dot_hi(before, la[:, 256:512]))
    lo = _lane_lo((n, LANES))
    for w, out in ((w_f, sf_ref), (w_b, sb_ref)):
        kw = gk * w
        for c in range(2):
            kwc = kw[:, c * LANES:(c + 1) * LANES]
            k_lo = jnp.where(lo, kwc, 0.0).astype(BF16)
            k_hi = jnp.where(lo, 0.0, kwc).astype(BF16)
            v0 = gv[:, (2 * c) * GLA_DV:(2 * c + 1) * GLA_DV]
            v1 = gv[:, (2 * c + 1) * GLA_DV:(2 * c + 2) * GLA_DV]
            out[c] = _dot_tn(v0, k_lo) + _dot_tn(v1, k_hi)


def _context_side(ctx, modc, g1, w_ctx, kg2, bd128, wd, bdec):
    B, n, _ = ctx.shape
    full = lambda shape: pl.BlockSpec(shape, lambda b: (0,) * len(shape))
    kv_spec = pl.BlockSpec((None, n, KV_WIDTH), lambda b: (b, 0, 0))
    st_spec = pl.BlockSpec((None, 2, LANES, GLA_DV), lambda b: (b, 0, 0, 0))
    kv_shape = jax.ShapeDtypeStruct((B, n, KV_WIDTH), BF16)
    st_shape = jax.ShapeDtypeStruct((B, 2, LANES, GLA_DV), F32)
    return pl.pallas_call(
        _ctx_kernel,
        grid=(B,),
        in_specs=[pl.BlockSpec((None, n, D_MODEL), lambda b: (b, 0, 0)),
                  full(modc.shape), full(g1.shape), full(w_ctx.shape), full(kg2.shape),
                  full(bd128.shape), full(wd.shape), full(bdec.shape)],
        out_specs=[kv_spec, kv_spec, kv_spec, kv_spec, st_spec, st_spec],
        out_shape=[kv_shape, kv_shape, kv_shape, kv_shape, st_shape, st_shape],
        compiler_params=_cparams(("arbitrary",)),
        name="context_side",
    )(ctx, modc, g1, w_ctx, kg2, bd128, wd, bdec)


def _swap16(t):
    n = t.shape[1]
    first = (lax.broadcasted_iota(jnp.int32, t.shape, 1) % 32) < ROPE_FREQS
    return jnp.where(first, pltpu.roll(t, n - ROPE_FREQS, 1), pltpu.roll(t, ROPE_FREQS, 1))


def _inproj_kernel(x_ref, mod_ref, g1_ref, w_ref, qg_ref, kg_ref, bd_ref, cos_ref, sin_ref,
                   q_ref, k_ref, ks_ref, v_ref, vs_ref, gq_ref, gk_ref, gv_ref, gg_ref, lr_ref):
    h = _rms_mod(x_ref[...], g1_ref[...], mod_ref[0:1, :], mod_ref[1:2, :]).astype(BF16)
    cos = cos_ref[...]
    sin = sin_ref[...]

    def head_norm_rope(t, g, bd, reps):
        sq_hi, sq_lo = _split2(t * t)
        ms = _dot(sq_hi, bd) + _dot(sq_lo, bd)
        tn = t * lax.rsqrt(ms + EPS) * g
        c = jnp.concatenate([cos] * reps, axis=1) if reps > 1 else cos
        s = jnp.concatenate([sin] * reps, axis=1) if reps > 1 else sin
        return tn * c + _swap16(tn) * s

    aq = _dot(h, w_ref[:, 0:512])
    q = head_norm_rope(aq, qg_ref[...], bd_ref[...], 4) * (HEAD_DIM ** -0.5)
    q_ref[...] = q.astype(BF16)
    akv = _dot(h, w_ref[:, 512:768])
    k = head_norm_rope(akv[:, 0:128], kg_ref[...], bd_ref[0:128, 0:128], 1)
    k_ref[...] = k.astype(BF16)
    ks_ref[...] = pltpu.roll(k, HEAD_DIM, 1).astype(BF16)
    v = akv[:, 128:256]
    v_ref[...] = v.astype(BF16)
    vs_ref[...] = pltpu.roll(v, HEAD_DIM, 1).astype(BF16)
    gqk = _dot(h, w_ref[:, 768:1280])
    gq_ref[...] = (gqk[:, 0:256] * (GLA_DK ** -0.5)).astype(BF16)
    gk_ref[...] = gqk[:, 256:512].astype(BF16)
    gv_ref[...] = _dot(h, w_ref[:, 1280:1792]).astype(BF16)
    gg_ref[...] = _dot(h, w_ref[:, 1792:2304]).astype(BF16)
    lr_ref[...] = _dot(h, w_ref[:, 2304:2432])


def _input_projection(x, mod3, g1, w_in_r, qg, kg2, bd512, cos_t, sin_t, tm):
    B, L, _ = x.shape
    full = lambda shape: pl.BlockSpec(shape, lambda b, i: (0,) * len(shape))
    tok = lambda w: pl.BlockSpec((None, tm, w), lambda b, i: (b, i, 0))
    widths = (ATTN_WIDTH, KV_WIDTH, KV_WIDTH, KV_WIDTH, KV_WIDTH,
              GLA_QK_WIDTH, GLA_QK_WIDTH, GLA_WIDTH, GLA_WIDTH, LANES)
    dtypes = (BF16,) * 9 + (F32,)
    return pl.pallas_call(
        _inproj_kernel,
        grid=(B, L // tm),
        in_specs=[tok(D_MODEL),
                  pl.BlockSpec((None, 6, D_MODEL), lambda b, i: (b, 0, 0)),
                  full(g1.shape), full(w_in_r.shape), full(qg.shape), full(kg2.shape),
                  full(bd512.shape),
                  pl.BlockSpec((tm, LANES), lambda b, i: (i, 0)),
                  pl.BlockSpec((tm, LANES), lambda b, i: (i, 0))],
        out_specs=[tok(w) for w in widths],
        out_shape=[jax.ShapeDtypeStruct((B, L, w), dt) for w, dt in zip(widths, dtypes)],
        compiler_params=_cparams(("arbitrary", "arbitrary")),
        name="input_projection",
    )(x, mod3, g1, w_in_r, qg, kg2, bd512, cos_t, sin_t)


def _attn_kernel(sink_ref, q_ref, kp_ref, ko_ref, kn_ref, ksp_ref, kso_ref, ksn_ref,
                 vp_ref, vo_ref, vn_ref, vsp_ref, vso_ref, vsn_ref,
                 kc_ref, kcs_ref, vc_ref, vcs_ref, o_ref):
    i = pl.program_id(1)
    ni = pl.num_programs(1)
    nsub = q_ref.shape[0] // BLOCK
    win = 3 * BLOCK
    k_win = jnp.concatenate([kp_ref[...], ko_ref[...], kn_ref[...]], axis=0)
    ks_win = jnp.concatenate([ksp_ref[...], kso_ref[...], ksn_ref[...]], axis=0)
    v_win = jnp.concatenate([vp_ref[...], vo_ref[...], vn_ref[...]], axis=0)
    vs_win = jnp.concatenate([vsp_ref[...], vso_ref[...], vsn_ref[...]], axis=0)
    lo_w = _lane_lo(k_win.shape)
    lo_c = _lane_lo(kc_ref.shape)
    zero = jnp.zeros((), BF16)
    one = jnp.ones((), BF16)
    qi = lax.broadcasted_iota(jnp.int32, (2 * BLOCK, BLOCK), 0) % BLOCK
    kj = lax.broadcasted_iota(jnp.int32, (2 * BLOCK, BLOCK), 1)
    no_prev = jnp.where(i > 0, 0, BLOCK)
    no_next = jnp.where(i < ni - 1, 0, BLOCK)
    row_first = lax.broadcasted_iota(jnp.int32, (2 * BLOCK, 1), 0) < BLOCK
    lo_o = _lane_lo((2 * BLOCK, LANES))
    for h in range(N_KV_HEADS):
        kx_w, ky_w = (k_win, ks_win) if h == 0 else (ks_win, k_win)
        vx_w, vy_w = (v_win, vs_win) if h == 0 else (vs_win, v_win)
        kx_c, ky_c = (kc_ref[...], kcs_ref[...]) if h == 0 else (kcs_ref[...], kc_ref[...])
        vx_c, vy_c = (vc_ref[...], vcs_ref[...]) if h == 0 else (vcs_ref[...], vc_ref[...])
        operands = ((jnp.where(lo_c, kx_c, zero), jnp.where(lo_w, kx_w, zero),
                     jnp.where(lo_c, vx_c, one), jnp.where(lo_w, vx_w, one)),
                    (jnp.where(lo_c, zero, ky_c), jnp.where(lo_w, zero, ky_w),
                     jnp.where(lo_c, one, vy_c), jnp.where(lo_w, one, vy_w)))
        for t in range(nsub):
            rows = slice(t * BLOCK, (t + 1) * BLOCK)
            keys = slice(t * BLOCK, t * BLOCK + win)
            qs = jnp.concatenate([q_ref[rows, (2 * h) * LANES:(2 * h + 1) * LANES],
                                  q_ref[rows, (2 * h + 1) * LANES:(2 * h + 2) * LANES]], axis=0)
            prev_ok = kj >= qi + (no_prev if t == 0 else 0)
            next_ok = kj <= qi - (no_next if t == nsub - 1 else 0)
            outs = []
            for half, (kk_c, kk_w, vv_c, vv_w) in enumerate(operands):
                s_c = _dot_nt(qs, kk_c)
                s_w = _dot_nt(qs, kk_w[keys])
                s_p = jnp.where(prev_ok, s_w[:, 0:BLOCK], NEG_INF)
                s_o = s_w[:, BLOCK:2 * BLOCK]
                s_n = jnp.where(next_ok, s_w[:, 2 * BLOCK:win], NEG_INF)
                sink = jnp.where(row_first, sink_ref[4 * h + half], sink_ref[4 * h + 2 + half])
                m = jnp.maximum(jnp.maximum(jnp.max(s_c, axis=-1, keepdims=True),
                                            jnp.max(jnp.maximum(jnp.maximum(s_p, s_o), s_n),
                                                    axis=-1, keepdims=True)), sink)
                e_c = jnp.exp(s_c - m).astype(BF16)
                e_w = jnp.concatenate([jnp.exp(s_p - m), jnp.exp(s_o - m), jnp.exp(s_n - m)],
                                      axis=1).astype(BF16)
                acc = _dot(e_c, vv_c) + _dot(e_w, vv_w[keys])
                den = pltpu.roll(acc, HEAD_DIM, 1) + jnp.exp(sink - m)
                outs.append(acc / den)
            o = jnp.where(lo_o, outs[0], outs[1]).astype(BF16)
            o_ref[rows, (2 * h) * LANES:(2 * h + 1) * LANES] = o[0:BLOCK]
            o_ref[rows, (2 * h + 1) * LANES:(2 * h + 2) * LANES] = o[BLOCK:2 * BLOCK]


def _window_attention(sink, q, k, ks, v, vs, kc, kcs, vc, vcs, tq):
    B, L, _ = q.shape
    nb = L // BLOCK
    nsub = tq // BLOCK
    n_ctx = kc.shape[1]
    prev = pl.BlockSpec((None, BLOCK, KV_WIDTH), lambda b, n: (b, jnp.maximum(n * nsub - 1, 0), 0))
    own = pl.BlockSpec((None, tq, KV_WIDTH), lambda b, n: (b, n, 0))
    nxt = pl.BlockSpec((None, BLOCK, KV_WIDTH),
                       lambda b, n: (b, jnp.minimum((n + 1) * nsub, nb - 1), 0))
    cspec = pl.BlockSpec((None, n_ctx, KV_WIDTH), lambda b, n: (b, 0, 0))
    return pl.pallas_call(
        _attn_kernel,
        grid=(B, L // tq),
        in_specs=[pl.BlockSpec(memory_space=pltpu.SMEM),
                  pl.BlockSpec((None, tq, ATTN_WIDTH), lambda b, n: (b, n, 0)),
                  prev, own, nxt, prev, own, nxt, prev, own, nxt, prev, own, nxt,
                  cspec, cspec, cspec, cspec],
        out_specs=pl.BlockSpec((None, tq, ATTN_WIDTH), lambda b, n: (b, n, 0)),
        out_shape=jax.ShapeDtypeStruct((B, L, ATTN_WIDTH), BF16),
        compiler_params=_cparams(("arbitrary", "arbitrary")),
        name="window_attention",
    )(sink, q, k, k, k, ks, ks, ks, v, v, v, vs, vs, vs, kc, kcs, vc, vcs)


SUPER = 256
CH_PER = SUPER // CHUNK


def _dot_split(m, parts):
    return _dot(m, parts[0]) + _dot(m, parts[1])


def _gla_kernel(gq_ref, gk_ref, gv_ref, gg_ref, lr_ref, wd_ref, bdec_ref, gn_ref, sf_ref, sb_ref,
                o_ref, la_ref, oi_ref, qg_ref, kv_ref, sb16_ref, dec_ref, st_ref):
    L = gq_ref.shape[0]
    nsuper = L // SUPER
    nchunk = L // CHUNK
    half = CHUNK // 2
    lr_hi, lr_lo = _split2(lr_ref[...])
    wd_hi, wd_lo = _split2(wd_ref[...])
    z = _dot(lr_hi, wd_hi) + _dot(lr_lo, wd_hi) + _dot(lr_hi, wd_lo) + bdec_ref[...]
    la_ref[...] = (jnp.minimum(z, 0.0) - jnp.log(1.0 + jnp.exp(-jnp.abs(z)))) * (1.0 / GATE_NORMALIZER)

    r = lax.broadcasted_iota(jnp.int32, (SUPER, SUPER), 0)
    cidx = lax.broadcasted_iota(jnp.int32, (SUPER, SUPER), 1)
    same = (r // CHUNK) == (cidx // CHUNK)
    pr = r % CHUNK
    pc = cidx % CHUNK
    one = jnp.float32(1.0)
    zero = jnp.float32(0.0)
    in_f = jnp.where(pc <= pr, one, zero)
    in_b = jnp.where(pc >= pr, one, zero)
    ref_f = jnp.where(pc < half, one, zero)
    ref_b = jnp.where(pc >= half, one, zero)
    m1_f = jnp.where(same, in_f - ref_f, zero).astype(BF16)
    m1_b = jnp.where(same, in_b - ref_b, zero).astype(BF16)
    mask_f = jnp.where(same, in_f, zero) > 0.5
    mask_b = jnp.where(same, in_b, zero) > 0.5
    rr = lax.broadcasted_iota(jnp.int32, (2 * CH_PER, SUPER), 0)
    rc = lax.broadcasted_iota(jnp.int32, (2 * CH_PER, SUPER), 1)
    in_chunk = jnp.where((rc // CHUNK) == (rr % CH_PER), one, zero)
    first = jnp.where((rc % CHUNK) < half, 1, 0)
    is_tot = jnp.where(rr >= CH_PER, 1, 0)
    rs_f = (in_chunk * jnp.where(first != is_tot, one, zero)).astype(BF16)
    rs_b = (in_chunk * jnp.where(first == is_tot, one, zero)).astype(BF16)
    hi_lane = jnp.where(_lane_lo((SUPER, LANES)), 0, 1)

    def phase1(s, carry):
        rows = pl.ds(pl.multiple_of(s * SUPER, SUPER), SUPER)
        q = gq_ref[rows, :].astype(F32)
        k = gk_ref[rows, :].astype(F32)
        acc = [None] * GLA_HEADS
        for d, (m1, rs, mask) in enumerate(((m1_f, rs_f, mask_f), (m1_b, rs_b, mask_b))):
            parts = _split2(la_ref[rows, d * GLA_QK_WIDTH:(d + 1) * GLA_QK_WIDTH])
            x1 = _dot_split(m1, parts)
            erow = jnp.exp(_dot_split(rs, parts))
            dec = erow[0:CH_PER] * erow[CH_PER:2 * CH_PER]
            dec_ref[d, s] = jnp.concatenate([dec, dec], axis=0)
            qe = q * jnp.exp(x1)
            ke = k * jnp.exp(-x1)
            qgs, kds = [], []
            for j in range(CH_PER):
                rj = slice(j * CHUNK, (j + 1) * CHUNK)
                qgs.append(qe[rj] * erow[j:j + 1])
                kds.append(ke[rj] * erow[CH_PER + j:CH_PER + j + 1])
            qg_ref[d, rows, :] = jnp.concatenate(qgs, axis=0).astype(BF16)
            kd = jnp.concatenate(kds, axis=0)
            for c in range(2):
                cl = slice(c * LANES, (c + 1) * LANES)
                qe_c = qe[:, cl].astype(BF16)
                ke_c = ke[:, cl]
                kd_c = kd[:, cl]
                kvt = [None] * CH_PER
                for hh in range(2):
                    hd = 2 * c + hh
                    v = gv_ref[rows, hd * GLA_DV:(hd + 1) * GLA_DV]
                    ke_m = jnp.where(hi_lane == hh, ke_c, zero).astype(BF16)
                    a = jnp.where(mask, _dot_nt(qe_c, ke_m), zero).astype(BF16)
                    oi = _dot(a, v)
                    acc[hd] = oi if acc[hd] is None else acc[hd] + oi
                    kd_m = jnp.where(hi_lane == hh, kd_c, zero).astype(BF16)
                    for j in range(CH_PER):
                        rj = slice(j * CHUNK, (j + 1) * CHUNK)
                        t = _dot_tn(v[rj], kd_m[rj])
                        kvt[j] = t if kvt[j] is None else kvt[j] + t
                for j in range(CH_PER):
                    kv_ref[d, c, s * CH_PER + j] = kvt[j]
        for hd in range(GLA_HEADS):
            oi_ref[rows, hd * GLA_DV:(hd + 1) * GLA_DV] = acc[hd]
        return carry

    lax.fori_loop(0, nsuper, phase1, 0)

    st_ref[0] = sf_ref[0]
    st_ref[1] = sf_ref[1]
    st_ref[2] = sb_ref[0]
    st_ref[3] = sb_ref[1]

    def phase2(n, carry):
        for d in range(2):
            idx = n if d == 0 else nchunk - 1 - n
            dec = dec_ref[d, idx // CH_PER, pl.ds(idx % CH_PER, 1), :]
            for c in range(2):
                st = st_ref[2 * d + c]
                sb16_ref[d, c, idx] = st.astype(BF16)
                st_ref[2 * d + c] = dec[:, c * LANES:(c + 1) * LANES] * st + kv_ref[d, c, idx]
        return carry

    lax.fori_loop(0, nchunk, phase2, 0)

    lo64 = _lane_lo((CHUNK, LANES))
    zero_b = jnp.zeros((), BF16)

    def phase3(s, carry):
        r0 = pl.multiple_of(s * SUPER, SUPER)
        rows = pl.ds(r0, SUPER)
        inter = [[None] * CH_PER for _ in range(GLA_HEADS)]
        for j in range(CH_PER):
            rj = pl.ds(r0 + j * CHUNK, CHUNK)
            for d in range(2):
                for c in range(2):
                    qg_c = qg_ref[d, rj, c * LANES:(c + 1) * LANES]
                    lhs = jnp.concatenate([jnp.where(lo64, qg_c, zero_b),
                                           jnp.where(lo64, zero_b, qg_c)], axis=0)
                    t = _dot_nt(lhs, sb16_ref[d, c, s * CH_PER + j])
                    for hh in range(2):
                        piece = t[hh * CHUNK:(hh + 1) * CHUNK]
                        hd = 2 * c + hh
                        inter[hd][j] = piece if inter[hd][j] is None else inter[hd][j] + piece
        for hd in range(GLA_HEADS):
            cl = slice(hd * GLA_DV, (hd + 1) * GLA_DV)
            o = oi_ref[rows, cl] + jnp.concatenate(inter[hd], axis=0)
            y = o * lax.rsqrt(jnp.mean(o * o, axis=-1, keepdims=True) + EPS) * gn_ref[:, cl]
            g = gg_ref[rows, cl].astype(F32)
            o_ref[rows, cl] = (y * (g * jax.nn.sigmoid(g))).astype(BF16)
        return carry

    lax.fori_loop(0, nsuper, phase3, 0)


def _gla(gq, gk, gv, gg, lr, wd, bdec, gn, s_f, s_b):
    B, L, _ = gq.shape
    nchunk = L // CHUNK
    full = lambda shape: pl.BlockSpec(shape, lambda b: (0,) * len(shape))
    tok = lambda w: pl.BlockSpec((None, L, w), lambda b: (b, 0, 0))
    st_spec = pl.BlockSpec((None, 2, LANES, GLA_DV), lambda b: (b, 0, 0, 0))
    return pl.pallas_call(
        _gla_kernel,
        grid=(B,),
        in_specs=[tok(GLA_QK_WIDTH), tok(GLA_QK_WIDTH), tok(GLA_WIDTH), tok(GLA_WIDTH), tok(LANES),
                  full(wd.shape), full(bdec.shape), full(gn.shape), st_spec, st_spec],
        out_specs=tok(GLA_WIDTH),
        out_shape=jax.ShapeDtypeStruct((B, L, GLA_WIDTH), BF16),
        scratch_shapes=[pltpu.VMEM((L, 2 * GLA_QK_WIDTH), F32),
                        pltpu.VMEM((L, GLA_WIDTH), F32),
                        pltpu.VMEM((2, L, GLA_QK_WIDTH), BF16),
                        pltpu.VMEM((2, 2, nchunk, GLA_DV, LANES), F32),
                        pltpu.VMEM((2, 2, nchunk, GLA_DV, LANES), BF16),
                        pltpu.VMEM((2, L // SUPER, 2 * CH_PER, GLA_QK_WIDTH), F32),
                        pltpu.VMEM((4, GLA_DV, LANES), F32)],
        compiler_params=_cparams(("arbitrary",)),
        name="gla_bidirectional",
    )(gq, gk, gv, gg, lr, wd, bdec, gn, s_f, s_b)


def _outproj_kernel(attn_ref, gla_ref, x_ref, mod_ref, w_ref, g2_ref, wr_ref,
                    x1_ref, h2_ref, afft_ref):
    y = _dot(attn_ref[...], w_ref[0:ATTN_WIDTH, :]) + _dot(gla_ref[...], w_ref[ATTN_WIDTH:, :])
    x1 = x_ref[...] + mod_ref[2:3, :] * y
    x1_ref[...] = x1
    h2 = _rms_mod(x1, g2_ref[...], mod_ref[3:4, :], mod_ref[4:5, :]).astype(BF16)
    h2_ref[...] = h2
    logits = _dot(h2, wr_ref[...])
    live = lax.broadcasted_iota(jnp.int32, logits.shape, 1) < N_EXPERTS
    logits = jnp.where(live, logits, NEG_INF)
    e = jnp.exp(logits - jnp.max(logits, axis=-1, keepdims=True))
    aff = e / jnp.sum(e, axis=-1, keepdims=True)
    afft_ref[...] = aff.T[0:N_EXPERTS, :]


def _output_projection(attn, gla, x, mod3, w_out, g2, w_router, tm):
    B, L, _ = x.shape
    full = lambda shape: pl.BlockSpec(shape, lambda b, i: (0,) * len(shape))
    tok = lambda w: pl.BlockSpec((None, tm, w), lambda b, i: (b, i, 0))
    return pl.pallas_call(
        _outproj_kernel,
        grid=(B, L // tm),
        in_specs=[tok(ATTN_WIDTH), tok(GLA_WIDTH), tok(D_MODEL),
                  pl.BlockSpec((None, 6, D_MODEL), lambda b, i: (b, 0, 0)),
                  full(w_out.shape), full(g2.shape), full(w_router.shape)],
        out_specs=[tok(D_MODEL), tok(D_MODEL),
                   pl.BlockSpec((None, N_EXPERTS, tm), lambda b, i: (b, 0, i))],
        out_shape=[jax.ShapeDtypeStruct((B, L, D_MODEL), F32),
                   jax.ShapeDtypeStruct((B, L, D_MODEL), BF16),
                   jax.ShapeDtypeStruct((B, N_EXPERTS, L), F32)],
        compiler_params=_cparams(("arbitrary", "arbitrary")),
        name="output_projection_router",
    )(attn, gla, x, mod3, w_out, g2, w_router)


def _topk_kernel(afft_ref, post_ref, pos_ref, gsel_ref, *, cap):
    aff = afft_ref[...]
    E, L = aff.shape
    bits = pltpu.bitcast(aff, jnp.int32)

    def search(i, thr):
        cand = thr | jnp.left_shift(jnp.int32(1), 30 - i)
        cnt = jnp.sum(jnp.where(bits >= cand, 1.0, 0.0), axis=-1, keepdims=True)
        return jnp.where(cnt >= cap, cand, thr)

    thr = lax.fori_loop(0, 31, search, jnp.zeros((E, 1), jnp.int32))
    above = bits > thr
    tie = bits == thr
    need = cap - jnp.sum(jnp.where(above, 1.0, 0.0), axis=-1, keepdims=True)

    upper = (lax.broadcasted_iota(jnp.int32, (LANES, LANES), 0)
             <= lax.broadcasted_iota(jnp.int32, (LANES, LANES), 1)).astype(BF16)

    def prefix(mask):
        parts = []
        run = jnp.zeros((E, 1), F32)
        for j in range(L // LANES):
            blk = jnp.where(mask[:, j * LANES:(j + 1) * LANES], 1.0, 0.0).astype(BF16)
            loc = _dot(blk, upper) + run
            parts.append(loc)
            run = loc[:, LANES - 1:LANES]
        return jnp.concatenate(parts, axis=1)

    tie_rank = prefix(tie)
    sel = above | (tie & (tie_rank <= need))
    slot = prefix(sel).astype(jnp.int32) - 1
    post = jnp.where(sel, slot, -1)
    post_ref[...] = post
    gsel = jnp.where(sel, aff, 0.0)
    pad_i = jnp.full((LANES - E, L), -1, jnp.int32)
    pad_f = jnp.zeros((LANES - E, L), F32)
    pos_ref[...] = jnp.concatenate([post, pad_i], axis=0).T
    gsel_ref[...] = jnp.concatenate([gsel, pad_f], axis=0).T


def _expert_choice(afft, cap):
    B, E, L = afft.shape
    return pl.pallas_call(
        functools.partial(_topk_kernel, cap=cap),
        grid=(B,),
        in_specs=[pl.BlockSpec((None, E, L), lambda b: (b, 0, 0))],
        out_specs=[pl.BlockSpec((None, E, L), lambda b: (b, 0, 0)),
                   pl.BlockSpec((None, L, LANES), lambda b: (b, 0, 0)),
                   pl.BlockSpec((None, L, LANES), lambda b: (b, 0, 0))],
        out_shape=[jax.ShapeDtypeStruct((B, E, L), jnp.int32),
                   jax.ShapeDtypeStruct((B, L, LANES), jnp.int32),
                   jax.ShapeDtypeStruct((B, L, LANES), F32)],
        compiler_params=_cparams(("arbitrary",)),
        name="expert_choice_topk",
    )(afft)


def _dispatch_kernel(post_ref, h2_ref, xs_ref, *, cap):
    L = h2_ref.shape[0]
    slot = lax.broadcasted_iota(jnp.int32, (cap, L), 0)
    onehot = jnp.where(post_ref[...] == slot, 1.0, 0.0).astype(BF16)
    xs_ref[...] = _dot(onehot, h2_ref[...]).astype(BF16)


def _dispatch(post4, h2, cap):
    B, E, _, L = post4.shape
    return pl.pallas_call(
        functools.partial(_dispatch_kernel, cap=cap),
        grid=(B, E),
        in_specs=[pl.BlockSpec((None, None, 1, L), lambda b, e: (b, e, 0, 0)),
                  pl.BlockSpec((None, L, D_MODEL), lambda b, e: (b, 0, 0))],
        out_specs=pl.BlockSpec((None, None, cap, D_MODEL), lambda b, e: (b, e, 0, 0)),
        out_shape=jax.ShapeDtypeStruct((B, E, cap, D_MODEL), BF16),
        compiler_params=_cparams(("arbitrary", "arbitrary")),
        name="moe_dispatch",
    )(post4, h2)


def _ffn_kernel(xs_ref, wg_ref, wu_ref, wd_ref, y_ref):
    nbatch, cap, d = xs_ref.shape
    xs = xs_ref[...].reshape(nbatch * cap, d)
    f = wg_ref.shape[1]
    half = f // 2
    acc = None
    for j in range(2):
        cols = slice(j * half, (j + 1) * half)
        g = _dot(xs, wg_ref[:, cols])
        u = _dot(xs, wu_ref[:, cols])
        hid = (g * jax.nn.sigmoid(g) * u).astype(BF16)
        part = _dot(hid, wd_ref[cols, :])
        acc = part if acc is None else acc + part
    y_ref[...] = acc.astype(BF16).reshape(nbatch, cap, d)


def _expert_ffn(xs, w_gate, w_up, w_down, nbatch):
    B, E, cap, d = xs.shape
    f = w_gate.shape[2]
    tok = pl.BlockSpec((nbatch, None, cap, d), lambda e, b: (b, e, 0, 0))
    return pl.pallas_call(
        _ffn_kernel,
        grid=(E, B // nbatch),
        in_specs=[tok,
                  pl.BlockSpec((None, d, f), lambda e, b: (e, 0, 0)),
                  pl.BlockSpec((None, d, f), lambda e, b: (e, 0, 0)),
                  pl.BlockSpec((None, f, d), lambda e, b: (e, 0, 0))],
        out_specs=tok,
        out_shape=jax.ShapeDtypeStruct((B, E, cap, d), BF16),
        compiler_params=_cparams(("arbitrary", "arbitrary")),
        name="expert_swiglu",
    )(xs, w_gate, w_up, w_down)


def _combine_kernel(pos_ref, gsel_ref, y_ref, x1_ref, mod_ref, o_ref, acc_ref, *, cap):
    tt = pos_ref.shape[0]
    slot = lax.broadcasted_iota(jnp.int32, (tt, cap), 1)
    for e in range(N_EXPERTS):
        onehot = jnp.where(pos_ref[:, e:e + 1] == slot, 1.0, 0.0).astype(BF16)
        part = gsel_ref[:, e:e + 1] * _dot(onehot, y_ref[e])
        if e == 0:
            acc_ref[...] = part
        else:
            acc_ref[...] += part
    o_ref[...] = x1_ref[...] + mod_ref[5:6, :] * acc_ref[...]


def _combine(pos, gsel, y, x1, mod3, cap, tt):
    B, L, _ = x1.shape
    tok = lambda w: pl.BlockSpec((None, tt, w), lambda b, i: (b, i, 0))
    return pl.pallas_call(
        functools.partial(_combine_kernel, cap=cap),
        grid=(B, L // tt),
        in_specs=[tok(LANES), tok(LANES),
                  pl.BlockSpec((None, N_EXPERTS, cap, D_MODEL), lambda b, i: (b, 0, 0, 0)),
                  tok(D_MODEL),
                  pl.BlockSpec((None, 6, D_MODEL), lambda b, i: (b, 0, 0))],
        out_specs=tok(D_MODEL),
        out_shape=jax.ShapeDtypeStruct((B, L, D_MODEL), F32),
        scratch_shapes=[pltpu.VMEM((tt, D_MODEL), F32)],
        compiler_params=_cparams(("arbitrary", "arbitrary")),
        name="moe_combine",
    )(pos, gsel, y, x1, mod3)


def _rope_tables(L):
    inv = ROPE_BASE ** (-jnp.arange(ROPE_FREQS, dtype=F32) / ROPE_FREQS)
    pos = jnp.arange(L)
    row = (pos // GRID_W).astype(F32)[:, None] * inv
    col = (pos % GRID_W).astype(F32)[:, None] * inv
    cos = jnp.concatenate([jnp.cos(row), jnp.cos(row), jnp.cos(col), jnp.cos(col)], axis=1)
    sin = jnp.concatenate([-jnp.sin(row), jnp.sin(row), -jnp.sin(col), jnp.sin(col)], axis=1)
    return jnp.tile(cos, (1, 2)), jnp.tile(sin, (1, 2))


def _head_mean_matrix(n):
    idx = np.arange(n) // HEAD_DIM
    return jnp.asarray((idx[:, None] == idx[None, :]).astype(np.float32) / HEAD_DIM, dtype=BF16)


def kernel(x, c, ctx, c_ctx, w_mod, b_mod, norm1_g, w_in, q_norm_g, k_norm_g, attn_sink,
           w_decay_fwd, b_decay_fwd, w_decay_bwd, b_decay_bwd, gla_norm_g, w_out, norm2_g,
           w_router, w_e_gate, w_e_up, w_e_down):
    B, L, D = x.shape
    cap = CAPACITY_FACTOR * L // N_EXPERTS
    layer = 0

    rows = ((B + 1 + 7) // 8) * 8
    cc = jnp.concatenate([c, c_ctx[None, :], jnp.zeros((rows - B - 1, D), F32)], axis=0)
    mod_all = _modulation(cc, w_mod[layer], b_mod[layer])
    mod3 = mod_all[:B].reshape(B, 6, D)
    modc = mod_all[B].reshape(6, D)

    w = w_in[layer]
    o = np.cumsum([0, ATTN_WIDTH, KV_WIDTH, KV_WIDTH, GLA_QK_WIDTH, GLA_QK_WIDTH,
                   GLA_WIDTH, GLA_WIDTH, GATE_RANK, GATE_RANK])
    w_lr = jnp.concatenate([w[:, o[7]:o[9]], jnp.zeros((D, LANES - 2 * GATE_RANK), F32)], axis=1)
    w_in_r = jnp.concatenate([w[:, :o[7]], w_lr], axis=1).astype(BF16)
    w_ctx = jnp.concatenate([w[:, o[1]:o[3]], w[:, o[4]:o[6]], w_lr], axis=1).astype(BF16)
    wd = jnp.zeros((LANES, 2 * GLA_QK_WIDTH), F32)
    wd = wd.at[0:GATE_RANK, 0:GLA_QK_WIDTH].set(w_decay_fwd[layer])
    wd = wd.at[GATE_RANK:2 * GATE_RANK, GLA_QK_WIDTH:].set(w_decay_bwd[layer])
    bdec = jnp.concatenate([b_decay_fwd[layer], b_decay_bwd[layer]])[None, :]
    g1 = norm1_g[layer][None, :]
    g2 = norm2_g[layer][None, :]
    qg = jnp.tile(q_norm_g[layer], N_Q_HEADS)[None, :]
    kg2 = jnp.tile(k_norm_g[layer], N_KV_HEADS)[None, :]
    gn = jnp.tile(gla_norm_g[layer], GLA_HEADS)[None, :]
    bd512 = _head_mean_matrix(ATTN_WIDTH)
    bd128 = _head_mean_matrix(KV_WIDTH)
    cos_t, sin_t = _rope_tables(L)
    w_router_p = jnp.concatenate([w_router[layer], jnp.zeros((D, LANES - N_EXPERTS), F32)],
                                 axis=1).astype(BF16)

    kc, kcs, vc, vcs, s_f, s_b = _context_side(ctx, modc, g1, w_ctx, kg2, bd128, wd, bdec)
    q, k, ks, v, vs, gq, gk, gv, gg, lr = _input_projection(
        x, mod3, g1, w_in_r, qg, kg2, bd512, cos_t, sin_t, tm=512)
    attn = _window_attention(attn_sink[layer], q, k, ks, v, vs, kc, kcs, vc, vcs, tq=512)
    gla = _gla(gq, gk, gv, gg, lr, wd, bdec, gn, s_f, s_b)
    x1, h2, afft = _output_projection(attn, gla, x, mod3, w_out[layer].astype(BF16), g2,
                                      w_router_p, tm=512)
    post, pos, gsel = _expert_choice(afft, cap)
    xs = _dispatch(post.reshape(B, N_EXPERTS, 1, L), h2, cap)
    y = _expert_ffn(xs, w_e_gate[layer].astype(BF16), w_e_up[layer].astype(BF16),
                    w_e_down[layer].astype(BF16), nbatch=4)
    return _combine(pos, gsel, y, x1, mod3, cap, tt=512)
```

```python
import functools

import jax
import jax.numpy as jnp
import numpy as np
from jax import lax
from jax.experimental import pallas as pl
from jax.experimental.pallas import tpu as pltpu

D_MODEL = 1024
GRID_W = 64
HEAD_DIM = 64
N_Q_HEADS = 8
N_KV_HEADS = 2
BLOCK = 128
ROPE_FREQS = 16
ROPE_BASE = 10000.0
GLA_HEADS = 4
GLA_DV = 128
GLA_DK = 64
GATE_RANK = 16
GATE_NORMALIZER = 16.0
CHUNK = 64
N_EXPERTS = 16
CAPACITY_FACTOR = 2
ATTN_WIDTH = 512
KV_WIDTH = 128
GLA_QK_WIDTH = 256
GLA_WIDTH = 512
EPS = 1e-6
NEG_INF = -1e30
LOG2E = 1.4426950408889634

LANES = 128
VMEM_LIMIT = 56 * 1024 * 1024

F32 = jnp.float32
BF16 = jnp.bfloat16
HI = lax.Precision.HIGHEST


def _cparams(sem):
    return pltpu.CompilerParams(dimension_semantics=sem, vmem_limit_bytes=VMEM_LIMIT)


def _dot(a, b):
    return jnp.dot(a, b, preferred_element_type=F32)


def _dot_hi(a, b):
    return jnp.dot(a, b, preferred_element_type=F32, precision=HI)


def _dot_nt(a, b):
    return lax.dot_general(a, b, (((1,), (1,)), ((), ())), preferred_element_type=F32)


def _dot_tn(a, b, precision=None):
    return lax.dot_general(a, b, (((0,), (0,)), ((), ())), preferred_element_type=F32,
                           precision=precision)


def _split2(t):
    hi = t.astype(BF16)
    lo = (t - hi.astype(F32)).astype(BF16)
    return hi, lo


def _rms_mod(t, g, shift, scale):
    y = t * lax.rsqrt(jnp.mean(t * t, axis=-1, keepdims=True) + EPS)
    return (y * g) * (1.0 + scale) + shift


def _lane_lo(shape):
    return (lax.broadcasted_iota(jnp.int32, shape, len(shape) - 1) % LANES) < HEAD_DIM


def _mod_kernel(c_ref, w_ref, b_ref, o_ref):
    c = c_ref[...]
    s = c * jax.nn.sigmoid(c)
    o_ref[...] = _dot_hi(s, w_ref[...]) + b_ref[...]


def _modulation(cc, w_mod, b_mod):
    m = cc.shape[0]
    n = w_mod.shape[1]
    tn = 1024
    return pl.pallas_call(
        _mod_kernel,
        grid=(n // tn,),
        in_specs=[pl.BlockSpec((m, D_MODEL), lambda j: (0, 0)),
                  pl.BlockSpec((D_MODEL, tn), lambda j: (0, j)),
                  pl.BlockSpec((1, tn), lambda j: (0, j))],
        out_specs=pl.BlockSpec((m, tn), lambda j: (0, j)),
        out_shape=jax.ShapeDtypeStruct((m, n), F32),
        compiler_params=_cparams(("arbitrary",)),
        name="adaln_mod",
    )(cc, w_mod, b_mod.reshape(1, n))


def _ctx_kernel(ctx_ref, mod_ref, g1_ref, w_ref, kg_ref, bd_ref, wd_ref, bdec_ref,
                kc_ref, vc_ref, sf_ref, sb_ref):
    n = ctx_ref.shape[0]
    h = _rms_mod(ctx_ref[...], g1_ref[...], mod_ref[0:1, :], mod_ref[1:2, :]).astype(BF16)
    pc = _dot(h, w_ref[...])
    ak = pc[:, 0:128]
    av = pc[:, 128:256]
    gk = pc[:, 256:512]
    gv = pc[:, 512:1024].astype(BF16)
    lr = pc[:, 1024:1152]
    sq_hi, sq_lo = _split2(ak * ak)
    ms = _dot(sq_hi, bd_ref[...]) + _dot(sq_lo, bd_ref[...])
    kn = ak * lax.rsqrt(ms + EPS) * kg_ref[...]
    kc_ref[...] = kn.astype(BF16)
    vc_ref[...] = av.astype(BF16)
    z = _dot_hi(lr, wd_ref[...]) + bdec_ref[...]
    la = (jnp.minimum(z, 0.0) - jnp.log(1.0 + jnp.exp(-jnp.abs(z)))) * (1.0 / GATE_NORMALIZER)
    r = lax.broadcasted_iota(jnp.int32, (n, n), 0)
    cidx = lax.broadcasted_iota(jnp.int32, (n, n), 1)
    after = (cidx > r).astype(F32)
    before = (cidx < r).astype(F32)
    w_f = jnp.exp(_dot_hi(after, la[:, 0:256]))
    w_b = jnp.exp(_dot_hi(before, la[:, 256:512]))
    lo = _lane_lo((n, LANES))
    for w, out in ((w_f, sf_ref), (w_b, sb_ref)):
        kw = gk * w
        for c in range(2):
            kwc = kw[:, c * LANES:(c + 1) * LANES]
            k_lo = jnp.where(lo, kwc, 0.0).astype(BF16)
            k_hi = jnp.where(lo, 0.0, kwc).astype(BF16)
            v0 = gv[:, (2 * c) * GLA_DV:(2 * c + 1) * GLA_DV]
            v1 = gv[:, (2 * c + 1) * GLA_DV:(2 * c + 2) * GLA_DV]
            out[c] = _dot_tn(v0, k_lo) + _dot_tn(v1, k_hi)


def _context_side(ctx, modc, g1, w_ctx, kg2, bd128, wd, bdec):
    B, n, _ = ctx.shape
    full = lambda shape: pl.BlockSpec(shape, lambda b: (0,) * len(shape))
    kv_spec = pl.BlockSpec((None, n, KV_WIDTH), lambda b: (b, 0, 0))
    st_spec = pl.BlockSpec((None, 2, LANES, GLA_DV), lambda b: (b, 0, 0, 0))
    kv_shape = jax.ShapeDtypeStruct((B, n, KV_WIDTH), BF16)
    st_shape = jax.ShapeDtypeStruct((B, 2, LANES, GLA_DV), F32)
    return pl.pallas_call(
        _ctx_kernel,
        grid=(B,),
        in_specs=[pl.BlockSpec((None, n, D_MODEL), lambda b: (b, 0, 0)),
                  full(modc.shape), full(g1.shape), full(w_ctx.shape), full(kg2.shape),
                  full(bd128.shape), full(wd.shape), full(bdec.shape)],
        out_specs=[kv_spec, kv_spec, st_spec, st_spec],
        out_shape=[kv_shape, kv_shape, st_shape, st_shape],
        compiler_params=_cparams(("arbitrary",)),
        name="context_side",
    )(ctx, modc, g1, w_ctx, kg2, bd128, wd, bdec)


def _swap16(t):
    n = t.shape[1]
    first = (lax.broadcasted_iota(jnp.int32, t.shape, 1) % 32) < ROPE_FREQS
    return jnp.where(first, pltpu.roll(t, n - ROPE_FREQS, 1), pltpu.roll(t, ROPE_FREQS, 1))


def _inproj_kernel(x_ref, mod_ref, g1_ref, w_ref, qg_ref, kg_ref, bd_ref, cos_ref, sin_ref,
                   q_ref, k_ref, v_ref, gq_ref, gk_ref, gv_ref, gg_ref, lr_ref):
    h = _rms_mod(x_ref[...], g1_ref[...], mod_ref[0:1, :], mod_ref[1:2, :]).astype(BF16)
    cos = cos_ref[...]
    sin = sin_ref[...]

    def head_norm_rope(t, g, bd, reps):
        ms = _dot((t * t).astype(BF16), bd)
        tn = t * lax.rsqrt(ms + EPS) * g
        c = jnp.concatenate([cos] * reps, axis=1) if reps > 1 else cos
        s = jnp.concatenate([sin] * reps, axis=1) if reps > 1 else sin
        return tn * c + _swap16(tn) * s

    aq = _dot(h, w_ref[:, 0:512])
    q = head_norm_rope(aq, qg_ref[...], bd_ref[...], 4) * (HEAD_DIM ** -0.5 * LOG2E)
    q_ref[...] = q.astype(BF16)
    akv = _dot(h, w_ref[:, 512:768])
    k = head_norm_rope(akv[:, 0:128], kg_ref[...], bd_ref[0:128, 0:128], 1)
    k_ref[...] = k.astype(BF16)
    v_ref[...] = akv[:, 128:256].astype(BF16)
    gqk = _dot(h, w_ref[:, 768:1280])
    gq_ref[...] = (gqk[:, 0:256] * (GLA_DK ** -0.5)).astype(BF16)
    gk_ref[...] = gqk[:, 256:512].astype(BF16)
    gv_ref[...] = _dot(h, w_ref[:, 1280:1792]).astype(BF16)
    gg_ref[...] = _dot(h, w_ref[:, 1792:2304]).astype(BF16)
    lr_ref[...] = _dot(h, w_ref[:, 2304:2432])


def _input_projection(x, mod3, g1, w_in_r, qg, kg2, bd512, cos_t, sin_t, tm):
    B, L, _ = x.shape
    full = lambda shape: pl.BlockSpec(shape, lambda b, i: (0,) * len(shape))
    tok = lambda w: pl.BlockSpec((None, tm, w), lambda b, i: (b, i, 0))
    widths = (ATTN_WIDTH, KV_WIDTH, KV_WIDTH, GLA_QK_WIDTH, GLA_QK_WIDTH, GLA_WIDTH, GLA_WIDTH, LANES)
    dtypes = (BF16,) * 7 + (F32,)
    return pl.pallas_call(
        _inproj_kernel,
        grid=(B, L // tm),
        in_specs=[tok(D_MODEL),
                  pl.BlockSpec((None, 6, D_MODEL), lambda b, i: (b, 0, 0)),
                  full(g1.shape), full(w_in_r.shape), full(qg.shape), full(kg2.shape),
                  full(bd512.shape),
                  pl.BlockSpec((tm, LANES), lambda b, i: (i, 0)),
                  pl.BlockSpec((tm, LANES), lambda b, i: (i, 0))],
        out_specs=[tok(w) for w in widths],
        out_shape=[jax.ShapeDtypeStruct((B, L, w), dt) for w, dt in zip(widths, dtypes)],
        compiler_params=_cparams(("arbitrary", "arbitrary")),
        name="input_projection",
    )(x, mod3, g1, w_in_r, qg, kg2, bd512, cos_t, sin_t)


def _attn_kernel(sink_ref, q_ref, kp_ref, ko_ref, kn_ref, vp_ref, vo_ref, vn_ref,
                 kc_ref, vc_ref, o_ref):
    i = pl.program_id(1)
    ni = pl.num_programs(1)
    nsub = q_ref.shape[0] // BLOCK
    ncol = ATTN_WIDTH // LANES
    win = 3 * BLOCK
    ucol = 4
    half_rows = ucol * BLOCK
    k_win = jnp.concatenate([kp_ref[...], ko_ref[...], kn_ref[...]], axis=0)
    v_win = jnp.concatenate([vp_ref[...], vo_ref[...], vn_ref[...]], axis=0)
    k_ctx = kc_ref[...]
    lo_w = _lane_lo(v_win.shape)
    lo_c = _lane_lo(vc_ref.shape)
    lo_q = _lane_lo((BLOCK, LANES))
    zero = jnp.zeros((), BF16)
    one = jnp.ones((), BF16)
    v0_c, v0_w = jnp.where(lo_c, vc_ref[...], one), jnp.where(lo_w, v_win, one)
    v1_c, v1_w = jnp.where(lo_c, one, vc_ref[...]), jnp.where(lo_w, one, v_win)
    qi = lax.broadcasted_iota(jnp.int32, (half_rows, BLOCK), 0) % BLOCK
    kj = lax.broadcasted_iota(jnp.int32, (half_rows, BLOCK), 1)
    no_prev = jnp.where(i > 0, 0, BLOCK)
    no_next = jnp.where(i < ni - 1, 0, BLOCK)
    row_head = lax.broadcasted_iota(jnp.int32, (half_rows, 1), 0) // BLOCK
    lo_o = _lane_lo((half_rows, LANES))
    for t in range(nsub):
        rows = slice(t * BLOCK, (t + 1) * BLOCK)
        keys = slice(t * BLOCK, t * BLOCK + win)
        cols = [q_ref[rows, c * LANES:(c + 1) * LANES] for c in range(ncol)]
        prev_ok = kj >= qi + (no_prev if t == 0 else 0)
        next_ok = kj <= qi - (no_next if t == nsub - 1 else 0)
        for p in range(ncol // ucol):
            outs = []
            for g, (vv_c, vv_w) in enumerate(((v0_c, v0_w), (v1_c, v1_w))):
                qs = jnp.concatenate([jnp.where(lo_q, qc, zero) if g == 0 else jnp.where(lo_q, zero, qc)
                                      for qc in cols[ucol * p:ucol * (p + 1)]], axis=0)
                head = g * ncol + ucol * p
                sink_g = jnp.full((half_rows, 1), sink_ref[head + ucol - 1], F32)
                for j in range(ucol - 2, -1, -1):
                    sink_g = jnp.where(row_head <= j, sink_ref[head + j], sink_g)
                sink_g = sink_g * LOG2E
                s_c = _dot_nt(qs, k_ctx)
                s_w = _dot_nt(qs, k_win[keys])
                s_p = jnp.where(prev_ok, s_w[:, 0:BLOCK], NEG_INF)
                s_o = s_w[:, BLOCK:2 * BLOCK]
                s_n = jnp.where(next_ok, s_w[:, 2 * BLOCK:win], NEG_INF)
                m = jnp.maximum(jnp.maximum(jnp.max(s_c, axis=-1, keepdims=True),
                                            jnp.max(jnp.maximum(jnp.maximum(s_p, s_o), s_n),
                                                    axis=-1, keepdims=True)), sink_g)
                e_c = jnp.exp2(s_c - m).astype(BF16)
                e_w = jnp.concatenate([jnp.exp2(s_p - m), jnp.exp2(s_o - m), jnp.exp2(s_n - m)],
                                      axis=1).astype(BF16)
                acc = _dot(e_c, vv_c) + _dot(e_w, vv_w[keys])
                outs.append(acc / (pltpu.roll(acc, HEAD_DIM, 1) + jnp.exp2(sink_g - m)))
            o = jnp.where(lo_o, outs[0], outs[1]).astype(BF16)
            for j in range(ucol):
                c = ucol * p + j
                o_ref[rows, c * LANES:(c + 1) * LANES] = o[j * BLOCK:(j + 1) * BLOCK]


def _window_attention(sink, q, k, v, kc, vc, tq):
    B, L, _ = q.shape
    nb = L // BLOCK
    nsub = tq // BLOCK
    n_ctx = kc.shape[1]
    prev = pl.BlockSpec((None, BLOCK, KV_WIDTH), lambda b, n: (b, jnp.maximum(n * nsub - 1, 0), 0))
    own = pl.BlockSpec((None, tq, KV_WIDTH), lambda b, n: (b, n, 0))
    nxt = pl.BlockSpec((None, BLOCK, KV_WIDTH),
                       lambda b, n: (b, jnp.minimum((n + 1) * nsub, nb - 1), 0))
    cspec = pl.BlockSpec((None, n_ctx, KV_WIDTH), lambda b, n: (b, 0, 0))
    return pl.pallas_call(
        _attn_kernel,
        grid=(B, L // tq),
        in_specs=[pl.BlockSpec(memory_space=pltpu.SMEM),
                  pl.BlockSpec((None, tq, ATTN_WIDTH), lambda b, n: (b, n, 0)),
                  prev, own, nxt, prev, own, nxt, cspec, cspec],
        out_specs=pl.BlockSpec((None, tq, ATTN_WIDTH), lambda b, n: (b, n, 0)),
        out_shape=jax.ShapeDtypeStruct((B, L, ATTN_WIDTH), BF16),
        compiler_params=_cparams(("arbitrary", "arbitrary")),
        name="window_attention",
    )(sink, q, k, k, k, v, v, v, kc, vc)


SUPER = 256
CH_PER = SUPER // CHUNK


def _dot_split(m, parts):
    return _dot(m, parts[0]) + _dot(m, parts[1])


def _gla_kernel(gq_ref, gk_ref, gv_ref, gg_ref, lr_ref, wd_ref, bdec_ref, gn_ref, sf_ref, sb_ref,
                o_ref, la_ref, oi_ref, qg_ref, kv_ref, sb16_ref, dec_ref, st_ref):
    L = gq_ref.shape[0]
    nsuper = L // SUPER
    nchunk = L // CHUNK
    half = CHUNK // 2
    lr_hi, lr_lo = _split2(lr_ref[...])
    wd_hi, wd_lo = _split2(wd_ref[...])
    z = _dot(lr_hi, wd_hi) + _dot(lr_lo, wd_hi) + _dot(lr_hi, wd_lo) + bdec_ref[...]
    la_ref[...] = (jnp.minimum(z, 0.0) - jnp.log(1.0 + jnp.exp(-jnp.abs(z)))) * (1.0 / GATE_NORMALIZER)

    r = lax.broadcasted_iota(jnp.int32, (SUPER, SUPER), 0)
    cidx = lax.broadcasted_iota(jnp.int32, (SUPER, SUPER), 1)
    same = (r // CHUNK) == (cidx // CHUNK)
    pr = r % CHUNK
    pc = cidx % CHUNK
    one = jnp.float32(1.0)
    zero = jnp.float32(0.0)
    in_f = jnp.where(pc <= pr, one, zero)
    in_b = jnp.where(pc >= pr, one, zero)
    ref_f = jnp.where(pc < half, one, zero)
    ref_b = jnp.where(pc >= half, one, zero)
    m1_f = jnp.where(same, in_f - ref_f, zero).astype(BF16)
    m1_b = jnp.where(same, in_b - ref_b, zero).astype(BF16)
    mask_f = jnp.where(same, in_f, zero) > 0.5
    mask_b = jnp.where(same, in_b, zero) > 0.5
    rr = lax.broadcasted_iota(jnp.int32, (2 * CH_PER, SUPER), 0)
    rc = lax.broadcasted_iota(jnp.int32, (2 * CH_PER, SUPER), 1)
    in_chunk = jnp.where((rc // CHUNK) == (rr % CH_PER), one, zero)
    first = jnp.where((rc % CHUNK) < half, 1, 0)
    is_tot = jnp.where(rr >= CH_PER, 1, 0)
    rs_f = (in_chunk * jnp.where(first != is_tot, one, zero)).astype(BF16)
    rs_b = (in_chunk * jnp.where(first == is_tot, one, zero)).astype(BF16)
    hi_lane = jnp.where(_lane_lo((SUPER, LANES)), 0, 1)

    def phase1(s, carry):
        rows = pl.ds(pl.multiple_of(s * SUPER, SUPER), SUPER)
        q = gq_ref[rows, :].astype(F32)
        k = gk_ref[rows, :].astype(F32)
        acc = [None] * GLA_HEADS
        for d, (m1, rs, mask) in enumerate(((m1_f, rs_f, mask_f), (m1_b, rs_b, mask_b))):
            parts = _split2(la_ref[rows, d * GLA_QK_WIDTH:(d + 1) * GLA_QK_WIDTH])
            x1 = _dot_split(m1, parts)
            erow = jnp.exp(_dot_split(rs, parts))
            dec = erow[0:CH_PER] * erow[CH_PER:2 * CH_PER]
            dec_ref[d, s] = jnp.concatenate([dec, dec], axis=0)
            qe = q * jnp.exp(x1)
            ke = k * jnp.exp(-x1)
            qgs, kds = [], []
            for j in range(CH_PER):
                rj = slice(j * CHUNK, (j + 1) * CHUNK)
                qgs.append(qe[rj] * erow[j:j + 1])
                kds.append(ke[rj] * erow[CH_PER + j:CH_PER + j + 1])
            qg_ref[d, rows, :] = jnp.concatenate(qgs, axis=0).astype(BF16)
            kd = jnp.concatenate(kds, axis=0)
            for c in range(2):
                cl = slice(c * LANES, (c + 1) * LANES)
                qe_c = qe[:, cl].astype(BF16)
                ke_c = ke[:, cl]
                kd_c = kd[:, cl]
                kvt = [None] * CH_PER
                for hh in range(2):
                    hd = 2 * c + hh
                    v = gv_ref[rows, hd * GLA_DV:(hd + 1) * GLA_DV]
                    ke_m = jnp.where(hi_lane == hh, ke_c, zero).astype(BF16)
                    a = jnp.where(mask, _dot_nt(qe_c, ke_m), zero).astype(BF16)
                    oi = _dot(a, v)
                    acc[hd] = oi if acc[hd] is None else acc[hd] + oi
                    kd_m = jnp.where(hi_lane == hh, kd_c, zero).astype(BF16)
                    for j in range(CH_PER):
                        rj = slice(j * CHUNK, (j + 1) * CHUNK)
                        t = _dot_tn(v[rj], kd_m[rj])
                        kvt[j] = t if kvt[j] is None else kvt[j] + t
                for j in range(CH_PER):
                    kv_ref[d, c, s * CH_PER + j] = kvt[j]
        for hd in range(GLA_HEADS):
            oi_ref[rows, hd * GLA_DV:(hd + 1) * GLA_DV] = acc[hd]
        return carry

    lax.fori_loop(0, nsuper, phase1, 0)

    st_ref[0] = sf_ref[0]
    st_ref[1] = sf_ref[1]
    st_ref[2] = sb_ref[0]
    st_ref[3] = sb_ref[1]

    def phase2(n, carry):
        for d in range(2):
            idx = n if d == 0 else nchunk - 1 - n
            dec = dec_ref[d, idx // CH_PER, pl.ds(idx % CH_PER, 1), :]
            for c in range(2):
                st = st_ref[2 * d + c]
                sb16_ref[d, c, idx] = st.astype(BF16)
                st_ref[2 * d + c] = dec[:, c * LANES:(c + 1) * LANES] * st + kv_ref[d, c, idx]
        return carry

    lax.fori_loop(0, nchunk, phase2, 0)

    lo64 = _lane_lo((CHUNK, LANES))
    zero_b = jnp.zeros((), BF16)

    def phase3(s, carry):
        r0 = pl.multiple_of(s * SUPER, SUPER)
        rows = pl.ds(r0, SUPER)
        inter = [[None] * CH_PER for _ in range(GLA_HEADS)]
        for j in range(CH_PER):
            rj = pl.ds(r0 + j * CHUNK, CHUNK)
            for d in range(2):
                for c in range(2):
                    qg_c = qg_ref[d, rj, c * LANES:(c + 1) * LANES]
                    lhs = jnp.concatenate([jnp.where(lo64, qg_c, zero_b),
                                           jnp.where(lo64, zero_b, qg_c)], axis=0)
                    t = _dot_nt(lhs, sb16_ref[d, c, s * CH_PER + j])
                    for hh in range(2):
                        piece = t[hh * CHUNK:(hh + 1) * CHUNK]
                        hd = 2 * c + hh
                        inter[hd][j] = piece if inter[hd][j] is None else inter[hd][j] + piece
        for hd in range(GLA_HEADS):
            cl = slice(hd * GLA_DV, (hd + 1) * GLA_DV)
            o = oi_ref[rows, cl] + jnp.concatenate(inter[hd], axis=0)
            y = o * lax.rsqrt(jnp.mean(o * o, axis=-1, keepdims=True) + EPS) * gn_ref[:, cl]
            g = gg_ref[rows, cl].astype(F32)
            o_ref[rows, cl] = (y * (g * jax.nn.sigmoid(g))).astype(BF16)
        return carry

    lax.fori_loop(0, nsuper, phase3, 0)


def _gla(gq, gk, gv, gg, lr, wd, bdec, gn, s_f, s_b):
    B, L, _ = gq.shape
    nchunk = L // CHUNK
    full = lambda shape: pl.BlockSpec(shape, lambda b: (0,) * len(shape))
    tok = lambda w: pl.BlockSpec((None, L, w), lambda b: (b, 0, 0))
    st_spec = pl.BlockSpec((None, 2, LANES, GLA_DV), lambda b: (b, 0, 0, 0))
    return pl.pallas_call(
        _gla_kernel,
        grid=(B,),
        in_specs=[tok(GLA_QK_WIDTH), tok(GLA_QK_WIDTH), tok(GLA_WIDTH), tok(GLA_WIDTH), tok(LANES),
                  full(wd.shape), full(bdec.shape), full(gn.shape), st_spec, st_spec],
        out_specs=tok(GLA_WIDTH),
        out_shape=jax.ShapeDtypeStruct((B, L, GLA_WIDTH), BF16),
        scratch_shapes=[pltpu.VMEM((L, 2 * GLA_QK_WIDTH), F32),
                        pltpu.VMEM((L, GLA_WIDTH), F32),
                        pltpu.VMEM((2, L, GLA_QK_WIDTH), BF16),
                        pltpu.VMEM((2, 2, nchunk, GLA_DV, LANES), F32),
                        pltpu.VMEM((2, 2, nchunk, GLA_DV, LANES), BF16),
                        pltpu.VMEM((2, L // SUPER, 2 * CH_PER, GLA_QK_WIDTH), F32),
                        pltpu.VMEM((4, GLA_DV, LANES), F32)],
        compiler_params=_cparams(("arbitrary",)),
        name="gla_bidirectional",
    )(gq, gk, gv, gg, lr, wd, bdec, gn, s_f, s_b)


def _outproj_kernel(attn_ref, gla_ref, x_ref, mod_ref, w_ref, g2_ref, wr_ref,
                    x1_ref, h2_ref, afft_ref):
    y = _dot(attn_ref[...], w_ref[0:ATTN_WIDTH, :]) + _dot(gla_ref[...], w_ref[ATTN_WIDTH:, :])
    x1 = x_ref[...] + mod_ref[2:3, :] * y
    x1_ref[...] = x1
    h2 = _rms_mod(x1, g2_ref[...], mod_ref[3:4, :], mod_ref[4:5, :]).astype(BF16)
    h2_ref[...] = h2
    logits = _dot_nt(wr_ref[...], h2)
    e = jnp.exp(logits - jnp.max(logits, axis=0, keepdims=True))
    afft_ref[...] = e / jnp.sum(e, axis=0, keepdims=True)


def _output_projection(attn, gla, x, mod3, w_out, g2, w_router, tm):
    B, L, _ = x.shape
    full = lambda shape: pl.BlockSpec(shape, lambda b, i: (0,) * len(shape))
    tok = lambda w: pl.BlockSpec((None, tm, w), lambda b, i: (b, i, 0))
    return pl.pallas_call(
        _outproj_kernel,
        grid=(B, L // tm),
        in_specs=[tok(ATTN_WIDTH), tok(GLA_WIDTH), tok(D_MODEL),
                  pl.BlockSpec((None, 6, D_MODEL), lambda b, i: (b, 0, 0)),
                  full(w_out.shape), full(g2.shape), full(w_router.shape)],
        out_specs=[tok(D_MODEL), tok(D_MODEL),
                   pl.BlockSpec((None, N_EXPERTS, tm), lambda b, i: (b, 0, i))],
        out_shape=[jax.ShapeDtypeStruct((B, L, D_MODEL), F32),
                   jax.ShapeDtypeStruct((B, L, D_MODEL), BF16),
                   jax.ShapeDtypeStruct((B, N_EXPERTS, L), F32)],
        compiler_params=_cparams(("arbitrary", "arbitrary")),
        name="output_projection_router",
    )(attn, gla, x, mod3, w_out, g2, w_router)


def _topk_kernel(afft_ref, post_ref, pos_ref, gsel_ref, *, cap):
    nbatch, E, L = afft_ref.shape
    aff = afft_ref[...].reshape(nbatch * E, L)
    bits = pltpu.bitcast(aff, jnp.int32)
    E = nbatch * E

    def search(i, thr):
        cand = thr | jnp.left_shift(jnp.int32(1), 30 - i)
        cnt = jnp.sum(jnp.where(bits >= cand, 1.0, 0.0), axis=-1, keepdims=True)
        return jnp.where(cnt >= cap, cand, thr)

    thr = lax.fori_loop(0, 31, search, jnp.zeros((E, 1), jnp.int32))
    above = bits > thr
    tie = bits == thr
    need = cap - jnp.sum(jnp.where(above, 1.0, 0.0), axis=-1, keepdims=True)

    upper = (lax.broadcasted_iota(jnp.int32, (LANES, LANES), 0)
             <= lax.broadcasted_iota(jnp.int32, (LANES, LANES), 1)).astype(BF16)

    def prefix(mask):
        parts = []
        run = jnp.zeros((E, 1), F32)
        for j in range(L // LANES):
            blk = jnp.where(mask[:, j * LANES:(j + 1) * LANES], 1.0, 0.0).astype(BF16)
            loc = _dot(blk, upper) + run
            parts.append(loc)
            run = loc[:, LANES - 1:LANES]
        return jnp.concatenate(parts, axis=1)

    tie_rank = prefix(tie)
    sel = above | (tie & (tie_rank <= need))
    slot = prefix(sel).astype(jnp.int32) - 1
    post = jnp.where(sel, slot, -1)
    gsel = jnp.where(sel, aff, 0.0)
    ne = E // nbatch
    pad_i = jnp.full((LANES - ne, L), -1, jnp.int32)
    pad_f = jnp.zeros((LANES - ne, L), F32)
    for bb in range(nbatch):
        rows = slice(bb * ne, (bb + 1) * ne)
        post_ref[bb] = post[rows]
        pos_ref[bb] = jnp.concatenate([post[rows], pad_i], axis=0).T
        gsel_ref[bb] = jnp.concatenate([gsel[rows], pad_f], axis=0).T


def _expert_choice(afft, cap, nbatch):
    B, E, L = afft.shape
    return pl.pallas_call(
        functools.partial(_topk_kernel, cap=cap),
        grid=(B // nbatch,),
        in_specs=[pl.BlockSpec((nbatch, E, L), lambda b: (b, 0, 0))],
        out_specs=[pl.BlockSpec((nbatch, E, L), lambda b: (b, 0, 0)),
                   pl.BlockSpec((nbatch, L, LANES), lambda b: (b, 0, 0)),
                   pl.BlockSpec((nbatch, L, LANES), lambda b: (b, 0, 0))],
        out_shape=[jax.ShapeDtypeStruct((B, E, L), jnp.int32),
                   jax.ShapeDtypeStruct((B, L, LANES), jnp.int32),
                   jax.ShapeDtypeStruct((B, L, LANES), F32)],
        compiler_params=_cparams(("arbitrary",)),
        name="expert_choice_topk",
    )(afft)


def _dispatch_kernel(post_ref, h2_ref, xs_ref, *, cap):
    L = h2_ref.shape[0]
    slot = lax.broadcasted_iota(jnp.int32, (cap, L), 0)
    onehot = jnp.where(post_ref[...] == slot, 1.0, 0.0).astype(BF16)
    xs_ref[...] = _dot(onehot, h2_ref[...]).astype(BF16)


def _dispatch(post4, h2, cap):
    B, E, _, L = post4.shape
    return pl.pallas_call(
        functools.partial(_dispatch_kernel, cap=cap),
        grid=(B, E),
        in_specs=[pl.BlockSpec((None, None, 1, L), lambda b, e: (b, e, 0, 0)),
                  pl.BlockSpec((None, L, D_MODEL), lambda b, e: (b, 0, 0))],
        out_specs=pl.BlockSpec((None, None, cap, D_MODEL), lambda b, e: (b, e, 0, 0)),
        out_shape=jax.ShapeDtypeStruct((B, E, cap, D_MODEL), BF16),
        compiler_params=_cparams(("arbitrary", "arbitrary")),
        name="moe_dispatch",
    )(post4, h2)


def _ffn_kernel(xs_ref, wg_ref, wu_ref, wd_ref, y_ref):
    nbatch, cap, d = xs_ref.shape
    xs = xs_ref[...].reshape(nbatch * cap, d)
    f = wg_ref.shape[1]
    half = f // 2
    acc = None
    for j in range(2):
        cols = slice(j * half, (j + 1) * half)
        g = _dot(xs, wg_ref[:, cols])
        u = _dot(xs, wu_ref[:, cols])
        hid = (g * jax.nn.sigmoid(g) * u).astype(BF16)
        part = _dot(hid, wd_ref[cols, :])
        acc = part if acc is None else acc + part
    y_ref[...] = acc.astype(BF16).reshape(nbatch, cap, d)


def _expert_ffn(xs, w_gate, w_up, w_down, nbatch):
    B, E, cap, d = xs.shape
    f = w_gate.shape[2]
    tok = pl.BlockSpec((nbatch, None, cap, d), lambda e, b: (b, e, 0, 0))
    return pl.pallas_call(
        _ffn_kernel,
        grid=(E, B // nbatch),
        in_specs=[tok,
                  pl.BlockSpec((None, d, f), lambda e, b: (e, 0, 0)),
                  pl.BlockSpec((None, d, f), lambda e, b: (e, 0, 0)),
                  pl.BlockSpec((None, f, d), lambda e, b: (e, 0, 0))],
        out_specs=tok,
        out_shape=jax.ShapeDtypeStruct((B, E, cap, d), BF16),
        compiler_params=_cparams(("arbitrary", "arbitrary")),
        name="expert_swiglu",
    )(xs, w_gate, w_up, w_down)


def _combine_kernel(pos_ref, gsel_ref, y_ref, x1_ref, mod_ref, o_ref, acc_ref, *, cap):
    tt = pos_ref.shape[0]
    slot = lax.broadcasted_iota(jnp.int32, (tt, cap), 1)
    for e in range(N_EXPERTS):
        onehot = jnp.where(pos_ref[:, e:e + 1] == slot, 1.0, 0.0).astype(BF16)
        part = gsel_ref[:, e:e + 1] * _dot(onehot, y_ref[e])
        if e == 0:
            acc_ref[...] = part
        else:
            acc_ref[...] += part
    o_ref[...] = x1_ref[...] + mod_ref[5:6, :] * acc_ref[...]


def _combine(pos, gsel, y, x1, mod3, cap, tt):
    B, L, _ = x1.shape
    tok = lambda w: pl.BlockSpec((None, tt, w), lambda b, i: (b, i, 0))
    return pl.pallas_call(
        functools.partial(_combine_kernel, cap=cap),
        grid=(B, L // tt),
        in_specs=[tok(LANES), tok(LANES),
                  pl.BlockSpec((None, N_EXPERTS, cap, D_MODEL), lambda b, i: (b, 0, 0, 0)),
                  tok(D_MODEL),
                  pl.BlockSpec((None, 6, D_MODEL), lambda b, i: (b, 0, 0))],
        out_specs=tok(D_MODEL),
        out_shape=jax.ShapeDtypeStruct((B, L, D_MODEL), F32),
        scratch_shapes=[pltpu.VMEM((tt, D_MODEL), F32)],
        compiler_params=_cparams(("arbitrary", "arbitrary")),
        name="moe_combine",
    )(pos, gsel, y, x1, mod3)


def _rope_tables(L):
    inv = ROPE_BASE ** (-jnp.arange(ROPE_FREQS, dtype=F32) / ROPE_FREQS)
    pos = jnp.arange(L)
    row = (pos // GRID_W).astype(F32)[:, None] * inv
    col = (pos % GRID_W).astype(F32)[:, None] * inv
    cos = jnp.concatenate([jnp.cos(row), jnp.cos(row), jnp.cos(col), jnp.cos(col)], axis=1)
    sin = jnp.concatenate([-jnp.sin(row), jnp.sin(row), -jnp.sin(col), jnp.sin(col)], axis=1)
    return jnp.tile(cos, (1, 2)), jnp.tile(sin, (1, 2))


def _head_mean_matrix(n):
    idx = np.arange(n) // HEAD_DIM
    return jnp.asarray((idx[:, None] == idx[None, :]).astype(np.float32) / HEAD_DIM, dtype=BF16)


def kernel(x, c, ctx, c_ctx, w_mod, b_mod, norm1_g, w_in, q_norm_g, k_norm_g, attn_sink,
           w_decay_fwd, b_decay_fwd, w_decay_bwd, b_decay_bwd, gla_norm_g, w_out, norm2_g,
           w_router, w_e_gate, w_e_up, w_e_down):
    B, L, D = x.shape
    cap = CAPACITY_FACTOR * L // N_EXPERTS
    layer = 0

    rows = ((B + 1 + 7) // 8) * 8
    cc = jnp.concatenate([c, c_ctx[None, :], jnp.zeros((rows - B - 1, D), F32)], axis=0)
    mod_all = _modulation(cc, w_mod[layer], b_mod[layer])
    mod3 = mod_all[:B].reshape(B, 6, D)
    modc = mod_all[B].reshape(6, D)

    w = w_in[layer]
    o = np.cumsum([0, ATTN_WIDTH, KV_WIDTH, KV_WIDTH, GLA_QK_WIDTH, GLA_QK_WIDTH,
                   GLA_WIDTH, GLA_WIDTH, GATE_RANK, GATE_RANK])
    w_lr = jnp.concatenate([w[:, o[7]:o[9]], jnp.zeros((D, LANES - 2 * GATE_RANK), F32)], axis=1)
    head_order = np.arange(N_Q_HEADS).reshape(N_KV_HEADS, -1).T.reshape(-1)
    attn_perm = (head_order[:, None] * HEAD_DIM + np.arange(HEAD_DIM)[None, :]).reshape(-1)
    w_in_r = jnp.concatenate([w[:, attn_perm], w[:, o[1]:o[7]], w_lr], axis=1).astype(BF16)
    w_out_r = jnp.concatenate([w_out[layer][attn_perm], w_out[layer][ATTN_WIDTH:]], axis=0).astype(BF16)
    w_ctx = jnp.concatenate([w[:, o[1]:o[3]], w[:, o[4]:o[6]], w_lr], axis=1).astype(BF16)
    wd = jnp.zeros((LANES, 2 * GLA_QK_WIDTH), F32)
    wd = wd.at[0:GATE_RANK, 0:GLA_QK_WIDTH].set(w_decay_fwd[layer])
    wd = wd.at[GATE_RANK:2 * GATE_RANK, GLA_QK_WIDTH:].set(w_decay_bwd[layer])
    bdec = jnp.concatenate([b_decay_fwd[layer], b_decay_bwd[layer]])[None, :]
    g1 = norm1_g[layer][None, :]
    g2 = norm2_g[layer][None, :]
    qg = jnp.tile(q_norm_g[layer], N_Q_HEADS)[None, :]
    kg2 = jnp.tile(k_norm_g[layer], N_KV_HEADS)[None, :]
    gn = jnp.tile(gla_norm_g[layer], GLA_HEADS)[None, :]
    bd512 = _head_mean_matrix(ATTN_WIDTH)
    bd128 = _head_mean_matrix(KV_WIDTH)
    cos_t, sin_t = _rope_tables(L)
    w_router_t = w_router[layer].T.astype(BF16)

    kc, vc, s_f, s_b = _context_side(ctx, modc, g1, w_ctx, kg2, bd128, wd, bdec)
    q, k, v, gq, gk, gv, gg, lr = _input_projection(
        x, mod3, g1, w_in_r, qg, kg2, bd512, cos_t, sin_t, tm=1024)
    attn = _window_attention(attn_sink[layer], q, k, v, kc, vc, tq=512)
    gla = _gla(gq, gk, gv, gg, lr, wd, bdec, gn, s_f, s_b)
    x1, h2, afft = _output_projection(attn, gla, x, mod3, w_out_r, g2, w_router_t, tm=1024)
    post, pos, gsel = _expert_choice(afft, cap, nbatch=4)
    xs = _dispatch(post.reshape(B, N_EXPERTS, 1, L), h2, cap)
    y = _expert_ffn(xs, w_e_gate[layer].astype(BF16), w_e_up[layer].astype(BF16),
                    w_e_down[layer].astype(BF16), nbatch=4)
    return _combine(pos, gsel, y, x1, mod3, cap, tt=512)
```

```python
import functools

import jax
import jax.numpy as jnp
import numpy as np
from jax import lax
from jax.experimental import pallas as pl
from jax.experimental.pallas import tpu as pltpu

D_MODEL = 1024
GRID_W = 64
HEAD_DIM = 64
N_Q_HEADS = 8
N_KV_HEADS = 2
BLOCK = 128
ROPE_FREQS = 16
ROPE_BASE = 10000.0
GLA_HEADS = 4
GLA_DV = 128
GLA_DK = 64
GATE_RANK = 16
GATE_NORMALIZER = 16.0
CHUNK = 64
N_EXPERTS = 16
CAPACITY_FACTOR = 2
ATTN_WIDTH = 512
KV_WIDTH = 128
GLA_QK_WIDTH = 256
GLA_WIDTH = 512
EPS = 1e-6
NEG_INF = -1e30
LOG2E = 1.4426950408889634

LANES = 128
VMEM_LIMIT = 56 * 1024 * 1024

F32 = jnp.float32
BF16 = jnp.bfloat16
HI = lax.Precision.HIGHEST


def _cparams(sem):
    return pltpu.CompilerParams(dimension_semantics=sem, vmem_limit_bytes=VMEM_LIMIT)


def _dot(a, b):
    return jnp.dot(a, b, preferred_element_type=F32)


def _dot_hi(a, b):
    return jnp.dot(a, b, preferred_element_type=F32, precision=HI)


def _dot_nt(a, b):
    return lax.dot_general(a, b, (((1,), (1,)), ((), ())), preferred_element_type=F32)


def _dot_tn(a, b, precision=None):
    return lax.dot_general(a, b, (((0,), (0,)), ((), ())), preferred_element_type=F32,
                           precision=precision)


def _split2(t):
    hi = t.astype(BF16)
    lo = (t - hi.astype(F32)).astype(BF16)
    return hi, lo


def _rms_mod(t, g, shift, scale):
    y = t * lax.rsqrt(jnp.mean(t * t, axis=-1, keepdims=True) + EPS)
    return (y * g) * (1.0 + scale) + shift


def _log_decay(lr, wd3, bias):
    hi = lr.astype(BF16)
    lo = (lr - hi.astype(F32)).astype(BF16)
    lane = lax.broadcasted_iota(jnp.int32, lr.shape, 1)
    second = (lane >= 2 * GATE_RANK) & (lane < 4 * GATE_RANK)
    z = _dot(jnp.where(second, lo, hi), wd3) + bias
    return (jnp.minimum(z, 0.0) - jnp.log(1.0 + jnp.exp(-jnp.abs(z)))) * (1.0 / GATE_NORMALIZER)


def _lane_lo(shape):
    return (lax.broadcasted_iota(jnp.int32, shape, len(shape) - 1) % LANES) < HEAD_DIM


def _mod_kernel(c_ref, w_ref, b_ref, o_ref):
    c = c_ref[...]
    s = c * jax.nn.sigmoid(c)
    o_ref[...] = _dot_hi(s, w_ref[...]) + b_ref[...]


def _modulation(cc, w_mod, b_mod):
    m = cc.shape[0]
    n = w_mod.shape[1]
    tn = 1024
    return pl.pallas_call(
        _mod_kernel,
        grid=(n // tn,),
        in_specs=[pl.BlockSpec((m, D_MODEL), lambda j: (0, 0)),
                  pl.BlockSpec((D_MODEL, tn), lambda j: (0, j)),
                  pl.BlockSpec((1, tn), lambda j: (0, j))],
        out_specs=pl.BlockSpec((m, tn), lambda j: (0, j)),
        out_shape=jax.ShapeDtypeStruct((m, n), F32),
        compiler_params=_cparams(("arbitrary",)),
        name="adaln_mod",
    )(cc, w_mod, b_mod.reshape(1, n))


def _ctx_kernel(ctx_ref, mod_ref, g1_ref, w_ref, kg_ref, bd_ref, wd_ref, bdec_ref,
                kc_ref, vc_ref, sf_ref, sb_ref):
    n = ctx_ref.shape[0]
    h = _rms_mod(ctx_ref[...], g1_ref[...], mod_ref[0:1, :], mod_ref[1:2, :]).astype(BF16)
    pc = _dot(h, w_ref[...])
    ak = pc[:, 0:128]
    av = pc[:, 128:256]
    gk = pc[:, 256:512]
    gv = pc[:, 512:1024].astype(BF16)
    lr = pc[:, 1024:1152]
    sq_hi, sq_lo = _split2(ak * ak)
    ms = _dot(sq_hi, bd_ref[...]) + _dot(sq_lo, bd_ref[...])
    kn = ak * lax.rsqrt(ms + EPS) * kg_ref[...]
    kc_ref[...] = kn.astype(BF16)
    vc_ref[...] = av.astype(BF16)
    la = _log_decay(lr, wd_ref[...], bdec_ref[...])
    r = lax.broadcasted_iota(jnp.int32, (n, n), 0)
    cidx = lax.broadcasted_iota(jnp.int32, (n, n), 1)
    after = (cidx > r).astype(F32)
    before = (cidx < r).astype(F32)
    w_f = jnp.exp(_dot_hi(after, la[:, 0:256]))
    w_b = jnp.exp(_dot_hi(before, la[:, 256:512]))
    lo = _lane_lo((n, LANES))
    for w, out in ((w_f, sf_ref), (w_b, sb_ref)):
        kw = gk * w
        for c in range(2):
            kwc = kw[:, c * LANES:(c + 1) * LANES]
            k_lo = jnp.where(lo, kwc, 0.0).astype(BF16)
            k_hi = jnp.where(lo, 0.0, kwc).astype(BF16)
            v0 = gv[:, (2 * c) * GLA_DV:(2 * c + 1) * GLA_DV]
            v1 = gv[:, (2 * c + 1) * GLA_DV:(2 * c + 2) * GLA_DV]
            out[c] = _dot_tn(v0, k_lo) + _dot_tn(v1, k_hi)


def _context_side(ctx, modc, g1, w_ctx, kg2, bd128, wd, bdec):
    B, n, _ = ctx.shape
    full = lambda shape: pl.BlockSpec(shape, lambda b: (0,) * len(shape))
    kv_spec = pl.BlockSpec((None, n, KV_WIDTH), lambda b: (b, 0, 0))
    st_spec = pl.BlockSpec((None, 2, LANES, GLA_DV), lambda b: (b, 0, 0, 0))
    kv_shape = jax.ShapeDtypeStruct((B, n, KV_WIDTH), BF16)
    st_shape = jax.ShapeDtypeStruct((B, 2, LANES, GLA_DV), F32)
    return pl.pallas_call(
        _ctx_kernel,
        grid=(B,),
        in_specs=[pl.BlockSpec((None, n, D_MODEL), lambda b: (b, 0, 0)),
                  full(modc.shape), full(g1.shape), full(w_ctx.shape), full(kg2.shape),
                  full(bd128.shape), full(wd.shape), full(bdec.shape)],
        out_specs=[kv_spec, kv_spec, st_spec, st_spec],
        out_shape=[kv_shape, kv_shape, st_shape, st_shape],
        compiler_params=_cparams(("arbitrary",)),
        name="context_side",
    )(ctx, modc, g1, w_ctx, kg2, bd128, wd, bdec)


def _swap16(t):
    n = t.shape[1]
    first = (lax.broadcasted_iota(jnp.int32, t.shape, 1) % 32) < ROPE_FREQS
    return jnp.where(first, pltpu.roll(t, n - ROPE_FREQS, 1), pltpu.roll(t, ROPE_FREQS, 1))


def _inproj_kernel(x_ref, mod_ref, g1_ref, w_ref, qg_ref, kg_ref, bd_ref, cos_ref, sin_ref,
                   q_ref, k_ref, v_ref, gq_ref, gk_ref, gv_ref, gg_ref, lr_ref):
    h = _rms_mod(x_ref[...], g1_ref[...], mod_ref[0:1, :], mod_ref[1:2, :]).astype(BF16)
    cos = cos_ref[...]
    sin = sin_ref[...]

    def head_norm_rope(t, g, bd, reps):
        ms = _dot((t * t).astype(BF16), bd)
        tn = t * lax.rsqrt(ms + EPS) * g
        c = jnp.concatenate([cos] * reps, axis=1) if reps > 1 else cos
        s = jnp.concatenate([sin] * reps, axis=1) if reps > 1 else sin
        return tn * c + _swap16(tn) * s

    aq = _dot(h, w_ref[:, 0:512])
    q = head_norm_rope(aq, qg_ref[...], bd_ref[...], 4) * (HEAD_DIM ** -0.5 * LOG2E)
    q_ref[...] = q.astype(BF16)
    akv = _dot(h, w_ref[:, 512:768])
    k = head_norm_rope(akv[:, 0:128], kg_ref[...], bd_ref[0:128, 0:128], 1)
    k_ref[...] = k.astype(BF16)
    v_ref[...] = akv[:, 128:256].astype(BF16)
    gqk = _dot(h, w_ref[:, 768:1280])
    gq_ref[...] = (gqk[:, 0:256] * (GLA_DK ** -0.5)).astype(BF16)
    gk_ref[...] = gqk[:, 256:512].astype(BF16)
    gv_ref[...] = _dot(h, w_ref[:, 1280:1792]).astype(BF16)
    gg_ref[...] = _dot(h, w_ref[:, 1792:2304]).astype(BF16)
    lr_ref[...] = _dot(h, w_ref[:, 2304:2432])


def _input_projection(x, mod3, g1, w_in_r, qg, kg2, bd512, cos_t, sin_t, tm):
    B, L, _ = x.shape
    full = lambda shape: pl.BlockSpec(shape, lambda b, i: (0,) * len(shape))
    tok = lambda w: pl.BlockSpec((None, tm, w), lambda b, i: (b, i, 0))
    widths = (ATTN_WIDTH, KV_WIDTH, KV_WIDTH, GLA_QK_WIDTH, GLA_QK_WIDTH, GLA_WIDTH, GLA_WIDTH, LANES)
    dtypes = (BF16,) * 7 + (F32,)
    return pl.pallas_call(
        _inproj_kernel,
        grid=(B, L // tm),
        in_specs=[tok(D_MODEL),
                  pl.BlockSpec((None, 6, D_MODEL), lambda b, i: (b, 0, 0)),
                  full(g1.shape), full(w_in_r.shape), full(qg.shape), full(kg2.shape),
                  full(bd512.shape),
                  pl.BlockSpec((tm, LANES), lambda b, i: (i, 0)),
                  pl.BlockSpec((tm, LANES), lambda b, i: (i, 0))],
        out_specs=[tok(w) for w in widths],
        out_shape=[jax.ShapeDtypeStruct((B, L, w), dt) for w, dt in zip(widths, dtypes)],
        compiler_params=_cparams(("arbitrary", "arbitrary")),
        name="input_projection",
    )(x, mod3, g1, w_in_r, qg, kg2, bd512, cos_t, sin_t)


def _attn_kernel(sink_ref, q_ref, kp_ref, ko_ref, kn_ref, vp_ref, vo_ref, vn_ref,
                 kc_ref, vc_ref, o_ref):
    i = pl.program_id(1)
    ni = pl.num_programs(1)
    nsub = q_ref.shape[0] // BLOCK
    ncol = ATTN_WIDTH // LANES
    win = 3 * BLOCK
    ucol = 4
    half_rows = ucol * BLOCK
    k_win = jnp.concatenate([kp_ref[...], ko_ref[...], kn_ref[...]], axis=0)
    v_win = jnp.concatenate([vp_ref[...], vo_ref[...], vn_ref[...]], axis=0)
    k_ctx = kc_ref[...]
    lo_w = _lane_lo(v_win.shape)
    lo_c = _lane_lo(vc_ref.shape)
    lo_q = _lane_lo((BLOCK, LANES))
    zero = jnp.zeros((), BF16)
    one = jnp.ones((), BF16)
    v0_c, v0_w = jnp.where(lo_c, vc_ref[...], one), jnp.where(lo_w, v_win, one)
    v1_c, v1_w = jnp.where(lo_c, one, vc_ref[...]), jnp.where(lo_w, one, v_win)
    qi = lax.broadcasted_iota(jnp.int32, (half_rows, BLOCK), 0) % BLOCK
    kj = lax.broadcasted_iota(jnp.int32, (half_rows, BLOCK), 1)
    no_prev = jnp.where(i > 0, 0, BLOCK)
    no_next = jnp.where(i < ni - 1, 0, BLOCK)
    row_head = lax.broadcasted_iota(jnp.int32, (half_rows, 1), 0) // BLOCK
    lo_o = _lane_lo((half_rows, LANES))
    for t in range(nsub):
        rows = slice(t * BLOCK, (t + 1) * BLOCK)
        keys = slice(t * BLOCK, t * BLOCK + win)
        cols = [q_ref[rows, c * LANES:(c + 1) * LANES] for c in range(ncol)]
        prev_ok = kj >= qi + (no_prev if t == 0 else 0)
        next_ok = kj <= qi - (no_next if t == nsub - 1 else 0)
        for p in range(ncol // ucol):
            outs = []
            for g, (vv_c, vv_w) in enumerate(((v0_c, v0_w), (v1_c, v1_w))):
                qs = jnp.concatenate([jnp.where(lo_q, qc, zero) if g == 0 else jnp.where(lo_q, zero, qc)
                                      for qc in cols[ucol * p:ucol * (p + 1)]], axis=0)
                head = g * ncol + ucol * p
                sink_g = jnp.full((half_rows, 1), sink_ref[head + ucol - 1], F32)
                for j in range(ucol - 2, -1, -1):
                    sink_g = jnp.where(row_head <= j, sink_ref[head + j], sink_g)
                sink_g = sink_g * LOG2E
                s_c = _dot_nt(qs, k_ctx)
                s_w = _dot_nt(qs, k_win[keys])
                s_p = jnp.where(prev_ok, s_w[:, 0:BLOCK], NEG_INF)
                s_o = s_w[:, BLOCK:2 * BLOCK]
                s_n = jnp.where(next_ok, s_w[:, 2 * BLOCK:win], NEG_INF)
                m = jnp.maximum(jnp.maximum(jnp.max(s_c, axis=-1, keepdims=True),
                                            jnp.max(jnp.maximum(jnp.maximum(s_p, s_o), s_n),
                                                    axis=-1, keepdims=True)), sink_g)
                e_c = jnp.exp2(s_c - m).astype(BF16)
                e_w = jnp.concatenate([jnp.exp2(s_p - m), jnp.exp2(s_o - m), jnp.exp2(s_n - m)],
                                      axis=1).astype(BF16)
                acc = _dot(e_c, vv_c) + _dot(e_w, vv_w[keys])
                outs.append(acc / (pltpu.roll(acc, HEAD_DIM, 1) + jnp.exp2(sink_g - m)))
            o = jnp.where(lo_o, outs[0], outs[1]).astype(BF16)
            for j in range(ucol):
                c = ucol * p + j
                o_ref[rows, c * LANES:(c + 1) * LANES] = o[j * BLOCK:(j + 1) * BLOCK]


def _window_attention(sink, q, k, v, kc, vc, tq):
    B, L, _ = q.shape
    nb = L // BLOCK
    nsub = tq // BLOCK
    n_ctx = kc.shape[1]
    prev = pl.BlockSpec((None, BLOCK, KV_WIDTH), lambda b, n: (b, jnp.maximum(n * nsub - 1, 0), 0))
    own = pl.BlockSpec((None, tq, KV_WIDTH), lambda b, n: (b, n, 0))
    nxt = pl.BlockSpec((None, BLOCK, KV_WIDTH),
                       lambda b, n: (b, jnp.minimum((n + 1) * nsub, nb - 1), 0))
    cspec = pl.BlockSpec((None, n_ctx, KV_WIDTH), lambda b, n: (b, 0, 0))
    return pl.pallas_call(
        _attn_kernel,
        grid=(B, L // tq),
        in_specs=[pl.BlockSpec(memory_space=pltpu.SMEM),
                  pl.BlockSpec((None, tq, ATTN_WIDTH), lambda b, n: (b, n, 0)),
                  prev, own, nxt, prev, own, nxt, cspec, cspec],
        out_specs=pl.BlockSpec((None, tq, ATTN_WIDTH), lambda b, n: (b, n, 0)),
        out_shape=jax.ShapeDtypeStruct((B, L, ATTN_WIDTH), BF16),
        compiler_params=_cparams(("arbitrary", "arbitrary")),
        name="window_attention",
    )(sink, q, k, k, k, v, v, v, kc, vc)


SUPER = 256
CH_PER = SUPER // CHUNK
HALF = 128


def _dot_split(m, parts):
    return _dot(m, parts[0]) + _dot(m, parts[1])


def _gla_kernel(gq_ref, gk_ref, gv_ref, gg_ref, lr_ref, wd_ref, bdec_ref, gn_ref, sf_ref, sb_ref,
                o_ref, la_ref, oi_ref, qg_ref, kv_ref, sb16_ref, dec_ref, st_ref):
    L = gq_ref.shape[0]
    nsuper = L // SUPER
    nchunk = L // CHUNK
    half = CHUNK // 2
    la_ref[...] = _log_decay(lr_ref[...], wd_ref[...], bdec_ref[...])

    r = lax.broadcasted_iota(jnp.int32, (SUPER, SUPER), 0)
    cidx = lax.broadcasted_iota(jnp.int32, (SUPER, SUPER), 1)
    same = (r // CHUNK) == (cidx // CHUNK)
    pr = r % CHUNK
    pc = cidx % CHUNK
    one = jnp.float32(1.0)
    zero = jnp.float32(0.0)
    in_f = jnp.where(pc <= pr, one, zero)
    in_b = jnp.where(pc >= pr, one, zero)
    ref_f = jnp.where(pc < half, one, zero)
    ref_b = jnp.where(pc >= half, one, zero)
    m1_f = jnp.where(same, in_f - ref_f, zero).astype(BF16)
    m1_b = jnp.where(same, in_b - ref_b, zero).astype(BF16)
    rh = lax.broadcasted_iota(jnp.int32, (HALF, 2 * HALF), 0)
    ch = lax.broadcasted_iota(jnp.int32, (HALF, 2 * HALF), 1) % HALF
    same_h = (rh // CHUNK) == (ch // CHUNK)
    mask_f = jnp.where(same_h, jnp.where(ch % CHUNK <= rh % CHUNK, one, zero), zero) > 0.5
    mask_b = jnp.where(same_h, jnp.where(ch % CHUNK >= rh % CHUNK, one, zero), zero) > 0.5
    rr = lax.broadcasted_iota(jnp.int32, (2 * CH_PER, SUPER), 0)
    rc = lax.broadcasted_iota(jnp.int32, (2 * CH_PER, SUPER), 1)
    in_chunk = jnp.where((rc // CHUNK) == (rr % CH_PER), one, zero)
    first = jnp.where((rc % CHUNK) < half, 1, 0)
    is_tot = jnp.where(rr >= CH_PER, 1, 0)
    rs_f = (in_chunk * jnp.where(first != is_tot, one, zero)).astype(BF16)
    rs_b = (in_chunk * jnp.where(first == is_tot, one, zero)).astype(BF16)
    lo_h = _lane_lo((HALF, LANES))
    zero_blk = jnp.zeros((HALF, GLA_DV), BF16)

    def phase1(s, carry):
        r0 = pl.multiple_of(s * SUPER, SUPER)
        rows = pl.ds(r0, SUPER)
        q = gq_ref[rows, :].astype(F32)
        k = gk_ref[rows, :].astype(F32)
        qes, kes, kds = [], [], []
        for d, (m1, rs) in enumerate(((m1_f, rs_f), (m1_b, rs_b))):
            parts = _split2(la_ref[rows, d * GLA_QK_WIDTH:(d + 1) * GLA_QK_WIDTH])
            x1 = _dot_split(m1, parts)
            erow = jnp.exp(_dot_split(rs, parts))
            dec = erow[0:CH_PER] * erow[CH_PER:2 * CH_PER]
            dec_ref[d, s] = jnp.concatenate([dec, dec], axis=0)
            qe = q * jnp.exp(x1)
            ke = k * jnp.exp(-x1)
            qg_parts, kd_parts = [], []
            for j in range(CH_PER):
                rj = slice(j * CHUNK, (j + 1) * CHUNK)
                qg_parts.append(qe[rj] * erow[j:j + 1])
                kd_parts.append(ke[rj] * erow[CH_PER + j:CH_PER + j + 1])
            qg_ref[d, rows, :] = jnp.concatenate(qg_parts, axis=0).astype(BF16)
            qes.append(qe.astype(BF16))
            kes.append(ke)
            kds.append(jnp.concatenate(kd_parts, axis=0).astype(BF16))
        for c in range(2):
            cl = slice(c * LANES, (c + 1) * LANES)
            vpair = gv_ref[rows, 2 * c * GLA_DV:(2 * c + 2) * GLA_DV]
            for blk in range(SUPER // HALF):
                rb = slice(blk * HALF, (blk + 1) * HALF)
                vbd = jnp.concatenate(
                    [jnp.concatenate([vpair[rb, 0:GLA_DV], zero_blk], axis=1),
                     jnp.concatenate([zero_blk, vpair[rb, GLA_DV:2 * GLA_DV]], axis=1)], axis=0)
                o2 = None
                for d, mask in enumerate((mask_f, mask_b)):
                    ke_cb = kes[d][rb, cl]
                    ke_st = jnp.concatenate([jnp.where(lo_h, ke_cb, zero),
                                             jnp.where(lo_h, zero, ke_cb)], axis=0).astype(BF16)
                    a = _dot_nt(qes[d][rb, cl], ke_st)
                    o = _dot(jnp.where(mask, a, zero).astype(BF16), vbd)
                    o2 = o if o2 is None else o2 + o
                oi_ref[pl.ds(r0 + blk * HALF, HALF), 2 * c * GLA_DV:(2 * c + 2) * GLA_DV] = o2
            for d in range(2):
                for j in range(CH_PER):
                    rj = slice(j * CHUNK, (j + 1) * CHUNK)
                    t = _dot_tn(kds[d][rj, cl], vpair[rj])
                    kv = jnp.concatenate([t[0:GLA_DK, 0:GLA_DV], t[GLA_DK:, GLA_DV:]], axis=0)
                    kv_ref[d, c, s * CH_PER + j] = kv.T
        return carry

    lax.fori_loop(0, nsuper, phase1, 0)

    st_ref[0] = sf_ref[0]
    st_ref[1] = sf_ref[1]
    st_ref[2] = sb_ref[0]
    st_ref[3] = sb_ref[1]

    def phase2(n, carry):
        for d in range(2):
            idx = n if d == 0 else nchunk - 1 - n
            dec = dec_ref[d, idx // CH_PER, pl.ds(idx % CH_PER, 1), :]
            for c in range(2):
                st = st_ref[2 * d + c]
                sb16_ref[c, idx, :, d * LANES:(d + 1) * LANES] = st.astype(BF16)
                st_ref[2 * d + c] = dec[:, c * LANES:(c + 1) * LANES] * st + kv_ref[d, c, idx]
        return carry

    lax.fori_loop(0, nchunk, phase2, 0)

    lo64 = _lane_lo((CHUNK, 2 * LANES))
    zero_b = jnp.zeros((), BF16)

    def phase3(s, carry):
        r0 = pl.multiple_of(s * SUPER, SUPER)
        rows = pl.ds(r0, SUPER)
        inter = [[None] * CH_PER for _ in range(GLA_HEADS)]
        for j in range(CH_PER):
            rj = pl.ds(r0 + j * CHUNK, CHUNK)
            for c in range(2):
                qg_c = jnp.concatenate([qg_ref[d, rj, c * LANES:(c + 1) * LANES] for d in range(2)], axis=1)
                lhs = jnp.concatenate([jnp.where(lo64, qg_c, zero_b),
                                       jnp.where(lo64, zero_b, qg_c)], axis=0)
                t = _dot_nt(lhs, sb16_ref[c, s * CH_PER + j])
                for hh in range(2):
                    inter[2 * c + hh][j] = t[hh * CHUNK:(hh + 1) * CHUNK]
        for hd in range(GLA_HEADS):
            cl = slice(hd * GLA_DV, (hd + 1) * GLA_DV)
            o = oi_ref[rows, cl] + jnp.concatenate(inter[hd], axis=0)
            y = o * lax.rsqrt(jnp.mean(o * o, axis=-1, keepdims=True) + EPS) * gn_ref[:, cl]
            g = gg_ref[rows, cl].astype(F32)
            o_ref[rows, cl] = (y * (g * jax.nn.sigmoid(g))).astype(BF16)
        return carry

    lax.fori_loop(0, nsuper, phase3, 0)


def _gla(gq, gk, gv, gg, lr, wd, bdec, gn, s_f, s_b):
    B, L, _ = gq.shape
    nchunk = L // CHUNK
    full = lambda shape: pl.BlockSpec(shape, lambda b: (0,) * len(shape))
    tok = lambda w: pl.BlockSpec((None, L, w), lambda b: (b, 0, 0))
    st_spec = pl.BlockSpec((None, 2, LANES, GLA_DV), lambda b: (b, 0, 0, 0))
    return pl.pallas_call(
        _gla_kernel,
        grid=(B,),
        in_specs=[tok(GLA_QK_WIDTH), tok(GLA_QK_WIDTH), tok(GLA_WIDTH), tok(GLA_WIDTH), tok(LANES),
                  full(wd.shape), full(bdec.shape), full(gn.shape), st_spec, st_spec],
        out_specs=tok(GLA_WIDTH),
        out_shape=jax.ShapeDtypeStruct((B, L, GLA_WIDTH), BF16),
        scratch_shapes=[pltpu.VMEM((L, 2 * GLA_QK_WIDTH), F32),
                        pltpu.VMEM((L, GLA_WIDTH), F32),
                        pltpu.VMEM((2, L, GLA_QK_WIDTH), BF16),
                        pltpu.VMEM((2, 2, nchunk, GLA_DV, LANES), F32),
                        pltpu.VMEM((2, nchunk, GLA_DV, 2 * LANES), BF16),
                        pltpu.VMEM((2, L // SUPER, 2 * CH_PER, GLA_QK_WIDTH), F32),
                        pltpu.VMEM((4, GLA_DV, LANES), F32)],
        compiler_params=_cparams(("arbitrary",)),
        name="gla_bidirectional",
    )(gq, gk, gv, gg, lr, wd, bdec, gn, s_f, s_b)


def _outproj_kernel(attn_ref, gla_ref, x_ref, mod_ref, w_ref, g2_ref, wr_ref,
                    x1_ref, h2_ref, afft_ref):
    y = _dot(attn_ref[...], w_ref[0:ATTN_WIDTH, :]) + _dot(gla_ref[...], w_ref[ATTN_WIDTH:, :])
    x1 = x_ref[...] + mod_ref[2:3, :] * y
    x1_ref[...] = x1
    h2 = _rms_mod(x1, g2_ref[...], mod_ref[3:4, :], mod_ref[4:5, :]).astype(BF16)
    h2_ref[...] = h2
    logits = _dot_nt(wr_ref[...], h2)
    e = jnp.exp(logits - jnp.max(logits, axis=0, keepdims=True))
    afft_ref[...] = e / jnp.sum(e, axis=0, keepdims=True)


def _output_projection(attn, gla, x, mod3, w_out, g2, w_router, tm):
    B, L, _ = x.shape
    full = lambda shape: pl.BlockSpec(shape, lambda b, i: (0,) * len(shape))
    tok = lambda w: pl.BlockSpec((None, tm, w), lambda b, i: (b, i, 0))
    return pl.pallas_call(
        _outproj_kernel,
        grid=(B, L // tm),
        in_specs=[tok(ATTN_WIDTH), tok(GLA_WIDTH), tok(D_MODEL),
                  pl.BlockSpec((None, 6, D_MODEL), lambda b, i: (b, 0, 0)),
                  full(w_out.shape), full(g2.shape), full(w_router.shape)],
        out_specs=[tok(D_MODEL), tok(D_MODEL),
                   pl.BlockSpec((None, N_EXPERTS, tm), lambda b, i: (b, 0, i))],
        out_shape=[jax.ShapeDtypeStruct((B, L, D_MODEL), F32),
                   jax.ShapeDtypeStruct((B, L, D_MODEL), BF16),
                   jax.ShapeDtypeStruct((B, N_EXPERTS, L), F32)],
        compiler_params=_cparams(("arbitrary", "arbitrary")),
        name="output_projection_router",
    )(attn, gla, x, mod3, w_out, g2, w_router)


def _topk_kernel(afft_ref, post_ref, pos_ref, gsel_ref, *, cap):
    nbatch, E, L = afft_ref.shape
    aff = afft_ref[...].reshape(nbatch * E, L)
    E = nbatch * E

    def search(i, thr):
        cand = thr | jnp.left_shift(jnp.int32(1), 30 - i)
        cnt = jnp.sum(jnp.where(aff >= pltpu.bitcast(cand, F32), 1.0, 0.0), axis=-1, keepdims=True)
        return jnp.where(cnt >= cap, cand, thr)

    thr_bits = lax.fori_loop(0, 31, search, jnp.zeros((E, 1), jnp.int32))
    thr = pltpu.bitcast(thr_bits, F32)
    above = aff > thr
    tie = aff == thr
    need = cap - jnp.sum(jnp.where(above, 1.0, 0.0), axis=-1, keepdims=True)

    upper = (lax.broadcasted_iota(jnp.int32, (LANES, LANES), 0)
             <= lax.broadcasted_iota(jnp.int32, (LANES, LANES), 1)).astype(BF16)

    def prefix(mask):
        parts = []
        run = jnp.zeros((E, 1), F32)
        for j in range(L // LANES):
            blk = jnp.where(mask[:, j * LANES:(j + 1) * LANES], 1.0, 0.0).astype(BF16)
            loc = _dot(blk, upper) + run
            parts.append(loc)
            run = loc[:, LANES - 1:LANES]
        return jnp.concatenate(parts, axis=1)

    tie_rank = prefix(tie)
    sel = above | (tie & (tie_rank <= need))
    slot = prefix(sel).astype(jnp.int32) - 1
    post = jnp.where(sel, slot, -1)
    gsel = jnp.where(sel, aff, 0.0)
    ne = E // nbatch
    pad_i = jnp.full((LANES - ne, L), -1, jnp.int32)
    pad_f = jnp.zeros((LANES - ne, L), F32)
    for bb in range(nbatch):
        rows = slice(bb * ne, (bb + 1) * ne)
        post_ref[bb] = post[rows]
        pos_ref[bb] = jnp.concatenate([post[rows], pad_i], axis=0).T
        gsel_ref[bb] = jnp.concatenate([gsel[rows], pad_f], axis=0).T


def _expert_choice(afft, cap, nbatch):
    B, E, L = afft.shape
    return pl.pallas_call(
        functools.partial(_topk_kernel, cap=cap),
        grid=(B // nbatch,),
        in_specs=[pl.BlockSpec((nbatch, E, L), lambda b: (b, 0, 0))],
        out_specs=[pl.BlockSpec((nbatch, E, L), lambda b: (b, 0, 0)),
                   pl.BlockSpec((nbatch, L, LANES), lambda b: (b, 0, 0)),
                   pl.BlockSpec((nbatch, L, LANES), lambda b: (b, 0, 0))],
        out_shape=[jax.ShapeDtypeStruct((B, E, L), jnp.int32),
                   jax.ShapeDtypeStruct((B, L, LANES), jnp.int32),
                   jax.ShapeDtypeStruct((B, L, LANES), F32)],
        compiler_params=_cparams(("arbitrary",)),
        name="expert_choice_topk",
    )(afft)


def _dispatch_kernel(post_ref, h2_ref, xs_ref, *, cap):
    L = h2_ref.shape[0]
    slot = lax.broadcasted_iota(jnp.int32, (cap, L), 0)
    onehot = jnp.where(post_ref[...] == slot, 1.0, 0.0).astype(BF16)
    xs_ref[...] = _dot(onehot, h2_ref[...]).astype(BF16)


def _dispatch(post4, h2, cap):
    B, E, _, L = post4.shape
    return pl.pallas_call(
        functools.partial(_dispatch_kernel, cap=cap),
        grid=(B, E),
        in_specs=[pl.BlockSpec((None, None, 1, L), lambda b, e: (b, e, 0, 0)),
                  pl.BlockSpec((None, L, D_MODEL), lambda b, e: (b, 0, 0))],
        out_specs=pl.BlockSpec((None, None, cap, D_MODEL), lambda b, e: (b, e, 0, 0)),
        out_shape=jax.ShapeDtypeStruct((B, E, cap, D_MODEL), BF16),
        compiler_params=_cparams(("arbitrary", "arbitrary")),
        name="moe_dispatch",
    )(post4, h2)


def _ffn_kernel(xs_ref, wg_ref, wu_ref, wd_ref, y_ref, wgb_ref, wub_ref, wdb_ref):
    nbatch, cap, d = xs_ref.shape

    @pl.when(pl.program_id(1) == 0)
    def _():
        wgb_ref[...] = wg_ref[...].astype(BF16)
        wub_ref[...] = wu_ref[...].astype(BF16)
        wdb_ref[...] = wd_ref[...].astype(BF16)

    xs = xs_ref[...].reshape(nbatch * cap, d)
    f = wg_ref.shape[1]
    half = f // 2
    acc = None
    for j in range(2):
        cols = slice(j * half, (j + 1) * half)
        g = _dot(xs, wgb_ref[:, cols])
        u = _dot(xs, wub_ref[:, cols])
        hid = (g * jax.nn.sigmoid(g) * u).astype(BF16)
        part = _dot(hid, wdb_ref[cols, :])
        acc = part if acc is None else acc + part
    y_ref[...] = acc.astype(BF16).reshape(nbatch, cap, d)


def _expert_ffn(xs, w_gate, w_up, w_down, nbatch):
    B, E, cap, d = xs.shape
    f = w_gate.shape[2]
    tok = pl.BlockSpec((nbatch, None, cap, d), lambda e, b: (b, e, 0, 0))
    return pl.pallas_call(
        _ffn_kernel,
        grid=(E, B // nbatch),
        in_specs=[tok,
                  pl.BlockSpec((None, d, f), lambda e, b: (e, 0, 0)),
                  pl.BlockSpec((None, d, f), lambda e, b: (e, 0, 0)),
                  pl.BlockSpec((None, f, d), lambda e, b: (e, 0, 0))],
        out_specs=tok,
        out_shape=jax.ShapeDtypeStruct((B, E, cap, d), BF16),
        scratch_shapes=[pltpu.VMEM((d, f), BF16), pltpu.VMEM((d, f), BF16), pltpu.VMEM((f, d), BF16)],
        compiler_params=_cparams(("arbitrary", "arbitrary")),
        name="expert_swiglu",
    )(xs, w_gate, w_up, w_down)


def _combine_kernel(pos_ref, gsel_ref, y_ref, x1_ref, mod_ref, o_ref, acc_ref, *, cap):
    tt = pos_ref.shape[0]
    slot = lax.broadcasted_iota(jnp.int32, (tt, cap), 1)
    for e in range(N_EXPERTS):
        onehot = jnp.where(pos_ref[:, e:e + 1] == slot, 1.0, 0.0).astype(BF16)
        part = gsel_ref[:, e:e + 1] * _dot(onehot, y_ref[e])
        if e == 0:
            acc_ref[...] = part
        else:
            acc_ref[...] += part
    o_ref[...] = x1_ref[...] + mod_ref[5:6, :] * acc_ref[...]


def _combine(pos, gsel, y, x1, mod3, cap, tt):
    B, L, _ = x1.shape
    tok = lambda w: pl.BlockSpec((None, tt, w), lambda b, i: (b, i, 0))
    return pl.pallas_call(
        functools.partial(_combine_kernel, cap=cap),
        grid=(B, L // tt),
        in_specs=[tok(LANES), tok(LANES),
                  pl.BlockSpec((None, N_EXPERTS, cap, D_MODEL), lambda b, i: (b, 0, 0, 0)),
                  tok(D_MODEL),
                  pl.BlockSpec((None, 6, D_MODEL), lambda b, i: (b, 0, 0))],
        out_specs=tok(D_MODEL),
        out_shape=jax.ShapeDtypeStruct((B, L, D_MODEL), F32),
        scratch_shapes=[pltpu.VMEM((tt, D_MODEL), F32)],
        compiler_params=_cparams(("arbitrary", "arbitrary")),
        name="moe_combine",
    )(pos, gsel, y, x1, mod3)


def _rope_tables(L):
    inv = ROPE_BASE ** (-jnp.arange(ROPE_FREQS, dtype=F32) / ROPE_FREQS)
    pos = jnp.arange(L)
    row = (pos // GRID_W).astype(F32)[:, None] * inv
    col = (pos % GRID_W).astype(F32)[:, None] * inv
    cos = jnp.concatenate([jnp.cos(row), jnp.cos(row), jnp.cos(col), jnp.cos(col)], axis=1)
    sin = jnp.concatenate([-jnp.sin(row), jnp.sin(row), -jnp.sin(col), jnp.sin(col)], axis=1)
    return jnp.tile(cos, (1, 2)), jnp.tile(sin, (1, 2))


def _head_mean_matrix(n):
    idx = np.arange(n) // HEAD_DIM
    return jnp.asarray((idx[:, None] == idx[None, :]).astype(np.float32) / HEAD_DIM, dtype=BF16)


def kernel(x, c, ctx, c_ctx, w_mod, b_mod, norm1_g, w_in, q_norm_g, k_norm_g, attn_sink,
           w_decay_fwd, b_decay_fwd, w_decay_bwd, b_decay_bwd, gla_norm_g, w_out, norm2_g,
           w_router, w_e_gate, w_e_up, w_e_down):
    B, L, D = x.shape
    cap = CAPACITY_FACTOR * L // N_EXPERTS
    layer = 0

    rows = ((B + 1 + 7) // 8) * 8
    cc = jnp.concatenate([c, c_ctx[None, :], jnp.zeros((rows - B - 1, D), F32)], axis=0)
    mod_all = _modulation(cc, w_mod[layer], b_mod[layer])
    mod3 = mod_all[:B].reshape(B, 6, D)
    modc = mod_all[B].reshape(6, D)

    w = w_in[layer]
    o = np.cumsum([0, ATTN_WIDTH, KV_WIDTH, KV_WIDTH, GLA_QK_WIDTH, GLA_QK_WIDTH,
                   GLA_WIDTH, GLA_WIDTH, GATE_RANK, GATE_RANK])
    w_lr = jnp.concatenate([w[:, o[7]:o[9]]] * 3 + [jnp.zeros((D, LANES - 6 * GATE_RANK), F32)], axis=1)
    head_order = np.arange(N_Q_HEADS).reshape(N_KV_HEADS, -1).T.reshape(-1)
    attn_perm = (head_order[:, None] * HEAD_DIM + np.arange(HEAD_DIM)[None, :]).reshape(-1)
    w_in_r = jnp.concatenate([w[:, attn_perm], w[:, o[1]:o[7]], w_lr], axis=1).astype(BF16)
    w_out_r = jnp.concatenate([w_out[layer][attn_perm], w_out[layer][ATTN_WIDTH:]], axis=0).astype(BF16)
    w_ctx = jnp.concatenate([w[:, o[1]:o[3]], w[:, o[4]:o[6]], w_lr], axis=1).astype(BF16)
    wd2 = jnp.zeros((2 * GATE_RANK, 2 * GLA_QK_WIDTH), F32)
    wd2 = wd2.at[0:GATE_RANK, 0:GLA_QK_WIDTH].set(w_decay_fwd[layer])
    wd2 = wd2.at[GATE_RANK:, GLA_QK_WIDTH:].set(w_decay_bwd[layer])
    wd_hi = wd2.astype(BF16)
    wd_lo = (wd2 - wd_hi.astype(F32)).astype(BF16)
    wd = jnp.concatenate([wd_hi, wd_hi, wd_lo,
                          jnp.zeros((LANES - 6 * GATE_RANK, 2 * GLA_QK_WIDTH), BF16)], axis=0)
    bdec = jnp.concatenate([b_decay_fwd[layer], b_decay_bwd[layer]])[None, :]
    g1 = norm1_g[layer][None, :]
    g2 = norm2_g[layer][None, :]
    qg = jnp.tile(q_norm_g[layer], N_Q_HEADS)[None, :]
    kg2 = jnp.tile(k_norm_g[layer], N_KV_HEADS)[None, :]
    gn = jnp.tile(gla_norm_g[layer], GLA_HEADS)[None, :]
    bd512 = _head_mean_matrix(ATTN_WIDTH)
    bd128 = _head_mean_matrix(KV_WIDTH)
    cos_t, sin_t = _rope_tables(L)
    w_router_t = w_router[layer].T.astype(BF16)

    kc, vc, s_f, s_b = _context_side(ctx, modc, g1, w_ctx, kg2, bd128, wd, bdec)
    q, k, v, gq, gk, gv, gg, lr = _input_projection(
        x, mod3, g1, w_in_r, qg, kg2, bd512, cos_t, sin_t, tm=1024)
    attn = _window_attention(attn_sink[layer], q, k, v, kc, vc, tq=512)
    gla = _gla(gq, gk, gv, gg, lr, wd, bdec, gn, s_f, s_b)
    x1, h2, afft = _output_projection(attn, gla, x, mod3, w_out_r, g2, w_router_t, tm=1024)
    post, pos, gsel = _expert_choice(afft, cap, nbatch=4)
    xs = _dispatch(post.reshape(B, N_EXPERTS, 1, L), h2, cap)
    y = _expert_ffn(xs, w_e_gate[layer], w_e_up[layer], w_e_down[layer], nbatch=4)
    return _combine(pos, gsel, y, x1, mod3, cap, tt=512)
```

```python
import functools

import jax
import jax.numpy as jnp
import numpy as np
from jax import lax
from jax.experimental import pallas as pl
from jax.experimental.pallas import tpu as pltpu

D_MODEL = 1024
GRID_W = 64
HEAD_DIM = 64
N_Q_HEADS = 8
N_KV_HEADS = 2
BLOCK = 128
ROPE_FREQS = 16
ROPE_BASE = 10000.0
GLA_HEADS = 4
GLA_DV = 128
GLA_DK = 64
GATE_RANK = 16
GATE_NORMALIZER = 16.0
CHUNK = 64
N_EXPERTS = 16
CAPACITY_FACTOR = 2
ATTN_WIDTH = 512
KV_WIDTH = 128
GLA_QK_WIDTH = 256
GLA_WIDTH = 512
EPS = 1e-6
NEG_INF = -1e30
LOG2E = 1.4426950408889634

LANES = 128
VMEM_LIMIT = 56 * 1024 * 1024

F32 = jnp.float32
BF16 = jnp.bfloat16
HI = lax.Precision.HIGHEST


def _cparams(sem):
    return pltpu.CompilerParams(dimension_semantics=sem, vmem_limit_bytes=VMEM_LIMIT)


def _dot(a, b):
    return jnp.dot(a, b, preferred_element_type=F32)


def _dot_hi(a, b):
    return jnp.dot(a, b, preferred_element_type=F32, precision=HI)


def _dot_nt(a, b):
    return lax.dot_general(a, b, (((1,), (1,)), ((), ())), preferred_element_type=F32)


def _dot_tn(a, b, precision=None):
    return lax.dot_general(a, b, (((0,), (0,)), ((), ())), preferred_element_type=F32,
                           precision=precision)


def _split2(t):
    hi = t.astype(BF16)
    lo = (t - hi.astype(F32)).astype(BF16)
    return hi, lo


def _rms_mod(t, g, shift, scale):
    y = t * lax.rsqrt(jnp.mean(t * t, axis=-1, keepdims=True) + EPS)
    return (y * g) * (1.0 + scale) + shift


def _log_decay(lr, wd3, bias):
    hi = lr.astype(BF16)
    lo = (lr - hi.astype(F32)).astype(BF16)
    lane = lax.broadcasted_iota(jnp.int32, lr.shape, 1)
    second = (lane >= 2 * GATE_RANK) & (lane < 4 * GATE_RANK)
    z = _dot(jnp.where(second, lo, hi), wd3) + bias
    return (jnp.minimum(z, 0.0) - jnp.log(1.0 + jnp.exp(-jnp.abs(z)))) * (1.0 / GATE_NORMALIZER)


def _lane_lo(shape):
    return (lax.broadcasted_iota(jnp.int32, shape, len(shape) - 1) % LANES) < HEAD_DIM


def _mod_kernel(c_ref, w_ref, b_ref, o_ref):
    c = c_ref[...]
    s = c * jax.nn.sigmoid(c)
    o_ref[...] = _dot_hi(s, w_ref[...]) + b_ref[...]


def _modulation(cc, w_mod, b_mod):
    m = cc.shape[0]
    n = w_mod.shape[1]
    tn = 1024
    return pl.pallas_call(
        _mod_kernel,
        grid=(n // tn,),
        in_specs=[pl.BlockSpec((m, D_MODEL), lambda j: (0, 0)),
                  pl.BlockSpec((D_MODEL, tn), lambda j: (0, j)),
                  pl.BlockSpec((1, tn), lambda j: (0, j))],
        out_specs=pl.BlockSpec((m, tn), lambda j: (0, j)),
        out_shape=jax.ShapeDtypeStruct((m, n), F32),
        compiler_params=_cparams(("arbitrary",)),
        name="adaln_mod",
    )(cc, w_mod, b_mod.reshape(1, n))


def _ctx_kernel(ctx_ref, mod_ref, g1_ref, w_ref, kg_ref, bd_ref, wd_ref, bdec_ref,
                kc_ref, vc_ref, sf_ref, sb_ref):
    n = ctx_ref.shape[0]
    h = _rms_mod(ctx_ref[...], g1_ref[...], mod_ref[0:1, :], mod_ref[1:2, :]).astype(BF16)
    pc = _dot(h, w_ref[...])
    ak = pc[:, 0:128]
    av = pc[:, 128:256]
    gk = pc[:, 256:512]
    gv = pc[:, 512:1024].astype(BF16)
    lr = pc[:, 1024:1152]
    sq_hi, sq_lo = _split2(ak * ak)
    ms = _dot(sq_hi, bd_ref[...]) + _dot(sq_lo, bd_ref[...])
    kn = ak * lax.rsqrt(ms + EPS) * kg_ref[...]
    kc_ref[...] = kn.astype(BF16)
    vc_ref[...] = av.astype(BF16)
    la = _log_decay(lr, wd_ref[...], bdec_ref[...])
    r = lax.broadcasted_iota(jnp.int32, (n, n), 0)
    cidx = lax.broadcasted_iota(jnp.int32, (n, n), 1)
    after = (cidx > r).astype(F32)
    before = (cidx < r).astype(F32)
    w_f = jnp.exp(_dot_hi(after, la[:, 0:256]))
    w_b = jnp.exp(_dot_hi(before, la[:, 256:512]))
    lo = _lane_lo((n, LANES))
    for w, out in ((w_f, sf_ref), (w_b, sb_ref)):
        kw = gk * w
        for c in range(2):
            kwc = kw[:, c * LANES:(c + 1) * LANES]
            k_lo = jnp.where(lo, kwc, 0.0).astype(BF16)
            k_hi = jnp.where(lo, 0.0, kwc).astype(BF16)
            v0 = gv[:, (2 * c) * GLA_DV:(2 * c + 1) * GLA_DV]
            v1 = gv[:, (2 * c + 1) * GLA_DV:(2 * c + 2) * GLA_DV]
            out[c] = _dot_tn(v0, k_lo) + _dot_tn(v1, k_hi)


def _context_side(ctx, modc, g1, w_ctx, kg2, bd128, wd, bdec):
    B, n, _ = ctx.shape
    full = lambda shape: pl.BlockSpec(shape, lambda b: (0,) * len(shape))
    kv_spec = pl.BlockSpec((None, n, KV_WIDTH), lambda b: (b, 0, 0))
    st_spec = pl.BlockSpec((None, 2, LANES, GLA_DV), lambda b: (b, 0, 0, 0))
    kv_shape = jax.ShapeDtypeStruct((B, n, KV_WIDTH), BF16)
    st_shape = jax.ShapeDtypeStruct((B, 2, LANES, GLA_DV), F32)
    return pl.pallas_call(
        _ctx_kernel,
        grid=(B,),
        in_specs=[pl.BlockSpec((None, n, D_MODEL), lambda b: (b, 0, 0)),
                  full(modc.shape), full(g1.shape), full(w_ctx.shape), full(kg2.shape),
                  full(bd128.shape), full(wd.shape), full(bdec.shape)],
        out_specs=[kv_spec, kv_spec, st_spec, st_spec],
        out_shape=[kv_shape, kv_shape, st_shape, st_shape],
        compiler_params=_cparams(("arbitrary",)),
        name="context_side",
    )(ctx, modc, g1, w_ctx, kg2, bd128, wd, bdec)


def _swap16(t):
    n = t.shape[1]
    first = (lax.broadcasted_iota(jnp.int32, t.shape, 1) % 32) < ROPE_FREQS
    return jnp.where(first, pltpu.roll(t, n - ROPE_FREQS, 1), pltpu.roll(t, ROPE_FREQS, 1))


def _inproj_kernel(x_ref, mod_ref, g1_ref, w_ref, qg_ref, kg_ref, bd_ref, cos_ref, sin_ref,
                   q_ref, k_ref, v_ref, gq_ref, gk_ref, gv_ref, gg_ref, lr_ref):
    h = _rms_mod(x_ref[...], g1_ref[...], mod_ref[0:1, :], mod_ref[1:2, :]).astype(BF16)
    cos = cos_ref[...]
    sin = sin_ref[...]

    def head_norm_rope(t, g, bd, reps):
        ms = _dot((t * t).astype(BF16), bd)
        tn = t * lax.rsqrt(ms + EPS) * g
        c = jnp.concatenate([cos] * reps, axis=1) if reps > 1 else cos
        s = jnp.concatenate([sin] * reps, axis=1) if reps > 1 else sin
        return tn * c + _swap16(tn) * s

    aq = _dot(h, w_ref[:, 0:512])
    q = head_norm_rope(aq, qg_ref[...], bd_ref[...], 4) * (HEAD_DIM ** -0.5 * LOG2E)
    q_ref[...] = q.astype(BF16)
    akv = _dot(h, w_ref[:, 512:768])
    k = head_norm_rope(akv[:, 0:128], kg_ref[...], bd_ref[0:128, 0:128], 1)
    k_ref[...] = k.astype(BF16)
    v_ref[...] = akv[:, 128:256].astype(BF16)
    gqk = _dot(h, w_ref[:, 768:1280])
    gq_ref[...] = (gqk[:, 0:256] * (GLA_DK ** -0.5)).astype(BF16)
    gk_ref[...] = gqk[:, 256:512].astype(BF16)
    gv_ref[...] = _dot(h, w_ref[:, 1280:1792]).astype(BF16)
    gg_ref[...] = _dot(h, w_ref[:, 1792:2304]).astype(BF16)
    lr_ref[...] = _dot(h, w_ref[:, 2304:2432])


def _input_projection(x, mod3, g1, w_in_r, qg, kg2, bd512, cos_t, sin_t, tm):
    B, L, _ = x.shape
    full = lambda shape: pl.BlockSpec(shape, lambda b, i: (0,) * len(shape))
    tok = lambda w: pl.BlockSpec((None, tm, w), lambda b, i: (b, i, 0))
    widths = (ATTN_WIDTH, KV_WIDTH, KV_WIDTH, GLA_QK_WIDTH, GLA_QK_WIDTH, GLA_WIDTH, GLA_WIDTH, LANES)
    dtypes = (BF16,) * 7 + (F32,)
    return pl.pallas_call(
        _inproj_kernel,
        grid=(B, L // tm),
        in_specs=[tok(D_MODEL),
                  pl.BlockSpec((None, 6, D_MODEL), lambda b, i: (b, 0, 0)),
                  full(g1.shape), full(w_in_r.shape), full(qg.shape), full(kg2.shape),
                  full(bd512.shape),
                  pl.BlockSpec((tm, LANES), lambda b, i: (i, 0)),
                  pl.BlockSpec((tm, LANES), lambda b, i: (i, 0))],
        out_specs=[tok(w) for w in widths],
        out_shape=[jax.ShapeDtypeStruct((B, L, w), dt) for w, dt in zip(widths, dtypes)],
        compiler_params=_cparams(("arbitrary", "arbitrary")),
        name="input_projection",
    )(x, mod3, g1, w_in_r, qg, kg2, bd512, cos_t, sin_t)


def _attn_kernel(sink_ref, q_ref, kp_ref, ko_ref, kn_ref, vp_ref, vo_ref, vn_ref,
                 kc_ref, vc_ref, o_ref):
    i = pl.program_id(1)
    ni = pl.num_programs(1)
    nsub = q_ref.shape[0] // BLOCK
    ncol = ATTN_WIDTH // LANES
    win = 3 * BLOCK
    ucol = 4
    half_rows = ucol * BLOCK
    k_win = jnp.concatenate([kp_ref[...], ko_ref[...], kn_ref[...]], axis=0)
    v_win = jnp.concatenate([vp_ref[...], vo_ref[...], vn_ref[...]], axis=0)
    k_ctx = kc_ref[...]
    lo_w = _lane_lo(v_win.shape)
    lo_c = _lane_lo(vc_ref.shape)
    lo_q = _lane_lo((BLOCK, LANES))
    zero = jnp.zeros((), BF16)
    one = jnp.ones((), BF16)
    v0_c, v0_w = jnp.where(lo_c, vc_ref[...], one), jnp.where(lo_w, v_win, one)
    v1_c, v1_w = jnp.where(lo_c, one, vc_ref[...]), jnp.where(lo_w, one, v_win)
    qi = lax.broadcasted_iota(jnp.int32, (half_rows, BLOCK), 0) % BLOCK
    kj = lax.broadcasted_iota(jnp.int32, (half_rows, BLOCK), 1)
    no_prev = jnp.where(i > 0, 0, BLOCK)
    no_next = jnp.where(i < ni - 1, 0, BLOCK)
    row_head = lax.broadcasted_iota(jnp.int32, (half_rows, 1), 0) // BLOCK
    lo_o = _lane_lo((half_rows, LANES))
    for t in range(nsub):
        rows = slice(t * BLOCK, (t + 1) * BLOCK)
        keys = slice(t * BLOCK, t * BLOCK + win)
        cols = [q_ref[rows, c * LANES:(c + 1) * LANES] for c in range(ncol)]
        prev_ok = kj >= qi + (no_prev if t == 0 else 0)
        next_ok = kj <= qi - (no_next if t == nsub - 1 else 0)
        for p in range(ncol // ucol):
            outs = []
            for g, (vv_c, vv_w) in enumerate(((v0_c, v0_w), (v1_c, v1_w))):
                qs = jnp.concatenate([jnp.where(lo_q, qc, zero) if g == 0 else jnp.where(lo_q, zero, qc)
                                      for qc in cols[ucol * p:ucol * (p + 1)]], axis=0)
                head = g * ncol + ucol * p
                sink_g = jnp.full((half_rows, 1), sink_ref[head + ucol - 1], F32)
                for j in range(ucol - 2, -1, -1):
                    sink_g = jnp.where(row_head <= j, sink_ref[head + j], sink_g)
                sink_g = sink_g * LOG2E
                s_c = _dot_nt(qs, k_ctx)
                s_w = _dot_nt(qs, k_win[keys])
                s_p = jnp.where(prev_ok, s_w[:, 0:BLOCK], NEG_INF)
                s_o = s_w[:, BLOCK:2 * BLOCK]
                s_n = jnp.where(next_ok, s_w[:, 2 * BLOCK:win], NEG_INF)
                m = jnp.maximum(jnp.maximum(jnp.max(s_c, axis=-1, keepdims=True),
                                            jnp.max(jnp.maximum(jnp.maximum(s_p, s_o), s_n),
                                                    axis=-1, keepdims=True)), sink_g)
                e_c = jnp.exp2(s_c - m).astype(BF16)
                e_w = jnp.concatenate([jnp.exp2(s_p - m), jnp.exp2(s_o - m), jnp.exp2(s_n - m)],
                                      axis=1).astype(BF16)
                acc = _dot(e_c, vv_c) + _dot(e_w, vv_w[keys])
                outs.append(acc / (pltpu.roll(acc, HEAD_DIM, 1) + jnp.exp2(sink_g - m)))
            o = jnp.where(lo_o, outs[0], outs[1]).astype(BF16)
            for j in range(ucol):
                c = ucol * p + j
                o_ref[rows, c * LANES:(c + 1) * LANES] = o[j * BLOCK:(j + 1) * BLOCK]


def _window_attention(sink, q, k, v, kc, vc, tq):
    B, L, _ = q.shape
    nb = L // BLOCK
    nsub = tq // BLOCK
    n_ctx = kc.shape[1]
    prev = pl.BlockSpec((None, BLOCK, KV_WIDTH), lambda b, n: (b, jnp.maximum(n * nsub - 1, 0), 0))
    own = pl.BlockSpec((None, tq, KV_WIDTH), lambda b, n: (b, n, 0))
    nxt = pl.BlockSpec((None, BLOCK, KV_WIDTH),
                       lambda b, n: (b, jnp.minimum((n + 1) * nsub, nb - 1), 0))
    cspec = pl.BlockSpec((None, n_ctx, KV_WIDTH), lambda b, n: (b, 0, 0))
    return pl.pallas_call(
        _attn_kernel,
        grid=(B, L // tq),
        in_specs=[pl.BlockSpec(memory_space=pltpu.SMEM),
                  pl.BlockSpec((None, tq, ATTN_WIDTH), lambda b, n: (b, n, 0)),
                  prev, own, nxt, prev, own, nxt, cspec, cspec],
        out_specs=pl.BlockSpec((None, tq, ATTN_WIDTH), lambda b, n: (b, n, 0)),
        out_shape=jax.ShapeDtypeStruct((B, L, ATTN_WIDTH), BF16),
        compiler_params=_cparams(("arbitrary", "arbitrary")),
        name="window_attention",
    )(sink, q, k, k, k, v, v, v, kc, vc)


SUPER = 256
CH_PER = SUPER // CHUNK
HALF = 128


def _dot_split(m, parts):
    return _dot(m, parts[0]) + _dot(m, parts[1])


def _gla_kernel(gq_ref, gk_ref, gv_ref, gg_ref, lr_ref, wd_ref, bdec_ref, gn_ref, sf_ref, sb_ref,
                o_ref, la_ref, oi_ref, qg_ref, kv_ref, sb16_ref, dec_ref, st_ref):
    L = gq_ref.shape[0]
    nsuper = L // SUPER
    nchunk = L // CHUNK
    half = CHUNK // 2
    la_ref[...] = _log_decay(lr_ref[...], wd_ref[...], bdec_ref[...])

    r = lax.broadcasted_iota(jnp.int32, (SUPER, SUPER), 0)
    cidx = lax.broadcasted_iota(jnp.int32, (SUPER, SUPER), 1)
    same = (r // CHUNK) == (cidx // CHUNK)
    pr = r % CHUNK
    pc = cidx % CHUNK
    one = jnp.float32(1.0)
    zero = jnp.float32(0.0)
    in_f = jnp.where(pc <= pr, one, zero)
    in_b = jnp.where(pc >= pr, one, zero)
    ref_f = jnp.where(pc < half, one, zero)
    ref_b = jnp.where(pc >= half, one, zero)
    m1_f = jnp.where(same, in_f - ref_f, zero).astype(BF16)
    m1_b = jnp.where(same, in_b - ref_b, zero).astype(BF16)
    rh = lax.broadcasted_iota(jnp.int32, (HALF, 2 * HALF), 0)
    ch = lax.broadcasted_iota(jnp.int32, (HALF, 2 * HALF), 1) % HALF
    same_h = (rh // CHUNK) == (ch // CHUNK)
    mask_f = jnp.where(same_h, jnp.where(ch % CHUNK <= rh % CHUNK, one, zero), zero) > 0.5
    mask_b = jnp.where(same_h, jnp.where(ch % CHUNK >= rh % CHUNK, one, zero), zero) > 0.5
    rr = lax.broadcasted_iota(jnp.int32, (2 * CH_PER, SUPER), 0)
    rc = lax.broadcasted_iota(jnp.int32, (2 * CH_PER, SUPER), 1)
    in_chunk = jnp.where((rc // CHUNK) == (rr % CH_PER), one, zero)
    first = jnp.where((rc % CHUNK) < half, 1, 0)
    is_tot = jnp.where(rr >= CH_PER, 1, 0)
    rs_f = (in_chunk * jnp.where(first != is_tot, one, zero)).astype(BF16)
    rs_b = (in_chunk * jnp.where(first == is_tot, one, zero)).astype(BF16)
    lo_h = _lane_lo((HALF, LANES))
    zero_blk = jnp.zeros((HALF, GLA_DV), BF16)

    def phase1(s, carry):
        r0 = pl.multiple_of(s * SUPER, SUPER)
        rows = pl.ds(r0, SUPER)
        q = gq_ref[rows, :].astype(F32)
        k = gk_ref[rows, :].astype(F32)
        qes, kes, kds = [], [], []
        for d, (m1, rs) in enumerate(((m1_f, rs_f), (m1_b, rs_b))):
            parts = _split2(la_ref[rows, d * GLA_QK_WIDTH:(d + 1) * GLA_QK_WIDTH])
            x1 = _dot_split(m1, parts)
            erow = jnp.exp(_dot_split(rs, parts))
            dec = erow[0:CH_PER] * erow[CH_PER:2 * CH_PER]
            dec_ref[d, s] = jnp.concatenate([dec, dec], axis=0)
            qe = q * jnp.exp(x1)
            ke = k * jnp.exp(-x1)
            qg_parts, kd_parts = [], []
            for j in range(CH_PER):
                rj = slice(j * CHUNK, (j + 1) * CHUNK)
                qg_parts.append(qe[rj] * erow[j:j + 1])
                kd_parts.append(ke[rj] * erow[CH_PER + j:CH_PER + j + 1])
            qg_ref[d, rows, :] = jnp.concatenate(qg_parts, axis=0).astype(BF16)
            qes.append(qe.astype(BF16))
            kes.append(ke)
            kds.append(jnp.concatenate(kd_parts, axis=0).astype(BF16))
        for c in range(2):
            cl = slice(c * LANES, (c + 1) * LANES)
            vpair = gv_ref[rows, 2 * c * GLA_DV:(2 * c + 2) * GLA_DV]
            for blk in range(SUPER // HALF):
                rb = slice(blk * HALF, (blk + 1) * HALF)
                vbd = jnp.concatenate(
                    [jnp.concatenate([vpair[rb, 0:GLA_DV], zero_blk], axis=1),
                     jnp.concatenate([zero_blk, vpair[rb, GLA_DV:2 * GLA_DV]], axis=1)], axis=0)
                o2 = None
                for d, mask in enumerate((mask_f, mask_b)):
                    ke_cb = kes[d][rb, cl]
                    ke_st = jnp.concatenate([jnp.where(lo_h, ke_cb, zero),
                                             jnp.where(lo_h, zero, ke_cb)], axis=0).astype(BF16)
                    a = _dot_nt(qes[d][rb, cl], ke_st)
                    o = _dot(jnp.where(mask, a, zero).astype(BF16), vbd)
                    o2 = o if o2 is None else o2 + o
                oi_ref[pl.ds(r0 + blk * HALF, HALF), 2 * c * GLA_DV:(2 * c + 2) * GLA_DV] = o2
            for d in range(2):
                for j in range(CH_PER):
                    rj = slice(j * CHUNK, (j + 1) * CHUNK)
                    t = _dot_tn(kds[d][rj, cl], vpair[rj])
                    kv = jnp.concatenate([t[0:GLA_DK, 0:GLA_DV], t[GLA_DK:, GLA_DV:]], axis=0)
                    kv_ref[d, c, s * CH_PER + j] = kv.T
        return carry

    lax.fori_loop(0, nsuper, phase1, 0)

    st_ref[0] = sf_ref[0]
    st_ref[1] = sf_ref[1]
    st_ref[2] = sb_ref[0]
    st_ref[3] = sb_ref[1]

    def phase2(n, carry):
        for d in range(2):
            idx = n if d == 0 else nchunk - 1 - n
            dec = dec_ref[d, idx // CH_PER, pl.ds(idx % CH_PER, 1), :]
            for c in range(2):
                st = st_ref[2 * d + c]
                sb16_ref[c, idx, :, d * LANES:(d + 1) * LANES] = st.astype(BF16)
                st_ref[2 * d + c] = dec[:, c * LANES:(c + 1) * LANES] * st + kv_ref[d, c, idx]
        return carry

    lax.fori_loop(0, nchunk, phase2, 0)

    lo64 = _lane_lo((CHUNK, 2 * LANES))
    zero_b = jnp.zeros((), BF16)

    def phase3(s, carry):
        r0 = pl.multiple_of(s * SUPER, SUPER)
        rows = pl.ds(r0, SUPER)
        inter = [[None] * CH_PER for _ in range(GLA_HEADS)]
        for j in range(CH_PER):
            rj = pl.ds(r0 + j * CHUNK, CHUNK)
            for c in range(2):
                qg_c = jnp.concatenate([qg_ref[d, rj, c * LANES:(c + 1) * LANES] for d in range(2)], axis=1)
                lhs = jnp.concatenate([jnp.where(lo64, qg_c, zero_b),
                                       jnp.where(lo64, zero_b, qg_c)], axis=0)
                t = _dot_nt(lhs, sb16_ref[c, s * CH_PER + j])
                for hh in range(2):
                    inter[2 * c + hh][j] = t[hh * CHUNK:(hh + 1) * CHUNK]
        for hd in range(GLA_HEADS):
            cl = slice(hd * GLA_DV, (hd + 1) * GLA_DV)
            o = oi_ref[rows, cl] + jnp.concatenate(inter[hd], axis=0)
            y = o * lax.rsqrt(jnp.mean(o * o, axis=-1, keepdims=True) + EPS) * gn_ref[:, cl]
            g = gg_ref[rows, cl].astype(F32)
            o_ref[rows, cl] = (y * (g * jax.nn.sigmoid(g))).astype(BF16)
        return carry

    lax.fori_loop(0, nsuper, phase3, 0)


def _gla(gq, gk, gv, gg, lr, wd, bdec, gn, s_f, s_b):
    B, L, _ = gq.shape
    nchunk = L // CHUNK
    full = lambda shape: pl.BlockSpec(shape, lambda b: (0,) * len(shape))
    tok = lambda w: pl.BlockSpec((None, L, w), lambda b: (b, 0, 0))
    st_spec = pl.BlockSpec((None, 2, LANES, GLA_DV), lambda b: (b, 0, 0, 0))
    return pl.pallas_call(
        _gla_kernel,
        grid=(B,),
        in_specs=[tok(GLA_QK_WIDTH), tok(GLA_QK_WIDTH), tok(GLA_WIDTH), tok(GLA_WIDTH), tok(LANES),
                  full(wd.shape), full(bdec.shape), full(gn.shape), st_spec, st_spec],
        out_specs=tok(GLA_WIDTH),
        out_shape=jax.ShapeDtypeStruct((B, L, GLA_WIDTH), BF16),
        scratch_shapes=[pltpu.VMEM((L, 2 * GLA_QK_WIDTH), F32),
                        pltpu.VMEM((L, GLA_WIDTH), F32),
                        pltpu.VMEM((2, L, GLA_QK_WIDTH), BF16),
                        pltpu.VMEM((2, 2, nchunk, GLA_DV, LANES), F32),
                        pltpu.VMEM((2, nchunk, GLA_DV, 2 * LANES), BF16),
                        pltpu.VMEM((2, L // SUPER, 2 * CH_PER, GLA_QK_WIDTH), F32),
                        pltpu.VMEM((4, GLA_DV, LANES), F32)],
        compiler_params=_cparams(("arbitrary",)),
        name="gla_bidirectional",
    )(gq, gk, gv, gg, lr, wd, bdec, gn, s_f, s_b)


def _outproj_kernel(attn_ref, gla_ref, x_ref, mod_ref, w_ref, g2_ref, wr_ref,
                    x1_ref, h2_ref, afft_ref):
    y = _dot(attn_ref[...], w_ref[0:ATTN_WIDTH, :]) + _dot(gla_ref[...], w_ref[ATTN_WIDTH:, :])
    x1 = x_ref[...] + mod_ref[2:3, :] * y
    x1_ref[...] = x1
    h2 = _rms_mod(x1, g2_ref[...], mod_ref[3:4, :], mod_ref[4:5, :]).astype(BF16)
    h2_ref[...] = h2
    logits = _dot_nt(wr_ref[...], h2)
    e = jnp.exp(logits - jnp.max(logits, axis=0, keepdims=True))
    afft_ref[...] = e / jnp.sum(e, axis=0, keepdims=True)


def _output_projection(attn, gla, x, mod3, w_out, g2, w_router, tm):
    B, L, _ = x.shape
    full = lambda shape: pl.BlockSpec(shape, lambda b, i: (0,) * len(shape))
    tok = lambda w: pl.BlockSpec((None, tm, w), lambda b, i: (b, i, 0))
    return pl.pallas_call(
        _outproj_kernel,
        grid=(B, L // tm),
        in_specs=[tok(ATTN_WIDTH), tok(GLA_WIDTH), tok(D_MODEL),
                  pl.BlockSpec((None, 6, D_MODEL), lambda b, i: (b, 0, 0)),
                  full(w_out.shape), full(g2.shape), full(w_router.shape)],
        out_specs=[tok(D_MODEL), tok(D_MODEL),
                   pl.BlockSpec((None, N_EXPERTS, tm), lambda b, i: (b, 0, i))],
        out_shape=[jax.ShapeDtypeStruct((B, L, D_MODEL), F32),
                   jax.ShapeDtypeStruct((B, L, D_MODEL), BF16),
                   jax.ShapeDtypeStruct((B, N_EXPERTS, L), F32)],
        compiler_params=_cparams(("arbitrary", "arbitrary")),
        name="output_projection_router",
    )(attn, gla, x, mod3, w_out, g2, w_router)


def _topk_kernel(afft_ref, post_ref, pos_ref, gsel_ref, *, cap):
    nbatch, E, L = afft_ref.shape
    aff = afft_ref[...].reshape(nbatch * E, L)
    E = nbatch * E

    def search(i, thr):
        cand = thr | jnp.left_shift(jnp.int32(1), 30 - i)
        cnt = jnp.sum(jnp.where(aff >= pltpu.bitcast(cand, F32), 1.0, 0.0), axis=-1, keepdims=True)
        return jnp.where(cnt >= cap, cand, thr)

    thr_bits = lax.fori_loop(0, 31, search, jnp.zeros((E, 1), jnp.int32))
    thr = pltpu.bitcast(thr_bits, F32)
    above = aff > thr
    tie = aff == thr
    need = cap - jnp.sum(jnp.where(above, 1.0, 0.0), axis=-1, keepdims=True)

    upper = (lax.broadcasted_iota(jnp.int32, (LANES, LANES), 0)
             <= lax.broadcasted_iota(jnp.int32, (LANES, LANES), 1)).astype(BF16)

    def prefix(mask):
        parts = []
        run = jnp.zeros((E, 1), F32)
        for j in range(L // LANES):
            blk = jnp.where(mask[:, j * LANES:(j + 1) * LANES], 1.0, 0.0).astype(BF16)
            loc = _dot(blk, upper) + run
            parts.append(loc)
            run = loc[:, LANES - 1:LANES]
        return jnp.concatenate(parts, axis=1)

    tie_rank = prefix(tie)
    sel = above | (tie & (tie_rank <= need))
    slot = prefix(sel).astype(jnp.int32) - 1
    post = jnp.where(sel, slot, -1)
    gsel = jnp.where(sel, aff, 0.0)
    ne = E // nbatch
    pad_i = jnp.full((LANES - ne, L), -1, jnp.int32)
    pad_f = jnp.zeros((LANES - ne, L), F32)
    for bb in range(nbatch):
        rows = slice(bb * ne, (bb + 1) * ne)
        post_ref[bb] = post[rows]
        pos_ref[bb] = jnp.concatenate([post[rows], pad_i], axis=0).T
        gsel_ref[bb] = jnp.concatenate([gsel[rows], pad_f], axis=0).T


def _expert_choice(afft, cap, nbatch):
    B, E, L = afft.shape
    return pl.pallas_call(
        functools.partial(_topk_kernel, cap=cap),
        grid=(B // nbatch,),
        in_specs=[pl.BlockSpec((nbatch, E, L), lambda b: (b, 0, 0))],
        out_specs=[pl.BlockSpec((nbatch, E, L), lambda b: (b, 0, 0)),
                   pl.BlockSpec((nbatch, L, LANES), lambda b: (b, 0, 0)),
                   pl.BlockSpec((nbatch, L, LANES), lambda b: (b, 0, 0))],
        out_shape=[jax.ShapeDtypeStruct((B, E, L), jnp.int32),
                   jax.ShapeDtypeStruct((B, L, LANES), jnp.int32),
                   jax.ShapeDtypeStruct((B, L, LANES), F32)],
        compiler_params=_cparams(("arbitrary",)),
        name="expert_choice_topk",
    )(afft)


def _dispatch_kernel(post_ref, h2_ref, xs_ref, *, cap):
    ne = post_ref.shape[0]
    L = h2_ref.shape[0]
    slot = lax.broadcasted_iota(jnp.int32, (cap, L), 0)
    onehot = jnp.concatenate([jnp.where(post_ref[e] == slot, 1.0, 0.0).astype(BF16) for e in range(ne)],
                             axis=0)
    xs_ref[...] = _dot(onehot, h2_ref[...]).astype(BF16).reshape(ne, cap, D_MODEL)


def _dispatch(post4, h2, cap, ne):
    B, E, _, L = post4.shape
    return pl.pallas_call(
        functools.partial(_dispatch_kernel, cap=cap),
        grid=(B, E // ne),
        in_specs=[pl.BlockSpec((None, ne, 1, L), lambda b, e: (b, e, 0, 0)),
                  pl.BlockSpec((None, L, D_MODEL), lambda b, e: (b, 0, 0))],
        out_specs=pl.BlockSpec((None, ne, cap, D_MODEL), lambda b, e: (b, e, 0, 0)),
        out_shape=jax.ShapeDtypeStruct((B, E, cap, D_MODEL), BF16),
        compiler_params=_cparams(("arbitrary", "arbitrary")),
        name="moe_dispatch",
    )(post4, h2)


def _ffn_kernel(xs_ref, wg_ref, wu_ref, wd_ref, y_ref, wgb_ref, wub_ref, wdb_ref):
    nbatch, cap, d = xs_ref.shape

    @pl.when(pl.program_id(1) == 0)
    def _():
        wgb_ref[...] = wg_ref[...].astype(BF16)
        wub_ref[...] = wu_ref[...].astype(BF16)
        wdb_ref[...] = wd_ref[...].astype(BF16)

    xs = xs_ref[...].reshape(nbatch * cap, d)
    f = wg_ref.shape[1]
    half = f // 2
    acc = None
    for j in range(2):
        cols = slice(j * half, (j + 1) * half)
        g = _dot(xs, wgb_ref[:, cols])
        u = _dot(xs, wub_ref[:, cols])
        hid = (g * jax.nn.sigmoid(g) * u).astype(BF16)
        part = _dot(hid, wdb_ref[cols, :])
        acc = part if acc is None else acc + part
    y_ref[...] = acc.astype(BF16).reshape(nbatch, cap, d)


def _expert_ffn(xs, w_gate, w_up, w_down, nbatch):
    B, E, cap, d = xs.shape
    f = w_gate.shape[2]
    tok = pl.BlockSpec((nbatch, None, cap, d), lambda e, b: (b, e, 0, 0))
    return pl.pallas_call(
        _ffn_kernel,
        grid=(E, B // nbatch),
        in_specs=[tok,
                  pl.BlockSpec((None, d, f), lambda e, b: (e, 0, 0)),
                  pl.BlockSpec((None, d, f), lambda e, b: (e, 0, 0)),
                  pl.BlockSpec((None, f, d), lambda e, b: (e, 0, 0))],
        out_specs=tok,
        out_shape=jax.ShapeDtypeStruct((B, E, cap, d), BF16),
        scratch_shapes=[pltpu.VMEM((d, f), BF16), pltpu.VMEM((d, f), BF16), pltpu.VMEM((f, d), BF16)],
        compiler_params=_cparams(("arbitrary", "arbitrary")),
        name="expert_swiglu",
    )(xs, w_gate, w_up, w_down)


def _combine_kernel(pos_ref, gsel_ref, y_ref, x1_ref, mod_ref, o_ref, acc_ref, *, cap):
    tt = pos_ref.shape[0]
    slot = lax.broadcasted_iota(jnp.int32, (tt, cap), 1)
    for e in range(N_EXPERTS):
        onehot = jnp.where(pos_ref[:, e:e + 1] == slot, 1.0, 0.0).astype(BF16)
        part = gsel_ref[:, e:e + 1] * _dot(onehot, y_ref[e])
        if e == 0:
            acc_ref[...] = part
        else:
            acc_ref[...] += part
    o_ref[...] = x1_ref[...] + mod_ref[5:6, :] * acc_ref[...]


def _combine(pos, gsel, y, x1, mod3, cap, tt):
    B, L, _ = x1.shape
    tok = lambda w: pl.BlockSpec((None, tt, w), lambda b, i: (b, i, 0))
    return pl.pallas_call(
        functools.partial(_combine_kernel, cap=cap),
        grid=(B, L // tt),
        in_specs=[tok(LANES), tok(LANES),
                  pl.BlockSpec((None, N_EXPERTS, cap, D_MODEL), lambda b, i: (b, 0, 0, 0)),
                  tok(D_MODEL),
                  pl.BlockSpec((None, 6, D_MODEL), lambda b, i: (b, 0, 0))],
        out_specs=tok(D_MODEL),
        out_shape=jax.ShapeDtypeStruct((B, L, D_MODEL), F32),
        scratch_shapes=[pltpu.VMEM((tt, D_MODEL), F32)],
        compiler_params=_cparams(("arbitrary", "arbitrary")),
        name="moe_combine",
    )(pos, gsel, y, x1, mod3)


def _rope_tables(L):
    inv = ROPE_BASE ** (-jnp.arange(ROPE_FREQS, dtype=F32) / ROPE_FREQS)
    pos = jnp.arange(L)
    row = (pos // GRID_W).astype(F32)[:, None] * inv
    col = (pos % GRID_W).astype(F32)[:, None] * inv
    cos = jnp.concatenate([jnp.cos(row), jnp.cos(row), jnp.cos(col), jnp.cos(col)], axis=1)
    sin = jnp.concatenate([-jnp.sin(row), jnp.sin(row), -jnp.sin(col), jnp.sin(col)], axis=1)
    return jnp.tile(cos, (1, 2)), jnp.tile(sin, (1, 2))


def _head_mean_matrix(n):
    idx = np.arange(n) // HEAD_DIM
    return jnp.asarray((idx[:, None] == idx[None, :]).astype(np.float32) / HEAD_DIM, dtype=BF16)


def kernel(x, c, ctx, c_ctx, w_mod, b_mod, norm1_g, w_in, q_norm_g, k_norm_g, attn_sink,
           w_decay_fwd, b_decay_fwd, w_decay_bwd, b_decay_bwd, gla_norm_g, w_out, norm2_g,
           w_router, w_e_gate, w_e_up, w_e_down):
    B, L, D = x.shape
    cap = CAPACITY_FACTOR * L // N_EXPERTS
    layer = 0

    rows = ((B + 1 + 7) // 8) * 8
    cc = jnp.concatenate([c, c_ctx[None, :], jnp.zeros((rows - B - 1, D), F32)], axis=0)
    mod_all = _modulation(cc, w_mod[layer], b_mod[layer])
    mod3 = mod_all[:B].reshape(B, 6, D)
    modc = mod_all[B].reshape(6, D)

    w = w_in[layer]
    o = np.cumsum([0, ATTN_WIDTH, KV_WIDTH, KV_WIDTH, GLA_QK_WIDTH, GLA_QK_WIDTH,
                   GLA_WIDTH, GLA_WIDTH, GATE_RANK, GATE_RANK])
    w_lr = jnp.concatenate([w[:, o[7]:o[9]]] * 3 + [jnp.zeros((D, LANES - 6 * GATE_RANK), F32)], axis=1)
    head_order = np.arange(N_Q_HEADS).reshape(N_KV_HEADS, -1).T.reshape(-1)
    attn_perm = (head_order[:, None] * HEAD_DIM + np.arange(HEAD_DIM)[None, :]).reshape(-1)
    w_in_r = jnp.concatenate([w[:, attn_perm], w[:, o[1]:o[7]], w_lr], axis=1).astype(BF16)
    w_out_r = jnp.concatenate([w_out[layer][attn_perm], w_out[layer][ATTN_WIDTH:]], axis=0).astype(BF16)
    w_ctx = jnp.concatenate([w[:, o[1]:o[3]], w[:, o[4]:o[6]], w_lr], axis=1).astype(BF16)
    wd2 = jnp.zeros((2 * GATE_RANK, 2 * GLA_QK_WIDTH), F32)
    wd2 = wd2.at[0:GATE_RANK, 0:GLA_QK_WIDTH].set(w_decay_fwd[layer])
    wd2 = wd2.at[GATE_RANK:, GLA_QK_WIDTH:].set(w_decay_bwd[layer])
    wd_hi = wd2.astype(BF16)
    wd_lo = (wd2 - wd_hi.astype(F32)).astype(BF16)
    wd = jnp.concatenate([wd_hi, wd_hi, wd_lo,
                          jnp.zeros((LANES - 6 * GATE_RANK, 2 * GLA_QK_WIDTH), BF16)], axis=0)
    bdec = jnp.concatenate([b_decay_fwd[layer], b_decay_bwd[layer]])[None, :]
    g1 = norm1_g[layer][None, :]
    g2 = norm2_g[layer][None, :]
    qg = jnp.tile(q_norm_g[layer], N_Q_HEADS)[None, :]
    kg2 = jnp.tile(k_norm_g[layer], N_KV_HEADS)[None, :]
    gn = jnp.tile(gla_norm_g[layer], GLA_HEADS)[None, :]
    bd512 = _head_mean_matrix(ATTN_WIDTH)
    bd128 = _head_mean_matrix(KV_WIDTH)
    cos_t, sin_t = _rope_tables(L)
    w_router_t = w_router[layer].T.astype(BF16)

    kc, vc, s_f, s_b = _context_side(ctx, modc, g1, w_ctx, kg2, bd128, wd, bdec)
    q, k, v, gq, gk, gv, gg, lr = _input_projection(
        x, mod3, g1, w_in_r, qg, kg2, bd512, cos_t, sin_t, tm=1024)
    attn = _window_attention(attn_sink[layer], q, k, v, kc, vc, tq=512)
    gla = _gla(gq, gk, gv, gg, lr, wd, bdec, gn, s_f, s_b)
    x1, h2, afft = _output_projection(attn, gla, x, mod3, w_out_r, g2, w_router_t, tm=1024)
    post, pos, gsel = _expert_choice(afft, cap, nbatch=4)
    xs = _dispatch(post.reshape(B, N_EXPERTS, 1, L), h2, cap, ne=4)
    y = _expert_ffn(xs, w_e_gate[layer], w_e_up[layer], w_e_down[layer], nbatch=4)
    return _combine(pos, gsel, y, x1, mod3, cap, tt=1024)
```

```python
import functools

import jax
import jax.numpy as jnp
import numpy as np
from jax import lax
from jax.experimental import pallas as pl
from jax.experimental.pallas import tpu as pltpu
from jax.experimental.pallas import tpu_sc as plsc

D_MODEL = 1024
GRID_W = 64
HEAD_DIM = 64
N_Q_HEADS = 8
N_KV_HEADS = 2
BLOCK = 128
ROPE_FREQS = 16
ROPE_BASE = 10000.0
GLA_HEADS = 4
GLA_DV = 128
GLA_DK = 64
GATE_RANK = 16
GATE_NORMALIZER = 16.0
CHUNK = 64
N_EXPERTS = 16
CAPACITY_FACTOR = 2
ATTN_WIDTH = 512
KV_WIDTH = 128
GLA_QK_WIDTH = 256
GLA_WIDTH = 512
EPS = 1e-6
NEG_INF = -1e30
LOG2E = 1.4426950408889634

LANES = 128
VMEM_LIMIT = 56 * 1024 * 1024

F32 = jnp.float32
BF16 = jnp.bfloat16
HI = lax.Precision.HIGHEST


def _cparams(sem):
    return pltpu.CompilerParams(dimension_semantics=sem, vmem_limit_bytes=VMEM_LIMIT)


def _dot(a, b):
    return jnp.dot(a, b, preferred_element_type=F32)


def _dot_hi(a, b):
    return jnp.dot(a, b, preferred_element_type=F32, precision=HI)


def _dot_nt(a, b):
    return lax.dot_general(a, b, (((1,), (1,)), ((), ())), preferred_element_type=F32)


def _dot_tn(a, b, precision=None):
    return lax.dot_general(a, b, (((0,), (0,)), ((), ())), preferred_element_type=F32,
                           precision=precision)


def _split2(t):
    hi = t.astype(BF16)
    lo = (t - hi.astype(F32)).astype(BF16)
    return hi, lo


def _rms_mod(t, g, shift, scale):
    y = t * lax.rsqrt(jnp.mean(t * t, axis=-1, keepdims=True) + EPS)
    return (y * g) * (1.0 + scale) + shift


def _log_decay(lr, wd3, bias):
    hi = lr.astype(BF16)
    lo = (lr - hi.astype(F32)).astype(BF16)
    lane = lax.broadcasted_iota(jnp.int32, lr.shape, 1)
    second = (lane >= 2 * GATE_RANK) & (lane < 4 * GATE_RANK)
    z = _dot(jnp.where(second, lo, hi), wd3) + bias
    return (jnp.minimum(z, 0.0) - jnp.log(1.0 + jnp.exp(-jnp.abs(z)))) * (1.0 / GATE_NORMALIZER)


def _lane_lo(shape):
    return (lax.broadcasted_iota(jnp.int32, shape, len(shape) - 1) % LANES) < HEAD_DIM


def _mod_kernel(c_ref, w_ref, b_ref, o_ref):
    c = c_ref[...]
    s = c * jax.nn.sigmoid(c)
    o_ref[...] = _dot_hi(s, w_ref[...]) + b_ref[...]


def _modulation(cc, w_mod, b_mod):
    m = cc.shape[0]
    n = w_mod.shape[1]
    tn = 1024
    return pl.pallas_call(
        _mod_kernel,
        grid=(n // tn,),
        in_specs=[pl.BlockSpec((m, D_MODEL), lambda j: (0, 0)),
                  pl.BlockSpec((D_MODEL, tn), lambda j: (0, j)),
                  pl.BlockSpec((1, tn), lambda j: (0, j))],
        out_specs=pl.BlockSpec((m, tn), lambda j: (0, j)),
        out_shape=jax.ShapeDtypeStruct((m, n), F32),
        compiler_params=_cparams(("arbitrary",)),
        name="adaln_mod",
    )(cc, w_mod, b_mod.reshape(1, n))


def _ctx_kernel(ctx_ref, mod_ref, g1_ref, w_ref, kg_ref, bd_ref, wd_ref, bdec_ref,
                kc_ref, vc_ref, sf_ref, sb_ref):
    n = ctx_ref.shape[0]
    h = _rms_mod(ctx_ref[...], g1_ref[...], mod_ref[0:1, :], mod_ref[1:2, :]).astype(BF16)
    pc = _dot(h, w_ref[...])
    ak = pc[:, 0:128]
    av = pc[:, 128:256]
    gk = pc[:, 256:512]
    gv = pc[:, 512:1024].astype(BF16)
    lr = pc[:, 1024:1152]
    sq_hi, sq_lo = _split2(ak * ak)
    ms = _dot(sq_hi, bd_ref[...]) + _dot(sq_lo, bd_ref[...])
    kn = ak * lax.rsqrt(ms + EPS) * kg_ref[...]
    kc_ref[...] = kn.astype(BF16)
    vc_ref[...] = av.astype(BF16)
    la = _log_decay(lr, wd_ref[...], bdec_ref[...])
    r = lax.broadcasted_iota(jnp.int32, (n, n), 0)
    cidx = lax.broadcasted_iota(jnp.int32, (n, n), 1)
    after = (cidx > r).astype(F32)
    before = (cidx < r).astype(F32)
    w_f = jnp.exp(_dot_hi(after, la[:, 0:256]))
    w_b = jnp.exp(_dot_hi(before, la[:, 256:512]))
    lo = _lane_lo((n, LANES))
    for w, out in ((w_f, sf_ref), (w_b, sb_ref)):
        kw = gk * w
        for c in range(2):
            kwc = kw[:, c * LANES:(c + 1) * LANES]
            k_lo = jnp.where(lo, kwc, 0.0).astype(BF16)
            k_hi = jnp.where(lo, 0.0, kwc).astype(BF16)
            v0 = gv[:, (2 * c) * GLA_DV:(2 * c + 1) * GLA_DV]
            v1 = gv[:, (2 * c + 1) * GLA_DV:(2 * c + 2) * GLA_DV]
            out[c] = _dot_tn(v0, k_lo) + _dot_tn(v1, k_hi)


def _context_side(ctx, modc, g1, w_ctx, kg2, bd128, wd, bdec):
    B, n, _ = ctx.shape
    full = lambda shape: pl.BlockSpec(shape, lambda b: (0,) * len(shape))
    kv_spec = pl.BlockSpec((None, n, KV_WIDTH), lambda b: (b, 0, 0))
    st_spec = pl.BlockSpec((None, 2, LANES, GLA_DV), lambda b: (b, 0, 0, 0))
    kv_shape = jax.ShapeDtypeStruct((B, n, KV_WIDTH), BF16)
    st_shape = jax.ShapeDtypeStruct((B, 2, LANES, GLA_DV), F32)
    return pl.pallas_call(
        _ctx_kernel,
        grid=(B,),
        in_specs=[pl.BlockSpec((None, n, D_MODEL), lambda b: (b, 0, 0)),
                  full(modc.shape), full(g1.shape), full(w_ctx.shape), full(kg2.shape),
                  full(bd128.shape), full(wd.shape), full(bdec.shape)],
        out_specs=[kv_spec, kv_spec, st_spec, st_spec],
        out_shape=[kv_shape, kv_shape, st_shape, st_shape],
        compiler_params=_cparams(("arbitrary",)),
        name="context_side",
    )(ctx, modc, g1, w_ctx, kg2, bd128, wd, bdec)


def _swap16(t):
    n = t.shape[1]
    first = (lax.broadcasted_iota(jnp.int32, t.shape, 1) % 32) < ROPE_FREQS
    return jnp.where(first, pltpu.roll(t, n - ROPE_FREQS, 1), pltpu.roll(t, ROPE_FREQS, 1))


def _inproj_kernel(x_ref, mod_ref, g1_ref, w_ref, qg_ref, kg_ref, bd_ref, cos_ref, sin_ref,
                   q_ref, k_ref, v_ref, gq_ref, gk_ref, gv_ref, gg_ref, lr_ref):
    h = _rms_mod(x_ref[...], g1_ref[...], mod_ref[0:1, :], mod_ref[1:2, :]).astype(BF16)
    cos = cos_ref[...]
    sin = sin_ref[...]

    def head_norm_rope(t, g, bd, reps):
        ms = _dot((t * t).astype(BF16), bd)
        tn = t * lax.rsqrt(ms + EPS) * g
        c = jnp.concatenate([cos] * reps, axis=1) if reps > 1 else cos
        s = jnp.concatenate([sin] * reps, axis=1) if reps > 1 else sin
        return tn * c + _swap16(tn) * s

    aq = _dot(h, w_ref[:, 0:512])
    q = head_norm_rope(aq, qg_ref[...], bd_ref[...], 4) * (HEAD_DIM ** -0.5 * LOG2E)
    q_ref[...] = q.astype(BF16)
    akv = _dot(h, w_ref[:, 512:768])
    k = head_norm_rope(akv[:, 0:128], kg_ref[...], bd_ref[0:128, 0:128], 1)
    k_ref[...] = k.astype(BF16)
    v_ref[...] = akv[:, 128:256].astype(BF16)
    gqk = _dot(h, w_ref[:, 768:1280])
    gq_ref[...] = (gqk[:, 0:256] * (GLA_DK ** -0.5)).astype(BF16)
    gk_ref[...] = gqk[:, 256:512].astype(BF16)
    gv_ref[...] = _dot(h, w_ref[:, 1280:1792]).astype(BF16)
    gg_ref[...] = _dot(h, w_ref[:, 1792:2304]).astype(BF16)
    lr_ref[...] = _dot(h, w_ref[:, 2304:2432])


def _input_projection(x, mod3, g1, w_in_r, qg, kg2, bd512, cos_t, sin_t, tm):
    B, L, _ = x.shape
    full = lambda shape: pl.BlockSpec(shape, lambda b, i: (0,) * len(shape))
    tok = lambda w: pl.BlockSpec((None, tm, w), lambda b, i: (b, i, 0))
    widths = (ATTN_WIDTH, KV_WIDTH, KV_WIDTH, GLA_QK_WIDTH, GLA_QK_WIDTH, GLA_WIDTH, GLA_WIDTH, LANES)
    dtypes = (BF16,) * 7 + (F32,)
    return pl.pallas_call(
        _inproj_kernel,
        grid=(B, L // tm),
        in_specs=[tok(D_MODEL),
                  pl.BlockSpec((None, 6, D_MODEL), lambda b, i: (b, 0, 0)),
                  full(g1.shape), full(w_in_r.shape), full(qg.shape), full(kg2.shape),
                  full(bd512.shape),
                  pl.BlockSpec((tm, LANES), lambda b, i: (i, 0)),
                  pl.BlockSpec((tm, LANES), lambda b, i: (i, 0))],
        out_specs=[tok(w) for w in widths],
        out_shape=[jax.ShapeDtypeStruct((B, L, w), dt) for w, dt in zip(widths, dtypes)],
        compiler_params=_cparams(("arbitrary", "arbitrary")),
        name="input_projection",
    )(x, mod3, g1, w_in_r, qg, kg2, bd512, cos_t, sin_t)


def _attn_kernel(sink_ref, q_ref, kp_ref, ko_ref, kn_ref, vp_ref, vo_ref, vn_ref,
                 kc_ref, vc_ref, o_ref):
    i = pl.program_id(1)
    ni = pl.num_programs(1)
    nsub = q_ref.shape[0] // BLOCK
    ncol = ATTN_WIDTH // LANES
    win = 3 * BLOCK
    ucol = 4
    half_rows = ucol * BLOCK
    k_win = jnp.concatenate([kp_ref[...], ko_ref[...], kn_ref[...]], axis=0)
    v_win = jnp.concatenate([vp_ref[...], vo_ref[...], vn_ref[...]], axis=0)
    k_ctx = kc_ref[...]
    lo_w = _lane_lo(v_win.shape)
    lo_c = _lane_lo(vc_ref.shape)
    lo_q = _lane_lo((BLOCK, LANES))
    zero = jnp.zeros((), BF16)
    one = jnp.ones((), BF16)
    v0_c, v0_w = jnp.where(lo_c, vc_ref[...], one), jnp.where(lo_w, v_win, one)
    v1_c, v1_w = jnp.where(lo_c, one, vc_ref[...]), jnp.where(lo_w, one, v_win)
    qi = lax.broadcasted_iota(jnp.int32, (half_rows, BLOCK), 0) % BLOCK
    kj = lax.broadcasted_iota(jnp.int32, (half_rows, BLOCK), 1)
    no_prev = jnp.where(i > 0, 0, BLOCK)
    no_next = jnp.where(i < ni - 1, 0, BLOCK)
    row_head = lax.broadcasted_iota(jnp.int32, (half_rows, 1), 0) // BLOCK
    lo_o = _lane_lo((half_rows, LANES))
    for t in range(nsub):
        rows = slice(t * BLOCK, (t + 1) * BLOCK)
        keys = slice(t * BLOCK, t * BLOCK + win)
        cols = [q_ref[rows, c * LANES:(c + 1) * LANES] for c in range(ncol)]
        prev_ok = kj >= qi + (no_prev if t == 0 else 0)
        next_ok = kj <= qi - (no_next if t == nsub - 1 else 0)
        for p in range(ncol // ucol):
            outs = []
            for g, (vv_c, vv_w) in enumerate(((v0_c, v0_w), (v1_c, v1_w))):
                qs = jnp.concatenate([jnp.where(lo_q, qc, zero) if g == 0 else jnp.where(lo_q, zero, qc)
                                      for qc in cols[ucol * p:ucol * (p + 1)]], axis=0)
                head = g * ncol + ucol * p
                sink_g = jnp.full((half_rows, 1), sink_ref[head + ucol - 1], F32)
                for j in range(ucol - 2, -1, -1):
                    sink_g = jnp.where(row_head <= j, sink_ref[head + j], sink_g)
                sink_g = sink_g * LOG2E
                s_c = _dot_nt(qs, k_ctx)
                s_w = _dot_nt(qs, k_win[keys])
                s_p = jnp.where(prev_ok, s_w[:, 0:BLOCK], NEG_INF)
                s_o = s_w[:, BLOCK:2 * BLOCK]
                s_n = jnp.where(next_ok, s_w[:, 2 * BLOCK:win], NEG_INF)
                m = jnp.maximum(jnp.maximum(jnp.max(s_c, axis=-1, keepdims=True),
                                            jnp.max(jnp.maximum(jnp.maximum(s_p, s_o), s_n),
                                                    axis=-1, keepdims=True)), sink_g)
                e_c = jnp.exp2(s_c - m).astype(BF16)
                e_w = jnp.concatenate([jnp.exp2(s_p - m), jnp.exp2(s_o - m), jnp.exp2(s_n - m)],
                                      axis=1).astype(BF16)
                acc = _dot(e_c, vv_c) + _dot(e_w, vv_w[keys])
                outs.append(acc / (pltpu.roll(acc, HEAD_DIM, 1) + jnp.exp2(sink_g - m)))
            o = jnp.where(lo_o, outs[0], outs[1]).astype(BF16)
            for j in range(ucol):
                c = ucol * p + j
                o_ref[rows, c * LANES:(c + 1) * LANES] = o[j * BLOCK:(j + 1) * BLOCK]


def _window_attention(sink, q, k, v, kc, vc, tq):
    B, L, _ = q.shape
    nb = L // BLOCK
    nsub = tq // BLOCK
    n_ctx = kc.shape[1]
    prev = pl.BlockSpec((None, BLOCK, KV_WIDTH), lambda b, n: (b, jnp.maximum(n * nsub - 1, 0), 0))
    own = pl.BlockSpec((None, tq, KV_WIDTH), lambda b, n: (b, n, 0))
    nxt = pl.BlockSpec((None, BLOCK, KV_WIDTH),
                       lambda b, n: (b, jnp.minimum((n + 1) * nsub, nb - 1), 0))
    cspec = pl.BlockSpec((None, n_ctx, KV_WIDTH), lambda b, n: (b, 0, 0))
    return pl.pallas_call(
        _attn_kernel,
        grid=(B, L // tq),
        in_specs=[pl.BlockSpec(memory_space=pltpu.SMEM),
                  pl.BlockSpec((None, tq, ATTN_WIDTH), lambda b, n: (b, n, 0)),
                  prev, own, nxt, prev, own, nxt, cspec, cspec],
        out_specs=pl.BlockSpec((None, tq, ATTN_WIDTH), lambda b, n: (b, n, 0)),
        out_shape=jax.ShapeDtypeStruct((B, L, ATTN_WIDTH), BF16),
        compiler_params=_cparams(("arbitrary", "arbitrary")),
        name="window_attention",
    )(sink, q, k, k, k, v, v, v, kc, vc)


SUPER = 256
CH_PER = SUPER // CHUNK
HALF = 128


def _dot_split(m, parts):
    return _dot(m, parts[0]) + _dot(m, parts[1])


def _gla_kernel(gq_ref, gk_ref, gv_ref, gg_ref, lr_ref, wd_ref, bdec_ref, gn_ref, sf_ref, sb_ref,
                o_ref, la_ref, oi_ref, qg_ref, kv_ref, sb16_ref, dec_ref, st_ref):
    L = gq_ref.shape[0]
    nsuper = L // SUPER
    nchunk = L // CHUNK
    half = CHUNK // 2
    la_ref[...] = _log_decay(lr_ref[...], wd_ref[...], bdec_ref[...])

    r = lax.broadcasted_iota(jnp.int32, (SUPER, SUPER), 0)
    cidx = lax.broadcasted_iota(jnp.int32, (SUPER, SUPER), 1)
    same = (r // CHUNK) == (cidx // CHUNK)
    pr = r % CHUNK
    pc = cidx % CHUNK
    one = jnp.float32(1.0)
    zero = jnp.float32(0.0)
    in_f = jnp.where(pc <= pr, one, zero)
    in_b = jnp.where(pc >= pr, one, zero)
    ref_f = jnp.where(pc < half, one, zero)
    ref_b = jnp.where(pc >= half, one, zero)
    m1_f = jnp.where(same, in_f - ref_f, zero).astype(BF16)
    m1_b = jnp.where(same, in_b - ref_b, zero).astype(BF16)
    rh = lax.broadcasted_iota(jnp.int32, (HALF, 2 * HALF), 0)
    ch = lax.broadcasted_iota(jnp.int32, (HALF, 2 * HALF), 1) % HALF
    same_h = (rh // CHUNK) == (ch // CHUNK)
    mask_f = jnp.where(same_h, jnp.where(ch % CHUNK <= rh % CHUNK, one, zero), zero) > 0.5
    mask_b = jnp.where(same_h, jnp.where(ch % CHUNK >= rh % CHUNK, one, zero), zero) > 0.5
    rr = lax.broadcasted_iota(jnp.int32, (2 * CH_PER, SUPER), 0)
    rc = lax.broadcasted_iota(jnp.int32, (2 * CH_PER, SUPER), 1)
    in_chunk = jnp.where((rc // CHUNK) == (rr % CH_PER), one, zero)
    first = jnp.where((rc % CHUNK) < half, 1, 0)
    is_tot = jnp.where(rr >= CH_PER, 1, 0)
    rs_f = (in_chunk * jnp.where(first != is_tot, one, zero)).astype(BF16)
    rs_b = (in_chunk * jnp.where(first == is_tot, one, zero)).astype(BF16)
    lo_h = _lane_lo((HALF, LANES))
    zero_blk = jnp.zeros((HALF, GLA_DV), BF16)

    def phase1(s, carry):
        r0 = pl.multiple_of(s * SUPER, SUPER)
        rows = pl.ds(r0, SUPER)
        q = gq_ref[rows, :].astype(F32)
        k = gk_ref[rows, :].astype(F32)
        qes, kes, kds = [], [], []
        for d, (m1, rs) in enumerate(((m1_f, rs_f), (m1_b, rs_b))):
            parts = _split2(la_ref[rows, d * GLA_QK_WIDTH:(d + 1) * GLA_QK_WIDTH])
            x1 = _dot_split(m1, parts)
            erow = jnp.exp(_dot_split(rs, parts))
            dec = erow[0:CH_PER] * erow[CH_PER:2 * CH_PER]
            dec_ref[d, s] = jnp.concatenate([dec, dec], axis=0)
            qe = q * jnp.exp(x1)
            ke = k * jnp.exp(-x1)
            qg_parts, kd_parts = [], []
            for j in range(CH_PER):
                rj = slice(j * CHUNK, (j + 1) * CHUNK)
                qg_parts.append(qe[rj] * erow[j:j + 1])
                kd_parts.append(ke[rj] * erow[CH_PER + j:CH_PER + j + 1])
            qg_ref[d, rows, :] = jnp.concatenate(qg_parts, axis=0).astype(BF16)
            qes.append(qe.astype(BF16))
            kes.append(ke)
            kds.append(jnp.concatenate(kd_parts, axis=0).astype(BF16))
        for c in range(2):
            cl = slice(c * LANES, (c + 1) * LANES)
            vpair = gv_ref[rows, 2 * c * GLA_DV:(2 * c + 2) * GLA_DV]
            for blk in range(SUPER // HALF):
                rb = slice(blk * HALF, (blk + 1) * HALF)
                vbd = jnp.concatenate(
                    [jnp.concatenate([vpair[rb, 0:GLA_DV], zero_blk], axis=1),
                     jnp.concatenate([zero_blk, vpair[rb, GLA_DV:2 * GLA_DV]], axis=1)], axis=0)
                o2 = None
                for d, mask in enumerate((mask_f, mask_b)):
                    ke_cb = kes[d][rb, cl]
                    ke_st = jnp.concatenate([jnp.where(lo_h, ke_cb, zero),
                                             jnp.where(lo_h, zero, ke_cb)], axis=0).astype(BF16)
                    a = _dot_nt(qes[d][rb, cl], ke_st)
                    o = _dot(jnp.where(mask, a, zero).astype(BF16), vbd)
                    o2 = o if o2 is None else o2 + o
                oi_ref[pl.ds(r0 + blk * HALF, HALF), 2 * c * GLA_DV:(2 * c + 2) * GLA_DV] = o2
            for d in range(2):
                for j in range(CH_PER):
                    rj = slice(j * CHUNK, (j + 1) * CHUNK)
                    t = _dot_tn(kds[d][rj, cl], vpair[rj])
                    kv = jnp.concatenate([t[0:GLA_DK, 0:GLA_DV], t[GLA_DK:, GLA_DV:]], axis=0)
                    kv_ref[d, c, s * CH_PER + j] = kv.T
        return carry

    lax.fori_loop(0, nsuper, phase1, 0)

    st_ref[0] = sf_ref[0]
    st_ref[1] = sf_ref[1]
    st_ref[2] = sb_ref[0]
    st_ref[3] = sb_ref[1]

    def phase2(n, carry):
        for d in range(2):
            idx = n if d == 0 else nchunk - 1 - n
            dec = dec_ref[d, idx // CH_PER, pl.ds(idx % CH_PER, 1), :]
            for c in range(2):
                st = st_ref[2 * d + c]
                sb16_ref[c, idx, :, d * LANES:(d + 1) * LANES] = st.astype(BF16)
                st_ref[2 * d + c] = dec[:, c * LANES:(c + 1) * LANES] * st + kv_ref[d, c, idx]
        return carry

    lax.fori_loop(0, nchunk, phase2, 0)

    lo64 = _lane_lo((CHUNK, 2 * LANES))
    zero_b = jnp.zeros((), BF16)

    def phase3(s, carry):
        r0 = pl.multiple_of(s * SUPER, SUPER)
        rows = pl.ds(r0, SUPER)
        inter = [[None] * CH_PER for _ in range(GLA_HEADS)]
        for j in range(CH_PER):
            rj = pl.ds(r0 + j * CHUNK, CHUNK)
            for c in range(2):
                qg_c = jnp.concatenate([qg_ref[d, rj, c * LANES:(c + 1) * LANES] for d in range(2)], axis=1)
                lhs = jnp.concatenate([jnp.where(lo64, qg_c, zero_b),
                                       jnp.where(lo64, zero_b, qg_c)], axis=0)
                t = _dot_nt(lhs, sb16_ref[c, s * CH_PER + j])
                for hh in range(2):
                    inter[2 * c + hh][j] = t[hh * CHUNK:(hh + 1) * CHUNK]
        for hd in range(GLA_HEADS):
            cl = slice(hd * GLA_DV, (hd + 1) * GLA_DV)
            o = oi_ref[rows, cl] + jnp.concatenate(inter[hd], axis=0)
            y = o * lax.rsqrt(jnp.mean(o * o, axis=-1, keepdims=True) + EPS) * gn_ref[:, cl]
            g = gg_ref[rows, cl].astype(F32)
            o_ref[rows, cl] = (y * (g * jax.nn.sigmoid(g))).astype(BF16)
        return carry

    lax.fori_loop(0, nsuper, phase3, 0)


def _gla(gq, gk, gv, gg, lr, wd, bdec, gn, s_f, s_b):
    B, L, _ = gq.shape
    nchunk = L // CHUNK
    full = lambda shape: pl.BlockSpec(shape, lambda b: (0,) * len(shape))
    tok = lambda w: pl.BlockSpec((None, L, w), lambda b: (b, 0, 0))
    st_spec = pl.BlockSpec((None, 2, LANES, GLA_DV), lambda b: (b, 0, 0, 0))
    return pl.pallas_call(
        _gla_kernel,
        grid=(B,),
        in_specs=[tok(GLA_QK_WIDTH), tok(GLA_QK_WIDTH), tok(GLA_WIDTH), tok(GLA_WIDTH), tok(LANES),
                  full(wd.shape), full(bdec.shape), full(gn.shape), st_spec, st_spec],
        out_specs=tok(GLA_WIDTH),
        out_shape=jax.ShapeDtypeStruct((B, L, GLA_WIDTH), BF16),
        scratch_shapes=[pltpu.VMEM((L, 2 * GLA_QK_WIDTH), F32),
                        pltpu.VMEM((L, GLA_WIDTH), F32),
                        pltpu.VMEM((2, L, GLA_QK_WIDTH), BF16),
                        pltpu.VMEM((2, 2, nchunk, GLA_DV, LANES), F32),
                        pltpu.VMEM((2, nchunk, GLA_DV, 2 * LANES), BF16),
                        pltpu.VMEM((2, L // SUPER, 2 * CH_PER, GLA_QK_WIDTH), F32),
                        pltpu.VMEM((4, GLA_DV, LANES), F32)],
        compiler_params=_cparams(("arbitrary",)),
        name="gla_bidirectional",
    )(gq, gk, gv, gg, lr, wd, bdec, gn, s_f, s_b)


def _outproj_kernel(attn_ref, gla_ref, x_ref, mod_ref, w_ref, g2_ref, wr_ref,
                    x1_ref, h2_ref, afft_ref):
    y = _dot(attn_ref[...], w_ref[0:ATTN_WIDTH, :]) + _dot(gla_ref[...], w_ref[ATTN_WIDTH:, :])
    x1 = x_ref[...] + mod_ref[2:3, :] * y
    x1_ref[...] = x1
    h2 = _rms_mod(x1, g2_ref[...], mod_ref[3:4, :], mod_ref[4:5, :]).astype(BF16)
    half = D_MODEL // 2
    hi = pltpu.bitcast(h2[:, 0:half].astype(F32), jnp.uint32)
    lo = pltpu.bitcast(h2[:, half:].astype(F32), jnp.uint32)
    h2_ref[...] = pltpu.bitcast(hi | (lo >> 16), F32)
    logits = _dot_nt(wr_ref[...], h2)
    e = jnp.exp(logits - jnp.max(logits, axis=0, keepdims=True))
    afft_ref[...] = e / jnp.sum(e, axis=0, keepdims=True)


def _output_projection(attn, gla, x, mod3, w_out, g2, w_router, tm):
    B, L, _ = x.shape
    full = lambda shape: pl.BlockSpec(shape, lambda b, i: (0,) * len(shape))
    tok = lambda w: pl.BlockSpec((None, tm, w), lambda b, i: (b, i, 0))
    return pl.pallas_call(
        _outproj_kernel,
        grid=(B, L // tm),
        in_specs=[tok(ATTN_WIDTH), tok(GLA_WIDTH), tok(D_MODEL),
                  pl.BlockSpec((None, 6, D_MODEL), lambda b, i: (b, 0, 0)),
                  full(w_out.shape), full(g2.shape), full(w_router.shape)],
        out_specs=[tok(D_MODEL), tok(D_MODEL // 2),
                   pl.BlockSpec((None, N_EXPERTS, tm), lambda b, i: (b, 0, i))],
        out_shape=[jax.ShapeDtypeStruct((B, L, D_MODEL), F32),
                   jax.ShapeDtypeStruct((B, L, D_MODEL // 2), F32),
                   jax.ShapeDtypeStruct((B, N_EXPERTS, L), F32)],
        compiler_params=_cparams(("arbitrary", "arbitrary")),
        name="output_projection_router",
    )(attn, gla, x, mod3, w_out, g2, w_router)


def _topk_kernel(afft_ref, post_ref, pos_ref, gsel_ref, *, cap):
    nbatch, E, L = afft_ref.shape
    aff = afft_ref[...].reshape(nbatch * E, L)
    E = nbatch * E

    def search(i, thr):
        cand = thr | jnp.left_shift(jnp.int32(1), 30 - i)
        cnt = jnp.sum(jnp.where(aff >= pltpu.bitcast(cand, F32), 1.0, 0.0), axis=-1, keepdims=True)
        return jnp.where(cnt >= cap, cand, thr)

    thr_bits = lax.fori_loop(0, 31, search, jnp.zeros((E, 1), jnp.int32))
    thr = pltpu.bitcast(thr_bits, F32)
    above = aff > thr
    tie = aff == thr
    need = cap - jnp.sum(jnp.where(above, 1.0, 0.0), axis=-1, keepdims=True)

    upper = (lax.broadcasted_iota(jnp.int32, (LANES, LANES), 0)
             <= lax.broadcasted_iota(jnp.int32, (LANES, LANES), 1)).astype(BF16)

    def prefix(mask):
        parts = []
        run = jnp.zeros((E, 1), F32)
        for j in range(L // LANES):
            blk = jnp.where(mask[:, j * LANES:(j + 1) * LANES], 1.0, 0.0).astype(BF16)
            loc = _dot(blk, upper) + run
            parts.append(loc)
            run = loc[:, LANES - 1:LANES]
        return jnp.concatenate(parts, axis=1)

    tie_rank = prefix(tie)
    sel = above | (tie & (tie_rank <= need))
    slot = prefix(sel).astype(jnp.int32) - 1
    post = jnp.where(sel, slot, -1)
    gsel = jnp.where(sel, aff, 0.0)
    ne = E // nbatch
    pad_i = jnp.full((LANES - ne, L), -1, jnp.int32)
    pad_f = jnp.zeros((LANES - ne, L), F32)
    for bb in range(nbatch):
        rows = slice(bb * ne, (bb + 1) * ne)
        post_ref[bb] = post[rows]
        pos_ref[bb] = jnp.concatenate([post[rows], pad_i], axis=0).T
        gsel_ref[bb] = jnp.concatenate([gsel[rows], pad_f], axis=0).T


def _expert_choice(afft, cap, nbatch):
    B, E, L = afft.shape
    return pl.pallas_call(
        functools.partial(_topk_kernel, cap=cap),
        grid=(B // nbatch,),
        in_specs=[pl.BlockSpec((nbatch, E, L), lambda b: (b, 0, 0))],
        out_specs=[pl.BlockSpec((nbatch, E, L), lambda b: (b, 0, 0)),
                   pl.BlockSpec((nbatch, L, LANES), lambda b: (b, 0, 0)),
                   pl.BlockSpec((nbatch, L, LANES), lambda b: (b, 0, 0))],
        out_shape=[jax.ShapeDtypeStruct((B, E, L), jnp.int32),
                   jax.ShapeDtypeStruct((B, L, LANES), jnp.int32),
                   jax.ShapeDtypeStruct((B, L, LANES), F32)],
        compiler_params=_cparams(("arbitrary",)),
        name="expert_choice_topk",
    )(afft)


SC_CORES = 2
SC_SUBCORES = 16
SC_LANES = 16
SC_WINDOW = 128


def _dispatch(rows, slots, cap):
    n_pair, L = slots.shape
    W = rows.shape[1]
    per_worker = n_pair // (SC_CORES * SC_SUBCORES)
    n_win = cap // SC_WINDOW
    mesh = plsc.VectorSubcoreMesh(core_axis_name="c", subcore_axis_name="s",
                                  num_cores=SC_CORES, num_subcores=SC_SUBCORES)

    def body(rows_hbm, slots_hbm, out_hbm, slot_v, *scratch):
        idx_v, buf_v, sem = scratch[:n_win], scratch[n_win], scratch[n_win + 1]
        worker = lax.axis_index("s") * SC_CORES + lax.axis_index("c")

        @pl.loop(0, per_worker)
        def _(p):
            pair = worker * per_worker + p
            first_tok = (pair // N_EXPERTS) * L
            pltpu.sync_copy(slots_hbm.at[pair], slot_v)

            @pl.loop(0, L // SC_LANES)
            def _(i):
                v = slot_v[pl.ds(i * SC_LANES, SC_LANES)]
                tok = lax.iota(jnp.int32, SC_LANES) + (i * SC_LANES + first_tok)
                for w in range(n_win):
                    in_win = (v >= w * SC_WINDOW) & (v < (w + 1) * SC_WINDOW)
                    plsc.store_scatter(idx_v[w], [v - w * SC_WINDOW], tok, mask=in_win)

            for w in range(n_win):
                pltpu.async_copy(rows_hbm.at[idx_v[w]], buf_v, sem).wait()
                pltpu.sync_copy(buf_v, out_hbm.at[pl.ds(pair * cap + w * SC_WINDOW, SC_WINDOW)])

    return pl.kernel(
        body,
        out_type=jax.ShapeDtypeStruct((n_pair * cap, W), rows.dtype),
        mesh=mesh,
        scratch_types=[pltpu.VMEM((L,), jnp.int32)]
        + [pltpu.VMEM((SC_WINDOW,), jnp.int32) for _ in range(n_win)]
        + [pltpu.VMEM((SC_WINDOW, W), rows.dtype), pltpu.SemaphoreType.DMA],
        compiler_params=pltpu.CompilerParams(needs_layout_passes=False),
        name="moe_dispatch_gather",
    )(rows, slots)


def _ffn_kernel(xs_ref, wg_ref, wu_ref, wd_ref, y_ref, wgb_ref, wub_ref, wdb_ref):
    nbatch, cap, dw = xs_ref.shape
    d = 2 * dw

    @pl.when(pl.program_id(1) == 0)
    def _():
        wgb_ref[...] = wg_ref[...].astype(BF16)
        wub_ref[...] = wu_ref[...].astype(BF16)
        wdb_ref[...] = wd_ref[...].astype(BF16)

    words = pltpu.bitcast(xs_ref[...].reshape(nbatch * cap, dw), jnp.uint32)
    xs = jnp.concatenate([pltpu.bitcast(words & jnp.uint32(0xFFFF0000), F32).astype(BF16),
                          pltpu.bitcast(words << 16, F32).astype(BF16)], axis=1)
    f = wg_ref.shape[1]
    half = f // 2
    acc = None
    for j in range(2):
        cols = slice(j * half, (j + 1) * half)
        g = _dot(xs, wgb_ref[:, cols])
        u = _dot(xs, wub_ref[:, cols])
        hid = (g * jax.nn.sigmoid(g) * u).astype(BF16)
        part = _dot(hid, wdb_ref[cols, :])
        acc = part if acc is None else acc + part
    y_ref[...] = acc.astype(BF16).reshape(nbatch, cap, d)


def _expert_ffn(xs, w_gate, w_up, w_down, nbatch):
    B, E, cap, dw = xs.shape
    d, f = w_gate.shape[1:]
    tok = pl.BlockSpec((nbatch, None, cap, d), lambda e, b: (b, e, 0, 0))
    return pl.pallas_call(
        _ffn_kernel,
        grid=(E, B // nbatch),
        in_specs=[pl.BlockSpec((nbatch, None, cap, dw), lambda e, b: (b, e, 0, 0)),
                  pl.BlockSpec((None, d, f), lambda e, b: (e, 0, 0)),
                  pl.BlockSpec((None, d, f), lambda e, b: (e, 0, 0)),
                  pl.BlockSpec((None, f, d), lambda e, b: (e, 0, 0))],
        out_specs=tok,
        out_shape=jax.ShapeDtypeStruct((B, E, cap, d), BF16),
        scratch_shapes=[pltpu.VMEM((d, f), BF16), pltpu.VMEM((d, f), BF16), pltpu.VMEM((f, d), BF16)],
        compiler_params=_cparams(("arbitrary", "arbitrary")),
        name="expert_swiglu",
    )(xs, w_gate, w_up, w_down)


def _combine_kernel(pos_ref, gsel_ref, y_ref, x1_ref, mod_ref, o_ref, acc_ref, *, cap):
    tt = pos_ref.shape[0]
    slot = lax.broadcasted_iota(jnp.int32, (tt, cap), 1)
    for e in range(N_EXPERTS):
        onehot = jnp.where(pos_ref[:, e:e + 1] == slot, 1.0, 0.0).astype(BF16)
        part = gsel_ref[:, e:e + 1] * _dot(onehot, y_ref[e])
        if e == 0:
            acc_ref[...] = part
        else:
            acc_ref[...] += part
    o_ref[...] = x1_ref[...] + mod_ref[5:6, :] * acc_ref[...]


def _combine(pos, gsel, y, x1, mod3, cap, tt):
    B, L, _ = x1.shape
    tok = lambda w: pl.BlockSpec((None, tt, w), lambda b, i: (b, i, 0))
    return pl.pallas_call(
        functools.partial(_combine_kernel, cap=cap),
        grid=(B, L // tt),
        in_specs=[tok(LANES), tok(LANES),
                  pl.BlockSpec((None, N_EXPERTS, cap, D_MODEL), lambda b, i: (b, 0, 0, 0)),
                  tok(D_MODEL),
                  pl.BlockSpec((None, 6, D_MODEL), lambda b, i: (b, 0, 0))],
        out_specs=tok(D_MODEL),
        out_shape=jax.ShapeDtypeStruct((B, L, D_MODEL), F32),
        scratch_shapes=[pltpu.VMEM((tt, D_MODEL), F32)],
        compiler_params=_cparams(("arbitrary", "arbitrary")),
        name="moe_combine",
    )(pos, gsel, y, x1, mod3)


def _rope_tables(L):
    inv = ROPE_BASE ** (-jnp.arange(ROPE_FREQS, dtype=F32) / ROPE_FREQS)
    pos = jnp.arange(L)
    row = (pos // GRID_W).astype(F32)[:, None] * inv
    col = (pos % GRID_W).astype(F32)[:, None] * inv
    cos = jnp.concatenate([jnp.cos(row), jnp.cos(row), jnp.cos(col), jnp.cos(col)], axis=1)
    sin = jnp.concatenate([-jnp.sin(row), jnp.sin(row), -jnp.sin(col), jnp.sin(col)], axis=1)
    return jnp.tile(cos, (1, 2)), jnp.tile(sin, (1, 2))


def _head_mean_matrix(n):
    idx = np.arange(n) // HEAD_DIM
    return jnp.asarray((idx[:, None] == idx[None, :]).astype(np.float32) / HEAD_DIM, dtype=BF16)


def kernel(x, c, ctx, c_ctx, w_mod, b_mod, norm1_g, w_in, q_norm_g, k_norm_g, attn_sink,
           w_decay_fwd, b_decay_fwd, w_decay_bwd, b_decay_bwd, gla_norm_g, w_out, norm2_g,
           w_router, w_e_gate, w_e_up, w_e_down):
    B, L, D = x.shape
    cap = CAPACITY_FACTOR * L // N_EXPERTS
    layer = 0

    rows = ((B + 1 + 7) // 8) * 8
    cc = jnp.concatenate([c, c_ctx[None, :], jnp.zeros((rows - B - 1, D), F32)], axis=0)
    mod_all = _modulation(cc, w_mod[layer], b_mod[layer])
    mod3 = mod_all[:B].reshape(B, 6, D)
    modc = mod_all[B].reshape(6, D)

    w = w_in[layer]
    o = np.cumsum([0, ATTN_WIDTH, KV_WIDTH, KV_WIDTH, GLA_QK_WIDTH, GLA_QK_WIDTH,
                   GLA_WIDTH, GLA_WIDTH, GATE_RANK, GATE_RANK])
    w_lr = jnp.concatenate([w[:, o[7]:o[9]]] * 3 + [jnp.zeros((D, LANES - 6 * GATE_RANK), F32)], axis=1)
    head_order = np.arange(N_Q_HEADS).reshape(N_KV_HEADS, -1).T.reshape(-1)
    attn_perm = (head_order[:, None] * HEAD_DIM + np.arange(HEAD_DIM)[None, :]).reshape(-1)
    w_in_r = jnp.concatenate([w[:, attn_perm], w[:, o[1]:o[7]], w_lr], axis=1).astype(BF16)
    w_out_r = jnp.concatenate([w_out[layer][attn_perm], w_out[layer][ATTN_WIDTH:]], axis=0).astype(BF16)
    w_ctx = jnp.concatenate([w[:, o[1]:o[3]], w[:, o[4]:o[6]], w_lr], axis=1).astype(BF16)
    wd2 = jnp.zeros((2 * GATE_RANK, 2 * GLA_QK_WIDTH), F32)
    wd2 = wd2.at[0:GATE_RANK, 0:GLA_QK_WIDTH].set(w_decay_fwd[layer])
    wd2 = wd2.at[GATE_RANK:, GLA_QK_WIDTH:].set(w_decay_bwd[layer])
    wd_hi = wd2.astype(BF16)
    wd_lo = (wd2 - wd_hi.astype(F32)).astype(BF16)
    wd = jnp.concatenate([wd_hi, wd_hi, wd_lo,
                          jnp.zeros((LANES - 6 * GATE_RANK, 2 * GLA_QK_WIDTH), BF16)], axis=0)
    bdec = jnp.concatenate([b_decay_fwd[layer], b_decay_bwd[layer]])[None, :]
    g1 = norm1_g[layer][None, :]
    g2 = norm2_g[layer][None, :]
    qg = jnp.tile(q_norm_g[layer], N_Q_HEADS)[None, :]
    kg2 = jnp.tile(k_norm_g[layer], N_KV_HEADS)[None, :]
    gn = jnp.tile(gla_norm_g[layer], GLA_HEADS)[None, :]
    bd512 = _head_mean_matrix(ATTN_WIDTH)
    bd128 = _head_mean_matrix(KV_WIDTH)
    cos_t, sin_t = _rope_tables(L)
    w_router_t = w_router[layer].T.astype(BF16)

    kc, vc, s_f, s_b = _context_side(ctx, modc, g1, w_ctx, kg2, bd128, wd, bdec)
    q, k, v, gq, gk, gv, gg, lr = _input_projection(
        x, mod3, g1, w_in_r, qg, kg2, bd512, cos_t, sin_t, tm=1024)
    attn = _window_attention(attn_sink[layer], q, k, v, kc, vc, tq=512)
    gla = _gla(gq, gk, gv, gg, lr, wd, bdec, gn, s_f, s_b)
    x1, h2, afft = _output_projection(attn, gla, x, mod3, w_out_r, g2, w_router_t, tm=1024)
    post, pos, gsel = _expert_choice(afft, cap, nbatch=4)
    xs = _dispatch(h2.reshape(B * L, D // 2), post.reshape(B * N_EXPERTS, L), cap)
    xs = xs.reshape(B, N_EXPERTS, cap, D // 2)
    y = _expert_ffn(xs, w_e_gate[layer], w_e_up[layer], w_e_down[layer], nbatch=4)
    return _combine(pos, gsel, y, x1, mod3, cap, tt=1024)
```

```python
import functools

import jax
import jax.numpy as jnp
import numpy as np
from jax import lax
from jax.experimental import pallas as pl
from jax.experimental.pallas import tpu as pltpu
from jax.experimental.pallas import tpu_sc as plsc

D_MODEL = 1024
GRID_W = 64
HEAD_DIM = 64
N_Q_HEADS = 8
N_KV_HEADS = 2
BLOCK = 128
ROPE_FREQS = 16
ROPE_BASE = 10000.0
GLA_HEADS = 4
GLA_DV = 128
GLA_DK = 64
GATE_RANK = 16
GATE_NORMALIZER = 16.0
CHUNK = 64
N_EXPERTS = 16
CAPACITY_FACTOR = 2
ATTN_WIDTH = 512
KV_WIDTH = 128
GLA_QK_WIDTH = 256
GLA_WIDTH = 512
EPS = 1e-6
NEG_INF = -1e30
LOG2E = 1.4426950408889634

LANES = 128
VMEM_LIMIT = 56 * 1024 * 1024

F32 = jnp.float32
BF16 = jnp.bfloat16
HI = lax.Precision.HIGHEST


def _cparams(sem):
    return pltpu.CompilerParams(dimension_semantics=sem, vmem_limit_bytes=VMEM_LIMIT)


def _dot(a, b):
    return jnp.dot(a, b, preferred_element_type=F32)


def _dot_hi(a, b):
    return jnp.dot(a, b, preferred_element_type=F32, precision=HI)


def _dot_nt(a, b):
    return lax.dot_general(a, b, (((1,), (1,)), ((), ())), preferred_element_type=F32)


def _dot_tn(a, b, precision=None):
    return lax.dot_general(a, b, (((0,), (0,)), ((), ())), preferred_element_type=F32,
                           precision=precision)


def _split2(t):
    hi = t.astype(BF16)
    lo = (t - hi.astype(F32)).astype(BF16)
    return hi, lo


def _rms_mod(t, g, shift, scale):
    y = t * lax.rsqrt(jnp.mean(t * t, axis=-1, keepdims=True) + EPS)
    return (y * g) * (1.0 + scale) + shift


def _log_decay(lr, wd3, bias):
    hi = lr.astype(BF16)
    lo = (lr - hi.astype(F32)).astype(BF16)
    lane = lax.broadcasted_iota(jnp.int32, lr.shape, 1)
    second = (lane >= 2 * GATE_RANK) & (lane < 4 * GATE_RANK)
    z = _dot(jnp.where(second, lo, hi), wd3) + bias
    return (jnp.minimum(z, 0.0) - jnp.log(1.0 + jnp.exp(-jnp.abs(z)))) * (1.0 / GATE_NORMALIZER)


def _lane_lo(shape):
    return (lax.broadcasted_iota(jnp.int32, shape, len(shape) - 1) % LANES) < HEAD_DIM


def _mod_kernel(c_ref, w_ref, b_ref, o_ref):
    c = c_ref[...]
    s = c * jax.nn.sigmoid(c)
    o_ref[...] = _dot_hi(s, w_ref[...]) + b_ref[...]


def _modulation(cc, w_mod, b_mod):
    m = cc.shape[0]
    n = w_mod.shape[1]
    tn = 1024
    return pl.pallas_call(
        _mod_kernel,
        grid=(n // tn,),
        in_specs=[pl.BlockSpec((m, D_MODEL), lambda j: (0, 0)),
                  pl.BlockSpec((D_MODEL, tn), lambda j: (0, j)),
                  pl.BlockSpec((1, tn), lambda j: (0, j))],
        out_specs=pl.BlockSpec((m, tn), lambda j: (0, j)),
        out_shape=jax.ShapeDtypeStruct((m, n), F32),
        compiler_params=_cparams(("arbitrary",)),
        name="adaln_mod",
    )(cc, w_mod, b_mod.reshape(1, n))


def _ctx_kernel(ctx_ref, mod_ref, g1_ref, w_ref, kg_ref, bd_ref, wd_ref, bdec_ref,
                kc_ref, vc_ref, sf_ref, sb_ref):
    n = ctx_ref.shape[0]
    h = _rms_mod(ctx_ref[...], g1_ref[...], mod_ref[0:1, :], mod_ref[1:2, :]).astype(BF16)
    pc = _dot(h, w_ref[...])
    ak = pc[:, 0:128]
    av = pc[:, 128:256]
    gk = pc[:, 256:512]
    gv = pc[:, 512:1024].astype(BF16)
    lr = pc[:, 1024:1152]
    sq_hi, sq_lo = _split2(ak * ak)
    ms = _dot(sq_hi, bd_ref[...]) + _dot(sq_lo, bd_ref[...])
    kn = ak * lax.rsqrt(ms + EPS) * kg_ref[...]
    kc_ref[...] = kn.astype(BF16)
    vc_ref[...] = av.astype(BF16)
    la = _log_decay(lr, wd_ref[...], bdec_ref[...])
    r = lax.broadcasted_iota(jnp.int32, (n, n), 0)
    cidx = lax.broadcasted_iota(jnp.int32, (n, n), 1)
    after = (cidx > r).astype(F32)
    before = (cidx < r).astype(F32)
    w_f = jnp.exp(_dot_hi(after, la[:, 0:256]))
    w_b = jnp.exp(_dot_hi(before, la[:, 256:512]))
    lo = _lane_lo((n, LANES))
    for w, out in ((w_f, sf_ref), (w_b, sb_ref)):
        kw = gk * w
        for c in range(2):
            kwc = kw[:, c * LANES:(c + 1) * LANES]
            k_lo = jnp.where(lo, kwc, 0.0).astype(BF16)
            k_hi = jnp.where(lo, 0.0, kwc).astype(BF16)
            v0 = gv[:, (2 * c) * GLA_DV:(2 * c + 1) * GLA_DV]
            v1 = gv[:, (2 * c + 1) * GLA_DV:(2 * c + 2) * GLA_DV]
            out[c] = _dot_tn(v0, k_lo) + _dot_tn(v1, k_hi)


def _context_side(ctx, modc, g1, w_ctx, kg2, bd128, wd, bdec):
    B, n, _ = ctx.shape
    full = lambda shape: pl.BlockSpec(shape, lambda b: (0,) * len(shape))
    kv_spec = pl.BlockSpec((None, n, KV_WIDTH), lambda b: (b, 0, 0))
    st_spec = pl.BlockSpec((None, 2, LANES, GLA_DV), lambda b: (b, 0, 0, 0))
    kv_shape = jax.ShapeDtypeStruct((B, n, KV_WIDTH), BF16)
    st_shape = jax.ShapeDtypeStruct((B, 2, LANES, GLA_DV), F32)
    return pl.pallas_call(
        _ctx_kernel,
        grid=(B,),
        in_specs=[pl.BlockSpec((None, n, D_MODEL), lambda b: (b, 0, 0)),
                  full(modc.shape), full(g1.shape), full(w_ctx.shape), full(kg2.shape),
                  full(bd128.shape), full(wd.shape), full(bdec.shape)],
        out_specs=[kv_spec, kv_spec, st_spec, st_spec],
        out_shape=[kv_shape, kv_shape, st_shape, st_shape],
        compiler_params=_cparams(("arbitrary",)),
        name="context_side",
    )(ctx, modc, g1, w_ctx, kg2, bd128, wd, bdec)


def _swap16(t):
    n = t.shape[1]
    first = (lax.broadcasted_iota(jnp.int32, t.shape, 1) % 32) < ROPE_FREQS
    return jnp.where(first, pltpu.roll(t, n - ROPE_FREQS, 1), pltpu.roll(t, ROPE_FREQS, 1))


def _inproj_kernel(x_ref, mod_ref, g1_ref, w_ref, qg_ref, kg_ref, bd_ref, cos_ref, sin_ref,
                   q_ref, k_ref, v_ref, gq_ref, gk_ref, gv_ref, gg_ref, lr_ref):
    h = _rms_mod(x_ref[...], g1_ref[...], mod_ref[0:1, :], mod_ref[1:2, :]).astype(BF16)
    cos = cos_ref[...]
    sin = sin_ref[...]

    def head_norm_rope(t, g, bd, reps):
        ms = _dot((t * t).astype(BF16), bd)
        tn = t * lax.rsqrt(ms + EPS) * g
        c = jnp.concatenate([cos] * reps, axis=1) if reps > 1 else cos
        s = jnp.concatenate([sin] * reps, axis=1) if reps > 1 else sin
        return tn * c + _swap16(tn) * s

    aq = _dot(h, w_ref[:, 0:512])
    q = head_norm_rope(aq, qg_ref[...], bd_ref[...], 4) * (HEAD_DIM ** -0.5 * LOG2E)
    q_ref[...] = q.astype(BF16)
    akv = _dot(h, w_ref[:, 512:768])
    k = head_norm_rope(akv[:, 0:128], kg_ref[...], bd_ref[0:128, 0:128], 1)
    k_ref[...] = k.astype(BF16)
    v_ref[...] = akv[:, 128:256].astype(BF16)
    gqk = _dot(h, w_ref[:, 768:1280])
    gq_ref[...] = (gqk[:, 0:256] * (GLA_DK ** -0.5)).astype(BF16)
    gk_ref[...] = gqk[:, 256:512].astype(BF16)
    gv_ref[...] = _dot(h, w_ref[:, 1280:1792]).astype(BF16)
    gg_ref[...] = _dot(h, w_ref[:, 1792:2304]).astype(BF16)
    lr_ref[...] = _dot(h, w_ref[:, 2304:2432])


def _input_projection(x, mod3, g1, w_in_r, qg, kg2, bd512, cos_t, sin_t, tm):
    B, L, _ = x.shape
    full = lambda shape: pl.BlockSpec(shape, lambda b, i: (0,) * len(shape))
    tok = lambda w: pl.BlockSpec((None, tm, w), lambda b, i: (b, i, 0))
    widths = (ATTN_WIDTH, KV_WIDTH, KV_WIDTH, GLA_QK_WIDTH, GLA_QK_WIDTH, GLA_WIDTH, GLA_WIDTH, LANES)
    dtypes = (BF16,) * 7 + (F32,)
    return pl.pallas_call(
        _inproj_kernel,
        grid=(B, L // tm),
        in_specs=[tok(D_MODEL),
                  pl.BlockSpec((None, 6, D_MODEL), lambda b, i: (b, 0, 0)),
                  full(g1.shape), full(w_in_r.shape), full(qg.shape), full(kg2.shape),
                  full(bd512.shape),
                  pl.BlockSpec((tm, LANES), lambda b, i: (i, 0)),
                  pl.BlockSpec((tm, LANES), lambda b, i: (i, 0))],
        out_specs=[tok(w) for w in widths],
        out_shape=[jax.ShapeDtypeStruct((B, L, w), dt) for w, dt in zip(widths, dtypes)],
        compiler_params=_cparams(("arbitrary", "arbitrary")),
        name="input_projection",
    )(x, mod3, g1, w_in_r, qg, kg2, bd512, cos_t, sin_t)


def _attn_kernel(sink_ref, q_ref, kp_ref, ko_ref, kn_ref, vp_ref, vo_ref, vn_ref,
                 kc_ref, vc_ref, o_ref):
    i = pl.program_id(1)
    ni = pl.num_programs(1)
    nsub = q_ref.shape[0] // BLOCK
    ncol = ATTN_WIDTH // LANES
    win = 3 * BLOCK
    ucol = 4
    half_rows = ucol * BLOCK
    k_win = jnp.concatenate([kp_ref[...], ko_ref[...], kn_ref[...]], axis=0)
    v_win = jnp.concatenate([vp_ref[...], vo_ref[...], vn_ref[...]], axis=0)
    k_ctx = kc_ref[...]
    lo_w = _lane_lo(v_win.shape)
    lo_c = _lane_lo(vc_ref.shape)
    lo_q = _lane_lo((BLOCK, LANES))
    zero = jnp.zeros((), BF16)
    one = jnp.ones((), BF16)
    v0_c, v0_w = jnp.where(lo_c, vc_ref[...], one), jnp.where(lo_w, v_win, one)
    v1_c, v1_w = jnp.where(lo_c, one, vc_ref[...]), jnp.where(lo_w, one, v_win)
    qi = lax.broadcasted_iota(jnp.int32, (half_rows, BLOCK), 0) % BLOCK
    kj = lax.broadcasted_iota(jnp.int32, (half_rows, BLOCK), 1)
    no_prev = jnp.where(i > 0, 0, BLOCK)
    no_next = jnp.where(i < ni - 1, 0, BLOCK)
    row_head = lax.broadcasted_iota(jnp.int32, (half_rows, 1), 0) // BLOCK
    lo_o = _lane_lo((half_rows, LANES))
    for t in range(nsub):
        rows = slice(t * BLOCK, (t + 1) * BLOCK)
        keys = slice(t * BLOCK, t * BLOCK + win)
        cols = [q_ref[rows, c * LANES:(c + 1) * LANES] for c in range(ncol)]
        prev_ok = kj >= qi + (no_prev if t == 0 else 0)
        next_ok = kj <= qi - (no_next if t == nsub - 1 else 0)
        for p in range(ncol // ucol):
            outs = []
            for g, (vv_c, vv_w) in enumerate(((v0_c, v0_w), (v1_c, v1_w))):
                qs = jnp.concatenate([jnp.where(lo_q, qc, zero) if g == 0 else jnp.where(lo_q, zero, qc)
                                      for qc in cols[ucol * p:ucol * (p + 1)]], axis=0)
                head = g * ncol + ucol * p
                sink_g = jnp.full((half_rows, 1), sink_ref[head + ucol - 1], F32)
                for j in range(ucol - 2, -1, -1):
                    sink_g = jnp.where(row_head <= j, sink_ref[head + j], sink_g)
                sink_g = sink_g * LOG2E
                s_c = _dot_nt(qs, k_ctx)
                s_w = _dot_nt(qs, k_win[keys])
                s_p = jnp.where(prev_ok, s_w[:, 0:BLOCK], NEG_INF)
                s_o = s_w[:, BLOCK:2 * BLOCK]
                s_n = jnp.where(next_ok, s_w[:, 2 * BLOCK:win], NEG_INF)
                m = jnp.maximum(jnp.maximum(jnp.max(s_c, axis=-1, keepdims=True),
                                            jnp.max(jnp.maximum(jnp.maximum(s_p, s_o), s_n),
                                                    axis=-1, keepdims=True)), sink_g)
                e_c = jnp.exp2(s_c - m).astype(BF16)
                e_w = jnp.concatenate([jnp.exp2(s_p - m), jnp.exp2(s_o - m), jnp.exp2(s_n - m)],
                                      axis=1).astype(BF16)
                acc = _dot(e_c, vv_c) + _dot(e_w, vv_w[keys])
                outs.append(acc / (pltpu.roll(acc, HEAD_DIM, 1) + jnp.exp2(sink_g - m)))
            o = jnp.where(lo_o, outs[0], outs[1]).astype(BF16)
            for j in range(ucol):
                c = ucol * p + j
                o_ref[rows, c * LANES:(c + 1) * LANES] = o[j * BLOCK:(j + 1) * BLOCK]


def _window_attention(sink, q, k, v, kc, vc, tq):
    B, L, _ = q.shape
    nb = L // BLOCK
    nsub = tq // BLOCK
    n_ctx = kc.shape[1]
    prev = pl.BlockSpec((None, BLOCK, KV_WIDTH), lambda b, n: (b, jnp.maximum(n * nsub - 1, 0), 0))
    own = pl.BlockSpec((None, tq, KV_WIDTH), lambda b, n: (b, n, 0))
    nxt = pl.BlockSpec((None, BLOCK, KV_WIDTH),
                       lambda b, n: (b, jnp.minimum((n + 1) * nsub, nb - 1), 0))
    cspec = pl.BlockSpec((None, n_ctx, KV_WIDTH), lambda b, n: (b, 0, 0))
    return pl.pallas_call(
        _attn_kernel,
        grid=(B, L // tq),
        in_specs=[pl.BlockSpec(memory_space=pltpu.SMEM),
                  pl.BlockSpec((None, tq, ATTN_WIDTH), lambda b, n: (b, n, 0)),
                  prev, own, nxt, prev, own, nxt, cspec, cspec],
        out_specs=pl.BlockSpec((None, tq, ATTN_WIDTH), lambda b, n: (b, n, 0)),
        out_shape=jax.ShapeDtypeStruct((B, L, ATTN_WIDTH), BF16),
        compiler_params=_cparams(("arbitrary", "arbitrary")),
        name="window_attention",
    )(sink, q, k, k, k, v, v, v, kc, vc)


SUPER = 256
CH_PER = SUPER // CHUNK
HALF = 128


def _dot_split(m, parts):
    return _dot(m, parts[0]) + _dot(m, parts[1])


def _gla_kernel(gq_ref, gk_ref, gv_ref, gg_ref, lr_ref, wd_ref, bdec_ref, gn_ref, sf_ref, sb_ref,
                o_ref, la_ref, oi_ref, qg_ref, kv_ref, sb16_ref, dec_ref, st_ref):
    L = gq_ref.shape[0]
    nsuper = L // SUPER
    nchunk = L // CHUNK
    half = CHUNK // 2
    la_ref[...] = _log_decay(lr_ref[...], wd_ref[...], bdec_ref[...])

    r = lax.broadcasted_iota(jnp.int32, (SUPER, SUPER), 0)
    cidx = lax.broadcasted_iota(jnp.int32, (SUPER, SUPER), 1)
    same = (r // CHUNK) == (cidx // CHUNK)
    pr = r % CHUNK
    pc = cidx % CHUNK
    one = jnp.float32(1.0)
    zero = jnp.float32(0.0)
    in_f = jnp.where(pc <= pr, one, zero)
    in_b = jnp.where(pc >= pr, one, zero)
    ref_f = jnp.where(pc < half, one, zero)
    ref_b = jnp.where(pc >= half, one, zero)
    m1_f = jnp.where(same, in_f - ref_f, zero).astype(BF16)
    m1_b = jnp.where(same, in_b - ref_b, zero).astype(BF16)
    rh = lax.broadcasted_iota(jnp.int32, (HALF, 2 * HALF), 0)
    ch = lax.broadcasted_iota(jnp.int32, (HALF, 2 * HALF), 1) % HALF
    same_h = (rh // CHUNK) == (ch // CHUNK)
    mask_f = jnp.where(same_h, jnp.where(ch % CHUNK <= rh % CHUNK, one, zero), zero) > 0.5
    mask_b = jnp.where(same_h, jnp.where(ch % CHUNK >= rh % CHUNK, one, zero), zero) > 0.5
    rr = lax.broadcasted_iota(jnp.int32, (2 * CH_PER, SUPER), 0)
    rc = lax.broadcasted_iota(jnp.int32, (2 * CH_PER, SUPER), 1)
    in_chunk = jnp.where((rc // CHUNK) == (rr % CH_PER), one, zero)
    first = jnp.where((rc % CHUNK) < half, 1, 0)
    is_tot = jnp.where(rr >= CH_PER, 1, 0)
    rs_f = (in_chunk * jnp.where(first != is_tot, one, zero)).astype(BF16)
    rs_b = (in_chunk * jnp.where(first == is_tot, one, zero)).astype(BF16)
    lo_h = _lane_lo((HALF, LANES))
    zero_blk = jnp.zeros((HALF, GLA_DV), BF16)

    def phase1(s, carry):
        r0 = pl.multiple_of(s * SUPER, SUPER)
        rows = pl.ds(r0, SUPER)
        q = gq_ref[rows, :].astype(F32)
        k = gk_ref[rows, :].astype(F32)
        qes, kes, kds = [], [], []
        for d, (m1, rs) in enumerate(((m1_f, rs_f), (m1_b, rs_b))):
            parts = _split2(la_ref[rows, d * GLA_QK_WIDTH:(d + 1) * GLA_QK_WIDTH])
            x1 = _dot_split(m1, parts)
            erow = jnp.exp(_dot_split(rs, parts))
            dec = erow[0:CH_PER] * erow[CH_PER:2 * CH_PER]
            dec_ref[d, s] = jnp.concatenate([dec, dec], axis=0)
            qe = q * jnp.exp(x1)
            ke = k * jnp.exp(-x1)
            qg_parts, kd_parts = [], []
            for j in range(CH_PER):
                rj = slice(j * CHUNK, (j + 1) * CHUNK)
                qg_parts.append(qe[rj] * erow[j:j + 1])
                kd_parts.append(ke[rj] * erow[CH_PER + j:CH_PER + j + 1])
            qg_ref[d, rows, :] = jnp.concatenate(qg_parts, axis=0).astype(BF16)
            qes.append(qe.astype(BF16))
            kes.append(ke)
            kds.append(jnp.concatenate(kd_parts, axis=0).astype(BF16))
        for c in range(2):
            cl = slice(c * LANES, (c + 1) * LANES)
            vpair = gv_ref[rows, 2 * c * GLA_DV:(2 * c + 2) * GLA_DV]
            for blk in range(SUPER // HALF):
                rb = slice(blk * HALF, (blk + 1) * HALF)
                vbd = jnp.concatenate(
                    [jnp.concatenate([vpair[rb, 0:GLA_DV], zero_blk], axis=1),
                     jnp.concatenate([zero_blk, vpair[rb, GLA_DV:2 * GLA_DV]], axis=1)], axis=0)
                o2 = None
                for d, mask in enumerate((mask_f, mask_b)):
                    ke_cb = kes[d][rb, cl]
                    ke_st = jnp.concatenate([jnp.where(lo_h, ke_cb, zero),
                                             jnp.where(lo_h, zero, ke_cb)], axis=0).astype(BF16)
                    a = _dot_nt(qes[d][rb, cl], ke_st)
                    o = _dot(jnp.where(mask, a, zero).astype(BF16), vbd)
                    o2 = o if o2 is None else o2 + o
                oi_ref[pl.ds(r0 + blk * HALF, HALF), 2 * c * GLA_DV:(2 * c + 2) * GLA_DV] = o2
            for d in range(2):
                for j in range(CH_PER):
                    rj = slice(j * CHUNK, (j + 1) * CHUNK)
                    t = _dot_tn(kds[d][rj, cl], vpair[rj])
                    kv = jnp.concatenate([t[0:GLA_DK, 0:GLA_DV], t[GLA_DK:, GLA_DV:]], axis=0)
                    kv_ref[d, c, s * CH_PER + j] = kv.T
        return carry

    lax.fori_loop(0, nsuper, phase1, 0)

    st_ref[0] = sf_ref[0]
    st_ref[1] = sf_ref[1]
    st_ref[2] = sb_ref[0]
    st_ref[3] = sb_ref[1]

    def phase2(n, carry):
        for d in range(2):
            idx = n if d == 0 else nchunk - 1 - n
            dec = dec_ref[d, idx // CH_PER, pl.ds(idx % CH_PER, 1), :]
            for c in range(2):
                st = st_ref[2 * d + c]
                sb16_ref[c, idx, :, d * LANES:(d + 1) * LANES] = st.astype(BF16)
                st_ref[2 * d + c] = dec[:, c * LANES:(c + 1) * LANES] * st + kv_ref[d, c, idx]
        return carry

    lax.fori_loop(0, nchunk, phase2, 0)

    lo64 = _lane_lo((CHUNK, 2 * LANES))
    zero_b = jnp.zeros((), BF16)

    def phase3(s, carry):
        r0 = pl.multiple_of(s * SUPER, SUPER)
        rows = pl.ds(r0, SUPER)
        inter = [[None] * CH_PER for _ in range(GLA_HEADS)]
        for j in range(CH_PER):
            rj = pl.ds(r0 + j * CHUNK, CHUNK)
            for c in range(2):
                qg_c = jnp.concatenate([qg_ref[d, rj, c * LANES:(c + 1) * LANES] for d in range(2)], axis=1)
                lhs = jnp.concatenate([jnp.where(lo64, qg_c, zero_b),
                                       jnp.where(lo64, zero_b, qg_c)], axis=0)
                t = _dot_nt(lhs, sb16_ref[c, s * CH_PER + j])
                for hh in range(2):
                    inter[2 * c + hh][j] = t[hh * CHUNK:(hh + 1) * CHUNK]
        for hd in range(GLA_HEADS):
            cl = slice(hd * GLA_DV, (hd + 1) * GLA_DV)
            o = oi_ref[rows, cl] + jnp.concatenate(inter[hd], axis=0)
            y = o * lax.rsqrt(jnp.mean(o * o, axis=-1, keepdims=True) + EPS) * gn_ref[:, cl]
            g = gg_ref[rows, cl].astype(F32)
            o_ref[rows, cl] = (y * (g * jax.nn.sigmoid(g))).astype(BF16)
        return carry

    lax.fori_loop(0, nsuper, phase3, 0)


def _gla(gq, gk, gv, gg, lr, wd, bdec, gn, s_f, s_b):
    B, L, _ = gq.shape
    nchunk = L // CHUNK
    full = lambda shape: pl.BlockSpec(shape, lambda b: (0,) * len(shape))
    tok = lambda w: pl.BlockSpec((None, L, w), lambda b: (b, 0, 0))
    st_spec = pl.BlockSpec((None, 2, LANES, GLA_DV), lambda b: (b, 0, 0, 0))
    return pl.pallas_call(
        _gla_kernel,
        grid=(B,),
        in_specs=[tok(GLA_QK_WIDTH), tok(GLA_QK_WIDTH), tok(GLA_WIDTH), tok(GLA_WIDTH), tok(LANES),
                  full(wd.shape), full(bdec.shape), full(gn.shape), st_spec, st_spec],
        out_specs=tok(GLA_WIDTH),
        out_shape=jax.ShapeDtypeStruct((B, L, GLA_WIDTH), BF16),
        scratch_shapes=[pltpu.VMEM((L, 2 * GLA_QK_WIDTH), F32),
                        pltpu.VMEM((L, GLA_WIDTH), F32),
                        pltpu.VMEM((2, L, GLA_QK_WIDTH), BF16),
                        pltpu.VMEM((2, 2, nchunk, GLA_DV, LANES), F32),
                        pltpu.VMEM((2, nchunk, GLA_DV, 2 * LANES), BF16),
                        pltpu.VMEM((2, L // SUPER, 2 * CH_PER, GLA_QK_WIDTH), F32),
                        pltpu.VMEM((4, GLA_DV, LANES), F32)],
        compiler_params=_cparams(("arbitrary",)),
        name="gla_bidirectional",
    )(gq, gk, gv, gg, lr, wd, bdec, gn, s_f, s_b)


def _outproj_kernel(attn_ref, gla_ref, x_ref, mod_ref, w_ref, g2_ref, wr_ref,
                    x1_ref, h2_ref, afft_ref):
    y = _dot(attn_ref[...], w_ref[0:ATTN_WIDTH, :]) + _dot(gla_ref[...], w_ref[ATTN_WIDTH:, :])
    x1 = x_ref[...] + mod_ref[2:3, :] * y
    x1_ref[...] = x1
    h2 = _rms_mod(x1, g2_ref[...], mod_ref[3:4, :], mod_ref[4:5, :]).astype(BF16)
    half = D_MODEL // 2
    hi = pltpu.bitcast(h2[:, 0:half].astype(F32), jnp.uint32)
    lo = pltpu.bitcast(h2[:, half:].astype(F32), jnp.uint32)
    h2_ref[...] = pltpu.bitcast(hi | (lo >> 16), F32)
    logits = _dot_nt(wr_ref[...], h2)
    e = jnp.exp(logits - jnp.max(logits, axis=0, keepdims=True))
    afft_ref[...] = e / jnp.sum(e, axis=0, keepdims=True)


def _output_projection(attn, gla, x, mod3, w_out, g2, w_router, tm):
    B, L, _ = x.shape
    full = lambda shape: pl.BlockSpec(shape, lambda b, i: (0,) * len(shape))
    tok = lambda w: pl.BlockSpec((None, tm, w), lambda b, i: (b, i, 0))
    return pl.pallas_call(
        _outproj_kernel,
        grid=(B, L // tm),
        in_specs=[tok(ATTN_WIDTH), tok(GLA_WIDTH), tok(D_MODEL),
                  pl.BlockSpec((None, 6, D_MODEL), lambda b, i: (b, 0, 0)),
                  full(w_out.shape), full(g2.shape), full(w_router.shape)],
        out_specs=[tok(D_MODEL), tok(D_MODEL // 2),
                   pl.BlockSpec((None, N_EXPERTS, tm), lambda b, i: (b, 0, i))],
        out_shape=[jax.ShapeDtypeStruct((B, L, D_MODEL), F32),
                   jax.ShapeDtypeStruct((B, L, D_MODEL // 2), F32),
                   jax.ShapeDtypeStruct((B, N_EXPERTS, L), F32)],
        compiler_params=_cparams(("arbitrary", "arbitrary")),
        name="output_projection_router",
    )(attn, gla, x, mod3, w_out, g2, w_router)


def _topk_kernel(afft_ref, post_ref, pos_ref, gsel_ref, *, cap):
    nbatch, E, L = afft_ref.shape
    aff = afft_ref[...].reshape(nbatch * E, L)
    E = nbatch * E

    def search(i, thr):
        cand = thr | jnp.left_shift(jnp.int32(1), 30 - i)
        cnt = jnp.sum(jnp.where(aff >= pltpu.bitcast(cand, F32), 1.0, 0.0), axis=-1, keepdims=True)
        return jnp.where(cnt >= cap, cand, thr)

    thr_bits = lax.fori_loop(0, 31, search, jnp.zeros((E, 1), jnp.int32))
    thr = pltpu.bitcast(thr_bits, F32)
    above = aff > thr
    tie = aff == thr
    need = cap - jnp.sum(jnp.where(above, 1.0, 0.0), axis=-1, keepdims=True)

    upper = (lax.broadcasted_iota(jnp.int32, (LANES, LANES), 0)
             <= lax.broadcasted_iota(jnp.int32, (LANES, LANES), 1)).astype(BF16)

    def prefix(mask):
        parts = []
        run = jnp.zeros((E, 1), F32)
        for j in range(L // LANES):
            blk = jnp.where(mask[:, j * LANES:(j + 1) * LANES], 1.0, 0.0).astype(BF16)
            loc = _dot(blk, upper) + run
            parts.append(loc)
            run = loc[:, LANES - 1:LANES]
        return jnp.concatenate(parts, axis=1)

    tie_rank = prefix(tie)
    sel = above | (tie & (tie_rank <= need))
    slot = prefix(sel).astype(jnp.int32) - 1
    post = jnp.where(sel, slot, -1)
    gsel = jnp.where(sel, aff, 0.0)
    ne = E // nbatch
    pad_i = jnp.full((LANES - ne, L), -1, jnp.int32)
    pad_f = jnp.zeros((LANES - ne, L), F32)
    for bb in range(nbatch):
        rows = slice(bb * ne, (bb + 1) * ne)
        post_ref[bb] = post[rows]
        pos_ref[bb] = jnp.concatenate([post[rows], pad_i], axis=0).T
        gsel_ref[bb] = jnp.concatenate([gsel[rows], pad_f], axis=0).T


def _expert_choice(afft, cap, nbatch):
    B, E, L = afft.shape
    return pl.pallas_call(
        functools.partial(_topk_kernel, cap=cap),
        grid=(B // nbatch,),
        in_specs=[pl.BlockSpec((nbatch, E, L), lambda b: (b, 0, 0))],
        out_specs=[pl.BlockSpec((nbatch, E, L), lambda b: (b, 0, 0)),
                   pl.BlockSpec((nbatch, L, LANES), lambda b: (b, 0, 0)),
                   pl.BlockSpec((nbatch, L, LANES), lambda b: (b, 0, 0))],
        out_shape=[jax.ShapeDtypeStruct((B, E, L), jnp.int32),
                   jax.ShapeDtypeStruct((B, L, LANES), jnp.int32),
                   jax.ShapeDtypeStruct((B, L, LANES), F32)],
        compiler_params=_cparams(("arbitrary",)),
        name="expert_choice_topk",
    )(afft)


SC_CORES = 2
SC_SUBCORES = 16
SC_LANES = 16
SC_WINDOW = 128


def _dispatch(rows, slots, cap, first_expert, n_experts):
    L = slots.shape[1]
    W = rows.shape[1]
    n_pair = slots.shape[0] // N_EXPERTS * n_experts
    per_worker = n_pair // (SC_CORES * SC_SUBCORES)
    n_win = cap // SC_WINDOW
    mesh = plsc.VectorSubcoreMesh(core_axis_name="c", subcore_axis_name="s",
                                  num_cores=SC_CORES, num_subcores=SC_SUBCORES)

    def body(rows_hbm, slots_hbm, out_hbm, slot_v, *scratch):
        idx_v, buf_v, sem = scratch[:n_win], scratch[n_win], scratch[n_win + 1]
        worker = lax.axis_index("s") * SC_CORES + lax.axis_index("c")

        @pl.loop(0, per_worker)
        def _(p):
            pair = worker * per_worker + p
            batch = pair // n_experts
            first_tok = batch * L
            pltpu.sync_copy(slots_hbm.at[batch * N_EXPERTS + first_expert + pair % n_experts], slot_v)

            @pl.loop(0, L // SC_LANES)
            def _(i):
                v = slot_v[pl.ds(i * SC_LANES, SC_LANES)]
                tok = lax.iota(jnp.int32, SC_LANES) + (i * SC_LANES + first_tok)
                for w in range(n_win):
                    in_win = (v >= w * SC_WINDOW) & (v < (w + 1) * SC_WINDOW)
                    plsc.store_scatter(idx_v[w], [v - w * SC_WINDOW], tok, mask=in_win)

            for w in range(n_win):
                pltpu.async_copy(rows_hbm.at[idx_v[w]], buf_v, sem).wait()
                pltpu.sync_copy(buf_v, out_hbm.at[pl.ds(pair * cap + w * SC_WINDOW, SC_WINDOW)])

    return pl.kernel(
        body,
        out_type=jax.ShapeDtypeStruct((n_pair * cap, W), rows.dtype),
        mesh=mesh,
        scratch_types=[pltpu.VMEM((L,), jnp.int32)]
        + [pltpu.VMEM((SC_WINDOW,), jnp.int32) for _ in range(n_win)]
        + [pltpu.VMEM((SC_WINDOW, W), rows.dtype), pltpu.SemaphoreType.DMA],
        compiler_params=pltpu.CompilerParams(needs_layout_passes=False),
        name="moe_dispatch_gather",
    )(rows, slots)


def _ffn_kernel(xs_ref, wg_ref, wu_ref, wd_ref, y_ref, wgb_ref, wub_ref, wdb_ref):
    nbatch, cap, dw = xs_ref.shape
    d = 2 * dw

    @pl.when(pl.program_id(1) == 0)
    def _():
        wgb_ref[...] = wg_ref[...].astype(BF16)
        wub_ref[...] = wu_ref[...].astype(BF16)
        wdb_ref[...] = wd_ref[...].astype(BF16)

    words = pltpu.bitcast(xs_ref[...].reshape(nbatch * cap, dw), jnp.uint32)
    xs = jnp.concatenate([pltpu.bitcast(words & jnp.uint32(0xFFFF0000), F32).astype(BF16),
                          pltpu.bitcast(words << 16, F32).astype(BF16)], axis=1)
    f = wg_ref.shape[1]
    half = f // 2
    acc = None
    for j in range(2):
        cols = slice(j * half, (j + 1) * half)
        g = _dot(xs, wgb_ref[:, cols])
        u = _dot(xs, wub_ref[:, cols])
        hid = (g * jax.nn.sigmoid(g) * u).astype(BF16)
        part = _dot(hid, wdb_ref[cols, :])
        acc = part if acc is None else acc + part
    y_ref[...] = acc.astype(BF16).reshape(nbatch, cap, d)


def _expert_ffn(xs, w_gate, w_up, w_down, first_expert, nbatch):
    B, E, cap, dw = xs.shape
    d, f = w_gate.shape[1:]
    tok = pl.BlockSpec((nbatch, None, cap, d), lambda e, b: (b, e, 0, 0))
    return pl.pallas_call(
        _ffn_kernel,
        grid=(E, B // nbatch),
        in_specs=[pl.BlockSpec((nbatch, None, cap, dw), lambda e, b: (b, e, 0, 0)),
                  pl.BlockSpec((None, d, f), lambda e, b: (e + first_expert, 0, 0)),
                  pl.BlockSpec((None, d, f), lambda e, b: (e + first_expert, 0, 0)),
                  pl.BlockSpec((None, f, d), lambda e, b: (e + first_expert, 0, 0))],
        out_specs=tok,
        out_shape=jax.ShapeDtypeStruct((B, E, cap, d), BF16),
        scratch_shapes=[pltpu.VMEM((d, f), BF16), pltpu.VMEM((d, f), BF16), pltpu.VMEM((f, d), BF16)],
        compiler_params=_cparams(("arbitrary", "arbitrary")),
        name="expert_swiglu",
    )(xs, w_gate, w_up, w_down)


def _combine_kernel(pos_ref, gsel_ref, *rest, cap, n_groups):
    y_refs = rest[:n_groups]
    x1_ref, mod_ref, o_ref, acc_ref = rest[n_groups:]
    tt = pos_ref.shape[0]
    per_group = N_EXPERTS // n_groups
    slot = lax.broadcasted_iota(jnp.int32, (tt, cap), 1)
    for e in range(N_EXPERTS):
        onehot = jnp.where(pos_ref[:, e:e + 1] == slot, 1.0, 0.0).astype(BF16)
        part = gsel_ref[:, e:e + 1] * _dot(onehot, y_refs[e // per_group][e % per_group])
        if e == 0:
            acc_ref[...] = part
        else:
            acc_ref[...] += part
    o_ref[...] = x1_ref[...] + mod_ref[5:6, :] * acc_ref[...]


def _combine(pos, gsel, ys, x1, mod3, cap, tt):
    B, L, _ = x1.shape
    per_group = ys[0].shape[1]
    tok = lambda w: pl.BlockSpec((None, tt, w), lambda b, i: (b, i, 0))
    return pl.pallas_call(
        functools.partial(_combine_kernel, cap=cap, n_groups=len(ys)),
        grid=(B, L // tt),
        in_specs=[tok(LANES), tok(LANES)]
        + [pl.BlockSpec((None, per_group, cap, D_MODEL), lambda b, i: (b, 0, 0, 0)) for _ in ys]
        + [tok(D_MODEL), pl.BlockSpec((None, 6, D_MODEL), lambda b, i: (b, 0, 0))],
        out_specs=tok(D_MODEL),
        out_shape=jax.ShapeDtypeStruct((B, L, D_MODEL), F32),
        scratch_shapes=[pltpu.VMEM((tt, D_MODEL), F32)],
        compiler_params=_cparams(("arbitrary", "arbitrary")),
        name="moe_combine",
    )(pos, gsel, *ys, x1, mod3)


def _rope_tables(L):
    inv = ROPE_BASE ** (-jnp.arange(ROPE_FREQS, dtype=F32) / ROPE_FREQS)
    pos = jnp.arange(L)
    row = (pos // GRID_W).astype(F32)[:, None] * inv
    col = (pos % GRID_W).astype(F32)[:, None] * inv
    cos = jnp.concatenate([jnp.cos(row), jnp.cos(row), jnp.cos(col), jnp.cos(col)], axis=1)
    sin = jnp.concatenate([-jnp.sin(row), jnp.sin(row), -jnp.sin(col), jnp.sin(col)], axis=1)
    return jnp.tile(cos, (1, 2)), jnp.tile(sin, (1, 2))


def _head_mean_matrix(n):
    idx = np.arange(n) // HEAD_DIM
    return jnp.asarray((idx[:, None] == idx[None, :]).astype(np.float32) / HEAD_DIM, dtype=BF16)


def kernel(x, c, ctx, c_ctx, w_mod, b_mod, norm1_g, w_in, q_norm_g, k_norm_g, attn_sink,
           w_decay_fwd, b_decay_fwd, w_decay_bwd, b_decay_bwd, gla_norm_g, w_out, norm2_g,
           w_router, w_e_gate, w_e_up, w_e_down):
    B, L, D = x.shape
    cap = CAPACITY_FACTOR * L // N_EXPERTS
    layer = 0

    rows = ((B + 1 + 7) // 8) * 8
    cc = jnp.concatenate([c, c_ctx[None, :], jnp.zeros((rows - B - 1, D), F32)], axis=0)
    mod_all = _modulation(cc, w_mod[layer], b_mod[layer])
    mod3 = mod_all[:B].reshape(B, 6, D)
    modc = mod_all[B].reshape(6, D)

    w = w_in[layer]
    o = np.cumsum([0, ATTN_WIDTH, KV_WIDTH, KV_WIDTH, GLA_QK_WIDTH, GLA_QK_WIDTH,
                   GLA_WIDTH, GLA_WIDTH, GATE_RANK, GATE_RANK])
    w_lr = jnp.concatenate([w[:, o[7]:o[9]]] * 3 + [jnp.zeros((D, LANES - 6 * GATE_RANK), F32)], axis=1)
    head_order = np.arange(N_Q_HEADS).reshape(N_KV_HEADS, -1).T.reshape(-1)
    attn_perm = (head_order[:, None] * HEAD_DIM + np.arange(HEAD_DIM)[None, :]).reshape(-1)
    w_in_r = jnp.concatenate([w[:, attn_perm], w[:, o[1]:o[7]], w_lr], axis=1).astype(BF16)
    w_out_r = jnp.concatenate([w_out[layer][attn_perm], w_out[layer][ATTN_WIDTH:]], axis=0).astype(BF16)
    w_ctx = jnp.concatenate([w[:, o[1]:o[3]], w[:, o[4]:o[6]], w_lr], axis=1).astype(BF16)
    wd2 = jnp.zeros((2 * GATE_RANK, 2 * GLA_QK_WIDTH), F32)
    wd2 = wd2.at[0:GATE_RANK, 0:GLA_QK_WIDTH].set(w_decay_fwd[layer])
    wd2 = wd2.at[GATE_RANK:, GLA_QK_WIDTH:].set(w_decay_bwd[layer])
    wd_hi = wd2.astype(BF16)
    wd_lo = (wd2 - wd_hi.astype(F32)).astype(BF16)
    wd = jnp.concatenate([wd_hi, wd_hi, wd_lo,
                          jnp.zeros((LANES - 6 * GATE_RANK, 2 * GLA_QK_WIDTH), BF16)], axis=0)
    bdec = jnp.concatenate([b_decay_fwd[layer], b_decay_bwd[layer]])[None, :]
    g1 = norm1_g[layer][None, :]
    g2 = norm2_g[layer][None, :]
    qg = jnp.tile(q_norm_g[layer], N_Q_HEADS)[None, :]
    kg2 = jnp.tile(k_norm_g[layer], N_KV_HEADS)[None, :]
    gn = jnp.tile(gla_norm_g[layer], GLA_HEADS)[None, :]
    bd512 = _head_mean_matrix(ATTN_WIDTH)
    bd128 = _head_mean_matrix(KV_WIDTH)
    cos_t, sin_t = _rope_tables(L)
    w_router_t = w_router[layer].T.astype(BF16)

    kc, vc, s_f, s_b = _context_side(ctx, modc, g1, w_ctx, kg2, bd128, wd, bdec)
    q, k, v, gq, gk, gv, gg, lr = _input_projection(
        x, mod3, g1, w_in_r, qg, kg2, bd512, cos_t, sin_t, tm=1024)
    attn = _window_attention(attn_sink[layer], q, k, v, kc, vc, tq=512)
    gla = _gla(gq, gk, gv, gg, lr, wd, bdec, gn, s_f, s_b)
    x1, h2, afft = _output_projection(attn, gla, x, mod3, w_out_r, g2, w_router_t, tm=1024)
    post, pos, gsel = _expert_choice(afft, cap, nbatch=4)
    n_groups = 4
    per_group = N_EXPERTS // n_groups
    rows = h2.reshape(B * L, D // 2)
    slots = post.reshape(B * N_EXPERTS, L)
    ys = []
    for g in range(n_groups):
        xs = _dispatch(rows, slots, cap, g * per_group, per_group).reshape(B, per_group, cap, D // 2)
        ys.append(_expert_ffn(xs, w_e_gate[layer], w_e_up[layer], w_e_down[layer], g * per_group, nbatch=4))
    return _combine(pos, gsel, ys, x1, mod3, cap, tt=1024)
```

```python
import functools

import jax
import jax.numpy as jnp
import numpy as np
from jax import lax
from jax.experimental import pallas as pl
from jax.experimental.pallas import tpu as pltpu
from jax.experimental.pallas import tpu_sc as plsc

D_MODEL = 1024
GRID_W = 64
HEAD_DIM = 64
N_Q_HEADS = 8
N_KV_HEADS = 2
BLOCK = 128
ROPE_FREQS = 16
ROPE_BASE = 10000.0
GLA_HEADS = 4
GLA_DV = 128
GLA_DK = 64
GATE_RANK = 16
GATE_NORMALIZER = 16.0
CHUNK = 64
N_EXPERTS = 16
CAPACITY_FACTOR = 2
ATTN_WIDTH = 512
KV_WIDTH = 128
GLA_QK_WIDTH = 256
GLA_WIDTH = 512
EPS = 1e-6
NEG_INF = -1e30
LOG2E = 1.4426950408889634

LANES = 128
ROW_WORDS = D_MODEL // 2 + LANES
VMEM_LIMIT = 56 * 1024 * 1024

F32 = jnp.float32
BF16 = jnp.bfloat16
HI = lax.Precision.HIGHEST


def _cparams(sem):
    return pltpu.CompilerParams(dimension_semantics=sem, vmem_limit_bytes=VMEM_LIMIT)


def _dot(a, b):
    return jnp.dot(a, b, preferred_element_type=F32)


def _dot_hi(a, b):
    return jnp.dot(a, b, preferred_element_type=F32, precision=HI)


def _dot_nt(a, b):
    return lax.dot_general(a, b, (((1,), (1,)), ((), ())), preferred_element_type=F32)


def _dot_tn(a, b, precision=None):
    return lax.dot_general(a, b, (((0,), (0,)), ((), ())), preferred_element_type=F32,
                           precision=precision)


def _split2(t):
    hi = t.astype(BF16)
    lo = (t - hi.astype(F32)).astype(BF16)
    return hi, lo


def _rms_mod(t, g, shift, scale):
    y = t * lax.rsqrt(jnp.mean(t * t, axis=-1, keepdims=True) + EPS)
    return (y * g) * (1.0 + scale) + shift


def _log_decay(lr, wd3, bias):
    hi = lr.astype(BF16)
    lo = (lr - hi.astype(F32)).astype(BF16)
    lane = lax.broadcasted_iota(jnp.int32, lr.shape, 1)
    second = (lane >= 2 * GATE_RANK) & (lane < 4 * GATE_RANK)
    z = _dot(jnp.where(second, lo, hi), wd3) + bias
    return (jnp.minimum(z, 0.0) - jnp.log(1.0 + jnp.exp(-jnp.abs(z)))) * (1.0 / GATE_NORMALIZER)


def _lane_lo(shape):
    return (lax.broadcasted_iota(jnp.int32, shape, len(shape) - 1) % LANES) < HEAD_DIM


def _mod_kernel(c_ref, w_ref, b_ref, o_ref):
    c = c_ref[...]
    s = c * jax.nn.sigmoid(c)
    o_ref[...] = _dot_hi(s, w_ref[...]) + b_ref[...]


def _modulation(cc, w_mod, b_mod):
    m = cc.shape[0]
    n = w_mod.shape[1]
    tn = 1024
    return pl.pallas_call(
        _mod_kernel,
        grid=(n // tn,),
        in_specs=[pl.BlockSpec((m, D_MODEL), lambda j: (0, 0)),
                  pl.BlockSpec((D_MODEL, tn), lambda j: (0, j)),
                  pl.BlockSpec((1, tn), lambda j: (0, j))],
        out_specs=pl.BlockSpec((m, tn), lambda j: (0, j)),
        out_shape=jax.ShapeDtypeStruct((m, n), F32),
        compiler_params=_cparams(("arbitrary",)),
        name="adaln_mod",
    )(cc, w_mod, b_mod.reshape(1, n))


def _ctx_kernel(ctx_ref, mod_ref, g1_ref, w_ref, kg_ref, bd_ref, wd_ref, bdec_ref,
                kc_ref, vc_ref, sf_ref, sb_ref):
    n = ctx_ref.shape[0]
    h = _rms_mod(ctx_ref[...], g1_ref[...], mod_ref[0:1, :], mod_ref[1:2, :]).astype(BF16)
    pc = _dot(h, w_ref[...])
    ak = pc[:, 0:128]
    av = pc[:, 128:256]
    gk = pc[:, 256:512]
    gv = pc[:, 512:1024].astype(BF16)
    lr = pc[:, 1024:1152]
    sq_hi, sq_lo = _split2(ak * ak)
    ms = _dot(sq_hi, bd_ref[...]) + _dot(sq_lo, bd_ref[...])
    kn = ak * lax.rsqrt(ms + EPS) * kg_ref[...]
    kc_ref[...] = kn.astype(BF16)
    vc_ref[...] = av.astype(BF16)
    la = _log_decay(lr, wd_ref[...], bdec_ref[...])
    r = lax.broadcasted_iota(jnp.int32, (n, n), 0)
    cidx = lax.broadcasted_iota(jnp.int32, (n, n), 1)
    after = (cidx > r).astype(F32)
    before = (cidx < r).astype(F32)
    w_f = jnp.exp(_dot_hi(after, la[:, 0:256]))
    w_b = jnp.exp(_dot_hi(before, la[:, 256:512]))
    lo = _lane_lo((n, LANES))
    for w, out in ((w_f, sf_ref), (w_b, sb_ref)):
        kw = gk * w
        for c in range(2):
            kwc = kw[:, c * LANES:(c + 1) * LANES]
            k_lo = jnp.where(lo, kwc, 0.0).astype(BF16)
            k_hi = jnp.where(lo, 0.0, kwc).astype(BF16)
            v0 = gv[:, (2 * c) * GLA_DV:(2 * c + 1) * GLA_DV]
            v1 = gv[:, (2 * c + 1) * GLA_DV:(2 * c + 2) * GLA_DV]
            out[c] = _dot_tn(v0, k_lo) + _dot_tn(v1, k_hi)


def _context_side(ctx, modc, g1, w_ctx, kg2, bd128, wd, bdec):
    B, n, _ = ctx.shape
    full = lambda shape: pl.BlockSpec(shape, lambda b: (0,) * len(shape))
    kv_spec = pl.BlockSpec((None, n, KV_WIDTH), lambda b: (b, 0, 0))
    st_spec = pl.BlockSpec((None, 2, LANES, GLA_DV), lambda b: (b, 0, 0, 0))
    kv_shape = jax.ShapeDtypeStruct((B, n, KV_WIDTH), BF16)
    st_shape = jax.ShapeDtypeStruct((B, 2, LANES, GLA_DV), F32)
    return pl.pallas_call(
        _ctx_kernel,
        grid=(B,),
        in_specs=[pl.BlockSpec((None, n, D_MODEL), lambda b: (b, 0, 0)),
                  full(modc.shape), full(g1.shape), full(w_ctx.shape), full(kg2.shape),
                  full(bd128.shape), full(wd.shape), full(bdec.shape)],
        out_specs=[kv_spec, kv_spec, st_spec, st_spec],
        out_shape=[kv_shape, kv_shape, st_shape, st_shape],
        compiler_params=_cparams(("arbitrary",)),
        name="context_side",
    )(ctx, modc, g1, w_ctx, kg2, bd128, wd, bdec)


def _swap16(t):
    n = t.shape[1]
    first = (lax.broadcasted_iota(jnp.int32, t.shape, 1) % 32) < ROPE_FREQS
    return jnp.where(first, pltpu.roll(t, n - ROPE_FREQS, 1), pltpu.roll(t, ROPE_FREQS, 1))


def _inproj_kernel(x_ref, mod_ref, g1_ref, w_ref, qg_ref, kg_ref, bd_ref, cos_ref, sin_ref,
                   q_ref, k_ref, v_ref, gq_ref, gk_ref, gv_ref, gg_ref, lr_ref):
    h = _rms_mod(x_ref[...], g1_ref[...], mod_ref[0:1, :], mod_ref[1:2, :]).astype(BF16)
    cos = cos_ref[...]
    sin = sin_ref[...]

    def head_norm_rope(t, g, bd, reps):
        ms = _dot((t * t).astype(BF16), bd)
        tn = t * lax.rsqrt(ms + EPS) * g
        c = jnp.concatenate([cos] * reps, axis=1) if reps > 1 else cos
        s = jnp.concatenate([sin] * reps, axis=1) if reps > 1 else sin
        return tn * c + _swap16(tn) * s

    aq = _dot(h, w_ref[:, 0:512])
    q = head_norm_rope(aq, qg_ref[...], bd_ref[...], 4) * (HEAD_DIM ** -0.5 * LOG2E)
    q_ref[...] = q.astype(BF16)
    akv = _dot(h, w_ref[:, 512:768])
    k = head_norm_rope(akv[:, 0:128], kg_ref[...], bd_ref[0:128, 0:128], 1)
    k_ref[...] = k.astype(BF16)
    v_ref[...] = akv[:, 128:256].astype(BF16)
    gqk = _dot(h, w_ref[:, 768:1280])
    gq_ref[...] = (gqk[:, 0:256] * (GLA_DK ** -0.5)).astype(BF16)
    gk_ref[...] = gqk[:, 256:512].astype(BF16)
    gv_ref[...] = _dot(h, w_ref[:, 1280:1792]).astype(BF16)
    gg_ref[...] = _dot(h, w_ref[:, 1792:2304]).astype(BF16)
    lr_ref[...] = _dot(h, w_ref[:, 2304:2432])


def _input_projection(x, mod3, g1, w_in_r, qg, kg2, bd512, cos_t, sin_t, tm):
    B, L, _ = x.shape
    full = lambda shape: pl.BlockSpec(shape, lambda b, i: (0,) * len(shape))
    tok = lambda w: pl.BlockSpec((None, tm, w), lambda b, i: (b, i, 0))
    widths = (ATTN_WIDTH, KV_WIDTH, KV_WIDTH, GLA_QK_WIDTH, GLA_QK_WIDTH, GLA_WIDTH, GLA_WIDTH, LANES)
    dtypes = (BF16,) * 7 + (F32,)
    return pl.pallas_call(
        _inproj_kernel,
        grid=(B, L // tm),
        in_specs=[tok(D_MODEL),
                  pl.BlockSpec((None, 6, D_MODEL), lambda b, i: (b, 0, 0)),
                  full(g1.shape), full(w_in_r.shape), full(qg.shape), full(kg2.shape),
                  full(bd512.shape),
                  pl.BlockSpec((tm, LANES), lambda b, i: (i, 0)),
                  pl.BlockSpec((tm, LANES), lambda b, i: (i, 0))],
        out_specs=[tok(w) for w in widths],
        out_shape=[jax.ShapeDtypeStruct((B, L, w), dt) for w, dt in zip(widths, dtypes)],
        compiler_params=_cparams(("arbitrary", "arbitrary")),
        name="input_projection",
    )(x, mod3, g1, w_in_r, qg, kg2, bd512, cos_t, sin_t)


def _attn_kernel(sink_ref, q_ref, kp_ref, ko_ref, kn_ref, vp_ref, vo_ref, vn_ref,
                 kc_ref, vc_ref, o_ref):
    i = pl.program_id(1)
    ni = pl.num_programs(1)
    nsub = q_ref.shape[0] // BLOCK
    ncol = ATTN_WIDTH // LANES
    win = 3 * BLOCK
    ucol = 4
    half_rows = ucol * BLOCK
    k_win = jnp.concatenate([kp_ref[...], ko_ref[...], kn_ref[...]], axis=0)
    v_win = jnp.concatenate([vp_ref[...], vo_ref[...], vn_ref[...]], axis=0)
    k_ctx = kc_ref[...]
    lo_w = _lane_lo(v_win.shape)
    lo_c = _lane_lo(vc_ref.shape)
    lo_q = _lane_lo((BLOCK, LANES))
    zero = jnp.zeros((), BF16)
    one = jnp.ones((), BF16)
    v0_c, v0_w = jnp.where(lo_c, vc_ref[...], one), jnp.where(lo_w, v_win, one)
    v1_c, v1_w = jnp.where(lo_c, one, vc_ref[...]), jnp.where(lo_w, one, v_win)
    qi = lax.broadcasted_iota(jnp.int32, (half_rows, BLOCK), 0) % BLOCK
    kj = lax.broadcasted_iota(jnp.int32, (half_rows, BLOCK), 1)
    no_prev = jnp.where(i > 0, 0, BLOCK)
    no_next = jnp.where(i < ni - 1, 0, BLOCK)
    row_head = lax.broadcasted_iota(jnp.int32, (half_rows, 1), 0) // BLOCK
    lo_o = _lane_lo((half_rows, LANES))
    for t in range(nsub):
        rows = slice(t * BLOCK, (t + 1) * BLOCK)
        keys = slice(t * BLOCK, t * BLOCK + win)
        cols = [q_ref[rows, c * LANES:(c + 1) * LANES] for c in range(ncol)]
        prev_ok = kj >= qi + (no_prev if t == 0 else 0)
        next_ok = kj <= qi - (no_next if t == nsub - 1 else 0)
        for p in range(ncol // ucol):
            outs = []
            qs = jnp.concatenate([jnp.where(lo_q, qc, zero) for qc in cols[ucol * p:ucol * (p + 1)]]
                                 + [jnp.where(lo_q, zero, qc) for qc in cols[ucol * p:ucol * (p + 1)]], axis=0)
            s_c_all = _dot_nt(qs, k_ctx)
            s_w_all = _dot_nt(qs, k_win[keys])
            for g, (vv_c, vv_w) in enumerate(((v0_c, v0_w), (v1_c, v1_w))):
                head = g * ncol + ucol * p
                sink_g = jnp.full((half_rows, 1), sink_ref[head + ucol - 1], F32)
                for j in range(ucol - 2, -1, -1):
                    sink_g = jnp.where(row_head <= j, sink_ref[head + j], sink_g)
                sink_g = sink_g * LOG2E
                s_c = s_c_all[g * half_rows:(g + 1) * half_rows]
                s_w = s_w_all[g * half_rows:(g + 1) * half_rows]
                s_p = jnp.where(prev_ok, s_w[:, 0:BLOCK], NEG_INF)
                s_o = s_w[:, BLOCK:2 * BLOCK]
                s_n = jnp.where(next_ok, s_w[:, 2 * BLOCK:win], NEG_INF)
                m = jnp.maximum(jnp.maximum(jnp.max(s_c, axis=-1, keepdims=True),
                                            jnp.max(jnp.maximum(jnp.maximum(s_p, s_o), s_n),
                                                    axis=-1, keepdims=True)), sink_g)
                e_c = jnp.exp2(s_c - m).astype(BF16)
                e_w = jnp.concatenate([jnp.exp2(s_p - m), jnp.exp2(s_o - m), jnp.exp2(s_n - m)],
                                      axis=1).astype(BF16)
                acc = _dot(e_c, vv_c) + _dot(e_w, vv_w[keys])
                outs.append(acc / (pltpu.roll(acc, HEAD_DIM, 1) + jnp.exp2(sink_g - m)))
            o = jnp.where(lo_o, outs[0], outs[1]).astype(BF16)
            for j in range(ucol):
                c = ucol * p + j
                o_ref[rows, c * LANES:(c + 1) * LANES] = o[j * BLOCK:(j + 1) * BLOCK]


def _window_attention(sink, q, k, v, kc, vc, tq):
    B, L, _ = q.shape
    nb = L // BLOCK
    nsub = tq // BLOCK
    n_ctx = kc.shape[1]
    prev = pl.BlockSpec((None, BLOCK, KV_WIDTH), lambda b, n: (b, jnp.maximum(n * nsub - 1, 0), 0))
    own = pl.BlockSpec((None, tq, KV_WIDTH), lambda b, n: (b, n, 0))
    nxt = pl.BlockSpec((None, BLOCK, KV_WIDTH),
                       lambda b, n: (b, jnp.minimum((n + 1) * nsub, nb - 1), 0))
    cspec = pl.BlockSpec((None, n_ctx, KV_WIDTH), lambda b, n: (b, 0, 0))
    return pl.pallas_call(
        _attn_kernel,
        grid=(B, L // tq),
        in_specs=[pl.BlockSpec(memory_space=pltpu.SMEM),
                  pl.BlockSpec((None, tq, ATTN_WIDTH), lambda b, n: (b, n, 0)),
                  prev, own, nxt, prev, own, nxt, cspec, cspec],
        out_specs=pl.BlockSpec((None, tq, ATTN_WIDTH), lambda b, n: (b, n, 0)),
        out_shape=jax.ShapeDtypeStruct((B, L, ATTN_WIDTH), BF16),
        compiler_params=_cparams(("arbitrary", "arbitrary")),
        name="window_attention",
    )(sink, q, k, k, k, v, v, v, kc, vc)


SUPER = 256
CH_PER = SUPER // CHUNK
HALF = 128


def _dot_split(m, parts):
    return _dot(m, parts[0]) + _dot(m, parts[1])


def _gla_kernel(gq_ref, gk_ref, gv_ref, gg_ref, lr_ref, wd_ref, bdec_ref, gn_ref, sf_ref, sb_ref,
                o_ref, la_ref, oi_ref, qg_ref, kv_ref, sb16_ref, dec_ref, st_ref):
    L = gq_ref.shape[0]
    nsuper = L // SUPER
    nchunk = L // CHUNK
    half = CHUNK // 2
    la_ref[...] = _log_decay(lr_ref[...], wd_ref[...], bdec_ref[...])

    r = lax.broadcasted_iota(jnp.int32, (SUPER, SUPER), 0)
    cidx = lax.broadcasted_iota(jnp.int32, (SUPER, SUPER), 1)
    same = (r // CHUNK) == (cidx // CHUNK)
    pr = r % CHUNK
    pc = cidx % CHUNK
    one = jnp.float32(1.0)
    zero = jnp.float32(0.0)
    in_f = jnp.where(pc <= pr, one, zero)
    in_b = jnp.where(pc >= pr, one, zero)
    ref_f = jnp.where(pc < half, one, zero)
    ref_b = jnp.where(pc >= half, one, zero)
    m1_f = jnp.where(same, in_f - ref_f, zero).astype(BF16)
    m1_b = jnp.where(same, in_b - ref_b, zero).astype(BF16)
    rh = lax.broadcasted_iota(jnp.int32, (HALF, 2 * HALF), 0)
    ch = lax.broadcasted_iota(jnp.int32, (HALF, 2 * HALF), 1) % HALF
    same_h = (rh // CHUNK) == (ch // CHUNK)
    mask_f = jnp.where(same_h, jnp.where(ch % CHUNK <= rh % CHUNK, one, zero), zero) > 0.5
    mask_b = jnp.where(same_h, jnp.where(ch % CHUNK >= rh % CHUNK, one, zero), zero) > 0.5
    rr = lax.broadcasted_iota(jnp.int32, (2 * CH_PER, SUPER), 0)
    rc = lax.broadcasted_iota(jnp.int32, (2 * CH_PER, SUPER), 1)
    in_chunk = jnp.where((rc // CHUNK) == (rr % CH_PER), one, zero)
    first = jnp.where((rc % CHUNK) < half, 1, 0)
    is_tot = jnp.where(rr >= CH_PER, 1, 0)
    rs_f = (in_chunk * jnp.where(first != is_tot, one, zero)).astype(BF16)
    rs_b = (in_chunk * jnp.where(first == is_tot, one, zero)).astype(BF16)
    lo_h = _lane_lo((HALF, LANES))
    zero_blk = jnp.zeros((HALF, GLA_DV), BF16)

    def phase1(s, carry):
        r0 = pl.multiple_of(s * SUPER, SUPER)
        rows = pl.ds(r0, SUPER)
        q = gq_ref[rows, :].astype(F32)
        k = gk_ref[rows, :].astype(F32)
        qes, kes, kds = [], [], []
        for d, (m1, rs) in enumerate(((m1_f, rs_f), (m1_b, rs_b))):
            parts = _split2(la_ref[rows, d * GLA_QK_WIDTH:(d + 1) * GLA_QK_WIDTH])
            x1 = _dot_split(m1, parts)
            erow = jnp.exp(_dot_split(rs, parts))
            dec = erow[0:CH_PER] * erow[CH_PER:2 * CH_PER]
            dec_ref[d, s] = jnp.concatenate([dec, dec], axis=0)
            qe = q * jnp.exp(x1)
            ke = k * jnp.exp(-x1)
            qg_parts, kd_parts = [], []
            for j in range(CH_PER):
                rj = slice(j * CHUNK, (j + 1) * CHUNK)
                qg_parts.append(qe[rj] * erow[j:j + 1])
                kd_parts.append(ke[rj] * erow[CH_PER + j:CH_PER + j + 1])
            qg_ref[d, rows, :] = jnp.concatenate(qg_parts, axis=0).astype(BF16)
            qes.append(qe.astype(BF16))
            kes.append(ke)
            kds.append(jnp.concatenate(kd_parts, axis=0).astype(BF16))
        for c in range(2):
            cl = slice(c * LANES, (c + 1) * LANES)
            vpair = gv_ref[rows, 2 * c * GLA_DV:(2 * c + 2) * GLA_DV]
            for blk in range(SUPER // HALF):
                rb = slice(blk * HALF, (blk + 1) * HALF)
                vbd = jnp.concatenate(
                    [jnp.concatenate([vpair[rb, 0:GLA_DV], zero_blk], axis=1),
                     jnp.concatenate([zero_blk, vpair[rb, GLA_DV:2 * GLA_DV]], axis=1)], axis=0)
                o2 = None
                for d, mask in enumerate((mask_f, mask_b)):
                    ke_cb = kes[d][rb, cl]
                    ke_st = jnp.concatenate([jnp.where(lo_h, ke_cb, zero),
                                             jnp.where(lo_h, zero, ke_cb)], axis=0).astype(BF16)
                    a = _dot_nt(qes[d][rb, cl], ke_st)
                    o = _dot(jnp.where(mask, a, zero).astype(BF16), vbd)
                    o2 = o if o2 is None else o2 + o
                oi_ref[pl.ds(r0 + blk * HALF, HALF), 2 * c * GLA_DV:(2 * c + 2) * GLA_DV] = o2
            for d in range(2):
                for j in range(CH_PER):
                    rj = slice(j * CHUNK, (j + 1) * CHUNK)
                    t = _dot_tn(kds[d][rj, cl], vpair[rj])
                    kv = jnp.concatenate([t[0:GLA_DK, 0:GLA_DV], t[GLA_DK:, GLA_DV:]], axis=0)
                    kv_ref[d, c, s * CH_PER + j] = kv.T
        return carry

    lax.fori_loop(0, nsuper, phase1, 0)

    st_ref[0] = sf_ref[0]
    st_ref[1] = sf_ref[1]
    st_ref[2] = sb_ref[0]
    st_ref[3] = sb_ref[1]

    def phase2(n, carry):
        for d in range(2):
            idx = n if d == 0 else nchunk - 1 - n
            dec = dec_ref[d, idx // CH_PER, pl.ds(idx % CH_PER, 1), :]
            for c in range(2):
                st = st_ref[2 * d + c]
                sb16_ref[c, idx, :, d * LANES:(d + 1) * LANES] = st.astype(BF16)
                st_ref[2 * d + c] = dec[:, c * LANES:(c + 1) * LANES] * st + kv_ref[d, c, idx]
        return carry

    lax.fori_loop(0, nchunk, phase2, 0)

    lo64 = _lane_lo((CHUNK, 2 * LANES))
    zero_b = jnp.zeros((), BF16)

    def phase3(s, carry):
        r0 = pl.multiple_of(s * SUPER, SUPER)
        rows = pl.ds(r0, SUPER)
        inter = [[None] * CH_PER for _ in range(GLA_HEADS)]
        for j in range(CH_PER):
            rj = pl.ds(r0 + j * CHUNK, CHUNK)
            for c in range(2):
                qg_c = jnp.concatenate([qg_ref[d, rj, c * LANES:(c + 1) * LANES] for d in range(2)], axis=1)
                lhs = jnp.concatenate([jnp.where(lo64, qg_c, zero_b),
                                       jnp.where(lo64, zero_b, qg_c)], axis=0)
                t = _dot_nt(lhs, sb16_ref[c, s * CH_PER + j])
                for hh in range(2):
                    inter[2 * c + hh][j] = t[hh * CHUNK:(hh + 1) * CHUNK]
        for hd in range(GLA_HEADS):
            cl = slice(hd * GLA_DV, (hd + 1) * GLA_DV)
            o = oi_ref[rows, cl] + jnp.concatenate(inter[hd], axis=0)
            y = o * lax.rsqrt(jnp.mean(o * o, axis=-1, keepdims=True) + EPS) * gn_ref[:, cl]
            g = gg_ref[rows, cl].astype(F32)
            o_ref[rows, cl] = (y * (g * jax.nn.sigmoid(g))).astype(BF16)
        return carry

    lax.fori_loop(0, nsuper, phase3, 0)


def _gla(gq, gk, gv, gg, lr, wd, bdec, gn, s_f, s_b):
    B, L, _ = gq.shape
    nchunk = L // CHUNK
    full = lambda shape: pl.BlockSpec(shape, lambda b: (0,) * len(shape))
    tok = lambda w: pl.BlockSpec((None, L, w), lambda b: (b, 0, 0))
    st_spec = pl.BlockSpec((None, 2, LANES, GLA_DV), lambda b: (b, 0, 0, 0))
    return pl.pallas_call(
        _gla_kernel,
        grid=(B,),
        in_specs=[tok(GLA_QK_WIDTH), tok(GLA_QK_WIDTH), tok(GLA_WIDTH), tok(GLA_WIDTH), tok(LANES),
                  full(wd.shape), full(bdec.shape), full(gn.shape), st_spec, st_spec],
        out_specs=tok(GLA_WIDTH),
        out_shape=jax.ShapeDtypeStruct((B, L, GLA_WIDTH), BF16),
        scratch_shapes=[pltpu.VMEM((L, 2 * GLA_QK_WIDTH), F32),
                        pltpu.VMEM((L, GLA_WIDTH), F32),
                        pltpu.VMEM((2, L, GLA_QK_WIDTH), BF16),
                        pltpu.VMEM((2, 2, nchunk, GLA_DV, LANES), F32),
                        pltpu.VMEM((2, nchunk, GLA_DV, 2 * LANES), BF16),
                        pltpu.VMEM((2, L // SUPER, 2 * CH_PER, GLA_QK_WIDTH), F32),
                        pltpu.VMEM((4, GLA_DV, LANES), F32)],
        compiler_params=_cparams(("arbitrary",)),
        name="gla_bidirectional",
    )(gq, gk, gv, gg, lr, wd, bdec, gn, s_f, s_b)


def _outproj_kernel(attn_ref, gla_ref, x_ref, mod_ref, w_ref, g2_ref, wr_ref,
                    x1_ref, h2_ref, afft_ref):
    y = _dot(attn_ref[...], w_ref[0:ATTN_WIDTH, :]) + _dot(gla_ref[...], w_ref[ATTN_WIDTH:, :])
    x1 = x_ref[...] + mod_ref[2:3, :] * y
    x1_ref[...] = x1
    h2 = _rms_mod(x1, g2_ref[...], mod_ref[3:4, :], mod_ref[4:5, :]).astype(BF16)
    half = D_MODEL // 2
    hi = pltpu.bitcast(h2[:, 0:half].astype(F32), jnp.uint32)
    lo = pltpu.bitcast(h2[:, half:].astype(F32), jnp.uint32)
    h2_ref[:, 0:half] = pltpu.bitcast(hi | (lo >> 16), F32)
    logits = _dot_nt(wr_ref[...], h2)
    e = jnp.exp(logits - jnp.max(logits, axis=0, keepdims=True))
    afft = e / jnp.sum(e, axis=0, keepdims=True)
    afft_ref[...] = afft
    pad = jnp.zeros((LANES - N_EXPERTS, afft.shape[1]), F32)
    h2_ref[:, half:half + LANES] = jnp.concatenate([afft, pad], axis=0).T


def _output_projection(attn, gla, x, mod3, w_out, g2, w_router, tm):
    B, L, _ = x.shape
    full = lambda shape: pl.BlockSpec(shape, lambda b, i: (0,) * len(shape))
    tok = lambda w: pl.BlockSpec((None, tm, w), lambda b, i: (b, i, 0))
    return pl.pallas_call(
        _outproj_kernel,
        grid=(B, L // tm),
        in_specs=[tok(ATTN_WIDTH), tok(GLA_WIDTH), tok(D_MODEL),
                  pl.BlockSpec((None, 6, D_MODEL), lambda b, i: (b, 0, 0)),
                  full(w_out.shape), full(g2.shape), full(w_router.shape)],
        out_specs=[tok(D_MODEL), tok(ROW_WORDS),
                   pl.BlockSpec((None, N_EXPERTS, tm), lambda b, i: (b, 0, i))],
        out_shape=[jax.ShapeDtypeStruct((B, L, D_MODEL), F32),
                   jax.ShapeDtypeStruct((B, L, ROW_WORDS), F32),
                   jax.ShapeDtypeStruct((B, N_EXPERTS, L), F32)],
        compiler_params=_cparams(("arbitrary", "arbitrary")),
        name="output_projection_router",
    )(attn, gla, x, mod3, w_out, g2, w_router)


def _topk_kernel(afft_ref, post_ref, pos_ref, gsel_ref, *, cap):
    nbatch, E, L = afft_ref.shape
    aff = afft_ref[...].reshape(nbatch * E, L)
    E = nbatch * E

    def search(i, thr):
        cand = thr | jnp.left_shift(jnp.int32(1), 30 - i)
        cnt = jnp.sum(jnp.where(aff >= pltpu.bitcast(cand, F32), 1.0, 0.0), axis=-1, keepdims=True)
        return jnp.where(cnt >= cap, cand, thr)

    thr_bits = lax.fori_loop(0, 31, search, jnp.zeros((E, 1), jnp.int32))
    thr = pltpu.bitcast(thr_bits, F32)
    above = aff > thr
    tie = aff == thr
    need = cap - jnp.sum(jnp.where(above, 1.0, 0.0), axis=-1, keepdims=True)

    upper = (lax.broadcasted_iota(jnp.int32, (LANES, LANES), 0)
             <= lax.broadcasted_iota(jnp.int32, (LANES, LANES), 1)).astype(BF16)

    def prefix(mask):
        parts = []
        run = jnp.zeros((E, 1), F32)
        for j in range(L // LANES):
            blk = jnp.where(mask[:, j * LANES:(j + 1) * LANES], 1.0, 0.0).astype(BF16)
            loc = _dot(blk, upper) + run
            parts.append(loc)
            run = loc[:, LANES - 1:LANES]
        return jnp.concatenate(parts, axis=1)

    tie_rank = prefix(tie)
    sel = above | (tie & (tie_rank <= need))
    slot = prefix(sel).astype(jnp.int32) - 1
    post = jnp.where(sel, slot, -1)
    gsel = jnp.where(sel, aff, 0.0)
    ne = E // nbatch
    pad_i = jnp.full((LANES - ne, L), -1, jnp.int32)
    pad_f = jnp.zeros((LANES - ne, L), F32)
    for bb in range(nbatch):
        rows = slice(bb * ne, (bb + 1) * ne)
        post_ref[bb] = post[rows]
        pos_ref[bb] = jnp.concatenate([post[rows], pad_i], axis=0).T
        gsel_ref[bb] = jnp.concatenate([gsel[rows], pad_f], axis=0).T


def _expert_choice(afft, cap, nbatch):
    B, E, L = afft.shape
    return pl.pallas_call(
        functools.partial(_topk_kernel, cap=cap),
        grid=(B // nbatch,),
        in_specs=[pl.BlockSpec((nbatch, E, L), lambda b: (b, 0, 0))],
        out_specs=[pl.BlockSpec((nbatch, E, L), lambda b: (b, 0, 0)),
                   pl.BlockSpec((nbatch, L, LANES), lambda b: (b, 0, 0)),
                   pl.BlockSpec((nbatch, L, LANES), lambda b: (b, 0, 0))],
        out_shape=[jax.ShapeDtypeStruct((B, E, L), jnp.int32),
                   jax.ShapeDtypeStruct((B, L, LANES), jnp.int32),
                   jax.ShapeDtypeStruct((B, L, LANES), F32)],
        compiler_params=_cparams(("arbitrary",)),
        name="expert_choice_topk",
    )(afft)


SC_CORES = 2
SC_SUBCORES = 16
SC_LANES = 16
SC_WINDOW = 128


def _dispatch(rows, slots, cap, first_expert, n_experts):
    L = slots.shape[1]
    W = rows.shape[1]
    n_pair = slots.shape[0] // N_EXPERTS * n_experts
    per_worker = n_pair // (SC_CORES * SC_SUBCORES)
    n_win = cap // SC_WINDOW
    mesh = plsc.VectorSubcoreMesh(core_axis_name="c", subcore_axis_name="s",
                                  num_cores=SC_CORES, num_subcores=SC_SUBCORES)

    def body(rows_hbm, slots_hbm, out_hbm, slot_v, *scratch):
        idx_v, buf_v, sem = scratch[:n_win], scratch[n_win], scratch[n_win + 1]
        worker = lax.axis_index("s") * SC_CORES + lax.axis_index("c")

        @pl.loop(0, per_worker)
        def _(p):
            pair = worker * per_worker + p
            batch = pair // n_experts
            first_tok = batch * L
            pltpu.sync_copy(slots_hbm.at[batch * N_EXPERTS + first_expert + pair % n_experts], slot_v)

            @pl.loop(0, L // SC_LANES)
            def _(i):
                v = slot_v[pl.ds(i * SC_LANES, SC_LANES)]
                tok = lax.iota(jnp.int32, SC_LANES) + (i * SC_LANES + first_tok)
                for w in range(n_win):
                    in_win = (v >= w * SC_WINDOW) & (v < (w + 1) * SC_WINDOW)
                    plsc.store_scatter(idx_v[w], [v - w * SC_WINDOW], tok, mask=in_win)

            for w in range(n_win):
                pltpu.async_copy(rows_hbm.at[idx_v[w]], buf_v, sem).wait()
                pltpu.sync_copy(buf_v, out_hbm.at[pl.ds(pair * cap + w * SC_WINDOW, SC_WINDOW)])

    return pl.kernel(
        body,
        out_type=jax.ShapeDtypeStruct((n_pair * cap, W), rows.dtype),
        mesh=mesh,
        scratch_types=[pltpu.VMEM((L,), jnp.int32)]
        + [pltpu.VMEM((SC_WINDOW,), jnp.int32) for _ in range(n_win)]
        + [pltpu.VMEM((SC_WINDOW, W), rows.dtype), pltpu.SemaphoreType.DMA],
        compiler_params=pltpu.CompilerParams(needs_layout_passes=False),
        name="moe_dispatch_gather",
    )(rows, slots)


def _ffn_kernel(xs_ref, mod_ref, wg_ref, wu_ref, wd_ref, y_ref, wgb_ref, wub_ref, wdb_ref, *,
                first_expert, scaled):
    nbatch, cap, _ = xs_ref.shape
    d = wg_ref.shape[0]
    dw = d // 2

    @pl.when(pl.program_id(1) == 0)
    def _():
        wgb_ref[...] = wg_ref[...].astype(BF16)
        wub_ref[...] = wu_ref[...].astype(BF16)
        wdb_ref[...] = wd_ref[...].astype(BF16)

    words = pltpu.bitcast(xs_ref[:, :, 0:dw].reshape(nbatch * cap, dw), jnp.uint32)
    xs = jnp.concatenate([pltpu.bitcast(words & jnp.uint32(0xFFFF0000), F32).astype(BF16),
                          pltpu.bitcast(words << 16, F32).astype(BF16)], axis=1)
    f = wg_ref.shape[1]
    half = f // 2
    acc = None
    for j in range(2):
        cols = slice(j * half, (j + 1) * half)
        g = _dot(xs, wgb_ref[:, cols])
        u = _dot(xs, wub_ref[:, cols])
        hid = (g * jax.nn.sigmoid(g) * u).astype(BF16)
        part = _dot(hid, wdb_ref[cols, :])
        acc = part if acc is None else acc + part
    if scaled:
        aff = xs_ref[:, :, dw:dw + LANES].reshape(nbatch * cap, LANES)
        lane = lax.broadcasted_iota(jnp.int32, aff.shape, 1)
        gate = jnp.sum(jnp.where(lane == first_expert + pl.program_id(0), aff, 0.0), axis=-1, keepdims=True)
        y = (acc * gate).reshape(nbatch, cap, d)
        for i in range(nbatch):
            y_ref[i] = y[i] * mod_ref[i, 5:6, :]
    else:
        y_ref[...] = acc.astype(BF16).reshape(nbatch, cap, d)


def _expert_ffn(xs, mod3, w_gate, w_up, w_down, first_expert, nbatch, scaled):
    B, E, cap, row_words = xs.shape
    d, f = w_gate.shape[1:]
    tok = pl.BlockSpec((nbatch, None, cap, d), lambda e, b: (b, e, 0, 0))
    return pl.pallas_call(
        functools.partial(_ffn_kernel, first_expert=first_expert, scaled=scaled),
        grid=(E, B // nbatch),
        in_specs=[pl.BlockSpec((nbatch, None, cap, row_words), lambda e, b: (b, e, 0, 0)),
                  pl.BlockSpec((nbatch, 6, d), lambda e, b: (b, 0, 0)),
                  pl.BlockSpec((None, d, f), lambda e, b: (e + first_expert, 0, 0)),
                  pl.BlockSpec((None, d, f), lambda e, b: (e + first_expert, 0, 0)),
                  pl.BlockSpec((None, f, d), lambda e, b: (e + first_expert, 0, 0))],
        out_specs=tok,
        out_shape=jax.ShapeDtypeStruct((B, E, cap, d), F32 if scaled else BF16),
        scratch_shapes=[pltpu.VMEM((d, f), BF16), pltpu.VMEM((d, f), BF16), pltpu.VMEM((f, d), BF16)],
        compiler_params=_cparams(("arbitrary", "arbitrary")),
        name="expert_swiglu",
    )(xs, mod3, w_gate, w_up, w_down)


SC_ADD_ROWS = 32


def _combine_add(acc, y, slots, cap, first_expert, n_experts):
    L = slots.shape[1]
    D = y.shape[1]
    assert slots.shape[0] // N_EXPERTS == SC_CORES * SC_SUBCORES
    n_win = cap // SC_ADD_ROWS
    mesh = plsc.VectorSubcoreMesh(core_axis_name="c", subcore_axis_name="s",
                                  num_cores=SC_CORES, num_subcores=SC_SUBCORES)

    def body(acc_hbm, y_hbm, slots_hbm, slot_v, *scratch):
        idx_v, y_v, o_v, sem = scratch[:n_win], scratch[n_win], scratch[n_win + 1], scratch[n_win + 2]
        batch = lax.axis_index("s") * SC_CORES + lax.axis_index("c")

        @pl.loop(0, n_experts)
        def _(el):
            pltpu.sync_copy(slots_hbm.at[batch * N_EXPERTS + first_expert + el], slot_v)

            @pl.loop(0, L // SC_LANES)
            def _(i):
                v = slot_v[pl.ds(i * SC_LANES, SC_LANES)]
                tok = lax.iota(jnp.int32, SC_LANES) + (i * SC_LANES + batch * L)
                for w in range(n_win):
                    in_win = (v >= w * SC_ADD_ROWS) & (v < (w + 1) * SC_ADD_ROWS)
                    plsc.store_scatter(idx_v[w], [v - w * SC_ADD_ROWS], tok, mask=in_win)

            for w in range(n_win):
                row0 = (batch * n_experts + el) * cap + w * SC_ADD_ROWS
                pltpu.sync_copy(y_hbm.at[pl.ds(row0, SC_ADD_ROWS)], y_v)
                pltpu.async_copy(acc_hbm.at[idx_v[w]], o_v, sem).wait()

                @pl.loop(0, SC_ADD_ROWS)
                def _(r):
                    for c in range(D // SC_LANES):
                        lanes = pl.ds(c * SC_LANES, SC_LANES)
                        plsc.addupdate(o_v.at[r, lanes], y_v[r, lanes])

                pltpu.async_copy(o_v, acc_hbm.at[idx_v[w]], sem).wait()

    pl.kernel(
        body,
        out_type=(),
        mesh=mesh,
        scratch_types=[pltpu.VMEM((L,), jnp.int32)]
        + [pltpu.VMEM((SC_ADD_ROWS,), jnp.int32) for _ in range(n_win)]
        + [pltpu.VMEM((SC_ADD_ROWS, D), F32), pltpu.VMEM((SC_ADD_ROWS, D), F32), pltpu.SemaphoreType.DMA],
        compiler_params=pltpu.CompilerParams(needs_layout_passes=False),
        name="moe_combine_row_add",
    )(acc, y, slots)


def _combine_kernel(pos_ref, gsel_ref, *rest, cap, n_groups, first_expert):
    y_refs = rest[:n_groups]
    x1_ref, mod_ref, o_ref, acc_ref = rest[n_groups:]
    tt = pos_ref.shape[0]
    per_group = (N_EXPERTS - first_expert) // n_groups
    slot = lax.broadcasted_iota(jnp.int32, (tt, cap), 1)
    for e in range(first_expert, N_EXPERTS):
        j = e - first_expert
        onehot = jnp.where(pos_ref[:, e:e + 1] == slot, 1.0, 0.0).astype(BF16)
        part = gsel_ref[:, e:e + 1] * _dot(onehot, y_refs[j // per_group][j % per_group])
        if j == 0:
            acc_ref[...] = part
        else:
            acc_ref[...] += part
    o_ref[...] = x1_ref[...] + mod_ref[5:6, :] * acc_ref[...]


def _combine(pos, gsel, ys, x1, mod3, cap, tt, first_expert):
    B, L, _ = x1.shape
    per_group = ys[0].shape[1]
    tok = lambda w: pl.BlockSpec((None, tt, w), lambda b, i: (b, i, 0))
    return pl.pallas_call(
        functools.partial(_combine_kernel, cap=cap, n_groups=len(ys), first_expert=first_expert),
        grid=(B, L // tt),
        in_specs=[tok(LANES), tok(LANES)]
        + [pl.BlockSpec((None, per_group, cap, D_MODEL), lambda b, i: (b, 0, 0, 0)) for _ in ys]
        + [tok(D_MODEL), pl.BlockSpec((None, 6, D_MODEL), lambda b, i: (b, 0, 0))],
        out_specs=tok(D_MODEL),
        out_shape=jax.ShapeDtypeStruct((B, L, D_MODEL), F32),
        scratch_shapes=[pltpu.VMEM((tt, D_MODEL), F32)],
        compiler_params=_cparams(("arbitrary", "arbitrary")),
        name="moe_combine",
    )(pos, gsel, *ys, x1, mod3)


def _rope_tables(L):
    inv = ROPE_BASE ** (-jnp.arange(ROPE_FREQS, dtype=F32) / ROPE_FREQS)
    pos = jnp.arange(L)
    row = (pos // GRID_W).astype(F32)[:, None] * inv
    col = (pos % GRID_W).astype(F32)[:, None] * inv
    cos = jnp.concatenate([jnp.cos(row), jnp.cos(row), jnp.cos(col), jnp.cos(col)], axis=1)
    sin = jnp.concatenate([-jnp.sin(row), jnp.sin(row), -jnp.sin(col), jnp.sin(col)], axis=1)
    return jnp.tile(cos, (1, 2)), jnp.tile(sin, (1, 2))


def _head_mean_matrix(n):
    idx = np.arange(n) // HEAD_DIM
    return jnp.asarray((idx[:, None] == idx[None, :]).astype(np.float32) / HEAD_DIM, dtype=BF16)


def kernel(x, c, ctx, c_ctx, w_mod, b_mod, norm1_g, w_in, q_norm_g, k_norm_g, attn_sink,
           w_decay_fwd, b_decay_fwd, w_decay_bwd, b_decay_bwd, gla_norm_g, w_out, norm2_g,
           w_router, w_e_gate, w_e_up, w_e_down):
    B, L, D = x.shape
    cap = CAPACITY_FACTOR * L // N_EXPERTS
    layer = 0

    rows = ((B + 1 + 7) // 8) * 8
    cc = jnp.concatenate([c, c_ctx[None, :], jnp.zeros((rows - B - 1, D), F32)], axis=0)
    mod_all = _modulation(cc, w_mod[layer], b_mod[layer])
    mod3 = mod_all[:B].reshape(B, 6, D)
    modc = mod_all[B].reshape(6, D)

    w = w_in[layer]
    o = np.cumsum([0, ATTN_WIDTH, KV_WIDTH, KV_WIDTH, GLA_QK_WIDTH, GLA_QK_WIDTH,
                   GLA_WIDTH, GLA_WIDTH, GATE_RANK, GATE_RANK])
    w_lr = jnp.concatenate([w[:, o[7]:o[9]]] * 3 + [jnp.zeros((D, LANES - 6 * GATE_RANK), F32)], axis=1)
    head_order = np.arange(N_Q_HEADS).reshape(N_KV_HEADS, -1).T.reshape(-1)
    attn_perm = (head_order[:, None] * HEAD_DIM + np.arange(HEAD_DIM)[None, :]).reshape(-1)
    w_in_r = jnp.concatenate([w[:, attn_perm], w[:, o[1]:o[7]], w_lr], axis=1).astype(BF16)
    w_out_r = jnp.concatenate([w_out[layer][attn_perm], w_out[layer][ATTN_WIDTH:]], axis=0).astype(BF16)
    w_ctx = jnp.concatenate([w[:, o[1]:o[3]], w[:, o[4]:o[6]], w_lr], axis=1).astype(BF16)
    wd2 = jnp.zeros((2 * GATE_RANK, 2 * GLA_QK_WIDTH), F32)
    wd2 = wd2.at[0:GATE_RANK, 0:GLA_QK_WIDTH].set(w_decay_fwd[layer])
    wd2 = wd2.at[GATE_RANK:, GLA_QK_WIDTH:].set(w_decay_bwd[layer])
    wd_hi = wd2.astype(BF16)
    wd_lo = (wd2 - wd_hi.astype(F32)).astype(BF16)
    wd = jnp.concatenate([wd_hi, wd_hi, wd_lo,
                          jnp.zeros((LANES - 6 * GATE_RANK, 2 * GLA_QK_WIDTH), BF16)], axis=0)
    bdec = jnp.concatenate([b_decay_fwd[layer], b_decay_bwd[layer]])[None, :]
    g1 = norm1_g[layer][None, :]
    g2 = norm2_g[layer][None, :]
    qg = jnp.tile(q_norm_g[layer], N_Q_HEADS)[None, :]
    kg2 = jnp.tile(k_norm_g[layer], N_KV_HEADS)[None, :]
    gn = jnp.tile(gla_norm_g[layer], GLA_HEADS)[None, :]
    bd512 = _head_mean_matrix(ATTN_WIDTH)
    bd128 = _head_mean_matrix(KV_WIDTH)
    cos_t, sin_t = _rope_tables(L)
    w_router_t = w_router[layer].T.astype(BF16)

    kc, vc, s_f, s_b = _context_side(ctx, modc, g1, w_ctx, kg2, bd128, wd, bdec)
    q, k, v, gq, gk, gv, gg, lr = _input_projection(
        x, mod3, g1, w_in_r, qg, kg2, bd512, cos_t, sin_t, tm=1024)
    attn = _window_attention(attn_sink[layer], q, k, v, kc, vc, tq=512)
    gla = _gla(gq, gk, gv, gg, lr, wd, bdec, gn, s_f, s_b)
    x1, h2, afft = _output_projection(attn, gla, x, mod3, w_out_r, g2, w_router_t, tm=1024)
    post, pos, gsel = _expert_choice(afft, cap, nbatch=4)
    n_groups = 4
    per_group = N_EXPERTS // n_groups
    rows = h2.reshape(B * L, ROW_WORDS)
    slots = post.reshape(B * N_EXPERTS, L)
    stream = jax.new_ref(x1.reshape(B * L, D))
    ys = []
    for g in range(n_groups):
        xs = _dispatch(rows, slots, cap, g * per_group, per_group).reshape(B, per_group, cap, ROW_WORDS)
        y = _expert_ffn(xs, mod3, w_e_gate[layer], w_e_up[layer], w_e_down[layer], g * per_group,
                        nbatch=4, scaled=(g == 0))
        if g == 0:
            _combine_add(stream, y.reshape(B * per_group * cap, D), slots, cap, 0, per_group)
        else:
            ys.append(y)
    x1 = jax.freeze(stream).reshape(B, L, D)
    return _combine(pos, gsel, ys, x1, mod3, cap, tt=1024, first_expert=per_group)
```

```python
import functools

import jax
import jax.numpy as jnp
import numpy as np
from jax import lax
from jax.experimental import pallas as pl
from jax.experimental.pallas import tpu as pltpu
from jax.experimental.pallas import tpu_sc as plsc

D_MODEL = 1024
GRID_W = 64
HEAD_DIM = 64
N_Q_HEADS = 8
N_KV_HEADS = 2
BLOCK = 128
ROPE_FREQS = 16
ROPE_BASE = 10000.0
GLA_HEADS = 4
GLA_DV = 128
GLA_DK = 64
GATE_RANK = 16
GATE_NORMALIZER = 16.0
CHUNK = 64
N_EXPERTS = 16
CAPACITY_FACTOR = 2
ATTN_WIDTH = 512
KV_WIDTH = 128
GLA_QK_WIDTH = 256
GLA_WIDTH = 512
EPS = 1e-6
NEG_INF = -1e30
LOG2E = 1.4426950408889634

LANES = 128
ROW_WORDS = D_MODEL // 2 + LANES
VMEM_LIMIT = 56 * 1024 * 1024

F32 = jnp.float32
BF16 = jnp.bfloat16
HI = lax.Precision.HIGHEST


def _cparams(sem):
    return pltpu.CompilerParams(dimension_semantics=sem, vmem_limit_bytes=VMEM_LIMIT)


def _dot(a, b):
    return jnp.dot(a, b, preferred_element_type=F32)


def _dot_hi(a, b):
    return jnp.dot(a, b, preferred_element_type=F32, precision=HI)


def _dot_nt(a, b):
    return lax.dot_general(a, b, (((1,), (1,)), ((), ())), preferred_element_type=F32)


def _dot_tn(a, b, precision=None):
    return lax.dot_general(a, b, (((0,), (0,)), ((), ())), preferred_element_type=F32,
                           precision=precision)


def _split2(t):
    hi = t.astype(BF16)
    lo = (t - hi.astype(F32)).astype(BF16)
    return hi, lo


def _rms_mod(t, g, shift, scale):
    y = t * lax.rsqrt(jnp.mean(t * t, axis=-1, keepdims=True) + EPS)
    return (y * g) * (1.0 + scale) + shift


def _log_decay(lr, wd3, bias):
    hi = lr.astype(BF16)
    lo = (lr - hi.astype(F32)).astype(BF16)
    lane = lax.broadcasted_iota(jnp.int32, lr.shape, 1)
    second = (lane >= 2 * GATE_RANK) & (lane < 4 * GATE_RANK)
    z = _dot(jnp.where(second, lo, hi), wd3) + bias
    return (jnp.minimum(z, 0.0) - jnp.log(1.0 + jnp.exp(-jnp.abs(z)))) * (1.0 / GATE_NORMALIZER)


def _lane_lo(shape):
    return (lax.broadcasted_iota(jnp.int32, shape, len(shape) - 1) % LANES) < HEAD_DIM


def _mod_kernel(c_ref, w_ref, b_ref, o_ref):
    c = c_ref[...]
    s = c * jax.nn.sigmoid(c)
    o_ref[...] = _dot_hi(s, w_ref[...]) + b_ref[...]


def _modulation(cc, w_mod, b_mod):
    m = cc.shape[0]
    n = w_mod.shape[1]
    tn = 1024
    return pl.pallas_call(
        _mod_kernel,
        grid=(n // tn,),
        in_specs=[pl.BlockSpec((m, D_MODEL), lambda j: (0, 0)),
                  pl.BlockSpec((D_MODEL, tn), lambda j: (0, j)),
                  pl.BlockSpec((1, tn), lambda j: (0, j))],
        out_specs=pl.BlockSpec((m, tn), lambda j: (0, j)),
        out_shape=jax.ShapeDtypeStruct((m, n), F32),
        compiler_params=_cparams(("arbitrary",)),
        name="adaln_mod",
    )(cc, w_mod, b_mod.reshape(1, n))


def _ctx_kernel(ctx_ref, mod_ref, g1_ref, w_ref, kg_ref, bd_ref, wd_ref, bdec_ref,
                kc_ref, vc_ref, sf_ref, sb_ref):
    n = ctx_ref.shape[0]
    h = _rms_mod(ctx_ref[...], g1_ref[...], mod_ref[0:1, :], mod_ref[1:2, :]).astype(BF16)
    pc = _dot(h, w_ref[...])
    ak = pc[:, 0:128]
    av = pc[:, 128:256]
    gk = pc[:, 256:512]
    gv = pc[:, 512:1024].astype(BF16)
    lr = pc[:, 1024:1152]
    sq_hi, sq_lo = _split2(ak * ak)
    ms = _dot(sq_hi, bd_ref[...]) + _dot(sq_lo, bd_ref[...])
    kn = ak * lax.rsqrt(ms + EPS) * kg_ref[...]
    kc_ref[...] = kn.astype(BF16)
    vc_ref[...] = av.astype(BF16)
    la = _log_decay(lr, wd_ref[...], bdec_ref[...])
    r = lax.broadcasted_iota(jnp.int32, (n, n), 0)
    cidx = lax.broadcasted_iota(jnp.int32, (n, n), 1)
    after = (cidx > r).astype(F32)
    before = (cidx < r).astype(F32)
    w_f = jnp.exp(_dot_hi(after, la[:, 0:256]))
    w_b = jnp.exp(_dot_hi(before, la[:, 256:512]))
    lo = _lane_lo((n, LANES))
    for w, out in ((w_f, sf_ref), (w_b, sb_ref)):
        kw = gk * w
        for c in range(2):
            kwc = kw[:, c * LANES:(c + 1) * LANES]
            k_lo = jnp.where(lo, kwc, 0.0).astype(BF16)
            k_hi = jnp.where(lo, 0.0, kwc).astype(BF16)
            v0 = gv[:, (2 * c) * GLA_DV:(2 * c + 1) * GLA_DV]
            v1 = gv[:, (2 * c + 1) * GLA_DV:(2 * c + 2) * GLA_DV]
            out[c] = _dot_tn(v0, k_lo) + _dot_tn(v1, k_hi)


def _context_side(ctx, modc, g1, w_ctx, kg2, bd128, wd, bdec):
    B, n, _ = ctx.shape
    full = lambda shape: pl.BlockSpec(shape, lambda b: (0,) * len(shape))
    kv_spec = pl.BlockSpec((None, n, KV_WIDTH), lambda b: (b, 0, 0))
    st_spec = pl.BlockSpec((None, 2, LANES, GLA_DV), lambda b: (b, 0, 0, 0))
    kv_shape = jax.ShapeDtypeStruct((B, n, KV_WIDTH), BF16)
    st_shape = jax.ShapeDtypeStruct((B, 2, LANES, GLA_DV), F32)
    return pl.pallas_call(
        _ctx_kernel,
        grid=(B,),
        in_specs=[pl.BlockSpec((None, n, D_MODEL), lambda b: (b, 0, 0)),
                  full(modc.shape), full(g1.shape), full(w_ctx.shape), full(kg2.shape),
                  full(bd128.shape), full(wd.shape), full(bdec.shape)],
        out_specs=[kv_spec, kv_spec, st_spec, st_spec],
        out_shape=[kv_shape, kv_shape, st_shape, st_shape],
        compiler_params=_cparams(("arbitrary",)),
        name="context_side",
    )(ctx, modc, g1, w_ctx, kg2, bd128, wd, bdec)


def _swap16(t):
    n = t.shape[1]
    first = (lax.broadcasted_iota(jnp.int32, t.shape, 1) % 32) < ROPE_FREQS
    return jnp.where(first, pltpu.roll(t, n - ROPE_FREQS, 1), pltpu.roll(t, ROPE_FREQS, 1))


def _inproj_kernel(x_ref, mod_ref, g1_ref, w_ref, qg_ref, kg_ref, bd_ref, cos_ref, sin_ref,
                   q_ref, k_ref, v_ref, gq_ref, gk_ref, gv_ref, gg_ref, lr_ref):
    h = _rms_mod(x_ref[...], g1_ref[...], mod_ref[0:1, :], mod_ref[1:2, :]).astype(BF16)
    cos = cos_ref[...]
    sin = sin_ref[...]

    def head_norm_rope(t, g, bd, reps):
        ms = _dot((t * t).astype(BF16), bd)
        tn = t * lax.rsqrt(ms + EPS) * g
        c = jnp.concatenate([cos] * reps, axis=1) if reps > 1 else cos
        s = jnp.concatenate([sin] * reps, axis=1) if reps > 1 else sin
        return tn * c + _swap16(tn) * s

    aq = _dot(h, w_ref[:, 0:512])
    q = head_norm_rope(aq, qg_ref[...], bd_ref[...], 4) * (HEAD_DIM ** -0.5 * LOG2E)
    q_ref[...] = q.astype(BF16)
    akv = _dot(h, w_ref[:, 512:768])
    k = head_norm_rope(akv[:, 0:128], kg_ref[...], bd_ref[0:128, 0:128], 1)
    k_ref[...] = k.astype(BF16)
    v_ref[...] = akv[:, 128:256].astype(BF16)
    gqk = _dot(h, w_ref[:, 768:1280])
    gq_ref[...] = (gqk[:, 0:256] * (GLA_DK ** -0.5)).astype(BF16)
    gk_ref[...] = gqk[:, 256:512].astype(BF16)
    gv_ref[...] = _dot(h, w_ref[:, 1280:1792]).astype(BF16)
    gg_ref[...] = _dot(h, w_ref[:, 1792:2304]).astype(BF16)
    lr_ref[...] = _dot(h, w_ref[:, 2304:2432])


def _input_projection(x, mod3, g1, w_in_r, qg, kg2, bd512, cos_t, sin_t, tm):
    B, L, _ = x.shape
    full = lambda shape: pl.BlockSpec(shape, lambda b, i: (0,) * len(shape))
    tok = lambda w: pl.BlockSpec((None, tm, w), lambda b, i: (b, i, 0))
    widths = (ATTN_WIDTH, KV_WIDTH, KV_WIDTH, GLA_QK_WIDTH, GLA_QK_WIDTH, GLA_WIDTH, GLA_WIDTH, LANES)
    dtypes = (BF16,) * 7 + (F32,)
    return pl.pallas_call(
        _inproj_kernel,
        grid=(B, L // tm),
        in_specs=[tok(D_MODEL),
                  pl.BlockSpec((None, 6, D_MODEL), lambda b, i: (b, 0, 0)),
                  full(g1.shape), full(w_in_r.shape), full(qg.shape), full(kg2.shape),
                  full(bd512.shape),
                  pl.BlockSpec((tm, LANES), lambda b, i: (i, 0)),
                  pl.BlockSpec((tm, LANES), lambda b, i: (i, 0))],
        out_specs=[tok(w) for w in widths],
        out_shape=[jax.ShapeDtypeStruct((B, L, w), dt) for w, dt in zip(widths, dtypes)],
        compiler_params=_cparams(("arbitrary", "arbitrary")),
        name="input_projection",
    )(x, mod3, g1, w_in_r, qg, kg2, bd512, cos_t, sin_t)


def _attn_kernel(sink_ref, q_ref, kp_ref, ko_ref, kn_ref, vp_ref, vo_ref, vn_ref,
                 kc_ref, vc_ref, o_ref):
    i = pl.program_id(1)
    ni = pl.num_programs(1)
    nsub = q_ref.shape[0] // BLOCK
    ncol = ATTN_WIDTH // LANES
    win = 3 * BLOCK
    ucol = 4
    half_rows = ucol * BLOCK
    k_win = jnp.concatenate([kp_ref[...], ko_ref[...], kn_ref[...]], axis=0)
    v_win = jnp.concatenate([vp_ref[...], vo_ref[...], vn_ref[...]], axis=0)
    k_ctx = kc_ref[...]
    lo_w = _lane_lo(v_win.shape)
    lo_c = _lane_lo(vc_ref.shape)
    lo_q = _lane_lo((BLOCK, LANES))
    zero = jnp.zeros((), BF16)
    one = jnp.ones((), BF16)
    v0_c, v0_w = jnp.where(lo_c, vc_ref[...], one), jnp.where(lo_w, v_win, one)
    v1_c, v1_w = jnp.where(lo_c, one, vc_ref[...]), jnp.where(lo_w, one, v_win)
    qi = lax.broadcasted_iota(jnp.int32, (half_rows, BLOCK), 0) % BLOCK
    kj = lax.broadcasted_iota(jnp.int32, (half_rows, BLOCK), 1)
    no_prev = jnp.where(i > 0, 0, BLOCK)
    no_next = jnp.where(i < ni - 1, 0, BLOCK)
    row_head = lax.broadcasted_iota(jnp.int32, (half_rows, 1), 0) // BLOCK
    lo_o = _lane_lo((half_rows, LANES))
    for t in range(nsub):
        rows = slice(t * BLOCK, (t + 1) * BLOCK)
        keys = slice(t * BLOCK, t * BLOCK + win)
        cols = [q_ref[rows, c * LANES:(c + 1) * LANES] for c in range(ncol)]
        prev_ok = kj >= qi + (no_prev if t == 0 else 0)
        next_ok = kj <= qi - (no_next if t == nsub - 1 else 0)
        for p in range(ncol // ucol):
            outs = []
            qs = jnp.concatenate([jnp.where(lo_q, qc, zero) for qc in cols[ucol * p:ucol * (p + 1)]]
                                 + [jnp.where(lo_q, zero, qc) for qc in cols[ucol * p:ucol * (p + 1)]], axis=0)
            s_c_all = _dot_nt(qs, k_ctx)
            s_w_all = _dot_nt(qs, k_win[keys])
            for g, (vv_c, vv_w) in enumerate(((v0_c, v0_w), (v1_c, v1_w))):
                head = g * ncol + ucol * p
                sink_g = jnp.full((half_rows, 1), sink_ref[head + ucol - 1], F32)
                for j in range(ucol - 2, -1, -1):
                    sink_g = jnp.where(row_head <= j, sink_ref[head + j], sink_g)
                sink_g = sink_g * LOG2E
                s_c = s_c_all[g * half_rows:(g + 1) * half_rows]
                s_w = s_w_all[g * half_rows:(g + 1) * half_rows]
                s_p = jnp.where(prev_ok, s_w[:, 0:BLOCK], NEG_INF)
                s_o = s_w[:, BLOCK:2 * BLOCK]
                s_n = jnp.where(next_ok, s_w[:, 2 * BLOCK:win], NEG_INF)
                m = jnp.maximum(jnp.maximum(jnp.max(s_c, axis=-1, keepdims=True),
                                            jnp.max(jnp.maximum(jnp.maximum(s_p, s_o), s_n),
                                                    axis=-1, keepdims=True)), sink_g)
                e_c = jnp.exp2(s_c - m).astype(BF16)
                e_w = jnp.concatenate([jnp.exp2(s_p - m), jnp.exp2(s_o - m), jnp.exp2(s_n - m)],
                                      axis=1).astype(BF16)
                acc = _dot(e_c, vv_c) + _dot(e_w, vv_w[keys])
                outs.append(acc / (pltpu.roll(acc, HEAD_DIM, 1) + jnp.exp2(sink_g - m)))
            o = jnp.where(lo_o, outs[0], outs[1]).astype(BF16)
            for j in range(ucol):
                c = ucol * p + j
                o_ref[rows, c * LANES:(c + 1) * LANES] = o[j * BLOCK:(j + 1) * BLOCK]


def _window_attention(sink, q, k, v, kc, vc, tq):
    B, L, _ = q.shape
    nb = L // BLOCK
    nsub = tq // BLOCK
    n_ctx = kc.shape[1]
    prev = pl.BlockSpec((None, BLOCK, KV_WIDTH), lambda b, n: (b, jnp.maximum(n * nsub - 1, 0), 0))
    own = pl.BlockSpec((None, tq, KV_WIDTH), lambda b, n: (b, n, 0))
    nxt = pl.BlockSpec((None, BLOCK, KV_WIDTH),
                       lambda b, n: (b, jnp.minimum((n + 1) * nsub, nb - 1), 0))
    cspec = pl.BlockSpec((None, n_ctx, KV_WIDTH), lambda b, n: (b, 0, 0))
    return pl.pallas_call(
        _attn_kernel,
        grid=(B, L // tq),
        in_specs=[pl.BlockSpec(memory_space=pltpu.SMEM),
                  pl.BlockSpec((None, tq, ATTN_WIDTH), lambda b, n: (b, n, 0)),
                  prev, own, nxt, prev, own, nxt, cspec, cspec],
        out_specs=pl.BlockSpec((None, tq, ATTN_WIDTH), lambda b, n: (b, n, 0)),
        out_shape=jax.ShapeDtypeStruct((B, L, ATTN_WIDTH), BF16),
        compiler_params=_cparams(("arbitrary", "arbitrary")),
        name="window_attention",
    )(sink, q, k, k, k, v, v, v, kc, vc)


SUPER = 256
CH_PER = SUPER // CHUNK
HALF = 128


def _dot_split(m, parts):
    return _dot(m, parts[0]) + _dot(m, parts[1])


def _gla_kernel(gq_ref, gk_ref, gv_ref, gg_ref, lr_ref, wd_ref, bdec_ref, gn_ref, sf_ref, sb_ref,
                o_ref, la_ref, oi_ref, qg_ref, kv_ref, sb16_ref, dec_ref, st_ref):
    L = gq_ref.shape[0]
    nsuper = L // SUPER
    nchunk = L // CHUNK
    half = CHUNK // 2
    la_ref[...] = _log_decay(lr_ref[...], wd_ref[...], bdec_ref[...])

    r = lax.broadcasted_iota(jnp.int32, (SUPER, SUPER), 0)
    cidx = lax.broadcasted_iota(jnp.int32, (SUPER, SUPER), 1)
    same = (r // CHUNK) == (cidx // CHUNK)
    pr = r % CHUNK
    pc = cidx % CHUNK
    one = jnp.float32(1.0)
    zero = jnp.float32(0.0)
    in_f = jnp.where(pc <= pr, one, zero)
    in_b = jnp.where(pc >= pr, one, zero)
    ref_f = jnp.where(pc < half, one, zero)
    ref_b = jnp.where(pc >= half, one, zero)
    m1_f = jnp.where(same, in_f - ref_f, zero).astype(BF16)
    m1_b = jnp.where(same, in_b - ref_b, zero).astype(BF16)
    rh = lax.broadcasted_iota(jnp.int32, (HALF, 2 * HALF), 0)
    ch = lax.broadcasted_iota(jnp.int32, (HALF, 2 * HALF), 1) % HALF
    same_h = (rh // CHUNK) == (ch // CHUNK)
    mask_f = jnp.where(same_h, jnp.where(ch % CHUNK <= rh % CHUNK, one, zero), zero) > 0.5
    mask_b = jnp.where(same_h, jnp.where(ch % CHUNK >= rh % CHUNK, one, zero), zero) > 0.5
    rr = lax.broadcasted_iota(jnp.int32, (2 * CH_PER, SUPER), 0)
    rc = lax.broadcasted_iota(jnp.int32, (2 * CH_PER, SUPER), 1)
    in_chunk = jnp.where((rc // CHUNK) == (rr % CH_PER), one, zero)
    first = jnp.where((rc % CHUNK) < half, 1, 0)
    is_tot = jnp.where(rr >= CH_PER, 1, 0)
    rs_f = (in_chunk * jnp.where(first != is_tot, one, zero)).astype(BF16)
    rs_b = (in_chunk * jnp.where(first == is_tot, one, zero)).astype(BF16)
    lo_h = _lane_lo((HALF, LANES))
    zero_blk = jnp.zeros((HALF, GLA_DV), BF16)

    def phase1(s, carry):
        r0 = pl.multiple_of(s * SUPER, SUPER)
        rows = pl.ds(r0, SUPER)
        q = gq_ref[rows, :].astype(F32)
        k = gk_ref[rows, :].astype(F32)
        qes, kes, kds = [], [], []
        for d, (m1, rs) in enumerate(((m1_f, rs_f), (m1_b, rs_b))):
            parts = _split2(la_ref[rows, d * GLA_QK_WIDTH:(d + 1) * GLA_QK_WIDTH])
            x1 = _dot_split(m1, parts)
            erow = jnp.exp(_dot_split(rs, parts))
            dec = erow[0:CH_PER] * erow[CH_PER:2 * CH_PER]
            dec_ref[d, s] = jnp.concatenate([dec, dec], axis=0)
            qe = q * jnp.exp(x1)
            ke = k * jnp.exp(-x1)
            qg_parts, kd_parts = [], []
            for j in range(CH_PER):
                rj = slice(j * CHUNK, (j + 1) * CHUNK)
                qg_parts.append(qe[rj] * erow[j:j + 1])
                kd_parts.append(ke[rj] * erow[CH_PER + j:CH_PER + j + 1])
            qg_ref[d, rows, :] = jnp.concatenate(qg_parts, axis=0).astype(BF16)
            qes.append(qe.astype(BF16))
            kes.append(ke)
            kds.append(jnp.concatenate(kd_parts, axis=0).astype(BF16))
        for c in range(2):
            cl = slice(c * LANES, (c + 1) * LANES)
            vpair = gv_ref[rows, 2 * c * GLA_DV:(2 * c + 2) * GLA_DV]
            for blk in range(SUPER // HALF):
                rb = slice(blk * HALF, (blk + 1) * HALF)
                vbd = jnp.concatenate(
                    [jnp.concatenate([vpair[rb, 0:GLA_DV], zero_blk], axis=1),
                     jnp.concatenate([zero_blk, vpair[rb, GLA_DV:2 * GLA_DV]], axis=1)], axis=0)
                o2 = None
                for d, mask in enumerate((mask_f, mask_b)):
                    ke_cb = kes[d][rb, cl]
                    ke_st = jnp.concatenate([jnp.where(lo_h, ke_cb, zero),
                                             jnp.where(lo_h, zero, ke_cb)], axis=0).astype(BF16)
                    a = _dot_nt(qes[d][rb, cl], ke_st)
                    o = _dot(jnp.where(mask, a, zero).astype(BF16), vbd)
                    o2 = o if o2 is None else o2 + o
                oi_ref[pl.ds(r0 + blk * HALF, HALF), 2 * c * GLA_DV:(2 * c + 2) * GLA_DV] = o2
            for d in range(2):
                for j in range(CH_PER):
                    rj = slice(j * CHUNK, (j + 1) * CHUNK)
                    t = _dot_tn(kds[d][rj, cl], vpair[rj])
                    kv = jnp.concatenate([t[0:GLA_DK, 0:GLA_DV], t[GLA_DK:, GLA_DV:]], axis=0)
                    kv_ref[d, c, s * CH_PER + j] = kv.T
        return carry

    lax.fori_loop(0, nsuper, phase1, 0)

    st_ref[0] = sf_ref[0]
    st_ref[1] = sf_ref[1]
    st_ref[2] = sb_ref[0]
    st_ref[3] = sb_ref[1]

    def phase2(n, carry):
        for d in range(2):
            idx = n if d == 0 else nchunk - 1 - n
            dec = dec_ref[d, idx // CH_PER, pl.ds(idx % CH_PER, 1), :]
            for c in range(2):
                st = st_ref[2 * d + c]
                sb16_ref[c, idx, :, d * LANES:(d + 1) * LANES] = st.astype(BF16)
                st_ref[2 * d + c] = dec[:, c * LANES:(c + 1) * LANES] * st + kv_ref[d, c, idx]
        return carry

    lax.fori_loop(0, nchunk, phase2, 0)

    lo64 = _lane_lo((CHUNK, 2 * LANES))
    zero_b = jnp.zeros((), BF16)

    def phase3(s, carry):
        r0 = pl.multiple_of(s * SUPER, SUPER)
        rows = pl.ds(r0, SUPER)
        inter = [[None] * CH_PER for _ in range(GLA_HEADS)]
        for j in range(CH_PER):
            rj = pl.ds(r0 + j * CHUNK, CHUNK)
            for c in range(2):
                qg_c = jnp.concatenate([qg_ref[d, rj, c * LANES:(c + 1) * LANES] for d in range(2)], axis=1)
                lhs = jnp.concatenate([jnp.where(lo64, qg_c, zero_b),
                                       jnp.where(lo64, zero_b, qg_c)], axis=0)
                t = _dot_nt(lhs, sb16_ref[c, s * CH_PER + j])
                for hh in range(2):
                    inter[2 * c + hh][j] = t[hh * CHUNK:(hh + 1) * CHUNK]
        for hd in range(GLA_HEADS):
            cl = slice(hd * GLA_DV, (hd + 1) * GLA_DV)
            o = oi_ref[rows, cl] + jnp.concatenate(inter[hd], axis=0)
            y = o * lax.rsqrt(jnp.mean(o * o, axis=-1, keepdims=True) + EPS) * gn_ref[:, cl]
            g = gg_ref[rows, cl].astype(F32)
            o_ref[rows, cl] = (y * (g * jax.nn.sigmoid(g))).astype(BF16)
        return carry

    lax.fori_loop(0, nsuper, phase3, 0)


def _gla(gq, gk, gv, gg, lr, wd, bdec, gn, s_f, s_b):
    B, L, _ = gq.shape
    nchunk = L // CHUNK
    full = lambda shape: pl.BlockSpec(shape, lambda b: (0,) * len(shape))
    tok = lambda w: pl.BlockSpec((None, L, w), lambda b: (b, 0, 0))
    st_spec = pl.BlockSpec((None, 2, LANES, GLA_DV), lambda b: (b, 0, 0, 0))
    return pl.pallas_call(
        _gla_kernel,
        grid=(B,),
        in_specs=[tok(GLA_QK_WIDTH), tok(GLA_QK_WIDTH), tok(GLA_WIDTH), tok(GLA_WIDTH), tok(LANES),
                  full(wd.shape), full(bdec.shape), full(gn.shape), st_spec, st_spec],
        out_specs=tok(GLA_WIDTH),
        out_shape=jax.ShapeDtypeStruct((B, L, GLA_WIDTH), BF16),
        scratch_shapes=[pltpu.VMEM((L, 2 * GLA_QK_WIDTH), F32),
                        pltpu.VMEM((L, GLA_WIDTH), F32),
                        pltpu.VMEM((2, L, GLA_QK_WIDTH), BF16),
                        pltpu.VMEM((2, 2, nchunk, GLA_DV, LANES), F32),
                        pltpu.VMEM((2, nchunk, GLA_DV, 2 * LANES), BF16),
                        pltpu.VMEM((2, L // SUPER, 2 * CH_PER, GLA_QK_WIDTH), F32),
                        pltpu.VMEM((4, GLA_DV, LANES), F32)],
        compiler_params=_cparams(("arbitrary",)),
        name="gla_bidirectional",
    )(gq, gk, gv, gg, lr, wd, bdec, gn, s_f, s_b)


def _outproj_kernel(attn_ref, gla_ref, x_ref, mod_ref, w_ref, g2_ref, wr_ref,
                    x1_ref, h2_ref, afft_ref):
    y = _dot(attn_ref[...], w_ref[0:ATTN_WIDTH, :]) + _dot(gla_ref[...], w_ref[ATTN_WIDTH:, :])
    x1 = x_ref[...] + mod_ref[2:3, :] * y
    x1_ref[...] = x1
    h2 = _rms_mod(x1, g2_ref[...], mod_ref[3:4, :], mod_ref[4:5, :]).astype(BF16)
    half = D_MODEL // 2
    hi = pltpu.bitcast(h2[:, 0:half].astype(F32), jnp.uint32)
    lo = pltpu.bitcast(h2[:, half:].astype(F32), jnp.uint32)
    h2_ref[:, 0:half] = pltpu.bitcast(hi | (lo >> 16), F32)
    logits = _dot_nt(wr_ref[...], h2)
    e = jnp.exp(logits - jnp.max(logits, axis=0, keepdims=True))
    afft = e / jnp.sum(e, axis=0, keepdims=True)
    afft_ref[...] = afft
    pad = jnp.zeros((LANES - N_EXPERTS, afft.shape[1]), F32)
    h2_ref[:, half:half + LANES] = jnp.concatenate([afft, pad], axis=0).T


def _output_projection(attn, gla, x, mod3, w_out, g2, w_router, tm):
    B, L, _ = x.shape
    full = lambda shape: pl.BlockSpec(shape, lambda b, i: (0,) * len(shape))
    tok = lambda w: pl.BlockSpec((None, tm, w), lambda b, i: (b, i, 0))
    return pl.pallas_call(
        _outproj_kernel,
        grid=(B, L // tm),
        in_specs=[tok(ATTN_WIDTH), tok(GLA_WIDTH), tok(D_MODEL),
                  pl.BlockSpec((None, 6, D_MODEL), lambda b, i: (b, 0, 0)),
                  full(w_out.shape), full(g2.shape), full(w_router.shape)],
        out_specs=[tok(D_MODEL), tok(ROW_WORDS),
                   pl.BlockSpec((None, N_EXPERTS, tm), lambda b, i: (b, 0, i))],
        out_shape=[jax.ShapeDtypeStruct((B, L, D_MODEL), F32),
                   jax.ShapeDtypeStruct((B, L, ROW_WORDS), F32),
                   jax.ShapeDtypeStruct((B, N_EXPERTS, L), F32)],
        compiler_params=_cparams(("arbitrary", "arbitrary")),
        name="output_projection_router",
    )(attn, gla, x, mod3, w_out, g2, w_router)


def _topk_kernel(afft_ref, post_ref, pos_ref, gsel_ref, *, cap):
    nbatch, E, L = afft_ref.shape
    aff = afft_ref[...].reshape(nbatch * E, L)
    E = nbatch * E

    def search(i, thr):
        cand = thr | jnp.left_shift(jnp.int32(1), 30 - i)
        cnt = jnp.sum(jnp.where(aff >= pltpu.bitcast(cand, F32), 1.0, 0.0), axis=-1, keepdims=True)
        return jnp.where(cnt >= cap, cand, thr)

    thr_bits = lax.fori_loop(0, 31, search, jnp.zeros((E, 1), jnp.int32))
    thr = pltpu.bitcast(thr_bits, F32)
    above = aff > thr
    tie = aff == thr
    need = cap - jnp.sum(jnp.where(above, 1.0, 0.0), axis=-1, keepdims=True)

    upper = (lax.broadcasted_iota(jnp.int32, (LANES, LANES), 0)
             <= lax.broadcasted_iota(jnp.int32, (LANES, LANES), 1)).astype(BF16)

    def prefix(mask):
        parts = []
        run = jnp.zeros((E, 1), F32)
        for j in range(L // LANES):
            blk = jnp.where(mask[:, j * LANES:(j + 1) * LANES], 1.0, 0.0).astype(BF16)
            loc = _dot(blk, upper) + run
            parts.append(loc)
            run = loc[:, LANES - 1:LANES]
        return jnp.concatenate(parts, axis=1)

    tie_rank = prefix(tie)
    sel = above | (tie & (tie_rank <= need))
    slot = prefix(sel).astype(jnp.int32) - 1
    post = jnp.where(sel, slot, -1)
    gsel = jnp.where(sel, aff, 0.0)
    ne = E // nbatch
    pad_i = jnp.full((LANES - ne, L), -1, jnp.int32)
    pad_f = jnp.zeros((LANES - ne, L), F32)
    for bb in range(nbatch):
        rows = slice(bb * ne, (bb + 1) * ne)
        post_ref[bb] = post[rows]
        pos_ref[bb] = jnp.concatenate([post[rows], pad_i], axis=0).T
        gsel_ref[bb] = jnp.concatenate([gsel[rows], pad_f], axis=0).T


def _expert_choice(afft, cap, nbatch):
    B, E, L = afft.shape
    return pl.pallas_call(
        functools.partial(_topk_kernel, cap=cap),
        grid=(B // nbatch,),
        in_specs=[pl.BlockSpec((nbatch, E, L), lambda b: (b, 0, 0))],
        out_specs=[pl.BlockSpec((nbatch, E, L), lambda b: (b, 0, 0)),
                   pl.BlockSpec((nbatch, L, LANES), lambda b: (b, 0, 0)),
                   pl.BlockSpec((nbatch, L, LANES), lambda b: (b, 0, 0))],
        out_shape=[jax.ShapeDtypeStruct((B, E, L), jnp.int32),
                   jax.ShapeDtypeStruct((B, L, LANES), jnp.int32),
                   jax.ShapeDtypeStruct((B, L, LANES), F32)],
        compiler_params=_cparams(("arbitrary",)),
        name="expert_choice_topk",
    )(afft)


SC_CORES = 2
SC_SUBCORES = 16
SC_LANES = 16
SC_WINDOW = 128


def _dispatch(rows, slots, cap, first_expert, n_experts):
    L = slots.shape[1]
    W = rows.shape[1]
    n_pair = slots.shape[0] // N_EXPERTS * n_experts
    per_worker = n_pair // (SC_CORES * SC_SUBCORES)
    n_win = cap // SC_WINDOW
    mesh = plsc.VectorSubcoreMesh(core_axis_name="c", subcore_axis_name="s",
                                  num_cores=SC_CORES, num_subcores=SC_SUBCORES)

    def body(rows_hbm, slots_hbm, out_hbm, slot_v, *scratch):
        idx_v, buf_v, sem = scratch[:n_win], scratch[n_win], scratch[n_win + 1]
        worker = lax.axis_index("s") * SC_CORES + lax.axis_index("c")

        @pl.loop(0, per_worker)
        def _(p):
            pair = worker * per_worker + p
            batch = pair // n_experts
            first_tok = batch * L
            pltpu.sync_copy(slots_hbm.at[batch * N_EXPERTS + first_expert + pair % n_experts], slot_v)

            @pl.loop(0, L // SC_LANES)
            def _(i):
                v = slot_v[pl.ds(i * SC_LANES, SC_LANES)]
                tok = lax.iota(jnp.int32, SC_LANES) + (i * SC_LANES + first_tok)
                for w in range(n_win):
                    in_win = (v >= w * SC_WINDOW) & (v < (w + 1) * SC_WINDOW)
                    plsc.store_scatter(idx_v[w], [v - w * SC_WINDOW], tok, mask=in_win)

            for w in range(n_win):
                pltpu.async_copy(rows_hbm.at[idx_v[w]], buf_v, sem).wait()
                pltpu.sync_copy(buf_v, out_hbm.at[pl.ds(pair * cap + w * SC_WINDOW, SC_WINDOW)])

    return pl.kernel(
        body,
        out_type=jax.ShapeDtypeStruct((n_pair * cap, W), rows.dtype),
        mesh=mesh,
        scratch_types=[pltpu.VMEM((L,), jnp.int32)]
        + [pltpu.VMEM((SC_WINDOW,), jnp.int32) for _ in range(n_win)]
        + [pltpu.VMEM((SC_WINDOW, W), rows.dtype), pltpu.SemaphoreType.DMA],
        compiler_params=pltpu.CompilerParams(needs_layout_passes=False),
        name="moe_dispatch_gather",
    )(rows, slots)


def _ffn_kernel(xs_ref, mod_ref, wg_ref, wu_ref, wd_ref, y_ref, wgb_ref, wub_ref, wdb_ref, *,
                first_expert, scaled):
    nbatch, cap, _ = xs_ref.shape
    d = wg_ref.shape[0]
    dw = d // 2

    @pl.when(pl.program_id(1) == 0)
    def _():
        wgb_ref[...] = wg_ref[...].astype(BF16)
        wub_ref[...] = wu_ref[...].astype(BF16)
        wdb_ref[...] = wd_ref[...].astype(BF16)

    words = pltpu.bitcast(xs_ref[:, :, 0:dw].reshape(nbatch * cap, dw), jnp.uint32)
    xs = jnp.concatenate([pltpu.bitcast(words & jnp.uint32(0xFFFF0000), F32).astype(BF16),
                          pltpu.bitcast(words << 16, F32).astype(BF16)], axis=1)
    f = wg_ref.shape[1]
    half = f // 2
    acc = None
    for j in range(2):
        cols = slice(j * half, (j + 1) * half)
        g = _dot(xs, wgb_ref[:, cols])
        u = _dot(xs, wub_ref[:, cols])
        hid = (g * jax.nn.sigmoid(g) * u).astype(BF16)
        part = _dot(hid, wdb_ref[cols, :])
        acc = part if acc is None else acc + part
    if scaled:
        aff = xs_ref[:, :, dw:dw + LANES].reshape(nbatch * cap, LANES)
        lane = lax.broadcasted_iota(jnp.int32, aff.shape, 1)
        gate = jnp.sum(jnp.where(lane == first_expert + pl.program_id(0), aff, 0.0), axis=-1, keepdims=True)
        y = (acc * gate).reshape(nbatch, cap, d)
        for i in range(nbatch):
            y_ref[i] = y[i] * mod_ref[i, 5:6, :]
    else:
        y_ref[...] = acc.astype(BF16).reshape(nbatch, cap, d)


def _expert_ffn(xs, mod3, w_gate, w_up, w_down, first_expert, nbatch, scaled):
    B, E, cap, row_words = xs.shape
    d, f = w_gate.shape[1:]
    tok = pl.BlockSpec((nbatch, None, cap, d), lambda e, b: (b, e, 0, 0))
    return pl.pallas_call(
        functools.partial(_ffn_kernel, first_expert=first_expert, scaled=scaled),
        grid=(E, B // nbatch),
        in_specs=[pl.BlockSpec((nbatch, None, cap, row_words), lambda e, b: (b, e, 0, 0)),
                  pl.BlockSpec((nbatch, 6, d), lambda e, b: (b, 0, 0)),
                  pl.BlockSpec((None, d, f), lambda e, b: (e + first_expert, 0, 0)),
                  pl.BlockSpec((None, d, f), lambda e, b: (e + first_expert, 0, 0)),
                  pl.BlockSpec((None, f, d), lambda e, b: (e + first_expert, 0, 0))],
        out_specs=tok,
        out_shape=jax.ShapeDtypeStruct((B, E, cap, d), F32 if scaled else BF16),
        scratch_shapes=[pltpu.VMEM((d, f), BF16), pltpu.VMEM((d, f), BF16), pltpu.VMEM((f, d), BF16)],
        compiler_params=_cparams(("arbitrary", "arbitrary")),
        name="expert_swiglu",
    )(xs, mod3, w_gate, w_up, w_down)


SC_ADD_ROWS = 16


def _combine_add(acc, y, slots, cap, first_expert, n_experts):
    L = slots.shape[1]
    D = y.shape[1]
    assert slots.shape[0] // N_EXPERTS == SC_CORES * SC_SUBCORES
    n_win = cap // SC_ADD_ROWS
    mesh = plsc.VectorSubcoreMesh(core_axis_name="c", subcore_axis_name="s",
                                  num_cores=SC_CORES, num_subcores=SC_SUBCORES)

    def body(acc_hbm, y_hbm, slots_hbm, slot_v, *scratch):
        idx_v = scratch[:n_win]
        y_v = scratch[n_win:n_win + 2]
        o_v = scratch[n_win + 2:n_win + 4]
        sem_y, sem_g, sem_s = (scratch[n_win + 4 + 2 * k:n_win + 6 + 2 * k] for k in range(3))
        batch = lax.axis_index("s") * SC_CORES + lax.axis_index("c")

        @pl.loop(0, n_experts)
        def _(el):
            pltpu.sync_copy(slots_hbm.at[batch * N_EXPERTS + first_expert + el], slot_v)

            @pl.loop(0, L // SC_LANES)
            def _(i):
                v = slot_v[pl.ds(i * SC_LANES, SC_LANES)]
                tok = lax.iota(jnp.int32, SC_LANES) + (i * SC_LANES + batch * L)
                for w in range(n_win):
                    in_win = (v >= w * SC_ADD_ROWS) & (v < (w + 1) * SC_ADD_ROWS)
                    plsc.store_scatter(idx_v[w], [v - w * SC_ADD_ROWS], tok, mask=in_win)

            row0 = (batch * n_experts + el) * cap

            def fetch(w):
                b = w % 2
                return (pltpu.async_copy(y_hbm.at[pl.ds(row0 + w * SC_ADD_ROWS, SC_ADD_ROWS)], y_v[b], sem_y[b]),
                        pltpu.async_copy(acc_hbm.at[idx_v[w]], o_v[b], sem_g[b]))

            loads = fetch(0)
            stores = [None, None]
            for w in range(n_win):
                b = w % 2
                nxt = None
                if w + 1 < n_win:
                    if stores[1 - b] is not None:
                        stores[1 - b].wait()
                        stores[1 - b] = None
                    nxt = fetch(w + 1)
                loads[0].wait()
                loads[1].wait()

                @pl.loop(0, SC_ADD_ROWS)
                def _(r):
                    for c in range(D // SC_LANES):
                        lanes = pl.ds(c * SC_LANES, SC_LANES)
                        plsc.addupdate(o_v[b].at[r, lanes], y_v[b][r, lanes])

                stores[b] = pltpu.async_copy(o_v[b], acc_hbm.at[idx_v[w]], sem_s[b])
                loads = nxt
            for st in stores:
                if st is not None:
                    st.wait()

    pl.kernel(
        body,
        out_type=(),
        mesh=mesh,
        scratch_types=[pltpu.VMEM((L,), jnp.int32)]
        + [pltpu.VMEM((SC_ADD_ROWS,), jnp.int32) for _ in range(n_win)]
        + [pltpu.VMEM((SC_ADD_ROWS, D), F32) for _ in range(4)]
        + [pltpu.SemaphoreType.DMA for _ in range(6)],
        compiler_params=pltpu.CompilerParams(needs_layout_passes=False),
        name="moe_combine_row_add",
    )(acc, y, slots)


def _combine_kernel(pos_ref, gsel_ref, *rest, cap, n_groups, first_expert):
    y_refs = rest[:n_groups]
    x1_ref, mod_ref, o_ref, acc_ref = rest[n_groups:]
    tt = pos_ref.shape[0]
    per_group = (N_EXPERTS - first_expert) // n_groups
    slot = lax.broadcasted_iota(jnp.int32, (tt, cap), 1)
    for e in range(first_expert, N_EXPERTS):
        j = e - first_expert
        onehot = jnp.where(pos_ref[:, e:e + 1] == slot, 1.0, 0.0).astype(BF16)
        part = gsel_ref[:, e:e + 1] * _dot(onehot, y_refs[j // per_group][j % per_group])
        if j == 0:
            acc_ref[...] = part
        else:
            acc_ref[...] += part
    o_ref[...] = x1_ref[...] + mod_ref[5:6, :] * acc_ref[...]


def _combine(pos, gsel, ys, x1, mod3, cap, tt, first_expert):
    B, L, _ = x1.shape
    per_group = ys[0].shape[1]
    tok = lambda w: pl.BlockSpec((None, tt, w), lambda b, i: (b, i, 0))
    return pl.pallas_call(
        functools.partial(_combine_kernel, cap=cap, n_groups=len(ys), first_expert=first_expert),
        grid=(B, L // tt),
        in_specs=[tok(LANES), tok(LANES)]
        + [pl.BlockSpec((None, per_group, cap, D_MODEL), lambda b, i: (b, 0, 0, 0)) for _ in ys]
        + [tok(D_MODEL), pl.BlockSpec((None, 6, D_MODEL), lambda b, i: (b, 0, 0))],
        out_specs=tok(D_MODEL),
        out_shape=jax.ShapeDtypeStruct((B, L, D_MODEL), F32),
        scratch_shapes=[pltpu.VMEM((tt, D_MODEL), F32)],
        compiler_params=_cparams(("arbitrary", "arbitrary")),
        name="moe_combine",
    )(pos, gsel, *ys, x1, mod3)


def _rope_tables(L):
    inv = ROPE_BASE ** (-jnp.arange(ROPE_FREQS, dtype=F32) / ROPE_FREQS)
    pos = jnp.arange(L)
    row = (pos // GRID_W).astype(F32)[:, None] * inv
    col = (pos % GRID_W).astype(F32)[:, None] * inv
    cos = jnp.concatenate([jnp.cos(row), jnp.cos(row), jnp.cos(col), jnp.cos(col)], axis=1)
    sin = jnp.concatenate([-jnp.sin(row), jnp.sin(row), -jnp.sin(col), jnp.sin(col)], axis=1)
    return jnp.tile(cos, (1, 2)), jnp.tile(sin, (1, 2))


def _head_mean_matrix(n):
    idx = np.arange(n) // HEAD_DIM
    return jnp.asarray((idx[:, None] == idx[None, :]).astype(np.float32) / HEAD_DIM, dtype=BF16)


def kernel(x, c, ctx, c_ctx, w_mod, b_mod, norm1_g, w_in, q_norm_g, k_norm_g, attn_sink,
           w_decay_fwd, b_decay_fwd, w_decay_bwd, b_decay_bwd, gla_norm_g, w_out, norm2_g,
           w_router, w_e_gate, w_e_up, w_e_down):
    B, L, D = x.shape
    cap = CAPACITY_FACTOR * L // N_EXPERTS
    layer = 0

    rows = ((B + 1 + 7) // 8) * 8
    cc = jnp.concatenate([c, c_ctx[None, :], jnp.zeros((rows - B - 1, D), F32)], axis=0)
    mod_all = _modulation(cc, w_mod[layer], b_mod[layer])
    mod3 = mod_all[:B].reshape(B, 6, D)
    modc = mod_all[B].reshape(6, D)

    w = w_in[layer]
    o = np.cumsum([0, ATTN_WIDTH, KV_WIDTH, KV_WIDTH, GLA_QK_WIDTH, GLA_QK_WIDTH,
                   GLA_WIDTH, GLA_WIDTH, GATE_RANK, GATE_RANK])
    w_lr = jnp.concatenate([w[:, o[7]:o[9]]] * 3 + [jnp.zeros((D, LANES - 6 * GATE_RANK), F32)], axis=1)
    head_order = np.arange(N_Q_HEADS).reshape(N_KV_HEADS, -1).T.reshape(-1)
    attn_perm = (head_order[:, None] * HEAD_DIM + np.arange(HEAD_DIM)[None, :]).reshape(-1)
    w_in_r = jnp.concatenate([w[:, attn_perm], w[:, o[1]:o[7]], w_lr], axis=1).astype(BF16)
    w_out_r = jnp.concatenate([w_out[layer][attn_perm], w_out[layer][ATTN_WIDTH:]], axis=0).astype(BF16)
    w_ctx = jnp.concatenate([w[:, o[1]:o[3]], w[:, o[4]:o[6]], w_lr], axis=1).astype(BF16)
    wd2 = jnp.zeros((2 * GATE_RANK, 2 * GLA_QK_WIDTH), F32)
    wd2 = wd2.at[0:GATE_RANK, 0:GLA_QK_WIDTH].set(w_decay_fwd[layer])
    wd2 = wd2.at[GATE_RANK:, GLA_QK_WIDTH:].set(w_decay_bwd[layer])
    wd_hi = wd2.astype(BF16)
    wd_lo = (wd2 - wd_hi.astype(F32)).astype(BF16)
    wd = jnp.concatenate([wd_hi, wd_hi, wd_lo,
                          jnp.zeros((LANES - 6 * GATE_RANK, 2 * GLA_QK_WIDTH), BF16)], axis=0)
    bdec = jnp.concatenate([b_decay_fwd[layer], b_decay_bwd[layer]])[None, :]
    g1 = norm1_g[layer][None, :]
    g2 = norm2_g[layer][None, :]
    qg = jnp.tile(q_norm_g[layer], N_Q_HEADS)[None, :]
    kg2 = jnp.tile(k_norm_g[layer], N_KV_HEADS)[None, :]
    gn = jnp.tile(gla_norm_g[layer], GLA_HEADS)[None, :]
    bd512 = _head_mean_matrix(ATTN_WIDTH)
    bd128 = _head_mean_matrix(KV_WIDTH)
    cos_t, sin_t = _rope_tables(L)
    w_router_t = w_router[layer].T.astype(BF16)

    kc, vc, s_f, s_b = _context_side(ctx, modc, g1, w_ctx, kg2, bd128, wd, bdec)
    q, k, v, gq, gk, gv, gg, lr = _input_projection(
        x, mod3, g1, w_in_r, qg, kg2, bd512, cos_t, sin_t, tm=1024)
    attn = _window_attention(attn_sink[layer], q, k, v, kc, vc, tq=512)
    gla = _gla(gq, gk, gv, gg, lr, wd, bdec, gn, s_f, s_b)
    x1, h2, afft = _output_projection(attn, gla, x, mod3, w_out_r, g2, w_router_t, tm=1024)
    post, pos, gsel = _expert_choice(afft, cap, nbatch=4)
    n_groups = 4
    per_group = N_EXPERTS // n_groups
    rows = h2.reshape(B * L, ROW_WORDS)
    slots = post.reshape(B * N_EXPERTS, L)
    sc_groups = 2
    stream = jax.new_ref(x1.reshape(B * L, D))
    ys = []
    for g in range(n_groups):
        xs = _dispatch(rows, slots, cap, g * per_group, per_group).reshape(B, per_group, cap, ROW_WORDS)
        y = _expert_ffn(xs, mod3, w_e_gate[layer], w_e_up[layer], w_e_down[layer], g * per_group,
                        nbatch=4, scaled=(g < sc_groups))
        if g < sc_groups:
            _combine_add(stream, y.reshape(B * per_group * cap, D), slots, cap, g * per_group, per_group)
        else:
            ys.append(y)
    x1 = jax.freeze(stream).reshape(B, L, D)
    return _combine(pos, gsel, ys, x1, mod3, cap, tt=1024, first_expert=sc_groups * per_group)
```

```python
import functools

import jax
import jax.numpy as jnp
import numpy as np
from jax import lax
from jax.experimental import pallas as pl
from jax.experimental.pallas import tpu as pltpu
from jax.experimental.pallas import tpu_sc as plsc

D_MODEL = 1024
GRID_W = 64
HEAD_DIM = 64
N_Q_HEADS = 8
N_KV_HEADS = 2
BLOCK = 128
ROPE_FREQS = 16
ROPE_BASE = 10000.0
GLA_HEADS = 4
GLA_DV = 128
GLA_DK = 64
GATE_RANK = 16
GATE_NORMALIZER = 16.0
CHUNK = 64
N_EXPERTS = 16
CAPACITY_FACTOR = 2
ATTN_WIDTH = 512
KV_WIDTH = 128
GLA_QK_WIDTH = 256
GLA_WIDTH = 512
EPS = 1e-6
NEG_INF = -1e30
LOG2E = 1.4426950408889634

LANES = 128
ROW_WORDS = D_MODEL // 2 + LANES
VMEM_LIMIT = 56 * 1024 * 1024

F32 = jnp.float32
BF16 = jnp.bfloat16
HI = lax.Precision.HIGHEST


def _cparams(sem):
    return pltpu.CompilerParams(dimension_semantics=sem, vmem_limit_bytes=VMEM_LIMIT)


def _dot(a, b):
    return jnp.dot(a, b, preferred_element_type=F32)


def _dot_hi(a, b):
    return jnp.dot(a, b, preferred_element_type=F32, precision=HI)


def _dot_nt(a, b):
    return lax.dot_general(a, b, (((1,), (1,)), ((), ())), preferred_element_type=F32)


def _dot_tn(a, b, precision=None):
    return lax.dot_general(a, b, (((0,), (0,)), ((), ())), preferred_element_type=F32,
                           precision=precision)


def _split2(t):
    hi = t.astype(BF16)
    lo = (t - hi.astype(F32)).astype(BF16)
    return hi, lo


def _rms_mod(t, g, shift, scale):
    y = t * lax.rsqrt(jnp.mean(t * t, axis=-1, keepdims=True) + EPS)
    return (y * g) * (1.0 + scale) + shift


def _log_decay(lr, wd3, bias):
    hi = lr.astype(BF16)
    lo = (lr - hi.astype(F32)).astype(BF16)
    lane = lax.broadcasted_iota(jnp.int32, lr.shape, 1)
    second = (lane >= 2 * GATE_RANK) & (lane < 4 * GATE_RANK)
    z = _dot(jnp.where(second, lo, hi), wd3) + bias
    return (jnp.minimum(z, 0.0) - jnp.log(1.0 + jnp.exp(-jnp.abs(z)))) * (1.0 / GATE_NORMALIZER)


def _lane_lo(shape):
    return (lax.broadcasted_iota(jnp.int32, shape, len(shape) - 1) % LANES) < HEAD_DIM


def _mod_kernel(c_ref, w_ref, b_ref, o_ref):
    c = c_ref[...]
    s = c * jax.nn.sigmoid(c)
    o_ref[...] = _dot_hi(s, w_ref[...]) + b_ref[...]


def _modulation(cc, w_mod, b_mod):
    m = cc.shape[0]
    n = w_mod.shape[1]
    tn = 1024
    return pl.pallas_call(
        _mod_kernel,
        grid=(n // tn,),
        in_specs=[pl.BlockSpec((m, D_MODEL), lambda j: (0, 0)),
                  pl.BlockSpec((D_MODEL, tn), lambda j: (0, j)),
                  pl.BlockSpec((1, tn), lambda j: (0, j))],
        out_specs=pl.BlockSpec((m, tn), lambda j: (0, j)),
        out_shape=jax.ShapeDtypeStruct((m, n), F32),
        compiler_params=_cparams(("arbitrary",)),
        name="adaln_mod",
    )(cc, w_mod, b_mod.reshape(1, n))


def _ctx_kernel(ctx_ref, mod_ref, g1_ref, w_ref, kg_ref, bd_ref, wd_ref, bdec_ref,
                kc_ref, vc_ref, sf_ref, sb_ref):
    n = ctx_ref.shape[0]
    h = _rms_mod(ctx_ref[...], g1_ref[...], mod_ref[0:1, :], mod_ref[1:2, :]).astype(BF16)
    pc = _dot(h, w_ref[...])
    ak = pc[:, 0:128]
    av = pc[:, 128:256]
    gk = pc[:, 256:512]
    gv = pc[:, 512:1024].astype(BF16)
    lr = pc[:, 1024:1152]
    sq_hi, sq_lo = _split2(ak * ak)
    ms = _dot(sq_hi, bd_ref[...]) + _dot(sq_lo, bd_ref[...])
    kn = ak * lax.rsqrt(ms + EPS) * kg_ref[...]
    kc_ref[...] = kn.astype(BF16)
    vc_ref[...] = av.astype(BF16)
    la = _log_decay(lr, wd_ref[...], bdec_ref[...])
    r = lax.broadcasted_iota(jnp.int32, (n, n), 0)
    cidx = lax.broadcasted_iota(jnp.int32, (n, n), 1)
    after = (cidx > r).astype(F32)
    before = (cidx < r).astype(F32)
    w_f = jnp.exp(_dot_hi(after, la[:, 0:256]))
    w_b = jnp.exp(_dot_hi(before, la[:, 256:512]))
    lo = _lane_lo((n, LANES))
    for w, out in ((w_f, sf_ref), (w_b, sb_ref)):
        kw = gk * w
        for c in range(2):
            kwc = kw[:, c * LANES:(c + 1) * LANES]
            k_lo = jnp.where(lo, kwc, 0.0).astype(BF16)
            k_hi = jnp.where(lo, 0.0, kwc).astype(BF16)
            v0 = gv[:, (2 * c) * GLA_DV:(2 * c + 1) * GLA_DV]
            v1 = gv[:, (2 * c + 1) * GLA_DV:(2 * c + 2) * GLA_DV]
            out[c] = _dot_tn(v0, k_lo) + _dot_tn(v1, k_hi)


def _context_side(ctx, modc, g1, w_ctx, kg2, bd128, wd, bdec):
    B, n, _ = ctx.shape
    full = lambda shape: pl.BlockSpec(shape, lambda b: (0,) * len(shape))
    kv_spec = pl.BlockSpec((None, n, KV_WIDTH), lambda b: (b, 0, 0))
    st_spec = pl.BlockSpec((None, 2, LANES, GLA_DV), lambda b: (b, 0, 0, 0))
    kv_shape = jax.ShapeDtypeStruct((B, n, KV_WIDTH), BF16)
    st_shape = jax.ShapeDtypeStruct((B, 2, LANES, GLA_DV), F32)
    return pl.pallas_call(
        _ctx_kernel,
        grid=(B,),
        in_specs=[pl.BlockSpec((None, n, D_MODEL), lambda b: (b, 0, 0)),
                  full(modc.shape), full(g1.shape), full(w_ctx.shape), full(kg2.shape),
                  full(bd128.shape), full(wd.shape), full(bdec.shape)],
        out_specs=[kv_spec, kv_spec, st_spec, st_spec],
        out_shape=[kv_shape, kv_shape, st_shape, st_shape],
        compiler_params=_cparams(("arbitrary",)),
        name="context_side",
    )(ctx, modc, g1, w_ctx, kg2, bd128, wd, bdec)


def _swap16(t):
    n = t.shape[1]
    first = (lax.broadcasted_iota(jnp.int32, t.shape, 1) % 32) < ROPE_FREQS
    return jnp.where(first, pltpu.roll(t, n - ROPE_FREQS, 1), pltpu.roll(t, ROPE_FREQS, 1))


def _inproj_kernel(x_ref, mod_ref, g1_ref, w_ref, qg_ref, kg_ref, bd_ref, cos_ref, sin_ref,
                   q_ref, k_ref, v_ref, gq_ref, gk_ref, gv_ref, gg_ref, lr_ref):
    h = _rms_mod(x_ref[...], g1_ref[...], mod_ref[0:1, :], mod_ref[1:2, :]).astype(BF16)
    cos = cos_ref[...]
    sin = sin_ref[...]

    def head_norm_rope(t, g, bd, reps):
        ms = _dot((t * t).astype(BF16), bd)
        tn = t * lax.rsqrt(ms + EPS) * g
        c = jnp.concatenate([cos] * reps, axis=1) if reps > 1 else cos
        s = jnp.concatenate([sin] * reps, axis=1) if reps > 1 else sin
        return tn * c + _swap16(tn) * s

    aq = _dot(h, w_ref[:, 0:512])
    q = head_norm_rope(aq, qg_ref[...], bd_ref[...], 4) * (HEAD_DIM ** -0.5 * LOG2E)
    q_ref[...] = q.astype(BF16)
    akv = _dot(h, w_ref[:, 512:768])
    k = head_norm_rope(akv[:, 0:128], kg_ref[...], bd_ref[0:128, 0:128], 1)
    k_ref[...] = k.astype(BF16)
    v_ref[...] = akv[:, 128:256].astype(BF16)
    gqk = _dot(h, w_ref[:, 768:1280])
    gq_ref[...] = (gqk[:, 0:256] * (GLA_DK ** -0.5)).astype(BF16)
    gk_ref[...] = gqk[:, 256:512].astype(BF16)
    gv_ref[...] = _dot(h, w_ref[:, 1280:1792]).astype(BF16)
    gg_ref[...] = _dot(h, w_ref[:, 1792:2304]).astype(BF16)
    lr_ref[...] = _dot(h, w_ref[:, 2304:2432])


def _input_projection(x, mod3, g1, w_in_r, qg, kg2, bd512, cos_t, sin_t, tm):
    B, L, _ = x.shape
    full = lambda shape: pl.BlockSpec(shape, lambda b, i: (0,) * len(shape))
    tok = lambda w: pl.BlockSpec((None, tm, w), lambda b, i: (b, i, 0))
    widths = (ATTN_WIDTH, KV_WIDTH, KV_WIDTH, GLA_QK_WIDTH, GLA_QK_WIDTH, GLA_WIDTH, GLA_WIDTH, LANES)
    dtypes = (BF16,) * 7 + (F32,)
    return pl.pallas_call(
        _inproj_kernel,
        grid=(B, L // tm),
        in_specs=[tok(D_MODEL),
                  pl.BlockSpec((None, 6, D_MODEL), lambda b, i: (b, 0, 0)),
                  full(g1.shape), full(w_in_r.shape), full(qg.shape), full(kg2.shape),
                  full(bd512.shape),
                  pl.BlockSpec((tm, LANES), lambda b, i: (i, 0)),
                  pl.BlockSpec((tm, LANES), lambda b, i: (i, 0))],
        out_specs=[tok(w) for w in widths],
        out_shape=[jax.ShapeDtypeStruct((B, L, w), dt) for w, dt in zip(widths, dtypes)],
        compiler_params=_cparams(("arbitrary", "arbitrary")),
        name="input_projection",
    )(x, mod3, g1, w_in_r, qg, kg2, bd512, cos_t, sin_t)


def _attn_kernel(sink_ref, q_ref, kp_ref, ko_ref, kn_ref, vp_ref, vo_ref, vn_ref,
                 kc_ref, vc_ref, o_ref):
    i = pl.program_id(1)
    ni = pl.num_programs(1)
    nsub = q_ref.shape[0] // BLOCK
    ncol = ATTN_WIDTH // LANES
    win = 3 * BLOCK
    ucol = 4
    half_rows = ucol * BLOCK
    k_win = jnp.concatenate([kp_ref[...], ko_ref[...], kn_ref[...]], axis=0)
    v_win = jnp.concatenate([vp_ref[...], vo_ref[...], vn_ref[...]], axis=0)
    k_ctx = kc_ref[...]
    lo_w = _lane_lo(v_win.shape)
    lo_c = _lane_lo(vc_ref.shape)
    lo_q = _lane_lo((BLOCK, LANES))
    zero = jnp.zeros((), BF16)
    one = jnp.ones((), BF16)
    v0_c, v0_w = jnp.where(lo_c, vc_ref[...], one), jnp.where(lo_w, v_win, one)
    v1_c, v1_w = jnp.where(lo_c, one, vc_ref[...]), jnp.where(lo_w, one, v_win)
    qi = lax.broadcasted_iota(jnp.int32, (half_rows, BLOCK), 0) % BLOCK
    kj = lax.broadcasted_iota(jnp.int32, (half_rows, BLOCK), 1)
    no_prev = jnp.where(i > 0, 0, BLOCK)
    no_next = jnp.where(i < ni - 1, 0, BLOCK)
    row_head = lax.broadcasted_iota(jnp.int32, (half_rows, 1), 0) // BLOCK
    lo_o = _lane_lo((half_rows, LANES))
    for t in range(nsub):
        rows = slice(t * BLOCK, (t + 1) * BLOCK)
        keys = slice(t * BLOCK, t * BLOCK + win)
        cols = [q_ref[rows, c * LANES:(c + 1) * LANES] for c in range(ncol)]
        prev_ok = kj >= qi + (no_prev if t == 0 else 0)
        next_ok = kj <= qi - (no_next if t == nsub - 1 else 0)
        for p in range(ncol // ucol):
            outs = []
            qs = jnp.concatenate([jnp.where(lo_q, qc, zero) for qc in cols[ucol * p:ucol * (p + 1)]]
                                 + [jnp.where(lo_q, zero, qc) for qc in cols[ucol * p:ucol * (p + 1)]], axis=0)
            s_c_all = _dot_nt(qs, k_ctx)
            s_w_all = _dot_nt(qs, k_win[keys])
            for g, (vv_c, vv_w) in enumerate(((v0_c, v0_w), (v1_c, v1_w))):
                head = g * ncol + ucol * p
                sink_g = jnp.full((half_rows, 1), sink_ref[head + ucol - 1], F32)
                for j in range(ucol - 2, -1, -1):
                    sink_g = jnp.where(row_head <= j, sink_ref[head + j], sink_g)
                sink_g = sink_g * LOG2E
                s_c = s_c_all[g * half_rows:(g + 1) * half_rows]
                s_w = s_w_all[g * half_rows:(g + 1) * half_rows]
                s_p = jnp.where(prev_ok, s_w[:, 0:BLOCK], NEG_INF)
                s_o = s_w[:, BLOCK:2 * BLOCK]
                s_n = jnp.where(next_ok, s_w[:, 2 * BLOCK:win], NEG_INF)
                m = jnp.maximum(jnp.maximum(jnp.max(s_c, axis=-1, keepdims=True),
                                            jnp.max(jnp.maximum(jnp.maximum(s_p, s_o), s_n),
                                                    axis=-1, keepdims=True)), sink_g)
                e_c = jnp.exp2(s_c - m).astype(BF16)
                e_w = jnp.concatenate([jnp.exp2(s_p - m), jnp.exp2(s_o - m), jnp.exp2(s_n - m)],
                                      axis=1).astype(BF16)
                acc = _dot(e_c, vv_c) + _dot(e_w, vv_w[keys])
                outs.append(acc / (pltpu.roll(acc, HEAD_DIM, 1) + jnp.exp2(sink_g - m)))
            o = jnp.where(lo_o, outs[0], outs[1]).astype(BF16)
            for j in range(ucol):
                c = ucol * p + j
                o_ref[rows, c * LANES:(c + 1) * LANES] = o[j * BLOCK:(j + 1) * BLOCK]


def _window_attention(sink, q, k, v, kc, vc, tq):
    B, L, _ = q.shape
    nb = L // BLOCK
    nsub = tq // BLOCK
    n_ctx = kc.shape[1]
    prev = pl.BlockSpec((None, BLOCK, KV_WIDTH), lambda b, n: (b, jnp.maximum(n * nsub - 1, 0), 0))
    own = pl.BlockSpec((None, tq, KV_WIDTH), lambda b, n: (b, n, 0))
    nxt = pl.BlockSpec((None, BLOCK, KV_WIDTH),
                       lambda b, n: (b, jnp.minimum((n + 1) * nsub, nb - 1), 0))
    cspec = pl.BlockSpec((None, n_ctx, KV_WIDTH), lambda b, n: (b, 0, 0))
    return pl.pallas_call(
        _attn_kernel,
        grid=(B, L // tq),
        in_specs=[pl.BlockSpec(memory_space=pltpu.SMEM),
                  pl.BlockSpec((None, tq, ATTN_WIDTH), lambda b, n: (b, n, 0)),
                  prev, own, nxt, prev, own, nxt, cspec, cspec],
        out_specs=pl.BlockSpec((None, tq, ATTN_WIDTH), lambda b, n: (b, n, 0)),
        out_shape=jax.ShapeDtypeStruct((B, L, ATTN_WIDTH), BF16),
        compiler_params=_cparams(("arbitrary", "arbitrary")),
        name="window_attention",
    )(sink, q, k, k, k, v, v, v, kc, vc)


SUPER = 256
CH_PER = SUPER // CHUNK
HALF = 128


def _dot_split(m, parts):
    return _dot(m, parts[0]) + _dot(m, parts[1])


def _gla_kernel(gq_ref, gk_ref, gv_ref, gg_ref, lr_ref, wd_ref, bdec_ref, gn_ref, sf_ref, sb_ref,
                o_ref, la_ref, oi_ref, qg_ref, kv_ref, sb16_ref, dec_ref, st_ref):
    L = gq_ref.shape[0]
    nsuper = L // SUPER
    nchunk = L // CHUNK
    half = CHUNK // 2
    la_ref[...] = _log_decay(lr_ref[...], wd_ref[...], bdec_ref[...])

    r = lax.broadcasted_iota(jnp.int32, (SUPER, SUPER), 0)
    cidx = lax.broadcasted_iota(jnp.int32, (SUPER, SUPER), 1)
    same = (r // CHUNK) == (cidx // CHUNK)
    pr = r % CHUNK
    pc = cidx % CHUNK
    one = jnp.float32(1.0)
    zero = jnp.float32(0.0)
    in_f = jnp.where(pc <= pr, one, zero)
    in_b = jnp.where(pc >= pr, one, zero)
    ref_f = jnp.where(pc < half, one, zero)
    ref_b = jnp.where(pc >= half, one, zero)
    m1_f = jnp.where(same, in_f - ref_f, zero).astype(BF16)
    m1_b = jnp.where(same, in_b - ref_b, zero).astype(BF16)
    rh = lax.broadcasted_iota(jnp.int32, (HALF, 2 * HALF), 0)
    ch = lax.broadcasted_iota(jnp.int32, (HALF, 2 * HALF), 1) % HALF
    same_h = (rh // CHUNK) == (ch // CHUNK)
    mask_f = jnp.where(same_h, jnp.where(ch % CHUNK <= rh % CHUNK, one, zero), zero) > 0.5
    mask_b = jnp.where(same_h, jnp.where(ch % CHUNK >= rh % CHUNK, one, zero), zero) > 0.5
    rr = lax.broadcasted_iota(jnp.int32, (2 * CH_PER, SUPER), 0)
    rc = lax.broadcasted_iota(jnp.int32, (2 * CH_PER, SUPER), 1)
    in_chunk = jnp.where((rc // CHUNK) == (rr % CH_PER), one, zero)
    first = jnp.where((rc % CHUNK) < half, 1, 0)
    is_tot = jnp.where(rr >= CH_PER, 1, 0)
    rs_f = (in_chunk * jnp.where(first != is_tot, one, zero)).astype(BF16)
    rs_b = (in_chunk * jnp.where(first == is_tot, one, zero)).astype(BF16)
    lo_h = _lane_lo((HALF, LANES))
    zero_blk = jnp.zeros((HALF, GLA_DV), BF16)

    def phase1(s, carry):
        r0 = pl.multiple_of(s * SUPER, SUPER)
        rows = pl.ds(r0, SUPER)
        q = gq_ref[rows, :].astype(F32)
        k = gk_ref[rows, :].astype(F32)
        qes, kes, kds = [], [], []
        for d, (m1, rs) in enumerate(((m1_f, rs_f), (m1_b, rs_b))):
            parts = _split2(la_ref[rows, d * GLA_QK_WIDTH:(d + 1) * GLA_QK_WIDTH])
            x1 = _dot_split(m1, parts)
            erow = jnp.exp(_dot_split(rs, parts))
            dec = erow[0:CH_PER] * erow[CH_PER:2 * CH_PER]
            dec_ref[d, s] = jnp.concatenate([dec, dec], axis=0)
            qe = q * jnp.exp(x1)
            ke = k * jnp.exp(-x1)
            qg_parts, kd_parts = [], []
            for j in range(CH_PER):
                rj = slice(j * CHUNK, (j + 1) * CHUNK)
                qg_parts.append(qe[rj] * erow[j:j + 1])
                kd_parts.append(ke[rj] * erow[CH_PER + j:CH_PER + j + 1])
            qg_ref[d, rows, :] = jnp.concatenate(qg_parts, axis=0).astype(BF16)
            qes.append(qe.astype(BF16))
            kes.append(ke)
            kds.append(jnp.concatenate(kd_parts, axis=0).astype(BF16))
        for c in range(2):
            cl = slice(c * LANES, (c + 1) * LANES)
            vpair = gv_ref[rows, 2 * c * GLA_DV:(2 * c + 2) * GLA_DV]
            for blk in range(SUPER // HALF):
                rb = slice(blk * HALF, (blk + 1) * HALF)
                vbd = jnp.concatenate(
                    [jnp.concatenate([vpair[rb, 0:GLA_DV], zero_blk], axis=1),
                     jnp.concatenate([zero_blk, vpair[rb, GLA_DV:2 * GLA_DV]], axis=1)], axis=0)
                o2 = None
                for d, mask in enumerate((mask_f, mask_b)):
                    ke_cb = kes[d][rb, cl]
                    ke_st = jnp.concatenate([jnp.where(lo_h, ke_cb, zero),
                                             jnp.where(lo_h, zero, ke_cb)], axis=0).astype(BF16)
                    a = _dot_nt(qes[d][rb, cl], ke_st)
                    o = _dot(jnp.where(mask, a, zero).astype(BF16), vbd)
                    o2 = o if o2 is None else o2 + o
                oi_ref[pl.ds(r0 + blk * HALF, HALF), 2 * c * GLA_DV:(2 * c + 2) * GLA_DV] = o2
            for d in range(2):
                for j in range(CH_PER):
                    rj = slice(j * CHUNK, (j + 1) * CHUNK)
                    t = _dot_tn(kds[d][rj, cl], vpair[rj])
                    kv = jnp.concatenate([t[0:GLA_DK, 0:GLA_DV], t[GLA_DK:, GLA_DV:]], axis=0)
                    kv_ref[d, c, s * CH_PER + j] = kv.T
        return carry

    lax.fori_loop(0, nsuper, phase1, 0)

    st_ref[0] = sf_ref[0]
    st_ref[1] = sf_ref[1]
    st_ref[2] = sb_ref[0]
    st_ref[3] = sb_ref[1]

    def phase2(n, carry):
        for d in range(2):
            idx = n if d == 0 else nchunk - 1 - n
            dec = dec_ref[d, idx // CH_PER, pl.ds(idx % CH_PER, 1), :]
            for c in range(2):
                st = st_ref[2 * d + c]
                sb16_ref[c, idx, :, d * LANES:(d + 1) * LANES] = st.astype(BF16)
                st_ref[2 * d + c] = dec[:, c * LANES:(c + 1) * LANES] * st + kv_ref[d, c, idx]
        return carry

    lax.fori_loop(0, nchunk, phase2, 0)

    lo64 = _lane_lo((CHUNK, 2 * LANES))
    zero_b = jnp.zeros((), BF16)

    def phase3(s, carry):
        r0 = pl.multiple_of(s * SUPER, SUPER)
        rows = pl.ds(r0, SUPER)
        inter = [[None] * CH_PER for _ in range(GLA_HEADS)]
        for j in range(CH_PER):
            rj = pl.ds(r0 + j * CHUNK, CHUNK)
            for c in range(2):
                qg_c = jnp.concatenate([qg_ref[d, rj, c * LANES:(c + 1) * LANES] for d in range(2)], axis=1)
                lhs = jnp.concatenate([jnp.where(lo64, qg_c, zero_b),
                                       jnp.where(lo64, zero_b, qg_c)], axis=0)
                t = _dot_nt(lhs, sb16_ref[c, s * CH_PER + j])
                for hh in range(2):
                    inter[2 * c + hh][j] = t[hh * CHUNK:(hh + 1) * CHUNK]
        for hd in range(GLA_HEADS):
            cl = slice(hd * GLA_DV, (hd + 1) * GLA_DV)
            o = oi_ref[rows, cl] + jnp.concatenate(inter[hd], axis=0)
            y = o * lax.rsqrt(jnp.mean(o * o, axis=-1, keepdims=True) + EPS) * gn_ref[:, cl]
            g = gg_ref[rows, cl].astype(F32)
            o_ref[rows, cl] = (y * (g * jax.nn.sigmoid(g))).astype(BF16)
        return carry

    lax.fori_loop(0, nsuper, phase3, 0)


def _gla(gq, gk, gv, gg, lr, wd, bdec, gn, s_f, s_b):
    B, L, _ = gq.shape
    nchunk = L // CHUNK
    full = lambda shape: pl.BlockSpec(shape, lambda b: (0,) * len(shape))
    tok = lambda w: pl.BlockSpec((None, L, w), lambda b: (b, 0, 0))
    st_spec = pl.BlockSpec((None, 2, LANES, GLA_DV), lambda b: (b, 0, 0, 0))
    return pl.pallas_call(
        _gla_kernel,
        grid=(B,),
        in_specs=[tok(GLA_QK_WIDTH), tok(GLA_QK_WIDTH), tok(GLA_WIDTH), tok(GLA_WIDTH), tok(LANES),
                  full(wd.shape), full(bdec.shape), full(gn.shape), st_spec, st_spec],
        out_specs=tok(GLA_WIDTH),
        out_shape=jax.ShapeDtypeStruct((B, L, GLA_WIDTH), BF16),
        scratch_shapes=[pltpu.VMEM((L, 2 * GLA_QK_WIDTH), F32),
                        pltpu.VMEM((L, GLA_WIDTH), F32),
                        pltpu.VMEM((2, L, GLA_QK_WIDTH), BF16),
                        pltpu.VMEM((2, 2, nchunk, GLA_DV, LANES), F32),
                        pltpu.VMEM((2, nchunk, GLA_DV, 2 * LANES), BF16),
                        pltpu.VMEM((2, L // SUPER, 2 * CH_PER, GLA_QK_WIDTH), F32),
                        pltpu.VMEM((4, GLA_DV, LANES), F32)],
        compiler_params=_cparams(("arbitrary",)),
        name="gla_bidirectional",
    )(gq, gk, gv, gg, lr, wd, bdec, gn, s_f, s_b)


def _outproj_kernel(attn_ref, gla_ref, x_ref, mod_ref, w_ref, g2_ref, wr_ref,
                    x1_ref, h2_ref, afft_ref):
    y = _dot(attn_ref[...], w_ref[0:ATTN_WIDTH, :]) + _dot(gla_ref[...], w_ref[ATTN_WIDTH:, :])
    x1 = x_ref[...] + mod_ref[2:3, :] * y
    x1_ref[...] = x1
    h2 = _rms_mod(x1, g2_ref[...], mod_ref[3:4, :], mod_ref[4:5, :]).astype(BF16)
    half = D_MODEL // 2
    hi = pltpu.bitcast(h2[:, 0:half].astype(F32), jnp.uint32)
    lo = pltpu.bitcast(h2[:, half:].astype(F32), jnp.uint32)
    h2_ref[:, 0:half] = pltpu.bitcast(hi | (lo >> 16), F32)
    logits = _dot_nt(wr_ref[...], h2)
    e = jnp.exp(logits - jnp.max(logits, axis=0, keepdims=True))
    afft = e / jnp.sum(e, axis=0, keepdims=True)
    afft_ref[...] = afft
    pad = jnp.zeros((LANES - N_EXPERTS, afft.shape[1]), F32)
    h2_ref[:, half:half + LANES] = jnp.concatenate([afft, pad], axis=0).T


def _output_projection(attn, gla, x, mod3, w_out, g2, w_router, tm):
    B, L, _ = x.shape
    full = lambda shape: pl.BlockSpec(shape, lambda b, i: (0,) * len(shape))
    tok = lambda w: pl.BlockSpec((None, tm, w), lambda b, i: (b, i, 0))
    return pl.pallas_call(
        _outproj_kernel,
        grid=(B, L // tm),
        in_specs=[tok(ATTN_WIDTH), tok(GLA_WIDTH), tok(D_MODEL),
                  pl.BlockSpec((None, 6, D_MODEL), lambda b, i: (b, 0, 0)),
                  full(w_out.shape), full(g2.shape), full(w_router.shape)],
        out_specs=[tok(D_MODEL), tok(ROW_WORDS),
                   pl.BlockSpec((None, N_EXPERTS, tm), lambda b, i: (b, 0, i))],
        out_shape=[jax.ShapeDtypeStruct((B, L, D_MODEL), F32),
                   jax.ShapeDtypeStruct((B, L, ROW_WORDS), F32),
                   jax.ShapeDtypeStruct((B, N_EXPERTS, L), F32)],
        compiler_params=_cparams(("arbitrary", "arbitrary")),
        name="output_projection_router",
    )(attn, gla, x, mod3, w_out, g2, w_router)


def _topk_kernel(afft_ref, post_ref, pos_ref, gsel_ref, *, cap):
    nbatch, E, L = afft_ref.shape
    aff = afft_ref[...].reshape(nbatch * E, L)
    E = nbatch * E

    def search(i, thr):
        cand = thr | jnp.left_shift(jnp.int32(1), 30 - i)
        cnt = jnp.sum(jnp.where(aff >= pltpu.bitcast(cand, F32), 1.0, 0.0), axis=-1, keepdims=True)
        return jnp.where(cnt >= cap, cand, thr)

    thr_bits = lax.fori_loop(0, 31, search, jnp.zeros((E, 1), jnp.int32))
    thr = pltpu.bitcast(thr_bits, F32)
    above = aff > thr
    tie = aff == thr
    need = cap - jnp.sum(jnp.where(above, 1.0, 0.0), axis=-1, keepdims=True)

    upper = (lax.broadcasted_iota(jnp.int32, (LANES, LANES), 0)
             <= lax.broadcasted_iota(jnp.int32, (LANES, LANES), 1)).astype(BF16)

    def prefix(mask):
        parts = []
        run = jnp.zeros((E, 1), F32)
        for j in range(L // LANES):
            blk = jnp.where(mask[:, j * LANES:(j + 1) * LANES], 1.0, 0.0).astype(BF16)
            loc = _dot(blk, upper) + run
            parts.append(loc)
            run = loc[:, LANES - 1:LANES]
        return jnp.concatenate(parts, axis=1)

    tie_rank = prefix(tie)
    sel = above | (tie & (tie_rank <= need))
    slot = prefix(sel).astype(jnp.int32) - 1
    post = jnp.where(sel, slot, -1)
    gsel = jnp.where(sel, aff, 0.0)
    ne = E // nbatch
    pad_i = jnp.full((LANES - ne, L), -1, jnp.int32)
    pad_f = jnp.zeros((LANES - ne, L), F32)
    for bb in range(nbatch):
        rows = slice(bb * ne, (bb + 1) * ne)
        post_ref[bb] = post[rows]
        pos_ref[bb] = jnp.concatenate([post[rows], pad_i], axis=0).T
        gsel_ref[bb] = jnp.concatenate([gsel[rows], pad_f], axis=0).T


def _expert_choice(afft, cap, nbatch):
    B, E, L = afft.shape
    return pl.pallas_call(
        functools.partial(_topk_kernel, cap=cap),
        grid=(B // nbatch,),
        in_specs=[pl.BlockSpec((nbatch, E, L), lambda b: (b, 0, 0))],
        out_specs=[pl.BlockSpec((nbatch, E, L), lambda b: (b, 0, 0)),
                   pl.BlockSpec((nbatch, L, LANES), lambda b: (b, 0, 0)),
                   pl.BlockSpec((nbatch, L, LANES), lambda b: (b, 0, 0))],
        out_shape=[jax.ShapeDtypeStruct((B, E, L), jnp.int32),
                   jax.ShapeDtypeStruct((B, L, LANES), jnp.int32),
                   jax.ShapeDtypeStruct((B, L, LANES), F32)],
        compiler_params=_cparams(("arbitrary",)),
        name="expert_choice_topk",
    )(afft)


SC_CORES = 2
SC_SUBCORES = 16
SC_LANES = 16
SC_WINDOW = 128


def _dispatch(rows, slots, cap, first_expert, n_experts):
    L = slots.shape[1]
    W = rows.shape[1]
    n_pair = slots.shape[0] // N_EXPERTS * n_experts
    per_worker = n_pair // (SC_CORES * SC_SUBCORES)
    n_win = cap // SC_WINDOW
    mesh = plsc.VectorSubcoreMesh(core_axis_name="c", subcore_axis_name="s",
                                  num_cores=SC_CORES, num_subcores=SC_SUBCORES)

    def body(rows_hbm, slots_hbm, out_hbm, slot_v, *scratch):
        idx_v, buf_v, sem = scratch[:n_win], scratch[n_win], scratch[n_win + 1]
        worker = lax.axis_index("s") * SC_CORES + lax.axis_index("c")

        @pl.loop(0, per_worker)
        def _(p):
            pair = worker * per_worker + p
            batch = pair // n_experts
            first_tok = batch * L
            pltpu.sync_copy(slots_hbm.at[batch * N_EXPERTS + first_expert + pair % n_experts], slot_v)

            @pl.loop(0, L // SC_LANES)
            def _(i):
                v = slot_v[pl.ds(i * SC_LANES, SC_LANES)]
                tok = lax.iota(jnp.int32, SC_LANES) + (i * SC_LANES + first_tok)
                for w in range(n_win):
                    in_win = (v >= w * SC_WINDOW) & (v < (w + 1) * SC_WINDOW)
                    plsc.store_scatter(idx_v[w], [v - w * SC_WINDOW], tok, mask=in_win)

            for w in range(n_win):
                pltpu.async_copy(rows_hbm.at[idx_v[w]], buf_v, sem).wait()
                pltpu.sync_copy(buf_v, out_hbm.at[pl.ds(pair * cap + w * SC_WINDOW, SC_WINDOW)])

    return pl.kernel(
        body,
        out_type=jax.ShapeDtypeStruct((n_pair * cap, W), rows.dtype),
        mesh=mesh,
        scratch_types=[pltpu.VMEM((L,), jnp.int32)]
        + [pltpu.VMEM((SC_WINDOW,), jnp.int32) for _ in range(n_win)]
        + [pltpu.VMEM((SC_WINDOW, W), rows.dtype), pltpu.SemaphoreType.DMA],
        compiler_params=pltpu.CompilerParams(needs_layout_passes=False),
        name="moe_dispatch_gather",
    )(rows, slots)


def _ffn_kernel(xs_ref, mod_ref, wg_ref, wu_ref, wd_ref, y_ref, wgb_ref, wub_ref, wdb_ref, *,
                first_expert, scaled):
    nbatch, cap, _ = xs_ref.shape
    d = wg_ref.shape[0]
    dw = d // 2

    @pl.when(pl.program_id(1) == 0)
    def _():
        wgb_ref[...] = wg_ref[...].astype(BF16)
        wub_ref[...] = wu_ref[...].astype(BF16)
        wdb_ref[...] = wd_ref[...].astype(BF16)

    words = pltpu.bitcast(xs_ref[:, :, 0:dw].reshape(nbatch * cap, dw), jnp.uint32)
    xs = jnp.concatenate([pltpu.bitcast(words & jnp.uint32(0xFFFF0000), F32).astype(BF16),
                          pltpu.bitcast(words << 16, F32).astype(BF16)], axis=1)
    f = wg_ref.shape[1]
    half = f // 2
    acc = None
    for j in range(2):
        cols = slice(j * half, (j + 1) * half)
        g = _dot(xs, wgb_ref[:, cols])
        u = _dot(xs, wub_ref[:, cols])
        hid = (g * jax.nn.sigmoid(g) * u).astype(BF16)
        part = _dot(hid, wdb_ref[cols, :])
        acc = part if acc is None else acc + part
    if scaled:
        aff = xs_ref[:, :, dw:dw + LANES].reshape(nbatch * cap, LANES)
        lane = lax.broadcasted_iota(jnp.int32, aff.shape, 1)
        gate = jnp.sum(jnp.where(lane == first_expert + pl.program_id(0), aff, 0.0), axis=-1, keepdims=True)
        y = (acc * gate).reshape(nbatch, cap, d)
        for i in range(nbatch):
            y_ref[i] = y[i] * mod_ref[i, 5:6, :]
    else:
        y_ref[...] = acc.astype(BF16).reshape(nbatch, cap, d)


def _expert_ffn(xs, mod3, w_gate, w_up, w_down, first_expert, nbatch, scaled):
    B, E, cap, row_words = xs.shape
    d, f = w_gate.shape[1:]
    tok = pl.BlockSpec((nbatch, None, cap, d), lambda e, b: (b, e, 0, 0))
    return pl.pallas_call(
        functools.partial(_ffn_kernel, first_expert=first_expert, scaled=scaled),
        grid=(E, B // nbatch),
        in_specs=[pl.BlockSpec((nbatch, None, cap, row_words), lambda e, b: (b, e, 0, 0)),
                  pl.BlockSpec((nbatch, 6, d), lambda e, b: (b, 0, 0)),
                  pl.BlockSpec((None, d, f), lambda e, b: (e + first_expert, 0, 0)),
                  pl.BlockSpec((None, d, f), lambda e, b: (e + first_expert, 0, 0)),
                  pl.BlockSpec((None, f, d), lambda e, b: (e + first_expert, 0, 0))],
        out_specs=tok,
        out_shape=jax.ShapeDtypeStruct((B, E, cap, d), F32 if scaled else BF16),
        scratch_shapes=[pltpu.VMEM((d, f), BF16), pltpu.VMEM((d, f), BF16), pltpu.VMEM((f, d), BF16)],
        compiler_params=_cparams(("arbitrary", "arbitrary")),
        name="expert_swiglu",
    )(xs, mod3, w_gate, w_up, w_down)


SC_ADD_ROWS = 16


def _combine_add(acc, y, slots, cap, first_expert, n_experts, after=()):
    L = slots.shape[1]
    D = y.shape[1]
    assert slots.shape[0] // N_EXPERTS == SC_CORES * SC_SUBCORES
    n_win = cap // SC_ADD_ROWS
    mesh = plsc.VectorSubcoreMesh(core_axis_name="c", subcore_axis_name="s",
                                  num_cores=SC_CORES, num_subcores=SC_SUBCORES)

    def body(acc_hbm, y_hbm, slots_hbm, *rest):
        slot_v, *scratch = rest[len(after):]
        idx_v = scratch[:n_win]
        y_v = scratch[n_win:n_win + 2]
        o_v = scratch[n_win + 2:n_win + 4]
        sem_y, sem_g, sem_s = (scratch[n_win + 4 + 2 * k:n_win + 6 + 2 * k] for k in range(3))
        batch = lax.axis_index("s") * SC_CORES + lax.axis_index("c")

        @pl.loop(0, n_experts)
        def _(el):
            pltpu.sync_copy(slots_hbm.at[batch * N_EXPERTS + first_expert + el], slot_v)

            @pl.loop(0, L // SC_LANES)
            def _(i):
                v = slot_v[pl.ds(i * SC_LANES, SC_LANES)]
                tok = lax.iota(jnp.int32, SC_LANES) + (i * SC_LANES + batch * L)
                for w in range(n_win):
                    in_win = (v >= w * SC_ADD_ROWS) & (v < (w + 1) * SC_ADD_ROWS)
                    plsc.store_scatter(idx_v[w], [v - w * SC_ADD_ROWS], tok, mask=in_win)

            row0 = (batch * n_experts + el) * cap

            def fetch(w):
                b = w % 2
                return (pltpu.async_copy(y_hbm.at[pl.ds(row0 + w * SC_ADD_ROWS, SC_ADD_ROWS)], y_v[b], sem_y[b]),
                        pltpu.async_copy(acc_hbm.at[idx_v[w]], o_v[b], sem_g[b]))

            loads = fetch(0)
            stores = [None, None]
            for w in range(n_win):
                b = w % 2
                nxt = None
                if w + 1 < n_win:
                    if stores[1 - b] is not None:
                        stores[1 - b].wait()
                        stores[1 - b] = None
                    nxt = fetch(w + 1)
                loads[0].wait()
                loads[1].wait()

                @pl.loop(0, SC_ADD_ROWS)
                def _(r):
                    for c in range(D // SC_LANES):
                        lanes = pl.ds(c * SC_LANES, SC_LANES)
                        plsc.addupdate(o_v[b].at[r, lanes], y_v[b][r, lanes])

                stores[b] = pltpu.async_copy(o_v[b], acc_hbm.at[idx_v[w]], sem_s[b])
                loads = nxt
            for st in stores:
                if st is not None:
                    st.wait()

    pl.kernel(
        body,
        out_type=(),
        mesh=mesh,
        scratch_types=[pltpu.VMEM((L,), jnp.int32)]
        + [pltpu.VMEM((SC_ADD_ROWS,), jnp.int32) for _ in range(n_win)]
        + [pltpu.VMEM((SC_ADD_ROWS, D), F32) for _ in range(4)]
        + [pltpu.SemaphoreType.DMA for _ in range(6)],
        compiler_params=pltpu.CompilerParams(needs_layout_passes=False),
        name="moe_combine_row_add",
    )(acc, y, slots, *after)


def _combine_kernel(pos_ref, gsel_ref, *rest, cap, n_groups, first_expert):
    y_refs = rest[:n_groups]
    x1_ref, mod_ref, o_ref, acc_ref = rest[n_groups:]
    tt = pos_ref.shape[0]
    per_group = (N_EXPERTS - first_expert) // n_groups
    slot = lax.broadcasted_iota(jnp.int32, (tt, cap), 1)
    for e in range(first_expert, N_EXPERTS):
        j = e - first_expert
        onehot = jnp.where(pos_ref[:, e:e + 1] == slot, 1.0, 0.0).astype(BF16)
        part = gsel_ref[:, e:e + 1] * _dot(onehot, y_refs[j // per_group][j % per_group])
        if j == 0:
            acc_ref[...] = part
        else:
            acc_ref[...] += part
    o_ref[...] = x1_ref[...] + mod_ref[5:6, :] * acc_ref[...]


def _combine(pos, gsel, ys, x1, mod3, cap, tt, first_expert):
    B, L, _ = x1.shape
    per_group = ys[0].shape[1]
    tok = lambda w: pl.BlockSpec((None, tt, w), lambda b, i: (b, i, 0))
    return pl.pallas_call(
        functools.partial(_combine_kernel, cap=cap, n_groups=len(ys), first_expert=first_expert),
        grid=(B, L // tt),
        in_specs=[tok(LANES), tok(LANES)]
        + [pl.BlockSpec((None, per_group, cap, D_MODEL), lambda b, i: (b, 0, 0, 0)) for _ in ys]
        + [tok(D_MODEL), pl.BlockSpec((None, 6, D_MODEL), lambda b, i: (b, 0, 0))],
        out_specs=tok(D_MODEL),
        out_shape=jax.ShapeDtypeStruct((B, L, D_MODEL), F32),
        scratch_shapes=[pltpu.VMEM((tt, D_MODEL), F32)],
        compiler_params=_cparams(("arbitrary", "arbitrary")),
        name="moe_combine",
    )(pos, gsel, *ys, x1, mod3)


def _rope_tables(L):
    inv = ROPE_BASE ** (-jnp.arange(ROPE_FREQS, dtype=F32) / ROPE_FREQS)
    pos = jnp.arange(L)
    row = (pos // GRID_W).astype(F32)[:, None] * inv
    col = (pos % GRID_W).astype(F32)[:, None] * inv
    cos = jnp.concatenate([jnp.cos(row), jnp.cos(row), jnp.cos(col), jnp.cos(col)], axis=1)
    sin = jnp.concatenate([-jnp.sin(row), jnp.sin(row), -jnp.sin(col), jnp.sin(col)], axis=1)
    return jnp.tile(cos, (1, 2)), jnp.tile(sin, (1, 2))


def _head_mean_matrix(n):
    idx = np.arange(n) // HEAD_DIM
    return jnp.asarray((idx[:, None] == idx[None, :]).astype(np.float32) / HEAD_DIM, dtype=BF16)


def kernel(x, c, ctx, c_ctx, w_mod, b_mod, norm1_g, w_in, q_norm_g, k_norm_g, attn_sink,
           w_decay_fwd, b_decay_fwd, w_decay_bwd, b_decay_bwd, gla_norm_g, w_out, norm2_g,
           w_router, w_e_gate, w_e_up, w_e_down):
    B, L, D = x.shape
    cap = CAPACITY_FACTOR * L // N_EXPERTS
    layer = 0

    rows = ((B + 1 + 7) // 8) * 8
    cc = jnp.concatenate([c, c_ctx[None, :], jnp.zeros((rows - B - 1, D), F32)], axis=0)
    mod_all = _modulation(cc, w_mod[layer], b_mod[layer])
    mod3 = mod_all[:B].reshape(B, 6, D)
    modc = mod_all[B].reshape(6, D)

    w = w_in[layer]
    o = np.cumsum([0, ATTN_WIDTH, KV_WIDTH, KV_WIDTH, GLA_QK_WIDTH, GLA_QK_WIDTH,
                   GLA_WIDTH, GLA_WIDTH, GATE_RANK, GATE_RANK])
    w_lr = jnp.concatenate([w[:, o[7]:o[9]]] * 3 + [jnp.zeros((D, LANES - 6 * GATE_RANK), F32)], axis=1)
    head_order = np.arange(N_Q_HEADS).reshape(N_KV_HEADS, -1).T.reshape(-1)
    attn_perm = (head_order[:, None] * HEAD_DIM + np.arange(HEAD_DIM)[None, :]).reshape(-1)
    w_in_r = jnp.concatenate([w[:, attn_perm], w[:, o[1]:o[7]], w_lr], axis=1).astype(BF16)
    w_out_r = jnp.concatenate([w_out[layer][attn_perm], w_out[layer][ATTN_WIDTH:]], axis=0).astype(BF16)
    w_ctx = jnp.concatenate([w[:, o[1]:o[3]], w[:, o[4]:o[6]], w_lr], axis=1).astype(BF16)
    wd2 = jnp.zeros((2 * GATE_RANK, 2 * GLA_QK_WIDTH), F32)
    wd2 = wd2.at[0:GATE_RANK, 0:GLA_QK_WIDTH].set(w_decay_fwd[layer])
    wd2 = wd2.at[GATE_RANK:, GLA_QK_WIDTH:].set(w_decay_bwd[layer])
    wd_hi = wd2.astype(BF16)
    wd_lo = (wd2 - wd_hi.astype(F32)).astype(BF16)
    wd = jnp.concatenate([wd_hi, wd_hi, wd_lo,
                          jnp.zeros((LANES - 6 * GATE_RANK, 2 * GLA_QK_WIDTH), BF16)], axis=0)
    bdec = jnp.concatenate([b_decay_fwd[layer], b_decay_bwd[layer]])[None, :]
    g1 = norm1_g[layer][None, :]
    g2 = norm2_g[layer][None, :]
    qg = jnp.tile(q_norm_g[layer], N_Q_HEADS)[None, :]
    kg2 = jnp.tile(k_norm_g[layer], N_KV_HEADS)[None, :]
    gn = jnp.tile(gla_norm_g[layer], GLA_HEADS)[None, :]
    bd512 = _head_mean_matrix(ATTN_WIDTH)
    bd128 = _head_mean_matrix(KV_WIDTH)
    cos_t, sin_t = _rope_tables(L)
    w_router_t = w_router[layer].T.astype(BF16)

    kc, vc, s_f, s_b = _context_side(ctx, modc, g1, w_ctx, kg2, bd128, wd, bdec)
    q, k, v, gq, gk, gv, gg, lr = _input_projection(
        x, mod3, g1, w_in_r, qg, kg2, bd512, cos_t, sin_t, tm=1024)
    attn = _window_attention(attn_sink[layer], q, k, v, kc, vc, tq=512)
    gla = _gla(gq, gk, gv, gg, lr, wd, bdec, gn, s_f, s_b)
    x1, h2, afft = _output_projection(attn, gla, x, mod3, w_out_r, g2, w_router_t, tm=1024)
    post, pos, gsel = _expert_choice(afft, cap, nbatch=4)
    n_groups = 4
    per_group = N_EXPERTS // n_groups
    rows = h2.reshape(B * L, ROW_WORDS)
    slots = post.reshape(B * N_EXPERTS, L)
    sc_groups = 3
    stream = jax.new_ref(x1.reshape(B * L, D))
    gathered = [_dispatch(rows, slots, cap, g * per_group, per_group) for g in range(n_groups)]
    ys = []
    for g in range(n_groups):
        xs = gathered[g].reshape(B, per_group, cap, ROW_WORDS)
        y = _expert_ffn(xs, mod3, w_e_gate[layer], w_e_up[layer], w_e_down[layer], g * per_group,
                        nbatch=4, scaled=(g < sc_groups))
        if g < sc_groups:
            _combine_add(stream, y.reshape(B * per_group * cap, D), slots, cap, g * per_group, per_group,
                         after=gathered[g + 1:])
        else:
            ys.append(y)
    x1 = jax.freeze(stream).reshape(B, L, D)
    return _combine(pos, gsel, ys, x1, mod3, cap, tt=1024, first_expert=sc_groups * per_group)
```

```python
import functools

import jax
import jax.numpy as jnp
import numpy as np
from jax import lax
from jax.experimental import pallas as pl
from jax.experimental.pallas import tpu as pltpu
from jax.experimental.pallas import tpu_sc as plsc

D_MODEL = 1024
GRID_W = 64
HEAD_DIM = 64
N_Q_HEADS = 8
N_KV_HEADS = 2
BLOCK = 128
ROPE_FREQS = 16
ROPE_BASE = 10000.0
GLA_HEADS = 4
GLA_DV = 128
GLA_DK = 64
GATE_RANK = 16
GATE_NORMALIZER = 16.0
CHUNK = 64
N_EXPERTS = 16
CAPACITY_FACTOR = 2
ATTN_WIDTH = 512
KV_WIDTH = 128
GLA_QK_WIDTH = 256
GLA_WIDTH = 512
EPS = 1e-6
NEG_INF = -1e30
LOG2E = 1.4426950408889634

LANES = 128
ROW_WORDS = D_MODEL // 2 + LANES
VMEM_LIMIT = 56 * 1024 * 1024

F32 = jnp.float32
BF16 = jnp.bfloat16
HI = lax.Precision.HIGHEST


def _cparams(sem):
    return pltpu.CompilerParams(dimension_semantics=sem, vmem_limit_bytes=VMEM_LIMIT)


def _dot(a, b):
    return jnp.dot(a, b, preferred_element_type=F32)


def _dot_hi(a, b):
    return jnp.dot(a, b, preferred_element_type=F32, precision=HI)


def _dot_nt(a, b):
    return lax.dot_general(a, b, (((1,), (1,)), ((), ())), preferred_element_type=F32)


def _dot_tn(a, b, precision=None):
    return lax.dot_general(a, b, (((0,), (0,)), ((), ())), preferred_element_type=F32,
                           precision=precision)


def _split2(t):
    hi = t.astype(BF16)
    lo = (t - hi.astype(F32)).astype(BF16)
    return hi, lo


def _rms_mod(t, g, shift, scale):
    y = t * lax.rsqrt(jnp.mean(t * t, axis=-1, keepdims=True) + EPS)
    return (y * g) * (1.0 + scale) + shift


def _log_decay(lr, wd3, bias):
    hi = lr.astype(BF16)
    lo = (lr - hi.astype(F32)).astype(BF16)
    lane = lax.broadcasted_iota(jnp.int32, lr.shape, 1)
    second = (lane >= 2 * GATE_RANK) & (lane < 4 * GATE_RANK)
    z = _dot(jnp.where(second, lo, hi), wd3) + bias
    return (jnp.minimum(z, 0.0) - jnp.log(1.0 + jnp.exp(-jnp.abs(z)))) * (1.0 / GATE_NORMALIZER)


def _lane_lo(shape):
    return (lax.broadcasted_iota(jnp.int32, shape, len(shape) - 1) % LANES) < HEAD_DIM


def _mod_kernel(c_ref, w_ref, b_ref, o_ref):
    c = c_ref[...]
    s = c * jax.nn.sigmoid(c)
    o_ref[...] = _dot_hi(s, w_ref[...]) + b_ref[...]


def _modulation(cc, w_mod, b_mod):
    m = cc.shape[0]
    n = w_mod.shape[1]
    tn = 1024
    return pl.pallas_call(
        _mod_kernel,
        grid=(n // tn,),
        in_specs=[pl.BlockSpec((m, D_MODEL), lambda j: (0, 0)),
                  pl.BlockSpec((D_MODEL, tn), lambda j: (0, j)),
                  pl.BlockSpec((1, tn), lambda j: (0, j))],
        out_specs=pl.BlockSpec((m, tn), lambda j: (0, j)),
        out_shape=jax.ShapeDtypeStruct((m, n), F32),
        compiler_params=_cparams(("arbitrary",)),
        name="adaln_mod",
    )(cc, w_mod, b_mod.reshape(1, n))


def _ctx_kernel(ctx_ref, mod_ref, g1_ref, w_ref, kg_ref, bd_ref, wd_ref, bdec_ref,
                kc_ref, vc_ref, sf_ref, sb_ref):
    n = ctx_ref.shape[0]
    h = _rms_mod(ctx_ref[...], g1_ref[...], mod_ref[0:1, :], mod_ref[1:2, :]).astype(BF16)
    pc = _dot(h, w_ref[...])
    ak = pc[:, 0:128]
    av = pc[:, 128:256]
    gk = pc[:, 256:512]
    gv = pc[:, 512:1024].astype(BF16)
    lr = pc[:, 1024:1152]
    sq_hi, sq_lo = _split2(ak * ak)
    ms = _dot(sq_hi, bd_ref[...]) + _dot(sq_lo, bd_ref[...])
    kn = ak * lax.rsqrt(ms + EPS) * kg_ref[...]
    kc_ref[...] = kn.astype(BF16)
    vc_ref[...] = av.astype(BF16)
    la = _log_decay(lr, wd_ref[...], bdec_ref[...])
    r = lax.broadcasted_iota(jnp.int32, (n, n), 0)
    cidx = lax.broadcasted_iota(jnp.int32, (n, n), 1)
    after = (cidx > r).astype(F32)
    before = (cidx < r).astype(F32)
    w_f = jnp.exp(_dot_hi(after, la[:, 0:256]))
    w_b = jnp.exp(_dot_hi(before, la[:, 256:512]))
    lo = _lane_lo((n, LANES))
    for w, out in ((w_f, sf_ref), (w_b, sb_ref)):
        kw = gk * w
        for c in range(2):
            kwc = kw[:, c * LANES:(c + 1) * LANES]
            k_lo = jnp.where(lo, kwc, 0.0).astype(BF16)
            k_hi = jnp.where(lo, 0.0, kwc).astype(BF16)
            v0 = gv[:, (2 * c) * GLA_DV:(2 * c + 1) * GLA_DV]
            v1 = gv[:, (2 * c + 1) * GLA_DV:(2 * c + 2) * GLA_DV]
            out[c] = _dot_tn(v0, k_lo) + _dot_tn(v1, k_hi)


def _context_side(ctx, modc, g1, w_ctx, kg2, bd128, wd, bdec):
    B, n, _ = ctx.shape
    full = lambda shape: pl.BlockSpec(shape, lambda b: (0,) * len(shape))
    kv_spec = pl.BlockSpec((None, n, KV_WIDTH), lambda b: (b, 0, 0))
    st_spec = pl.BlockSpec((None, 2, LANES, GLA_DV), lambda b: (b, 0, 0, 0))
    kv_shape = jax.ShapeDtypeStruct((B, n, KV_WIDTH), BF16)
    st_shape = jax.ShapeDtypeStruct((B, 2, LANES, GLA_DV), F32)
    return pl.pallas_call(
        _ctx_kernel,
        grid=(B,),
        in_specs=[pl.BlockSpec((None, n, D_MODEL), lambda b: (b, 0, 0)),
                  full(modc.shape), full(g1.shape), full(w_ctx.shape), full(kg2.shape),
                  full(bd128.shape), full(wd.shape), full(bdec.shape)],
        out_specs=[kv_spec, kv_spec, st_spec, st_spec],
        out_shape=[kv_shape, kv_shape, st_shape, st_shape],
        compiler_params=_cparams(("arbitrary",)),
        name="context_side",
    )(ctx, modc, g1, w_ctx, kg2, bd128, wd, bdec)


def _swap16(t):
    n = t.shape[1]
    first = (lax.broadcasted_iota(jnp.int32, t.shape, 1) % 32) < ROPE_FREQS
    return jnp.where(first, pltpu.roll(t, n - ROPE_FREQS, 1), pltpu.roll(t, ROPE_FREQS, 1))


def _inproj_kernel(x_ref, mod_ref, g1_ref, w_ref, qg_ref, kg_ref, bd_ref, cos_ref, sin_ref,
                   q_ref, k_ref, v_ref, gq_ref, gk_ref, gv_ref, gg_ref, lr_ref):
    h = _rms_mod(x_ref[...], g1_ref[...], mod_ref[0:1, :], mod_ref[1:2, :]).astype(BF16)
    cos = cos_ref[...]
    sin = sin_ref[...]

    def head_norm_rope(t, g, bd, reps):
        ms = _dot((t * t).astype(BF16), bd)
        tn = t * lax.rsqrt(ms + EPS) * g
        c = jnp.concatenate([cos] * reps, axis=1) if reps > 1 else cos
        s = jnp.concatenate([sin] * reps, axis=1) if reps > 1 else sin
        return tn * c + _swap16(tn) * s

    aq = _dot(h, w_ref[:, 0:512])
    q = head_norm_rope(aq, qg_ref[...], bd_ref[...], 4) * (HEAD_DIM ** -0.5 * LOG2E)
    q_ref[...] = q.astype(BF16)
    akv = _dot(h, w_ref[:, 512:768])
    k = head_norm_rope(akv[:, 0:128], kg_ref[...], bd_ref[0:128, 0:128], 1)
    k_ref[...] = k.astype(BF16)
    v_ref[...] = akv[:, 128:256].astype(BF16)
    gqk = _dot(h, w_ref[:, 768:1280])
    gq_ref[...] = (gqk[:, 0:256] * (GLA_DK ** -0.5)).astype(BF16)
    gk_ref[...] = gqk[:, 256:512].astype(BF16)
    gv_ref[...] = _dot(h, w_ref[:, 1280:1792]).astype(BF16)
    gg_ref[...] = _dot(h, w_ref[:, 1792:2304]).astype(BF16)
    lr_ref[...] = _dot(h, w_ref[:, 2304:2432])


def _input_projection(x, mod3, g1, w_in_r, qg, kg2, bd512, cos_t, sin_t, tm):
    B, L, _ = x.shape
    full = lambda shape: pl.BlockSpec(shape, lambda b, i: (0,) * len(shape))
    tok = lambda w: pl.BlockSpec((None, tm, w), lambda b, i: (b, i, 0))
    widths = (ATTN_WIDTH, KV_WIDTH, KV_WIDTH, GLA_QK_WIDTH, GLA_QK_WIDTH, GLA_WIDTH, GLA_WIDTH, LANES)
    dtypes = (BF16,) * 7 + (F32,)
    return pl.pallas_call(
        _inproj_kernel,
        grid=(B, L // tm),
        in_specs=[tok(D_MODEL),
                  pl.BlockSpec((None, 6, D_MODEL), lambda b, i: (b, 0, 0)),
                  full(g1.shape), full(w_in_r.shape), full(qg.shape), full(kg2.shape),
                  full(bd512.shape),
                  pl.BlockSpec((tm, LANES), lambda b, i: (i, 0)),
                  pl.BlockSpec((tm, LANES), lambda b, i: (i, 0))],
        out_specs=[tok(w) for w in widths],
        out_shape=[jax.ShapeDtypeStruct((B, L, w), dt) for w, dt in zip(widths, dtypes)],
        compiler_params=_cparams(("arbitrary", "arbitrary")),
        name="input_projection",
    )(x, mod3, g1, w_in_r, qg, kg2, bd512, cos_t, sin_t)


def _attn_kernel(sink_ref, q_ref, kp_ref, ko_ref, kn_ref, vp_ref, vo_ref, vn_ref,
                 kc_ref, vc_ref, o_ref):
    i = pl.program_id(1)
    ni = pl.num_programs(1)
    nsub = q_ref.shape[0] // BLOCK
    ncol = ATTN_WIDTH // LANES
    win = 3 * BLOCK
    ucol = 4
    half_rows = ucol * BLOCK
    k_win = jnp.concatenate([kp_ref[...], ko_ref[...], kn_ref[...]], axis=0)
    v_win = jnp.concatenate([vp_ref[...], vo_ref[...], vn_ref[...]], axis=0)
    k_ctx = kc_ref[...]
    lo_w = _lane_lo(v_win.shape)
    lo_c = _lane_lo(vc_ref.shape)
    lo_q = _lane_lo((BLOCK, LANES))
    zero = jnp.zeros((), BF16)
    one = jnp.ones((), BF16)
    v0_c, v0_w = jnp.where(lo_c, vc_ref[...], one), jnp.where(lo_w, v_win, one)
    v1_c, v1_w = jnp.where(lo_c, one, vc_ref[...]), jnp.where(lo_w, one, v_win)
    qi = lax.broadcasted_iota(jnp.int32, (half_rows, BLOCK), 0) % BLOCK
    kj = lax.broadcasted_iota(jnp.int32, (half_rows, BLOCK), 1)
    no_prev = jnp.where(i > 0, 0, BLOCK)
    no_next = jnp.where(i < ni - 1, 0, BLOCK)
    row_head = lax.broadcasted_iota(jnp.int32, (half_rows, 1), 0) // BLOCK
    lo_o = _lane_lo((half_rows, LANES))
    for t in range(nsub):
        rows = slice(t * BLOCK, (t + 1) * BLOCK)
        keys = slice(t * BLOCK, t * BLOCK + win)
        cols = [q_ref[rows, c * LANES:(c + 1) * LANES] for c in range(ncol)]
        prev_ok = kj >= qi + (no_prev if t == 0 else 0)
        next_ok = kj <= qi - (no_next if t == nsub - 1 else 0)
        for p in range(ncol // ucol):
            outs = []
            qs = jnp.concatenate([jnp.where(lo_q, qc, zero) for qc in cols[ucol * p:ucol * (p + 1)]]
                                 + [jnp.where(lo_q, zero, qc) for qc in cols[ucol * p:ucol * (p + 1)]], axis=0)
            s_c_all = _dot_nt(qs, k_ctx)
            s_w_all = _dot_nt(qs, k_win[keys])
            for g, (vv_c, vv_w) in enumerate(((v0_c, v0_w), (v1_c, v1_w))):
                head = g * ncol + ucol * p
                sink_g = jnp.full((half_rows, 1), sink_ref[head + ucol - 1], F32)
                for j in range(ucol - 2, -1, -1):
                    sink_g = jnp.where(row_head <= j, sink_ref[head + j], sink_g)
                sink_g = sink_g * LOG2E
                s_c = s_c_all[g * half_rows:(g + 1) * half_rows]
                s_w = s_w_all[g * half_rows:(g + 1) * half_rows]
                s_p = jnp.where(prev_ok, s_w[:, 0:BLOCK], NEG_INF)
                s_o = s_w[:, BLOCK:2 * BLOCK]
                s_n = jnp.where(next_ok, s_w[:, 2 * BLOCK:win], NEG_INF)
                m = jnp.maximum(jnp.maximum(jnp.max(s_c, axis=-1, keepdims=True),
                                            jnp.max(jnp.maximum(jnp.maximum(s_p, s_o), s_n),
                                                    axis=-1, keepdims=True)), sink_g)
                e_c = jnp.exp2(s_c - m).astype(BF16)
                e_w = jnp.concatenate([jnp.exp2(s_p - m), jnp.exp2(s_o - m), jnp.exp2(s_n - m)],
                                      axis=1).astype(BF16)
                acc = _dot(e_c, vv_c) + _dot(e_w, vv_w[keys])
                outs.append(acc / (pltpu.roll(acc, HEAD_DIM, 1) + jnp.exp2(sink_g - m)))
            o = jnp.where(lo_o, outs[0], outs[1]).astype(BF16)
            for j in range(ucol):
                c = ucol * p + j
                o_ref[rows, c * LANES:(c + 1) * LANES] = o[j * BLOCK:(j + 1) * BLOCK]


def _window_attention(sink, q, k, v, kc, vc, tq):
    B, L, _ = q.shape
    nb = L // BLOCK
    nsub = tq // BLOCK
    n_ctx = kc.shape[1]
    prev = pl.BlockSpec((None, BLOCK, KV_WIDTH), lambda b, n: (b, jnp.maximum(n * nsub - 1, 0), 0))
    own = pl.BlockSpec((None, tq, KV_WIDTH), lambda b, n: (b, n, 0))
    nxt = pl.BlockSpec((None, BLOCK, KV_WIDTH),
                       lambda b, n: (b, jnp.minimum((n + 1) * nsub, nb - 1), 0))
    cspec = pl.BlockSpec((None, n_ctx, KV_WIDTH), lambda b, n: (b, 0, 0))
    return pl.pallas_call(
        _attn_kernel,
        grid=(B, L // tq),
        in_specs=[pl.BlockSpec(memory_space=pltpu.SMEM),
                  pl.BlockSpec((None, tq, ATTN_WIDTH), lambda b, n: (b, n, 0)),
                  prev, own, nxt, prev, own, nxt, cspec, cspec],
        out_specs=pl.BlockSpec((None, tq, ATTN_WIDTH), lambda b, n: (b, n, 0)),
        out_shape=jax.ShapeDtypeStruct((B, L, ATTN_WIDTH), BF16),
        compiler_params=_cparams(("arbitrary", "arbitrary")),
        name="window_attention",
    )(sink, q, k, k, k, v, v, v, kc, vc)


SUPER = 256
CH_PER = SUPER // CHUNK
HALF = 128


def _dot_split(m, parts):
    return _dot(m, parts[0]) + _dot(m, parts[1])


def _gla_kernel(gq_ref, gk_ref, gv_ref, gg_ref, lr_ref, wd_ref, bdec_ref, gn_ref, sf_ref, sb_ref,
                o_ref, la_ref, oi_ref, qg_ref, kv_ref, sb16_ref, dec_ref, st_ref):
    L = gq_ref.shape[0]
    nsuper = L // SUPER
    nchunk = L // CHUNK
    half = CHUNK // 2
    la_ref[...] = _log_decay(lr_ref[...], wd_ref[...], bdec_ref[...])

    r = lax.broadcasted_iota(jnp.int32, (SUPER, SUPER), 0)
    cidx = lax.broadcasted_iota(jnp.int32, (SUPER, SUPER), 1)
    same = (r // CHUNK) == (cidx // CHUNK)
    pr = r % CHUNK
    pc = cidx % CHUNK
    one = jnp.float32(1.0)
    zero = jnp.float32(0.0)
    in_f = jnp.where(pc <= pr, one, zero)
    in_b = jnp.where(pc >= pr, one, zero)
    ref_f = jnp.where(pc < half, one, zero)
    ref_b = jnp.where(pc >= half, one, zero)
    m1_f = jnp.where(same, in_f - ref_f, zero).astype(BF16)
    m1_b = jnp.where(same, in_b - ref_b, zero).astype(BF16)
    rh = lax.broadcasted_iota(jnp.int32, (HALF, 2 * HALF), 0)
    ch = lax.broadcasted_iota(jnp.int32, (HALF, 2 * HALF), 1) % HALF
    same_h = (rh // CHUNK) == (ch // CHUNK)
    mask_f = jnp.where(same_h, jnp.where(ch % CHUNK <= rh % CHUNK, one, zero), zero) > 0.5
    mask_b = jnp.where(same_h, jnp.where(ch % CHUNK >= rh % CHUNK, one, zero), zero) > 0.5
    rr = lax.broadcasted_iota(jnp.int32, (2 * CH_PER, SUPER), 0)
    rc = lax.broadcasted_iota(jnp.int32, (2 * CH_PER, SUPER), 1)
    in_chunk = jnp.where((rc // CHUNK) == (rr % CH_PER), one, zero)
    first = jnp.where((rc % CHUNK) < half, 1, 0)
    is_tot = jnp.where(rr >= CH_PER, 1, 0)
    rs_f = (in_chunk * jnp.where(first != is_tot, one, zero)).astype(BF16)
    rs_b = (in_chunk * jnp.where(first == is_tot, one, zero)).astype(BF16)
    lo_h = _lane_lo((HALF, LANES))
    zero_blk = jnp.zeros((HALF, GLA_DV), BF16)

    def phase1(s, carry):
        r0 = pl.multiple_of(s * SUPER, SUPER)
        rows = pl.ds(r0, SUPER)
        q = gq_ref[rows, :].astype(F32)
        k = gk_ref[rows, :].astype(F32)
        qes, kes, kds = [], [], []
        for d, (m1, rs) in enumerate(((m1_f, rs_f), (m1_b, rs_b))):
            parts = _split2(la_ref[rows, d * GLA_QK_WIDTH:(d + 1) * GLA_QK_WIDTH])
            x1 = _dot_split(m1, parts)
            erow = jnp.exp(_dot_split(rs, parts))
            dec = erow[0:CH_PER] * erow[CH_PER:2 * CH_PER]
            dec_ref[d, s] = jnp.concatenate([dec, dec], axis=0)
            qe = q * jnp.exp(x1)
            ke = k * jnp.exp(-x1)
            qg_parts, kd_parts = [], []
            for j in range(CH_PER):
                rj = slice(j * CHUNK, (j + 1) * CHUNK)
                qg_parts.append(qe[rj] * erow[j:j + 1])
                kd_parts.append(ke[rj] * erow[CH_PER + j:CH_PER + j + 1])
            qg_ref[d, rows, :] = jnp.concatenate(qg_parts, axis=0).astype(BF16)
            qes.append(qe.astype(BF16))
            kes.append(ke)
            kds.append(jnp.concatenate(kd_parts, axis=0).astype(BF16))
        for c in range(2):
            cl = slice(c * LANES, (c + 1) * LANES)
            vpair = gv_ref[rows, 2 * c * GLA_DV:(2 * c + 2) * GLA_DV]
            for blk in range(SUPER // HALF):
                rb = slice(blk * HALF, (blk + 1) * HALF)
                vbd = jnp.concatenate(
                    [jnp.concatenate([vpair[rb, 0:GLA_DV], zero_blk], axis=1),
                     jnp.concatenate([zero_blk, vpair[rb, GLA_DV:2 * GLA_DV]], axis=1)], axis=0)
                o2 = None
                for d, mask in enumerate((mask_f, mask_b)):
                    ke_cb = kes[d][rb, cl]
                    ke_st = jnp.concatenate([jnp.where(lo_h, ke_cb, zero),
                                             jnp.where(lo_h, zero, ke_cb)], axis=0).astype(BF16)
                    a = _dot_nt(qes[d][rb, cl], ke_st)
                    o = _dot(jnp.where(mask, a, zero).astype(BF16), vbd)
                    o2 = o if o2 is None else o2 + o
                oi_ref[pl.ds(r0 + blk * HALF, HALF), 2 * c * GLA_DV:(2 * c + 2) * GLA_DV] = o2
            for d in range(2):
                for j in range(CH_PER):
                    rj = slice(j * CHUNK, (j + 1) * CHUNK)
                    t = _dot_tn(kds[d][rj, cl], vpair[rj])
                    kv = jnp.concatenate([t[0:GLA_DK, 0:GLA_DV], t[GLA_DK:, GLA_DV:]], axis=0)
                    kv_ref[d, c, s * CH_PER + j] = kv.T
        return carry

    lax.fori_loop(0, nsuper, phase1, 0)

    st_ref[0] = sf_ref[0]
    st_ref[1] = sf_ref[1]
    st_ref[2] = sb_ref[0]
    st_ref[3] = sb_ref[1]

    def phase2(n, carry):
        for d in range(2):
            idx = n if d == 0 else nchunk - 1 - n
            dec = dec_ref[d, idx // CH_PER, pl.ds(idx % CH_PER, 1), :]
            for c in range(2):
                st = st_ref[2 * d + c]
                sb16_ref[c, idx, :, d * LANES:(d + 1) * LANES] = st.astype(BF16)
                st_ref[2 * d + c] = dec[:, c * LANES:(c + 1) * LANES] * st + kv_ref[d, c, idx]
        return carry

    lax.fori_loop(0, nchunk, phase2, 0)

    lo64 = _lane_lo((CHUNK, 2 * LANES))
    zero_b = jnp.zeros((), BF16)

    def phase3(s, carry):
        r0 = pl.multiple_of(s * SUPER, SUPER)
        rows = pl.ds(r0, SUPER)
        inter = [[None] * CH_PER for _ in range(GLA_HEADS)]
        for j in range(CH_PER):
            rj = pl.ds(r0 + j * CHUNK, CHUNK)
            for c in range(2):
                qg_c = jnp.concatenate([qg_ref[d, rj, c * LANES:(c + 1) * LANES] for d in range(2)], axis=1)
                lhs = jnp.concatenate([jnp.where(lo64, qg_c, zero_b),
                                       jnp.where(lo64, zero_b, qg_c)], axis=0)
                t = _dot_nt(lhs, sb16_ref[c, s * CH_PER + j])
                for hh in range(2):
                    inter[2 * c + hh][j] = t[hh * CHUNK:(hh + 1) * CHUNK]
        for hd in range(GLA_HEADS):
            cl = slice(hd * GLA_DV, (hd + 1) * GLA_DV)
            o = oi_ref[rows, cl] + jnp.concatenate(inter[hd], axis=0)
            y = o * lax.rsqrt(jnp.mean(o * o, axis=-1, keepdims=True) + EPS) * gn_ref[:, cl]
            g = gg_ref[rows, cl].astype(F32)
            o_ref[rows, cl] = (y * (g * jax.nn.sigmoid(g))).astype(BF16)
        return carry

    lax.fori_loop(0, nsuper, phase3, 0)


def _gla(gq, gk, gv, gg, lr, wd, bdec, gn, s_f, s_b):
    B, L, _ = gq.shape
    nchunk = L // CHUNK
    full = lambda shape: pl.BlockSpec(shape, lambda b: (0,) * len(shape))
    tok = lambda w: pl.BlockSpec((None, L, w), lambda b: (b, 0, 0))
    st_spec = pl.BlockSpec((None, 2, LANES, GLA_DV), lambda b: (b, 0, 0, 0))
    return pl.pallas_call(
        _gla_kernel,
        grid=(B,),
        in_specs=[tok(GLA_QK_WIDTH), tok(GLA_QK_WIDTH), tok(GLA_WIDTH), tok(GLA_WIDTH), tok(LANES),
                  full(wd.shape), full(bdec.shape), full(gn.shape), st_spec, st_spec],
        out_specs=tok(GLA_WIDTH),
        out_shape=jax.ShapeDtypeStruct((B, L, GLA_WIDTH), BF16),
        scratch_shapes=[pltpu.VMEM((L, 2 * GLA_QK_WIDTH), F32),
                        pltpu.VMEM((L, GLA_WIDTH), F32),
                        pltpu.VMEM((2, L, GLA_QK_WIDTH), BF16),
                        pltpu.VMEM((2, 2, nchunk, GLA_DV, LANES), F32),
                        pltpu.VMEM((2, nchunk, GLA_DV, 2 * LANES), BF16),
                        pltpu.VMEM((2, L // SUPER, 2 * CH_PER, GLA_QK_WIDTH), F32),
                        pltpu.VMEM((4, GLA_DV, LANES), F32)],
        compiler_params=_cparams(("arbitrary",)),
        name="gla_bidirectional",
    )(gq, gk, gv, gg, lr, wd, bdec, gn, s_f, s_b)


def _outproj_kernel(attn_ref, gla_ref, x_ref, mod_ref, w_ref, g2_ref, wr_ref,
                    x1_ref, h2_ref, afft_ref):
    y = _dot(attn_ref[...], w_ref[0:ATTN_WIDTH, :]) + _dot(gla_ref[...], w_ref[ATTN_WIDTH:, :])
    x1 = x_ref[...] + mod_ref[2:3, :] * y
    x1_ref[...] = x1
    h2 = _rms_mod(x1, g2_ref[...], mod_ref[3:4, :], mod_ref[4:5, :]).astype(BF16)
    half = D_MODEL // 2
    hi = pltpu.bitcast(h2[:, 0:half].astype(F32), jnp.uint32)
    lo = pltpu.bitcast(h2[:, half:].astype(F32), jnp.uint32)
    h2_ref[:, 0:half] = pltpu.bitcast(hi | (lo >> 16), F32)
    logits = _dot_nt(wr_ref[...], h2)
    e = jnp.exp(logits - jnp.max(logits, axis=0, keepdims=True))
    afft = e / jnp.sum(e, axis=0, keepdims=True)
    afft_ref[...] = afft
    pad = jnp.zeros((LANES - N_EXPERTS, afft.shape[1]), F32)
    h2_ref[:, half:half + LANES] = jnp.concatenate([afft, pad], axis=0).T


def _output_projection(attn, gla, x, mod3, w_out, g2, w_router, tm):
    B, L, _ = x.shape
    full = lambda shape: pl.BlockSpec(shape, lambda b, i: (0,) * len(shape))
    tok = lambda w: pl.BlockSpec((None, tm, w), lambda b, i: (b, i, 0))
    return pl.pallas_call(
        _outproj_kernel,
        grid=(B, L // tm),
        in_specs=[tok(ATTN_WIDTH), tok(GLA_WIDTH), tok(D_MODEL),
                  pl.BlockSpec((None, 6, D_MODEL), lambda b, i: (b, 0, 0)),
                  full(w_out.shape), full(g2.shape), full(w_router.shape)],
        out_specs=[tok(D_MODEL), tok(ROW_WORDS),
                   pl.BlockSpec((None, N_EXPERTS, tm), lambda b, i: (b, 0, i))],
        out_shape=[jax.ShapeDtypeStruct((B, L, D_MODEL), F32),
                   jax.ShapeDtypeStruct((B, L, ROW_WORDS), F32),
                   jax.ShapeDtypeStruct((B, N_EXPERTS, L), F32)],
        compiler_params=_cparams(("arbitrary", "arbitrary")),
        name="output_projection_router",
    )(attn, gla, x, mod3, w_out, g2, w_router)


def _topk_kernel(afft_ref, post_ref, pos_ref, *, cap):
    nbatch, E, L = afft_ref.shape
    aff = afft_ref[...].reshape(nbatch * E, L)
    E = nbatch * E

    def search(i, thr):
        cand = thr | jnp.left_shift(jnp.int32(1), 30 - i)
        cnt = jnp.sum(jnp.where(aff >= pltpu.bitcast(cand, F32), 1.0, 0.0), axis=-1, keepdims=True)
        return jnp.where(cnt >= cap, cand, thr)

    thr_bits = lax.fori_loop(0, 31, search, jnp.zeros((E, 1), jnp.int32))
    thr = pltpu.bitcast(thr_bits, F32)
    above = aff > thr
    tie = aff == thr
    need = cap - jnp.sum(jnp.where(above, 1.0, 0.0), axis=-1, keepdims=True)

    upper = (lax.broadcasted_iota(jnp.int32, (LANES, LANES), 0)
             <= lax.broadcasted_iota(jnp.int32, (LANES, LANES), 1)).astype(BF16)

    def prefix(mask):
        parts = []
        run = jnp.zeros((E, 1), F32)
        for j in range(L // LANES):
            blk = jnp.where(mask[:, j * LANES:(j + 1) * LANES], 1.0, 0.0).astype(BF16)
            loc = _dot(blk, upper) + run
            parts.append(loc)
            run = loc[:, LANES - 1:LANES]
        return jnp.concatenate(parts, axis=1)

    tie_rank = prefix(tie)
    sel = above | (tie & (tie_rank <= need))
    slot = prefix(sel).astype(jnp.int32) - 1
    post = jnp.where(sel, slot, -1)
    ne = E // nbatch
    pad_i = jnp.full((LANES - ne, L), -1, jnp.int32)
    for bb in range(nbatch):
        rows = slice(bb * ne, (bb + 1) * ne)
        post_ref[bb] = post[rows]
        pos_ref[bb] = jnp.concatenate([post[rows], pad_i], axis=0).T


def _expert_choice(afft, cap, nbatch):
    B, E, L = afft.shape
    return pl.pallas_call(
        functools.partial(_topk_kernel, cap=cap),
        grid=(B // nbatch,),
        in_specs=[pl.BlockSpec((nbatch, E, L), lambda b: (b, 0, 0))],
        out_specs=[pl.BlockSpec((nbatch, E, L), lambda b: (b, 0, 0)),
                   pl.BlockSpec((nbatch, L, LANES), lambda b: (b, 0, 0))],
        out_shape=[jax.ShapeDtypeStruct((B, E, L), jnp.int32),
                   jax.ShapeDtypeStruct((B, L, LANES), jnp.int32)],
        compiler_params=_cparams(("arbitrary",)),
        name="expert_choice_topk",
    )(afft)


SC_CORES = 2
SC_SUBCORES = 16
SC_LANES = 16
SC_WINDOW = 128


def _dispatch(rows, slots, cap, first_expert, n_experts):
    L = slots.shape[1]
    W = rows.shape[1]
    n_pair = slots.shape[0] // N_EXPERTS * n_experts
    per_worker = n_pair // (SC_CORES * SC_SUBCORES)
    n_win = cap // SC_WINDOW
    mesh = plsc.VectorSubcoreMesh(core_axis_name="c", subcore_axis_name="s",
                                  num_cores=SC_CORES, num_subcores=SC_SUBCORES)

    def body(rows_hbm, slots_hbm, out_hbm, slot_v, *scratch):
        idx_v, buf_v, sem = scratch[:n_win], scratch[n_win], scratch[n_win + 1]
        worker = lax.axis_index("s") * SC_CORES + lax.axis_index("c")

        @pl.loop(0, per_worker)
        def _(p):
            pair = worker * per_worker + p
            batch = pair // n_experts
            first_tok = batch * L
            pltpu.sync_copy(slots_hbm.at[batch * N_EXPERTS + first_expert + pair % n_experts], slot_v)

            @pl.loop(0, L // SC_LANES)
            def _(i):
                v = slot_v[pl.ds(i * SC_LANES, SC_LANES)]
                tok = lax.iota(jnp.int32, SC_LANES) + (i * SC_LANES + first_tok)
                for w in range(n_win):
                    in_win = (v >= w * SC_WINDOW) & (v < (w + 1) * SC_WINDOW)
                    plsc.store_scatter(idx_v[w], [v - w * SC_WINDOW], tok, mask=in_win)

            for w in range(n_win):
                pltpu.async_copy(rows_hbm.at[idx_v[w]], buf_v, sem).wait()
                pltpu.sync_copy(buf_v, out_hbm.at[pl.ds(pair * cap + w * SC_WINDOW, SC_WINDOW)])

    return pl.kernel(
        body,
        out_type=jax.ShapeDtypeStruct((n_pair * cap, W), rows.dtype),
        mesh=mesh,
        scratch_types=[pltpu.VMEM((L,), jnp.int32)]
        + [pltpu.VMEM((SC_WINDOW,), jnp.int32) for _ in range(n_win)]
        + [pltpu.VMEM((SC_WINDOW, W), rows.dtype), pltpu.SemaphoreType.DMA],
        compiler_params=pltpu.CompilerParams(needs_layout_passes=False),
        name="moe_dispatch_gather",
    )(rows, slots)


def _ffn_kernel(xs_ref, mod_ref, wg_ref, wu_ref, wd_ref, y_ref, wgb_ref, wub_ref, wdb_ref, *,
                first_expert):
    nbatch, cap, _ = xs_ref.shape
    d = wg_ref.shape[0]
    dw = d // 2

    @pl.when(pl.program_id(1) == 0)
    def _():
        wgb_ref[...] = wg_ref[...].astype(BF16)
        wub_ref[...] = wu_ref[...].astype(BF16)
        wdb_ref[...] = wd_ref[...].astype(BF16)

    words = pltpu.bitcast(xs_ref[:, :, 0:dw].reshape(nbatch * cap, dw), jnp.uint32)
    xs = jnp.concatenate([pltpu.bitcast(words & jnp.uint32(0xFFFF0000), F32).astype(BF16),
                          pltpu.bitcast(words << 16, F32).astype(BF16)], axis=1)
    f = wg_ref.shape[1]
    half = f // 2
    acc = None
    for j in range(2):
        cols = slice(j * half, (j + 1) * half)
        g = _dot(xs, wgb_ref[:, cols])
        u = _dot(xs, wub_ref[:, cols])
        hid = (g * jax.nn.sigmoid(g) * u).astype(BF16)
        part = _dot(hid, wdb_ref[cols, :])
        acc = part if acc is None else acc + part
    aff = xs_ref[:, :, dw:dw + LANES].reshape(nbatch * cap, LANES)
    lane = lax.broadcasted_iota(jnp.int32, aff.shape, 1)
    gate = jnp.sum(jnp.where(lane == first_expert + pl.program_id(0), aff, 0.0), axis=-1, keepdims=True)
    y = (acc * gate).reshape(nbatch, cap, d)
    for i in range(nbatch):
        y_ref[i] = (y[i] * mod_ref[i, 5:6, :]).astype(y_ref.dtype)


def _expert_ffn(xs, mod3, w_gate, w_up, w_down, first_expert, nbatch, out_dtype):
    B, E, cap, row_words = xs.shape
    d, f = w_gate.shape[1:]
    tok = pl.BlockSpec((nbatch, None, cap, d), lambda e, b: (b, e, 0, 0))
    return pl.pallas_call(
        functools.partial(_ffn_kernel, first_expert=first_expert),
        grid=(E, B // nbatch),
        in_specs=[pl.BlockSpec((nbatch, None, cap, row_words), lambda e, b: (b, e, 0, 0)),
                  pl.BlockSpec((nbatch, 6, d), lambda e, b: (b, 0, 0)),
                  pl.BlockSpec((None, d, f), lambda e, b: (e + first_expert, 0, 0)),
                  pl.BlockSpec((None, d, f), lambda e, b: (e + first_expert, 0, 0)),
                  pl.BlockSpec((None, f, d), lambda e, b: (e + first_expert, 0, 0))],
        out_specs=tok,
        out_shape=jax.ShapeDtypeStruct((B, E, cap, d), out_dtype),
        scratch_shapes=[pltpu.VMEM((d, f), BF16), pltpu.VMEM((d, f), BF16), pltpu.VMEM((f, d), BF16)],
        compiler_params=_cparams(("arbitrary", "arbitrary")),
        name="expert_swiglu",
    )(xs, mod3, w_gate, w_up, w_down)


SC_ADD_ROWS = 16


def _combine_add(acc, y, slots, cap, first_expert, n_experts, after=()):
    L = slots.shape[1]
    D = y.shape[1]
    assert slots.shape[0] // N_EXPERTS == SC_CORES * SC_SUBCORES
    n_win = cap // SC_ADD_ROWS
    mesh = plsc.VectorSubcoreMesh(core_axis_name="c", subcore_axis_name="s",
                                  num_cores=SC_CORES, num_subcores=SC_SUBCORES)

    def body(acc_hbm, y_hbm, slots_hbm, *rest):
        slot_v, *scratch = rest[len(after):]
        idx_v = scratch[:n_win]
        y_v = scratch[n_win:n_win + 2]
        o_v = scratch[n_win + 2:n_win + 4]
        sem_y, sem_g, sem_s = (scratch[n_win + 4 + 2 * k:n_win + 6 + 2 * k] for k in range(3))
        batch = lax.axis_index("s") * SC_CORES + lax.axis_index("c")

        @pl.loop(0, n_experts)
        def _(el):
            pltpu.sync_copy(slots_hbm.at[batch * N_EXPERTS + first_expert + el], slot_v)

            @pl.loop(0, L // SC_LANES)
            def _(i):
                v = slot_v[pl.ds(i * SC_LANES, SC_LANES)]
                tok = lax.iota(jnp.int32, SC_LANES) + (i * SC_LANES + batch * L)
                for w in range(n_win):
                    in_win = (v >= w * SC_ADD_ROWS) & (v < (w + 1) * SC_ADD_ROWS)
                    plsc.store_scatter(idx_v[w], [v - w * SC_ADD_ROWS], tok, mask=in_win)

            row0 = (batch * n_experts + el) * cap

            def fetch(w):
                b = w % 2
                return (pltpu.async_copy(y_hbm.at[pl.ds(row0 + w * SC_ADD_ROWS, SC_ADD_ROWS)], y_v[b], sem_y[b]),
                        pltpu.async_copy(acc_hbm.at[idx_v[w]], o_v[b], sem_g[b]))

            loads = fetch(0)
            stores = [None, None]
            for w in range(n_win):
                b = w % 2
                nxt = None
                if w + 1 < n_win:
                    if stores[1 - b] is not None:
                        stores[1 - b].wait()
                        stores[1 - b] = None
                    nxt = fetch(w + 1)
                loads[0].wait()
                loads[1].wait()

                @pl.loop(0, SC_ADD_ROWS)
                def _(r):
                    for c in range(D // SC_LANES):
                        lanes = pl.ds(c * SC_LANES, SC_LANES)
                        plsc.addupdate(o_v[b].at[r, lanes], y_v[b][r, lanes])

                stores[b] = pltpu.async_copy(o_v[b], acc_hbm.at[idx_v[w]], sem_s[b])
                loads = nxt
            for st in stores:
                if st is not None:
                    st.wait()

    pl.kernel(
        body,
        out_type=(),
        mesh=mesh,
        scratch_types=[pltpu.VMEM((L,), jnp.int32)]
        + [pltpu.VMEM((SC_ADD_ROWS,), jnp.int32) for _ in range(n_win)]
        + [pltpu.VMEM((SC_ADD_ROWS, D), F32) for _ in range(4)]
        + [pltpu.SemaphoreType.DMA for _ in range(6)],
        compiler_params=pltpu.CompilerParams(needs_layout_passes=False),
        name="moe_combine_row_add",
    )(acc, y, slots, *after)


def _combine_kernel(pos_ref, y_ref, x1_ref, o_ref, *, cap, first_expert):
    tt = pos_ref.shape[0]
    n_experts = y_ref.shape[0]
    slot = lax.broadcasted_iota(jnp.int32, (tt, cap), 1)
    onehot = jnp.concatenate(
        [jnp.where(pos_ref[:, e:e + 1] == slot, 1.0, 0.0).astype(BF16)
         for e in range(first_expert, first_expert + n_experts)], axis=1)
    o_ref[...] = x1_ref[...] + _dot(onehot, y_ref[...].reshape(n_experts * cap, D_MODEL))


def _combine(pos, y, x1, cap, tt, first_expert):
    B, L, _ = x1.shape
    n_experts = y.shape[1]
    tok = lambda w: pl.BlockSpec((None, tt, w), lambda b, i: (b, i, 0))
    return pl.pallas_call(
        functools.partial(_combine_kernel, cap=cap, first_expert=first_expert),
        grid=(B, L // tt),
        in_specs=[tok(LANES),
                  pl.BlockSpec((None, n_experts, cap, D_MODEL), lambda b, i: (b, 0, 0, 0)),
                  tok(D_MODEL)],
        out_specs=tok(D_MODEL),
        out_shape=jax.ShapeDtypeStruct((B, L, D_MODEL), F32),
        compiler_params=_cparams(("arbitrary", "arbitrary")),
        name="moe_combine",
    )(pos, y, x1)


def _rope_tables(L):
    inv = ROPE_BASE ** (-jnp.arange(ROPE_FREQS, dtype=F32) / ROPE_FREQS)
    pos = jnp.arange(L)
    row = (pos // GRID_W).astype(F32)[:, None] * inv
    col = (pos % GRID_W).astype(F32)[:, None] * inv
    cos = jnp.concatenate([jnp.cos(row), jnp.cos(row), jnp.cos(col), jnp.cos(col)], axis=1)
    sin = jnp.concatenate([-jnp.sin(row), jnp.sin(row), -jnp.sin(col), jnp.sin(col)], axis=1)
    return jnp.tile(cos, (1, 2)), jnp.tile(sin, (1, 2))


def _head_mean_matrix(n):
    idx = np.arange(n) // HEAD_DIM
    return jnp.asarray((idx[:, None] == idx[None, :]).astype(np.float32) / HEAD_DIM, dtype=BF16)


def kernel(x, c, ctx, c_ctx, w_mod, b_mod, norm1_g, w_in, q_norm_g, k_norm_g, attn_sink,
           w_decay_fwd, b_decay_fwd, w_decay_bwd, b_decay_bwd, gla_norm_g, w_out, norm2_g,
           w_router, w_e_gate, w_e_up, w_e_down):
    B, L, D = x.shape
    cap = CAPACITY_FACTOR * L // N_EXPERTS
    layer = 0

    rows = ((B + 1 + 7) // 8) * 8
    cc = jnp.concatenate([c, c_ctx[None, :], jnp.zeros((rows - B - 1, D), F32)], axis=0)
    mod_all = _modulation(cc, w_mod[layer], b_mod[layer])
    mod3 = mod_all[:B].reshape(B, 6, D)
    modc = mod_all[B].reshape(6, D)

    w = w_in[layer]
    o = np.cumsum([0, ATTN_WIDTH, KV_WIDTH, KV_WIDTH, GLA_QK_WIDTH, GLA_QK_WIDTH,
                   GLA_WIDTH, GLA_WIDTH, GATE_RANK, GATE_RANK])
    w_lr = jnp.concatenate([w[:, o[7]:o[9]]] * 3 + [jnp.zeros((D, LANES - 6 * GATE_RANK), F32)], axis=1)
    head_order = np.arange(N_Q_HEADS).reshape(N_KV_HEADS, -1).T.reshape(-1)
    attn_perm = (head_order[:, None] * HEAD_DIM + np.arange(HEAD_DIM)[None, :]).reshape(-1)
    w_in_r = jnp.concatenate([w[:, attn_perm], w[:, o[1]:o[7]], w_lr], axis=1).astype(BF16)
    w_out_r = jnp.concatenate([w_out[layer][attn_perm], w_out[layer][ATTN_WIDTH:]], axis=0).astype(BF16)
    w_ctx = jnp.concatenate([w[:, o[1]:o[3]], w[:, o[4]:o[6]], w_lr], axis=1).astype(BF16)
    wd2 = jnp.zeros((2 * GATE_RANK, 2 * GLA_QK_WIDTH), F32)
    wd2 = wd2.at[0:GATE_RANK, 0:GLA_QK_WIDTH].set(w_decay_fwd[layer])
    wd2 = wd2.at[GATE_RANK:, GLA_QK_WIDTH:].set(w_decay_bwd[layer])
    wd_hi = wd2.astype(BF16)
    wd_lo = (wd2 - wd_hi.astype(F32)).astype(BF16)
    wd = jnp.concatenate([wd_hi, wd_hi, wd_lo,
                          jnp.zeros((LANES - 6 * GATE_RANK, 2 * GLA_QK_WIDTH), BF16)], axis=0)
    bdec = jnp.concatenate([b_decay_fwd[layer], b_decay_bwd[layer]])[None, :]
    g1 = norm1_g[layer][None, :]
    g2 = norm2_g[layer][None, :]
    qg = jnp.tile(q_norm_g[layer], N_Q_HEADS)[None, :]
    kg2 = jnp.tile(k_norm_g[layer], N_KV_HEADS)[None, :]
    gn = jnp.tile(gla_norm_g[layer], GLA_HEADS)[None, :]
    bd512 = _head_mean_matrix(ATTN_WIDTH)
    bd128 = _head_mean_matrix(KV_WIDTH)
    cos_t, sin_t = _rope_tables(L)
    w_router_t = w_router[layer].T.astype(BF16)

    kc, vc, s_f, s_b = _context_side(ctx, modc, g1, w_ctx, kg2, bd128, wd, bdec)
    q, k, v, gq, gk, gv, gg, lr = _input_projection(
        x, mod3, g1, w_in_r, qg, kg2, bd512, cos_t, sin_t, tm=1024)
    attn = _window_attention(attn_sink[layer], q, k, v, kc, vc, tq=512)
    gla = _gla(gq, gk, gv, gg, lr, wd, bdec, gn, s_f, s_b)
    x1, h2, afft = _output_projection(attn, gla, x, mod3, w_out_r, g2, w_router_t, tm=1024)
    post, pos = _expert_choice(afft, cap, nbatch=4)
    n_groups = 4
    per_group = N_EXPERTS // n_groups
    rows = h2.reshape(B * L, ROW_WORDS)
    slots = post.reshape(B * N_EXPERTS, L)
    stream = jax.new_ref(x1.reshape(B * L, D))
    gathered = [_dispatch(rows, slots, cap, g * per_group, per_group) for g in range(n_groups)]
    for g in range(n_groups):
        last = g == n_groups - 1
        xs = gathered[g].reshape(B, per_group, cap, ROW_WORDS)
        y = _expert_ffn(xs, mod3, w_e_gate[layer], w_e_up[layer], w_e_down[layer], g * per_group,
                        nbatch=4, out_dtype=BF16 if last else F32)
        if not last:
            _combine_add(stream, y.reshape(B * per_group * cap, D), slots, cap, g * per_group, per_group,
                         after=gathered[g + 1:])
    x1 = jax.freeze(stream).reshape(B, L, D)
    return _combine(pos, y, x1, cap, tt=1024, first_expert=(n_groups - 1) * per_group)
```

```python
import functools

import jax
import jax.numpy as jnp
import numpy as np
from jax import lax
from jax.experimental import pallas as pl
from jax.experimental.pallas import tpu as pltpu
from jax.experimental.pallas import tpu_sc as plsc

D_MODEL = 1024
GRID_W = 64
HEAD_DIM = 64
N_Q_HEADS = 8
N_KV_HEADS = 2
BLOCK = 128
ROPE_FREQS = 16
ROPE_BASE = 10000.0
GLA_HEADS = 4
GLA_DV = 128
GLA_DK = 64
GATE_RANK = 16
GATE_NORMALIZER = 16.0
CHUNK = 64
N_EXPERTS = 16
CAPACITY_FACTOR = 2
ATTN_WIDTH = 512
KV_WIDTH = 128
GLA_QK_WIDTH = 256
GLA_WIDTH = 512
EPS = 1e-6
NEG_INF = -1e30
LOG2E = 1.4426950408889634

LANES = 128
ROW_WORDS = D_MODEL // 2 + LANES
VMEM_LIMIT = 56 * 1024 * 1024

F32 = jnp.float32
BF16 = jnp.bfloat16
HI = lax.Precision.HIGHEST


def _cparams(sem):
    return pltpu.CompilerParams(dimension_semantics=sem, vmem_limit_bytes=VMEM_LIMIT)


def _dot(a, b):
    return jnp.dot(a, b, preferred_element_type=F32)


def _dot_hi(a, b):
    return jnp.dot(a, b, preferred_element_type=F32, precision=HI)


def _dot_nt(a, b):
    return lax.dot_general(a, b, (((1,), (1,)), ((), ())), preferred_element_type=F32)


def _dot_tn(a, b, precision=None):
    return lax.dot_general(a, b, (((0,), (0,)), ((), ())), preferred_element_type=F32,
                           precision=precision)


def _split2(t):
    hi = t.astype(BF16)
    lo = (t - hi.astype(F32)).astype(BF16)
    return hi, lo


def _rms_mod(t, g, shift, scale):
    y = t * lax.rsqrt(jnp.mean(t * t, axis=-1, keepdims=True) + EPS)
    return (y * g) * (1.0 + scale) + shift


def _log_decay(lr, wd3, bias):
    hi = lr.astype(BF16)
    lo = (lr - hi.astype(F32)).astype(BF16)
    lane = lax.broadcasted_iota(jnp.int32, lr.shape, 1)
    second = (lane >= 2 * GATE_RANK) & (lane < 4 * GATE_RANK)
    z = _dot(jnp.where(second, lo, hi), wd3) + bias
    return (jnp.minimum(z, 0.0) - jnp.log(1.0 + jnp.exp(-jnp.abs(z)))) * (1.0 / GATE_NORMALIZER)


def _lane_lo(shape):
    return (lax.broadcasted_iota(jnp.int32, shape, len(shape) - 1) % LANES) < HEAD_DIM


def _mod_kernel(c_ref, w_ref, b_ref, o_ref):
    c = c_ref[...]
    s = c * jax.nn.sigmoid(c)
    o_ref[...] = _dot_hi(s, w_ref[...]) + b_ref[...]


def _modulation(cc, w_mod, b_mod):
    m = cc.shape[0]
    n = w_mod.shape[1]
    tn = 1024
    return pl.pallas_call(
        _mod_kernel,
        grid=(n // tn,),
        in_specs=[pl.BlockSpec((m, D_MODEL), lambda j: (0, 0)),
                  pl.BlockSpec((D_MODEL, tn), lambda j: (0, j)),
                  pl.BlockSpec((1, tn), lambda j: (0, j))],
        out_specs=pl.BlockSpec((m, tn), lambda j: (0, j)),
        out_shape=jax.ShapeDtypeStruct((m, n), F32),
        compiler_params=_cparams(("arbitrary",)),
        name="adaln_mod",
    )(cc, w_mod, b_mod.reshape(1, n))


def _ctx_kernel(ctx_ref, mod_ref, g1_ref, w_ref, kg_ref, bd_ref, wd_ref, bdec_ref,
                kc_ref, vc_ref, sf_ref, sb_ref):
    n = ctx_ref.shape[0]
    h = _rms_mod(ctx_ref[...], g1_ref[...], mod_ref[0:1, :], mod_ref[1:2, :]).astype(BF16)
    pc = _dot(h, w_ref[...])
    ak = pc[:, 0:128]
    av = pc[:, 128:256]
    gk = pc[:, 256:512]
    gv = pc[:, 512:1024].astype(BF16)
    lr = pc[:, 1024:1152]
    sq_hi, sq_lo = _split2(ak * ak)
    ms = _dot(sq_hi, bd_ref[...]) + _dot(sq_lo, bd_ref[...])
    kn = ak * lax.rsqrt(ms + EPS) * kg_ref[...]
    kc_ref[...] = kn.astype(BF16)
    vc_ref[...] = av.astype(BF16)
    la = _log_decay(lr, wd_ref[...], bdec_ref[...])
    r = lax.broadcasted_iota(jnp.int32, (n, n), 0)
    cidx = lax.broadcasted_iota(jnp.int32, (n, n), 1)
    after = (cidx > r).astype(F32)
    before = (cidx < r).astype(F32)
    w_f = jnp.exp(_dot_hi(after, la[:, 0:256]))
    w_b = jnp.exp(_dot_hi(before, la[:, 256:512]))
    lo = _lane_lo((n, LANES))
    for w, out in ((w_f, sf_ref), (w_b, sb_ref)):
        kw = gk * w
        for c in range(2):
            kwc = kw[:, c * LANES:(c + 1) * LANES]
            k_lo = jnp.where(lo, kwc, 0.0).astype(BF16)
            k_hi = jnp.where(lo, 0.0, kwc).astype(BF16)
            v0 = gv[:, (2 * c) * GLA_DV:(2 * c + 1) * GLA_DV]
            v1 = gv[:, (2 * c + 1) * GLA_DV:(2 * c + 2) * GLA_DV]
            out[c] = _dot_tn(v0, k_lo) + _dot_tn(v1, k_hi)


def _context_side(ctx, modc, g1, w_ctx, kg2, bd128, wd, bdec):
    B, n, _ = ctx.shape
    full = lambda shape: pl.BlockSpec(shape, lambda b: (0,) * len(shape))
    kv_spec = pl.BlockSpec((None, n, KV_WIDTH), lambda b: (b, 0, 0))
    st_spec = pl.BlockSpec((None, 2, LANES, GLA_DV), lambda b: (b, 0, 0, 0))
    kv_shape = jax.ShapeDtypeStruct((B, n, KV_WIDTH), BF16)
    st_shape = jax.ShapeDtypeStruct((B, 2, LANES, GLA_DV), F32)
    return pl.pallas_call(
        _ctx_kernel,
        grid=(B,),
        in_specs=[pl.BlockSpec((None, n, D_MODEL), lambda b: (b, 0, 0)),
                  full(modc.shape), full(g1.shape), full(w_ctx.shape), full(kg2.shape),
                  full(bd128.shape), full(wd.shape), full(bdec.shape)],
        out_specs=[kv_spec, kv_spec, st_spec, st_spec],
        out_shape=[kv_shape, kv_shape, st_shape, st_shape],
        compiler_params=_cparams(("arbitrary",)),
        name="context_side",
    )(ctx, modc, g1, w_ctx, kg2, bd128, wd, bdec)


def _swap16(t):
    n = t.shape[1]
    first = (lax.broadcasted_iota(jnp.int32, t.shape, 1) % 32) < ROPE_FREQS
    return jnp.where(first, pltpu.roll(t, n - ROPE_FREQS, 1), pltpu.roll(t, ROPE_FREQS, 1))


def _inproj_kernel(x_ref, mod_ref, g1_ref, w_ref, qg_ref, kg_ref, bd_ref, cos_ref, sin_ref,
                   q_ref, k_ref, v_ref, gq_ref, gk_ref, gv_ref, gg_ref, lr_ref):
    h = _rms_mod(x_ref[...], g1_ref[...], mod_ref[0:1, :], mod_ref[1:2, :]).astype(BF16)
    cos = cos_ref[...]
    sin = sin_ref[...]

    def head_norm_rope(t, g, bd, reps):
        ms = _dot((t * t).astype(BF16), bd)
        tn = t * lax.rsqrt(ms + EPS) * g
        c = jnp.concatenate([cos] * reps, axis=1) if reps > 1 else cos
        s = jnp.concatenate([sin] * reps, axis=1) if reps > 1 else sin
        return tn * c + _swap16(tn) * s

    aq = _dot(h, w_ref[:, 0:512])
    q = head_norm_rope(aq, qg_ref[...], bd_ref[...], 4) * (HEAD_DIM ** -0.5 * LOG2E)
    q_ref[...] = q.astype(BF16)
    akv = _dot(h, w_ref[:, 512:768])
    k = head_norm_rope(akv[:, 0:128], kg_ref[...], bd_ref[0:128, 0:128], 1)
    k_ref[...] = k.astype(BF16)
    v_ref[...] = akv[:, 128:256].astype(BF16)
    gqk = _dot(h, w_ref[:, 768:1280])
    gq_ref[...] = (gqk[:, 0:256] * (GLA_DK ** -0.5)).astype(BF16)
    gk_ref[...] = gqk[:, 256:512].astype(BF16)
    gv_ref[...] = _dot(h, w_ref[:, 1280:1792]).astype(BF16)
    gg_ref[...] = _dot(h, w_ref[:, 1792:2304]).astype(BF16)
    lr_ref[...] = _dot(h, w_ref[:, 2304:2432])


def _input_projection(x, mod3, g1, w_in_r, qg, kg2, bd512, cos_t, sin_t, tm):
    B, L, _ = x.shape
    full = lambda shape: pl.BlockSpec(shape, lambda b, i: (0,) * len(shape))
    tok = lambda w: pl.BlockSpec((None, tm, w), lambda b, i: (b, i, 0))
    widths = (ATTN_WIDTH, KV_WIDTH, KV_WIDTH, GLA_QK_WIDTH, GLA_QK_WIDTH, GLA_WIDTH, GLA_WIDTH, LANES)
    dtypes = (BF16,) * 7 + (F32,)
    return pl.pallas_call(
        _inproj_kernel,
        grid=(B, L // tm),
        in_specs=[tok(D_MODEL),
                  pl.BlockSpec((None, 6, D_MODEL), lambda b, i: (b, 0, 0)),
                  full(g1.shape), full(w_in_r.shape), full(qg.shape), full(kg2.shape),
                  full(bd512.shape),
                  pl.BlockSpec((tm, LANES), lambda b, i: (i, 0)),
                  pl.BlockSpec((tm, LANES), lambda b, i: (i, 0))],
        out_specs=[tok(w) for w in widths],
        out_shape=[jax.ShapeDtypeStruct((B, L, w), dt) for w, dt in zip(widths, dtypes)],
        compiler_params=_cparams(("arbitrary", "arbitrary")),
        name="input_projection",
    )(x, mod3, g1, w_in_r, qg, kg2, bd512, cos_t, sin_t)


def _attn_kernel(sink_ref, q_ref, kp_ref, ko_ref, kn_ref, vp_ref, vo_ref, vn_ref,
                 kc_ref, vc_ref, o_ref):
    i = pl.program_id(1)
    ni = pl.num_programs(1)
    nsub = q_ref.shape[0] // BLOCK
    ncol = ATTN_WIDTH // LANES
    win = 3 * BLOCK
    ucol = 4
    half_rows = ucol * BLOCK
    k_win = jnp.concatenate([kp_ref[...], ko_ref[...], kn_ref[...]], axis=0)
    v_win = jnp.concatenate([vp_ref[...], vo_ref[...], vn_ref[...]], axis=0)
    k_ctx = kc_ref[...]
    lo_w = _lane_lo(v_win.shape)
    lo_c = _lane_lo(vc_ref.shape)
    lo_q = _lane_lo((BLOCK, LANES))
    zero = jnp.zeros((), BF16)
    one = jnp.ones((), BF16)
    v0_c, v0_w = jnp.where(lo_c, vc_ref[...], one), jnp.where(lo_w, v_win, one)
    v1_c, v1_w = jnp.where(lo_c, one, vc_ref[...]), jnp.where(lo_w, one, v_win)
    qi = lax.broadcasted_iota(jnp.int32, (half_rows, BLOCK), 0) % BLOCK
    kj = lax.broadcasted_iota(jnp.int32, (half_rows, BLOCK), 1)
    no_prev = jnp.where(i > 0, 0, BLOCK)
    no_next = jnp.where(i < ni - 1, 0, BLOCK)
    row_head = lax.broadcasted_iota(jnp.int32, (half_rows, 1), 0) // BLOCK
    lo_o = _lane_lo((half_rows, LANES))
    for t in range(nsub):
        rows = slice(t * BLOCK, (t + 1) * BLOCK)
        keys = slice(t * BLOCK, t * BLOCK + win)
        cols = [q_ref[rows, c * LANES:(c + 1) * LANES] for c in range(ncol)]
        prev_ok = kj >= qi + (no_prev if t == 0 else 0)
        next_ok = kj <= qi - (no_next if t == nsub - 1 else 0)
        for p in range(ncol // ucol):
            outs = []
            qs = jnp.concatenate([jnp.where(lo_q, qc, zero) for qc in cols[ucol * p:ucol * (p + 1)]]
                                 + [jnp.where(lo_q, zero, qc) for qc in cols[ucol * p:ucol * (p + 1)]], axis=0)
            s_c_all = _dot_nt(qs, k_ctx)
            s_w_all = _dot_nt(qs, k_win[keys])
            for g, (vv_c, vv_w) in enumerate(((v0_c, v0_w), (v1_c, v1_w))):
                head = g * ncol + ucol * p
                sink_g = jnp.full((half_rows, 1), sink_ref[head + ucol - 1], F32)
                for j in range(ucol - 2, -1, -1):
                    sink_g = jnp.where(row_head <= j, sink_ref[head + j], sink_g)
                sink_g = sink_g * LOG2E
                s_c = s_c_all[g * half_rows:(g + 1) * half_rows]
                s_w = s_w_all[g * half_rows:(g + 1) * half_rows]
                s_p = jnp.where(prev_ok, s_w[:, 0:BLOCK], NEG_INF)
                s_o = s_w[:, BLOCK:2 * BLOCK]
                s_n = jnp.where(next_ok, s_w[:, 2 * BLOCK:win], NEG_INF)
                m = jnp.maximum(jnp.maximum(jnp.max(s_c, axis=-1, keepdims=True),
                                            jnp.max(jnp.maximum(jnp.maximum(s_p, s_o), s_n),
                                                    axis=-1, keepdims=True)), sink_g)
                e_c = jnp.exp2(s_c - m).astype(BF16)
                e_w = jnp.concatenate([jnp.exp2(s_p - m), jnp.exp2(s_o - m), jnp.exp2(s_n - m)],
                                      axis=1).astype(BF16)
                acc = _dot(e_c, vv_c) + _dot(e_w, vv_w[keys])
                outs.append(acc / (pltpu.roll(acc, HEAD_DIM, 1) + jnp.exp2(sink_g - m)))
            o = jnp.where(lo_o, outs[0], outs[1]).astype(BF16)
            for j in range(ucol):
                c = ucol * p + j
                o_ref[rows, c * LANES:(c + 1) * LANES] = o[j * BLOCK:(j + 1) * BLOCK]


def _window_attention(sink, q, k, v, kc, vc, tq):
    B, L, _ = q.shape
    nb = L // BLOCK
    nsub = tq // BLOCK
    n_ctx = kc.shape[1]
    prev = pl.BlockSpec((None, BLOCK, KV_WIDTH), lambda b, n: (b, jnp.maximum(n * nsub - 1, 0), 0))
    own = pl.BlockSpec((None, tq, KV_WIDTH), lambda b, n: (b, n, 0))
    nxt = pl.BlockSpec((None, BLOCK, KV_WIDTH),
                       lambda b, n: (b, jnp.minimum((n + 1) * nsub, nb - 1), 0))
    cspec = pl.BlockSpec((None, n_ctx, KV_WIDTH), lambda b, n: (b, 0, 0))
    return pl.pallas_call(
        _attn_kernel,
        grid=(B, L // tq),
        in_specs=[pl.BlockSpec(memory_space=pltpu.SMEM),
                  pl.BlockSpec((None, tq, ATTN_WIDTH), lambda b, n: (b, n, 0)),
                  prev, own, nxt, prev, own, nxt, cspec, cspec],
        out_specs=pl.BlockSpec((None, tq, ATTN_WIDTH), lambda b, n: (b, n, 0)),
        out_shape=jax.ShapeDtypeStruct((B, L, ATTN_WIDTH), BF16),
        compiler_params=_cparams(("arbitrary", "arbitrary")),
        name="window_attention",
    )(sink, q, k, k, k, v, v, v, kc, vc)


SUPER = 256
CH_PER = SUPER // CHUNK
HALF = 128


def _dot_split(m, parts):
    return _dot(m, parts[0]) + _dot(m, parts[1])


def _gla_kernel(gq_ref, gk_ref, gv_ref, gg_ref, lr_ref, wd_ref, bdec_ref, gn_ref, sf_ref, sb_ref,
                o_ref, la_ref, oi_ref, qg_ref, kv_ref, sb16_ref, dec_ref, st_ref):
    L = gq_ref.shape[0]
    nsuper = L // SUPER
    nchunk = L // CHUNK
    half = CHUNK // 2
    la_ref[...] = _log_decay(lr_ref[...], wd_ref[...], bdec_ref[...])

    r = lax.broadcasted_iota(jnp.int32, (SUPER, SUPER), 0)
    cidx = lax.broadcasted_iota(jnp.int32, (SUPER, SUPER), 1)
    same = (r // CHUNK) == (cidx // CHUNK)
    pr = r % CHUNK
    pc = cidx % CHUNK
    one = jnp.float32(1.0)
    zero = jnp.float32(0.0)
    in_f = jnp.where(pc <= pr, one, zero)
    in_b = jnp.where(pc >= pr, one, zero)
    ref_f = jnp.where(pc < half, one, zero)
    ref_b = jnp.where(pc >= half, one, zero)
    m1_f = jnp.where(same, in_f - ref_f, zero).astype(BF16)
    m1_b = jnp.where(same, in_b - ref_b, zero).astype(BF16)
    rh = lax.broadcasted_iota(jnp.int32, (HALF, 2 * HALF), 0)
    ch = lax.broadcasted_iota(jnp.int32, (HALF, 2 * HALF), 1) % HALF
    same_h = (rh // CHUNK) == (ch // CHUNK)
    mask_f = jnp.where(same_h, jnp.where(ch % CHUNK <= rh % CHUNK, one, zero), zero) > 0.5
    mask_b = jnp.where(same_h, jnp.where(ch % CHUNK >= rh % CHUNK, one, zero), zero) > 0.5
    rr = lax.broadcasted_iota(jnp.int32, (2 * CH_PER, SUPER), 0)
    rc = lax.broadcasted_iota(jnp.int32, (2 * CH_PER, SUPER), 1)
    in_chunk = jnp.where((rc // CHUNK) == (rr % CH_PER), one, zero)
    first = jnp.where((rc % CHUNK) < half, 1, 0)
    is_tot = jnp.where(rr >= CH_PER, 1, 0)
    rs_f = (in_chunk * jnp.where(first != is_tot, one, zero)).astype(BF16)
    rs_b = (in_chunk * jnp.where(first == is_tot, one, zero)).astype(BF16)
    lo_h = _lane_lo((HALF, LANES))
    zero_blk = jnp.zeros((HALF, GLA_DV), BF16)

    def phase1(s, carry):
        r0 = pl.multiple_of(s * SUPER, SUPER)
        rows = pl.ds(r0, SUPER)
        q = gq_ref[rows, :].astype(F32)
        k = gk_ref[rows, :].astype(F32)
        qes, kes, kds = [], [], []
        for d, (m1, rs) in enumerate(((m1_f, rs_f), (m1_b, rs_b))):
            parts = _split2(la_ref[rows, d * GLA_QK_WIDTH:(d + 1) * GLA_QK_WIDTH])
            x1 = _dot_split(m1, parts)
            erow = jnp.exp(_dot_split(rs, parts))
            dec = erow[0:CH_PER] * erow[CH_PER:2 * CH_PER]
            dec_ref[d, s] = jnp.concatenate([dec, dec], axis=0)
            qe = q * jnp.exp(x1)
            ke = k * jnp.exp(-x1)
            qg_parts, kd_parts = [], []
            for j in range(CH_PER):
                rj = slice(j * CHUNK, (j + 1) * CHUNK)
                qg_parts.append(qe[rj] * erow[j:j + 1])
                kd_parts.append(ke[rj] * erow[CH_PER + j:CH_PER + j + 1])
            qg_ref[d, rows, :] = jnp.concatenate(qg_parts, axis=0).astype(BF16)
            qes.append(qe.astype(BF16))
            kes.append(ke)
            kds.append(jnp.concatenate(kd_parts, axis=0).astype(BF16))
        for c in range(2):
            cl = slice(c * LANES, (c + 1) * LANES)
            vpair = gv_ref[rows, 2 * c * GLA_DV:(2 * c + 2) * GLA_DV]
            for blk in range(SUPER // HALF):
                rb = slice(blk * HALF, (blk + 1) * HALF)
                vbd = jnp.concatenate(
                    [jnp.concatenate([vpair[rb, 0:GLA_DV], zero_blk], axis=1),
                     jnp.concatenate([zero_blk, vpair[rb, GLA_DV:2 * GLA_DV]], axis=1)], axis=0)
                o2 = None
                for d, mask in enumerate((mask_f, mask_b)):
                    ke_cb = kes[d][rb, cl]
                    ke_st = jnp.concatenate([jnp.where(lo_h, ke_cb, zero),
                                             jnp.where(lo_h, zero, ke_cb)], axis=0).astype(BF16)
                    a = _dot_nt(qes[d][rb, cl], ke_st)
                    o = _dot(jnp.where(mask, a, zero).astype(BF16), vbd)
                    o2 = o if o2 is None else o2 + o
                oi_ref[pl.ds(r0 + blk * HALF, HALF), 2 * c * GLA_DV:(2 * c + 2) * GLA_DV] = o2
            for d in range(2):
                for j in range(CH_PER):
                    rj = slice(j * CHUNK, (j + 1) * CHUNK)
                    t = _dot_tn(kds[d][rj, cl], vpair[rj])
                    kv = jnp.concatenate([t[0:GLA_DK, 0:GLA_DV], t[GLA_DK:, GLA_DV:]], axis=0)
                    kv_ref[d, c, s * CH_PER + j] = kv.T
        return carry

    lax.fori_loop(0, nsuper, phase1, 0)

    st_ref[0] = sf_ref[0]
    st_ref[1] = sf_ref[1]
    st_ref[2] = sb_ref[0]
    st_ref[3] = sb_ref[1]

    def phase2(n, carry):
        for d in range(2):
            idx = n if d == 0 else nchunk - 1 - n
            dec = dec_ref[d, idx // CH_PER, pl.ds(idx % CH_PER, 1), :]
            for c in range(2):
                st = st_ref[2 * d + c]
                sb16_ref[c, idx, :, d * LANES:(d + 1) * LANES] = st.astype(BF16)
                st_ref[2 * d + c] = dec[:, c * LANES:(c + 1) * LANES] * st + kv_ref[d, c, idx]
        return carry

    lax.fori_loop(0, nchunk, phase2, 0)

    lo64 = _lane_lo((CHUNK, 2 * LANES))
    zero_b = jnp.zeros((), BF16)

    def phase3(s, carry):
        r0 = pl.multiple_of(s * SUPER, SUPER)
        rows = pl.ds(r0, SUPER)
        inter = [[None] * CH_PER for _ in range(GLA_HEADS)]
        for j in range(CH_PER):
            rj = pl.ds(r0 + j * CHUNK, CHUNK)
            for c in range(2):
                qg_c = jnp.concatenate([qg_ref[d, rj, c * LANES:(c + 1) * LANES] for d in range(2)], axis=1)
                lhs = jnp.concatenate([jnp.where(lo64, qg_c, zero_b),
                                       jnp.where(lo64, zero_b, qg_c)], axis=0)
                t = _dot_nt(lhs, sb16_ref[c, s * CH_PER + j])
                for hh in range(2):
                    inter[2 * c + hh][j] = t[hh * CHUNK:(hh + 1) * CHUNK]
        for hd in range(GLA_HEADS):
            cl = slice(hd * GLA_DV, (hd + 1) * GLA_DV)
            o = oi_ref[rows, cl] + jnp.concatenate(inter[hd], axis=0)
            y = o * lax.rsqrt(jnp.mean(o * o, axis=-1, keepdims=True) + EPS) * gn_ref[:, cl]
            g = gg_ref[rows, cl].astype(F32)
            o_ref[rows, cl] = (y * (g * jax.nn.sigmoid(g))).astype(BF16)
        return carry

    lax.fori_loop(0, nsuper, phase3, 0)


def _gla(gq, gk, gv, gg, lr, wd, bdec, gn, s_f, s_b):
    B, L, _ = gq.shape
    nchunk = L // CHUNK
    full = lambda shape: pl.BlockSpec(shape, lambda b: (0,) * len(shape))
    tok = lambda w: pl.BlockSpec((None, L, w), lambda b: (b, 0, 0))
    st_spec = pl.BlockSpec((None, 2, LANES, GLA_DV), lambda b: (b, 0, 0, 0))
    return pl.pallas_call(
        _gla_kernel,
        grid=(B,),
        in_specs=[tok(GLA_QK_WIDTH), tok(GLA_QK_WIDTH), tok(GLA_WIDTH), tok(GLA_WIDTH), tok(LANES),
                  full(wd.shape), full(bdec.shape), full(gn.shape), st_spec, st_spec],
        out_specs=tok(GLA_WIDTH),
        out_shape=jax.ShapeDtypeStruct((B, L, GLA_WIDTH), BF16),
        scratch_shapes=[pltpu.VMEM((L, 2 * GLA_QK_WIDTH), F32),
                        pltpu.VMEM((L, GLA_WIDTH), F32),
                        pltpu.VMEM((2, L, GLA_QK_WIDTH), BF16),
                        pltpu.VMEM((2, 2, nchunk, GLA_DV, LANES), F32),
                        pltpu.VMEM((2, nchunk, GLA_DV, 2 * LANES), BF16),
                        pltpu.VMEM((2, L // SUPER, 2 * CH_PER, GLA_QK_WIDTH), F32),
                        pltpu.VMEM((4, GLA_DV, LANES), F32)],
        compiler_params=_cparams(("arbitrary",)),
        name="gla_bidirectional",
    )(gq, gk, gv, gg, lr, wd, bdec, gn, s_f, s_b)


def _outproj_kernel(attn_ref, gla_ref, x_ref, mod_ref, w_ref, g2_ref, wr_ref,
                    x1_ref, h2_ref, afft_ref):
    y = _dot(attn_ref[...], w_ref[0:ATTN_WIDTH, :]) + _dot(gla_ref[...], w_ref[ATTN_WIDTH:, :])
    x1 = x_ref[...] + mod_ref[2:3, :] * y
    x1_ref[...] = x1
    h2 = _rms_mod(x1, g2_ref[...], mod_ref[3:4, :], mod_ref[4:5, :]).astype(BF16)
    half = D_MODEL // 2
    hi = pltpu.bitcast(h2[:, 0:half].astype(F32), jnp.uint32)
    lo = pltpu.bitcast(h2[:, half:].astype(F32), jnp.uint32)
    h2_ref[:, 0:half] = pltpu.bitcast(hi | (lo >> 16), F32)
    logits = _dot_nt(wr_ref[...], h2)
    e = jnp.exp(logits - jnp.max(logits, axis=0, keepdims=True))
    afft = e / jnp.sum(e, axis=0, keepdims=True)
    afft_ref[...] = afft
    pad = jnp.zeros((LANES - N_EXPERTS, afft.shape[1]), F32)
    h2_ref[:, half:half + LANES] = jnp.concatenate([afft, pad], axis=0).T


def _output_projection(attn, gla, x, mod3, w_out, g2, w_router, tm):
    B, L, _ = x.shape
    full = lambda shape: pl.BlockSpec(shape, lambda b, i: (0,) * len(shape))
    tok = lambda w: pl.BlockSpec((None, tm, w), lambda b, i: (b, i, 0))
    return pl.pallas_call(
        _outproj_kernel,
        grid=(B, L // tm),
        in_specs=[tok(ATTN_WIDTH), tok(GLA_WIDTH), tok(D_MODEL),
                  pl.BlockSpec((None, 6, D_MODEL), lambda b, i: (b, 0, 0)),
                  full(w_out.shape), full(g2.shape), full(w_router.shape)],
        out_specs=[tok(D_MODEL), tok(ROW_WORDS),
                   pl.BlockSpec((None, N_EXPERTS, tm), lambda b, i: (b, 0, i))],
        out_shape=[jax.ShapeDtypeStruct((B, L, D_MODEL), F32),
                   jax.ShapeDtypeStruct((B, L, ROW_WORDS), F32),
                   jax.ShapeDtypeStruct((B, N_EXPERTS, L), F32)],
        compiler_params=_cparams(("arbitrary", "arbitrary")),
        name="output_projection_router",
    )(attn, gla, x, mod3, w_out, g2, w_router)


def _topk_kernel(afft_ref, post_ref, pos_ref, *, cap):
    nbatch, E, L = afft_ref.shape
    aff = afft_ref[...].reshape(nbatch * E, L)
    E = nbatch * E

    def search(i, thr):
        cand = thr | jnp.left_shift(jnp.int32(1), 30 - i)
        cnt = jnp.sum(jnp.where(aff >= pltpu.bitcast(cand, F32), 1.0, 0.0), axis=-1, keepdims=True)
        return jnp.where(cnt >= cap, cand, thr)

    thr_bits = lax.fori_loop(0, 31, search, jnp.zeros((E, 1), jnp.int32))
    thr = pltpu.bitcast(thr_bits, F32)
    above = aff > thr
    tie = aff == thr
    need = cap - jnp.sum(jnp.where(above, 1.0, 0.0), axis=-1, keepdims=True)

    upper = (lax.broadcasted_iota(jnp.int32, (LANES, LANES), 0)
             <= lax.broadcasted_iota(jnp.int32, (LANES, LANES), 1)).astype(BF16)

    def prefix(mask):
        parts = []
        run = jnp.zeros((E, 1), F32)
        for j in range(L // LANES):
            blk = jnp.where(mask[:, j * LANES:(j + 1) * LANES], 1.0, 0.0).astype(BF16)
            loc = _dot(blk, upper) + run
            parts.append(loc)
            run = loc[:, LANES - 1:LANES]
        return jnp.concatenate(parts, axis=1)

    tie_rank = prefix(tie)
    sel = above | (tie & (tie_rank <= need))
    slot = prefix(sel).astype(jnp.int32) - 1
    post = jnp.where(sel, slot, -1)
    ne = E // nbatch
    pad_i = jnp.full((LANES - ne, L), -1, jnp.int32)
    for bb in range(nbatch):
        rows = slice(bb * ne, (bb + 1) * ne)
        post_ref[bb] = post[rows]
        pos_ref[bb] = jnp.concatenate([post[rows], pad_i], axis=0).T


def _expert_choice(afft, cap, nbatch):
    B, E, L = afft.shape
    return pl.pallas_call(
        functools.partial(_topk_kernel, cap=cap),
        grid=(B // nbatch,),
        in_specs=[pl.BlockSpec((nbatch, E, L), lambda b: (b, 0, 0))],
        out_specs=[pl.BlockSpec((nbatch, E, L), lambda b: (b, 0, 0)),
                   pl.BlockSpec((nbatch, L, LANES), lambda b: (b, 0, 0))],
        out_shape=[jax.ShapeDtypeStruct((B, E, L), jnp.int32),
                   jax.ShapeDtypeStruct((B, L, LANES), jnp.int32)],
        compiler_params=_cparams(("arbitrary",)),
        name="expert_choice_topk",
    )(afft)


SC_CORES = 2
SC_SUBCORES = 16
SC_LANES = 16
SC_WINDOW = 128


def _dispatch(rows, slots, cap, first_expert, n_experts):
    L = slots.shape[1]
    W = rows.shape[1]
    n_pair = slots.shape[0] // N_EXPERTS * n_experts
    per_worker = n_pair // (SC_CORES * SC_SUBCORES)
    n_win = cap // SC_WINDOW
    mesh = plsc.VectorSubcoreMesh(core_axis_name="c", subcore_axis_name="s",
                                  num_cores=SC_CORES, num_subcores=SC_SUBCORES)

    def body(rows_hbm, slots_hbm, out_hbm, slot_v, *scratch):
        idx_v, buf_v, sem = scratch[:n_win], scratch[n_win], scratch[n_win + 1]
        worker = lax.axis_index("s") * SC_CORES + lax.axis_index("c")

        @pl.loop(0, per_worker)
        def _(p):
            pair = worker * per_worker + p
            batch = pair // n_experts
            first_tok = batch * L
            pltpu.sync_copy(slots_hbm.at[batch * N_EXPERTS + first_expert + pair % n_experts], slot_v)

            @pl.loop(0, L // SC_LANES)
            def _(i):
                v = slot_v[pl.ds(i * SC_LANES, SC_LANES)]
                tok = lax.iota(jnp.int32, SC_LANES) + (i * SC_LANES + first_tok)
                for w in range(n_win):
                    in_win = (v >= w * SC_WINDOW) & (v < (w + 1) * SC_WINDOW)
                    plsc.store_scatter(idx_v[w], [v - w * SC_WINDOW], tok, mask=in_win)

            for w in range(n_win):
                pltpu.async_copy(rows_hbm.at[idx_v[w]], buf_v, sem).wait()
                pltpu.sync_copy(buf_v, out_hbm.at[pl.ds(pair * cap + w * SC_WINDOW, SC_WINDOW)])

    return pl.kernel(
        body,
        out_type=jax.ShapeDtypeStruct((n_pair * cap, W), rows.dtype),
        mesh=mesh,
        scratch_types=[pltpu.VMEM((L,), jnp.int32)]
        + [pltpu.VMEM((SC_WINDOW,), jnp.int32) for _ in range(n_win)]
        + [pltpu.VMEM((SC_WINDOW, W), rows.dtype), pltpu.SemaphoreType.DMA],
        compiler_params=pltpu.CompilerParams(needs_layout_passes=False),
        name="moe_dispatch_gather",
    )(rows, slots)


def _ffn_kernel(xs_ref, mod_ref, wg_ref, wu_ref, wd_ref, y_ref, wgb_ref, wub_ref, wdb_ref, *,
                first_expert):
    nbatch, cap, _ = xs_ref.shape
    d = wg_ref.shape[0]
    dw = d // 2

    @pl.when(pl.program_id(1) == 0)
    def _():
        wgb_ref[...] = wg_ref[...].astype(BF16)
        wub_ref[...] = wu_ref[...].astype(BF16)
        wdb_ref[...] = wd_ref[...].astype(BF16)

    words = pltpu.bitcast(xs_ref[:, :, 0:dw].reshape(nbatch * cap, dw), jnp.uint32)
    xs = jnp.concatenate([pltpu.bitcast(words & jnp.uint32(0xFFFF0000), F32).astype(BF16),
                          pltpu.bitcast(words << 16, F32).astype(BF16)], axis=1)
    f = wg_ref.shape[1]
    half = f // 2
    acc = None
    for j in range(2):
        cols = slice(j * half, (j + 1) * half)
        g = _dot(xs, wgb_ref[:, cols])
        u = _dot(xs, wub_ref[:, cols])
        hid = (g * jax.nn.sigmoid(g) * u).astype(BF16)
        part = _dot(hid, wdb_ref[cols, :])
        acc = part if acc is None else acc + part
    aff = xs_ref[:, :, dw:dw + LANES].reshape(nbatch * cap, LANES)
    lane = lax.broadcasted_iota(jnp.int32, aff.shape, 1)
    gate = jnp.sum(jnp.where(lane == first_expert + pl.program_id(0), aff, 0.0), axis=-1, keepdims=True)
    y = (acc * gate).reshape(nbatch, cap, d)
    for i in range(nbatch):
        yi = (y[i] * mod_ref[i, 5:6, :]).astype(BF16)
        if y_ref.dtype == BF16:
            y_ref[i] = yi
        else:
            hi = pltpu.bitcast(yi[:, 0:dw].astype(F32), jnp.uint32)
            lo = pltpu.bitcast(yi[:, dw:].astype(F32), jnp.uint32)
            y_ref[i] = pltpu.bitcast(hi | (lo >> 16), jnp.int32)


def _expert_ffn(xs, mod3, w_gate, w_up, w_down, first_expert, nbatch, packed):
    B, E, cap, row_words = xs.shape
    d, f = w_gate.shape[1:]
    out_width, out_dtype = (d // 2, jnp.int32) if packed else (d, BF16)
    tok = pl.BlockSpec((nbatch, None, cap, out_width), lambda e, b: (b, e, 0, 0))
    return pl.pallas_call(
        functools.partial(_ffn_kernel, first_expert=first_expert),
        grid=(E, B // nbatch),
        in_specs=[pl.BlockSpec((nbatch, None, cap, row_words), lambda e, b: (b, e, 0, 0)),
                  pl.BlockSpec((nbatch, 6, d), lambda e, b: (b, 0, 0)),
                  pl.BlockSpec((None, d, f), lambda e, b: (e + first_expert, 0, 0)),
                  pl.BlockSpec((None, d, f), lambda e, b: (e + first_expert, 0, 0)),
                  pl.BlockSpec((None, f, d), lambda e, b: (e + first_expert, 0, 0))],
        out_specs=tok,
        out_shape=jax.ShapeDtypeStruct((B, E, cap, out_width), out_dtype),
        scratch_shapes=[pltpu.VMEM((d, f), BF16), pltpu.VMEM((d, f), BF16), pltpu.VMEM((f, d), BF16)],
        compiler_params=_cparams(("arbitrary", "arbitrary")),
        name="expert_swiglu",
    )(xs, mod3, w_gate, w_up, w_down)


SC_ADD_ROWS = 16


def _combine_add(acc, y, slots, cap, first_expert, n_experts, after=()):
    L = slots.shape[1]
    half = y.shape[1]
    D = 2 * half
    assert slots.shape[0] // N_EXPERTS == SC_CORES * SC_SUBCORES
    n_win = cap // SC_ADD_ROWS
    mesh = plsc.VectorSubcoreMesh(core_axis_name="c", subcore_axis_name="s",
                                  num_cores=SC_CORES, num_subcores=SC_SUBCORES)

    def body(acc_hbm, y_hbm, slots_hbm, *rest):
        slot_v, *scratch = rest[len(after):]
        idx_v = scratch[:n_win]
        y_v = scratch[n_win:n_win + 2]
        o_v = scratch[n_win + 2:n_win + 4]
        sem_y, sem_g, sem_s = (scratch[n_win + 4 + 2 * k:n_win + 6 + 2 * k] for k in range(3))
        batch = lax.axis_index("s") * SC_CORES + lax.axis_index("c")

        @pl.loop(0, n_experts)
        def _(el):
            pltpu.sync_copy(slots_hbm.at[batch * N_EXPERTS + first_expert + el], slot_v)

            @pl.loop(0, L // SC_LANES)
            def _(i):
                v = slot_v[pl.ds(i * SC_LANES, SC_LANES)]
                tok = lax.iota(jnp.int32, SC_LANES) + (i * SC_LANES + batch * L)
                for w in range(n_win):
                    in_win = (v >= w * SC_ADD_ROWS) & (v < (w + 1) * SC_ADD_ROWS)
                    plsc.store_scatter(idx_v[w], [v - w * SC_ADD_ROWS], tok, mask=in_win)

            row0 = (batch * n_experts + el) * cap

            def fetch(w):
                b = w % 2
                return (pltpu.async_copy(y_hbm.at[pl.ds(row0 + w * SC_ADD_ROWS, SC_ADD_ROWS)], y_v[b], sem_y[b]),
                        pltpu.async_copy(acc_hbm.at[idx_v[w]], o_v[b], sem_g[b]))

            loads = fetch(0)
            stores = [None, None]
            for w in range(n_win):
                b = w % 2
                nxt = None
                if w + 1 < n_win:
                    if stores[1 - b] is not None:
                        stores[1 - b].wait()
                        stores[1 - b] = None
                    nxt = fetch(w + 1)
                loads[0].wait()
                loads[1].wait()

                @pl.loop(0, SC_ADD_ROWS)
                def _(r):
                    for c in range(half // SC_LANES):
                        words = y_v[b][r, pl.ds(c * SC_LANES, SC_LANES)]
                        hi = lax.bitcast_convert_type(words & jnp.int32(-65536), F32)
                        lo = lax.bitcast_convert_type(words << 16, F32)
                        plsc.addupdate(o_v[b].at[r, pl.ds(c * SC_LANES, SC_LANES)], hi)
                        plsc.addupdate(o_v[b].at[r, pl.ds(half + c * SC_LANES, SC_LANES)], lo)

                stores[b] = pltpu.async_copy(o_v[b], acc_hbm.at[idx_v[w]], sem_s[b])
                loads = nxt
            for st in stores:
                if st is not None:
                    st.wait()

    pl.kernel(
        body,
        out_type=(),
        mesh=mesh,
        scratch_types=[pltpu.VMEM((L,), jnp.int32)]
        + [pltpu.VMEM((SC_ADD_ROWS,), jnp.int32) for _ in range(n_win)]
        + [pltpu.VMEM((SC_ADD_ROWS, half), jnp.int32) for _ in range(2)]
        + [pltpu.VMEM((SC_ADD_ROWS, D), F32) for _ in range(2)]
        + [pltpu.SemaphoreType.DMA for _ in range(6)],
        compiler_params=pltpu.CompilerParams(needs_layout_passes=False),
        name="moe_combine_row_add",
    )(acc, y, slots, *after)


def _combine_kernel(pos_ref, y_ref, x1_ref, o_ref, *, cap, first_expert):
    tt = pos_ref.shape[0]
    n_experts = y_ref.shape[0]
    slot = lax.broadcasted_iota(jnp.int32, (tt, cap), 1)
    onehot = jnp.concatenate(
        [jnp.where(pos_ref[:, e:e + 1] == slot, 1.0, 0.0).astype(BF16)
         for e in range(first_expert, first_expert + n_experts)], axis=1)
    o_ref[...] = x1_ref[...] + _dot(onehot, y_ref[...].reshape(n_experts * cap, D_MODEL))


def _combine(pos, y, x1, cap, tt, first_expert):
    B, L, _ = x1.shape
    n_experts = y.shape[1]
    tok = lambda w: pl.BlockSpec((None, tt, w), lambda b, i: (b, i, 0))
    return pl.pallas_call(
        functools.partial(_combine_kernel, cap=cap, first_expert=first_expert),
        grid=(B, L // tt),
        in_specs=[tok(LANES),
                  pl.BlockSpec((None, n_experts, cap, D_MODEL), lambda b, i: (b, 0, 0, 0)),
                  tok(D_MODEL)],
        out_specs=tok(D_MODEL),
        out_shape=jax.ShapeDtypeStruct((B, L, D_MODEL), F32),
        compiler_params=_cparams(("arbitrary", "arbitrary")),
        name="moe_combine",
    )(pos, y, x1)


def _rope_tables(L):
    inv = ROPE_BASE ** (-jnp.arange(ROPE_FREQS, dtype=F32) / ROPE_FREQS)
    pos = jnp.arange(L)
    row = (pos // GRID_W).astype(F32)[:, None] * inv
    col = (pos % GRID_W).astype(F32)[:, None] * inv
    cos = jnp.concatenate([jnp.cos(row), jnp.cos(row), jnp.cos(col), jnp.cos(col)], axis=1)
    sin = jnp.concatenate([-jnp.sin(row), jnp.sin(row), -jnp.sin(col), jnp.sin(col)], axis=1)
    return jnp.tile(cos, (1, 2)), jnp.tile(sin, (1, 2))


def _head_mean_matrix(n):
    idx = np.arange(n) // HEAD_DIM
    return jnp.asarray((idx[:, None] == idx[None, :]).astype(np.float32) / HEAD_DIM, dtype=BF16)


def kernel(x, c, ctx, c_ctx, w_mod, b_mod, norm1_g, w_in, q_norm_g, k_norm_g, attn_sink,
           w_decay_fwd, b_decay_fwd, w_decay_bwd, b_decay_bwd, gla_norm_g, w_out, norm2_g,
           w_router, w_e_gate, w_e_up, w_e_down):
    B, L, D = x.shape
    cap = CAPACITY_FACTOR * L // N_EXPERTS
    layer = 0

    rows = ((B + 1 + 7) // 8) * 8
    cc = jnp.concatenate([c, c_ctx[None, :], jnp.zeros((rows - B - 1, D), F32)], axis=0)
    mod_all = _modulation(cc, w_mod[layer], b_mod[layer])
    mod3 = mod_all[:B].reshape(B, 6, D)
    modc = mod_all[B].reshape(6, D)

    w = w_in[layer]
    o = np.cumsum([0, ATTN_WIDTH, KV_WIDTH, KV_WIDTH, GLA_QK_WIDTH, GLA_QK_WIDTH,
                   GLA_WIDTH, GLA_WIDTH, GATE_RANK, GATE_RANK])
    w_lr = jnp.concatenate([w[:, o[7]:o[9]]] * 3 + [jnp.zeros((D, LANES - 6 * GATE_RANK), F32)], axis=1)
    head_order = np.arange(N_Q_HEADS).reshape(N_KV_HEADS, -1).T.reshape(-1)
    attn_perm = (head_order[:, None] * HEAD_DIM + np.arange(HEAD_DIM)[None, :]).reshape(-1)
    w_in_r = jnp.concatenate([w[:, attn_perm], w[:, o[1]:o[7]], w_lr], axis=1).astype(BF16)
    w_out_r = jnp.concatenate([w_out[layer][attn_perm], w_out[layer][ATTN_WIDTH:]], axis=0).astype(BF16)
    w_ctx = jnp.concatenate([w[:, o[1]:o[3]], w[:, o[4]:o[6]], w_lr], axis=1).astype(BF16)
    wd2 = jnp.zeros((2 * GATE_RANK, 2 * GLA_QK_WIDTH), F32)
    wd2 = wd2.at[0:GATE_RANK, 0:GLA_QK_WIDTH].set(w_decay_fwd[layer])
    wd2 = wd2.at[GATE_RANK:, GLA_QK_WIDTH:].set(w_decay_bwd[layer])
    wd_hi = wd2.astype(BF16)
    wd_lo = (wd2 - wd_hi.astype(F32)).astype(BF16)
    wd = jnp.concatenate([wd_hi, wd_hi, wd_lo,
                          jnp.zeros((LANES - 6 * GATE_RANK, 2 * GLA_QK_WIDTH), BF16)], axis=0)
    bdec = jnp.concatenate([b_decay_fwd[layer], b_decay_bwd[layer]])[None, :]
    g1 = norm1_g[layer][None, :]
    g2 = norm2_g[layer][None, :]
    qg = jnp.tile(q_norm_g[layer], N_Q_HEADS)[None, :]
    kg2 = jnp.tile(k_norm_g[layer], N_KV_HEADS)[None, :]
    gn = jnp.tile(gla_norm_g[layer], GLA_HEADS)[None, :]
    bd512 = _head_mean_matrix(ATTN_WIDTH)
    bd128 = _head_mean_matrix(KV_WIDTH)
    cos_t, sin_t = _rope_tables(L)
    w_router_t = w_router[layer].T.astype(BF16)

    kc, vc, s_f, s_b = _context_side(ctx, modc, g1, w_ctx, kg2, bd128, wd, bdec)
    q, k, v, gq, gk, gv, gg, lr = _input_projection(
        x, mod3, g1, w_in_r, qg, kg2, bd512, cos_t, sin_t, tm=1024)
    attn = _window_attention(attn_sink[layer], q, k, v, kc, vc, tq=512)
    gla = _gla(gq, gk, gv, gg, lr, wd, bdec, gn, s_f, s_b)
    x1, h2, afft = _output_projection(attn, gla, x, mod3, w_out_r, g2, w_router_t, tm=1024)
    post, pos = _expert_choice(afft, cap, nbatch=4)
    n_groups = 4
    per_group = N_EXPERTS // n_groups
    rows = h2.reshape(B * L, ROW_WORDS)
    slots = post.reshape(B * N_EXPERTS, L)
    stream = jax.new_ref(x1.reshape(B * L, D))
    gathered = [_dispatch(rows, slots, cap, g * per_group, per_group) for g in range(n_groups)]
    for g in range(n_groups):
        last = g == n_groups - 1
        xs = gathered[g].reshape(B, per_group, cap, ROW_WORDS)
        y = _expert_ffn(xs, mod3, w_e_gate[layer], w_e_up[layer], w_e_down[layer], g * per_group,
                        nbatch=4, packed=not last)
        if not last:
            _combine_add(stream, y.reshape(B * per_group * cap, D // 2), slots, cap, g * per_group, per_group,
                         after=gathered[g + 1:])
    x1 = jax.freeze(stream).reshape(B, L, D)
    return _combine(pos, y, x1, cap, tt=1024, first_expert=(n_groups - 1) * per_group)
```

```python
import functools

import jax
import jax.numpy as jnp
import numpy as np
from jax import lax
from jax.experimental import pallas as pl
from jax.experimental.pallas import tpu as pltpu
from jax.experimental.pallas import tpu_sc as plsc

D_MODEL = 1024
GRID_W = 64
HEAD_DIM = 64
N_Q_HEADS = 8
N_KV_HEADS = 2
BLOCK = 128
ROPE_FREQS = 16
ROPE_BASE = 10000.0
GLA_HEADS = 4
GLA_DV = 128
GLA_DK = 64
GATE_RANK = 16
GATE_NORMALIZER = 16.0
CHUNK = 64
N_EXPERTS = 16
CAPACITY_FACTOR = 2
ATTN_WIDTH = 512
KV_WIDTH = 128
GLA_QK_WIDTH = 256
GLA_WIDTH = 512
EPS = 1e-6
NEG_INF = -1e30
LOG2E = 1.4426950408889634

LANES = 128
ROW_WORDS = D_MODEL // 2 + LANES
VMEM_LIMIT = 56 * 1024 * 1024

F32 = jnp.float32
BF16 = jnp.bfloat16
HI = lax.Precision.HIGHEST


def _cparams(sem):
    return pltpu.CompilerParams(dimension_semantics=sem, vmem_limit_bytes=VMEM_LIMIT)


def _dot(a, b):
    return jnp.dot(a, b, preferred_element_type=F32)


def _dot_hi(a, b):
    return jnp.dot(a, b, preferred_element_type=F32, precision=HI)


def _dot_nt(a, b):
    return lax.dot_general(a, b, (((1,), (1,)), ((), ())), preferred_element_type=F32)


def _dot_tn(a, b, precision=None):
    return lax.dot_general(a, b, (((0,), (0,)), ((), ())), preferred_element_type=F32,
                           precision=precision)


def _split2(t):
    hi = t.astype(BF16)
    lo = (t - hi.astype(F32)).astype(BF16)
    return hi, lo


def _rms_mod(t, g, shift, scale):
    y = t * lax.rsqrt(jnp.mean(t * t, axis=-1, keepdims=True) + EPS)
    return (y * g) * (1.0 + scale) + shift


def _log_decay(lr, wd3, bias):
    hi = lr.astype(BF16)
    lo = (lr - hi.astype(F32)).astype(BF16)
    lane = lax.broadcasted_iota(jnp.int32, lr.shape, 1)
    second = (lane >= 2 * GATE_RANK) & (lane < 4 * GATE_RANK)
    z = _dot(jnp.where(second, lo, hi), wd3) + bias
    return (jnp.minimum(z, 0.0) - jnp.log(1.0 + jnp.exp(-jnp.abs(z)))) * (1.0 / GATE_NORMALIZER)


def _lane_lo(shape):
    return (lax.broadcasted_iota(jnp.int32, shape, len(shape) - 1) % LANES) < HEAD_DIM


def _mod_kernel(c_ref, w_ref, b_ref, o_ref):
    c = c_ref[...]
    s = c * jax.nn.sigmoid(c)
    o_ref[...] = _dot_hi(s, w_ref[...]) + b_ref[...]


def _modulation(cc, w_mod, b_mod):
    m = cc.shape[0]
    n = w_mod.shape[1]
    tn = 1024
    return pl.pallas_call(
        _mod_kernel,
        grid=(n // tn,),
        in_specs=[pl.BlockSpec((m, D_MODEL), lambda j: (0, 0)),
                  pl.BlockSpec((D_MODEL, tn), lambda j: (0, j)),
                  pl.BlockSpec((1, tn), lambda j: (0, j))],
        out_specs=pl.BlockSpec((m, tn), lambda j: (0, j)),
        out_shape=jax.ShapeDtypeStruct((m, n), F32),
        compiler_params=_cparams(("arbitrary",)),
        name="adaln_mod",
    )(cc, w_mod, b_mod.reshape(1, n))


def _ctx_kernel(ctx_ref, mod_ref, g1_ref, w_ref, kg_ref, bd_ref, wd_ref, bdec_ref,
                kc_ref, vc_ref, sf_ref, sb_ref):
    n = ctx_ref.shape[0]
    h = _rms_mod(ctx_ref[...], g1_ref[...], mod_ref[0:1, :], mod_ref[1:2, :]).astype(BF16)
    pc = _dot(h, w_ref[...])
    ak = pc[:, 0:128]
    av = pc[:, 128:256]
    gk = pc[:, 256:512]
    gv = pc[:, 512:1024].astype(BF16)
    lr = pc[:, 1024:1152]
    sq_hi, sq_lo = _split2(ak * ak)
    ms = _dot(sq_hi, bd_ref[...]) + _dot(sq_lo, bd_ref[...])
    kn = ak * lax.rsqrt(ms + EPS) * kg_ref[...]
    kc_ref[...] = kn.astype(BF16)
    vc_ref[...] = av.astype(BF16)
    la = _log_decay(lr, wd_ref[...], bdec_ref[...])
    r = lax.broadcasted_iota(jnp.int32, (n, n), 0)
    cidx = lax.broadcasted_iota(jnp.int32, (n, n), 1)
    after = (cidx > r).astype(F32)
    before = (cidx < r).astype(F32)
    w_f = jnp.exp(_dot_hi(after, la[:, 0:256]))
    w_b = jnp.exp(_dot_hi(before, la[:, 256:512]))
    lo = _lane_lo((n, LANES))
    for w, out in ((w_f, sf_ref), (w_b, sb_ref)):
        kw = gk * w
        for c in range(2):
            kwc = kw[:, c * LANES:(c + 1) * LANES]
            k_lo = jnp.where(lo, kwc, 0.0).astype(BF16)
            k_hi = jnp.where(lo, 0.0, kwc).astype(BF16)
            v0 = gv[:, (2 * c) * GLA_DV:(2 * c + 1) * GLA_DV]
            v1 = gv[:, (2 * c + 1) * GLA_DV:(2 * c + 2) * GLA_DV]
            out[c] = _dot_tn(v0, k_lo) + _dot_tn(v1, k_hi)


def _context_side(ctx, modc, g1, w_ctx, kg2, bd128, wd, bdec):
    B, n, _ = ctx.shape
    full = lambda shape: pl.BlockSpec(shape, lambda b: (0,) * len(shape))
    kv_spec = pl.BlockSpec((None, n, KV_WIDTH), lambda b: (b, 0, 0))
    st_spec = pl.BlockSpec((None, 2, LANES, GLA_DV), lambda b: (b, 0, 0, 0))
    kv_shape = jax.ShapeDtypeStruct((B, n, KV_WIDTH), BF16)
    st_shape = jax.ShapeDtypeStruct((B, 2, LANES, GLA_DV), F32)
    return pl.pallas_call(
        _ctx_kernel,
        grid=(B,),
        in_specs=[pl.BlockSpec((None, n, D_MODEL), lambda b: (b, 0, 0)),
                  full(modc.shape), full(g1.shape), full(w_ctx.shape), full(kg2.shape),
                  full(bd128.shape), full(wd.shape), full(bdec.shape)],
        out_specs=[kv_spec, kv_spec, st_spec, st_spec],
        out_shape=[kv_shape, kv_shape, st_shape, st_shape],
        compiler_params=_cparams(("arbitrary",)),
        name="context_side",
    )(ctx, modc, g1, w_ctx, kg2, bd128, wd, bdec)


def _swap16(t):
    n = t.shape[1]
    first = (lax.broadcasted_iota(jnp.int32, t.shape, 1) % 32) < ROPE_FREQS
    return jnp.where(first, pltpu.roll(t, n - ROPE_FREQS, 1), pltpu.roll(t, ROPE_FREQS, 1))


def _inproj_kernel(x_ref, mod_ref, g1_ref, w_ref, qg_ref, kg_ref, bd_ref, cos_ref, sin_ref,
                   q_ref, k_ref, v_ref, gq_ref, gk_ref, gv_ref, gg_ref, lr_ref):
    h = _rms_mod(x_ref[...], g1_ref[...], mod_ref[0:1, :], mod_ref[1:2, :]).astype(BF16)
    cos = cos_ref[...]
    sin = sin_ref[...]

    def head_norm_rope(t, g, bd, reps):
        ms = _dot((t * t).astype(BF16), bd)
        tn = t * lax.rsqrt(ms + EPS) * g
        c = jnp.concatenate([cos] * reps, axis=1) if reps > 1 else cos
        s = jnp.concatenate([sin] * reps, axis=1) if reps > 1 else sin
        return tn * c + _swap16(tn) * s

    aq = _dot(h, w_ref[:, 0:512])
    q = head_norm_rope(aq, qg_ref[...], bd_ref[...], 4) * (HEAD_DIM ** -0.5 * LOG2E)
    q_ref[...] = q.astype(BF16)
    akv = _dot(h, w_ref[:, 512:768])
    k = head_norm_rope(akv[:, 0:128], kg_ref[...], bd_ref[0:128, 0:128], 1)
    k_ref[...] = k.astype(BF16)
    v_ref[...] = akv[:, 128:256].astype(BF16)
    gqk = _dot(h, w_ref[:, 768:1280])
    gq_ref[...] = (gqk[:, 0:256] * (GLA_DK ** -0.5)).astype(BF16)
    gk_ref[...] = gqk[:, 256:512].astype(BF16)
    gv_ref[...] = _dot(h, w_ref[:, 1280:1792]).astype(BF16)
    gg_ref[...] = _dot(h, w_ref[:, 1792:2304]).astype(BF16)
    lr_ref[...] = _dot(h, w_ref[:, 2304:2432])


def _input_projection(x, mod3, g1, w_in_r, qg, kg2, bd512, cos_t, sin_t, tm):
    B, L, _ = x.shape
    full = lambda shape: pl.BlockSpec(shape, lambda b, i: (0,) * len(shape))
    tok = lambda w: pl.BlockSpec((None, tm, w), lambda b, i: (b, i, 0))
    widths = (ATTN_WIDTH, KV_WIDTH, KV_WIDTH, GLA_QK_WIDTH, GLA_QK_WIDTH, GLA_WIDTH, GLA_WIDTH, LANES)
    dtypes = (BF16,) * 7 + (F32,)
    return pl.pallas_call(
        _inproj_kernel,
        grid=(B, L // tm),
        in_specs=[tok(D_MODEL),
                  pl.BlockSpec((None, 6, D_MODEL), lambda b, i: (b, 0, 0)),
                  full(g1.shape), full(w_in_r.shape), full(qg.shape), full(kg2.shape),
                  full(bd512.shape),
                  pl.BlockSpec((tm, LANES), lambda b, i: (i, 0)),
                  pl.BlockSpec((tm, LANES), lambda b, i: (i, 0))],
        out_specs=[tok(w) for w in widths],
        out_shape=[jax.ShapeDtypeStruct((B, L, w), dt) for w, dt in zip(widths, dtypes)],
        compiler_params=_cparams(("arbitrary", "arbitrary")),
        name="input_projection",
    )(x, mod3, g1, w_in_r, qg, kg2, bd512, cos_t, sin_t)


def _attn_kernel(sink_ref, q_ref, kp_ref, ko_ref, kn_ref, vp_ref, vo_ref, vn_ref,
                 kc_ref, vc_ref, o_ref):
    i = pl.program_id(1)
    ni = pl.num_programs(1)
    nsub = q_ref.shape[0] // BLOCK
    ncol = ATTN_WIDTH // LANES
    win = 3 * BLOCK
    ucol = 4
    half_rows = ucol * BLOCK
    k_win = jnp.concatenate([kp_ref[...], ko_ref[...], kn_ref[...]], axis=0)
    v_win = jnp.concatenate([vp_ref[...], vo_ref[...], vn_ref[...]], axis=0)
    k_ctx = kc_ref[...]
    lo_w = _lane_lo(v_win.shape)
    lo_c = _lane_lo(vc_ref.shape)
    lo_q = _lane_lo((BLOCK, LANES))
    zero = jnp.zeros((), BF16)
    one = jnp.ones((), BF16)
    v0_c, v0_w = jnp.where(lo_c, vc_ref[...], one), jnp.where(lo_w, v_win, one)
    v1_c, v1_w = jnp.where(lo_c, one, vc_ref[...]), jnp.where(lo_w, one, v_win)
    qi = lax.broadcasted_iota(jnp.int32, (half_rows, BLOCK), 0) % BLOCK
    kj = lax.broadcasted_iota(jnp.int32, (half_rows, BLOCK), 1)
    no_prev = jnp.where(i > 0, 0, BLOCK)
    no_next = jnp.where(i < ni - 1, 0, BLOCK)
    row_head = lax.broadcasted_iota(jnp.int32, (half_rows, 1), 0) // BLOCK
    lo_o = _lane_lo((half_rows, LANES))
    for t in range(nsub):
        rows = slice(t * BLOCK, (t + 1) * BLOCK)
        keys = slice(t * BLOCK, t * BLOCK + win)
        cols = [q_ref[rows, c * LANES:(c + 1) * LANES] for c in range(ncol)]
        prev_ok = kj >= qi + (no_prev if t == 0 else 0)
        next_ok = kj <= qi - (no_next if t == nsub - 1 else 0)
        for p in range(ncol // ucol):
            outs = []
            qs = jnp.concatenate([jnp.where(lo_q, qc, zero) for qc in cols[ucol * p:ucol * (p + 1)]]
                                 + [jnp.where(lo_q, zero, qc) for qc in cols[ucol * p:ucol * (p + 1)]], axis=0)
            s_c_all = _dot_nt(qs, k_ctx)
            s_w_all = _dot_nt(qs, k_win[keys])
            for g, (vv_c, vv_w) in enumerate(((v0_c, v0_w), (v1_c, v1_w))):
                head = g * ncol + ucol * p
                sink_g = jnp.full((half_rows, 1), sink_ref[head + ucol - 1], F32)
                for j in range(ucol - 2, -1, -1):
                    sink_g = jnp.where(row_head <= j, sink_ref[head + j], sink_g)
                sink_g = sink_g * LOG2E
                s_c = s_c_all[g * half_rows:(g + 1) * half_rows]
                s_w = s_w_all[g * half_rows:(g + 1) * half_rows]
                s_p = jnp.where(prev_ok, s_w[:, 0:BLOCK], NEG_INF)
                s_o = s_w[:, BLOCK:2 * BLOCK]
                s_n = jnp.where(next_ok, s_w[:, 2 * BLOCK:win], NEG_INF)
                m = jnp.maximum(jnp.maximum(jnp.max(s_c, axis=-1, keepdims=True),
                                            jnp.max(jnp.maximum(jnp.maximum(s_p, s_o), s_n),
                                                    axis=-1, keepdims=True)), sink_g)
                e_c = jnp.exp2(s_c - m).astype(BF16)
                e_w = jnp.concatenate([jnp.exp2(s_p - m), jnp.exp2(s_o - m), jnp.exp2(s_n - m)],
                                      axis=1).astype(BF16)
                acc = _dot(e_c, vv_c) + _dot(e_w, vv_w[keys])
                outs.append(acc / (pltpu.roll(acc, HEAD_DIM, 1) + jnp.exp2(sink_g - m)))
            o = jnp.where(lo_o, outs[0], outs[1]).astype(BF16)
            for j in range(ucol):
                c = ucol * p + j
                o_ref[rows, c * LANES:(c + 1) * LANES] = o[j * BLOCK:(j + 1) * BLOCK]


def _window_attention(sink, q, k, v, kc, vc, tq):
    B, L, _ = q.shape
    nb = L // BLOCK
    nsub = tq // BLOCK
    n_ctx = kc.shape[1]
    prev = pl.BlockSpec((None, BLOCK, KV_WIDTH), lambda b, n: (b, jnp.maximum(n * nsub - 1, 0), 0))
    own = pl.BlockSpec((None, tq, KV_WIDTH), lambda b, n: (b, n, 0))
    nxt = pl.BlockSpec((None, BLOCK, KV_WIDTH),
                       lambda b, n: (b, jnp.minimum((n + 1) * nsub, nb - 1), 0))
    cspec = pl.BlockSpec((None, n_ctx, KV_WIDTH), lambda b, n: (b, 0, 0))
    return pl.pallas_call(
        _attn_kernel,
        grid=(B, L // tq),
        in_specs=[pl.BlockSpec(memory_space=pltpu.SMEM),
                  pl.BlockSpec((None, tq, ATTN_WIDTH), lambda b, n: (b, n, 0)),
                  prev, own, nxt, prev, own, nxt, cspec, cspec],
        out_specs=pl.BlockSpec((None, tq, ATTN_WIDTH), lambda b, n: (b, n, 0)),
        out_shape=jax.ShapeDtypeStruct((B, L, ATTN_WIDTH), BF16),
        compiler_params=_cparams(("arbitrary", "arbitrary")),
        name="window_attention",
    )(sink, q, k, k, k, v, v, v, kc, vc)


SUPER = 256
CH_PER = SUPER // CHUNK
HALF = 128


def _dot_split(m, parts):
    return _dot(m, parts[0]) + _dot(m, parts[1])


def _gla_kernel(gq_ref, gk_ref, gv_ref, gg_ref, lr_ref, wd_ref, bdec_ref, gn_ref, sf_ref, sb_ref,
                o_ref, la_ref, oi_ref, qg_ref, kv_ref, sb16_ref, dec_ref, st_ref):
    L = gq_ref.shape[0]
    nsuper = L // SUPER
    nchunk = L // CHUNK
    half = CHUNK // 2
    la_ref[...] = _log_decay(lr_ref[...], wd_ref[...], bdec_ref[...])

    r = lax.broadcasted_iota(jnp.int32, (SUPER, SUPER), 0)
    cidx = lax.broadcasted_iota(jnp.int32, (SUPER, SUPER), 1)
    same = (r // CHUNK) == (cidx // CHUNK)
    pr = r % CHUNK
    pc = cidx % CHUNK
    one = jnp.float32(1.0)
    zero = jnp.float32(0.0)
    in_f = jnp.where(pc <= pr, one, zero)
    in_b = jnp.where(pc >= pr, one, zero)
    ref_f = jnp.where(pc < half, one, zero)
    ref_b = jnp.where(pc >= half, one, zero)
    m1_f = jnp.where(same, in_f - ref_f, zero).astype(BF16)
    m1_b = jnp.where(same, in_b - ref_b, zero).astype(BF16)
    rh = lax.broadcasted_iota(jnp.int32, (HALF, 2 * HALF), 0)
    ch = lax.broadcasted_iota(jnp.int32, (HALF, 2 * HALF), 1) % HALF
    same_h = (rh // CHUNK) == (ch // CHUNK)
    mask_f = jnp.where(same_h, jnp.where(ch % CHUNK <= rh % CHUNK, one, zero), zero) > 0.5
    mask_b = jnp.where(same_h, jnp.where(ch % CHUNK >= rh % CHUNK, one, zero), zero) > 0.5
    rr = lax.broadcasted_iota(jnp.int32, (2 * CH_PER, SUPER), 0)
    rc = lax.broadcasted_iota(jnp.int32, (2 * CH_PER, SUPER), 1)
    in_chunk = jnp.where((rc // CHUNK) == (rr % CH_PER), one, zero)
    first = jnp.where((rc % CHUNK) < half, 1, 0)
    is_tot = jnp.where(rr >= CH_PER, 1, 0)
    rs_f = (in_chunk * jnp.where(first != is_tot, one, zero)).astype(BF16)
    rs_b = (in_chunk * jnp.where(first == is_tot, one, zero)).astype(BF16)
    lo_h = _lane_lo((HALF, LANES))
    zero_blk = jnp.zeros((HALF, GLA_DV), BF16)

    def phase1(s, carry):
        r0 = pl.multiple_of(s * SUPER, SUPER)
        rows = pl.ds(r0, SUPER)
        q = gq_ref[rows, :].astype(F32)
        k = gk_ref[rows, :].astype(F32)
        qes, kes, kds = [], [], []
        for d, (m1, rs) in enumerate(((m1_f, rs_f), (m1_b, rs_b))):
            parts = _split2(la_ref[rows, d * GLA_QK_WIDTH:(d + 1) * GLA_QK_WIDTH])
            x1 = _dot_split(m1, parts)
            erow = jnp.exp(_dot_split(rs, parts))
            dec = erow[0:CH_PER] * erow[CH_PER:2 * CH_PER]
            dec_ref[d, s] = jnp.concatenate([dec, dec], axis=0)
            qe = q * jnp.exp(x1)
            ke = k * jnp.exp(-x1)
            qg_parts, kd_parts = [], []
            for j in range(CH_PER):
                rj = slice(j * CHUNK, (j + 1) * CHUNK)
                qg_parts.append(qe[rj] * erow[j:j + 1])
                kd_parts.append(ke[rj] * erow[CH_PER + j:CH_PER + j + 1])
            qg_ref[d, rows, :] = jnp.concatenate(qg_parts, axis=0).astype(BF16)
            qes.append(qe.astype(BF16))
            kes.append(ke)
            kds.append(jnp.concatenate(kd_parts, axis=0).astype(BF16))
        for c in range(2):
            cl = slice(c * LANES, (c + 1) * LANES)
            vpair = gv_ref[rows, 2 * c * GLA_DV:(2 * c + 2) * GLA_DV]
            for blk in range(SUPER // HALF):
                rb = slice(blk * HALF, (blk + 1) * HALF)
                vbd = jnp.concatenate(
                    [jnp.concatenate([vpair[rb, 0:GLA_DV], zero_blk], axis=1),
                     jnp.concatenate([zero_blk, vpair[rb, GLA_DV:2 * GLA_DV]], axis=1)], axis=0)
                o2 = None
                for d, mask in enumerate((mask_f, mask_b)):
                    ke_cb = kes[d][rb, cl]
                    ke_st = jnp.concatenate([jnp.where(lo_h, ke_cb, zero),
                                             jnp.where(lo_h, zero, ke_cb)], axis=0).astype(BF16)
                    a = _dot_nt(qes[d][rb, cl], ke_st)
                    o = _dot(jnp.where(mask, a, zero).astype(BF16), vbd)
                    o2 = o if o2 is None else o2 + o
                oi_ref[pl.ds(r0 + blk * HALF, HALF), 2 * c * GLA_DV:(2 * c + 2) * GLA_DV] = o2
            for d in range(2):
                for j in range(CH_PER):
                    rj = slice(j * CHUNK, (j + 1) * CHUNK)
                    t = _dot_tn(kds[d][rj, cl], vpair[rj])
                    kv = jnp.concatenate([t[0:GLA_DK, 0:GLA_DV], t[GLA_DK:, GLA_DV:]], axis=0)
                    kv_ref[d, c, s * CH_PER + j] = kv.T
        return carry

    lax.fori_loop(0, nsuper, phase1, 0)

    st_ref[0] = sf_ref[0]
    st_ref[1] = sf_ref[1]
    st_ref[2] = sb_ref[0]
    st_ref[3] = sb_ref[1]

    def phase2(n, carry):
        for d in range(2):
            idx = n if d == 0 else nchunk - 1 - n
            dec = dec_ref[d, idx // CH_PER, pl.ds(idx % CH_PER, 1), :]
            for c in range(2):
                st = st_ref[2 * d + c]
                sb16_ref[c, idx, :, d * LANES:(d + 1) * LANES] = st.astype(BF16)
                st_ref[2 * d + c] = dec[:, c * LANES:(c + 1) * LANES] * st + kv_ref[d, c, idx]
        return carry

    lax.fori_loop(0, nchunk, phase2, 0)

    lo64 = _lane_lo((CHUNK, 2 * LANES))
    zero_b = jnp.zeros((), BF16)

    def phase3(s, carry):
        r0 = pl.multiple_of(s * SUPER, SUPER)
        rows = pl.ds(r0, SUPER)
        inter = [[None] * CH_PER for _ in range(GLA_HEADS)]
        for j in range(CH_PER):
            rj = pl.ds(r0 + j * CHUNK, CHUNK)
            for c in range(2):
                qg_c = jnp.concatenate([qg_ref[d, rj, c * LANES:(c + 1) * LANES] for d in range(2)], axis=1)
                lhs = jnp.concatenate([jnp.where(lo64, qg_c, zero_b),
                                       jnp.where(lo64, zero_b, qg_c)], axis=0)
                t = _dot_nt(lhs, sb16_ref[c, s * CH_PER + j])
                for hh in range(2):
                    inter[2 * c + hh][j] = t[hh * CHUNK:(hh + 1) * CHUNK]
        for hd in range(GLA_HEADS):
            cl = slice(hd * GLA_DV, (hd + 1) * GLA_DV)
            o = oi_ref[rows, cl] + jnp.concatenate(inter[hd], axis=0)
            y = o * lax.rsqrt(jnp.mean(o * o, axis=-1, keepdims=True) + EPS) * gn_ref[:, cl]
            g = gg_ref[rows, cl].astype(F32)
            o_ref[rows, cl] = (y * (g * jax.nn.sigmoid(g))).astype(BF16)
        return carry

    lax.fori_loop(0, nsuper, phase3, 0)


def _gla(gq, gk, gv, gg, lr, wd, bdec, gn, s_f, s_b):
    B, L, _ = gq.shape
    nchunk = L // CHUNK
    full = lambda shape: pl.BlockSpec(shape, lambda b: (0,) * len(shape))
    tok = lambda w: pl.BlockSpec((None, L, w), lambda b: (b, 0, 0))
    st_spec = pl.BlockSpec((None, 2, LANES, GLA_DV), lambda b: (b, 0, 0, 0))
    return pl.pallas_call(
        _gla_kernel,
        grid=(B,),
        in_specs=[tok(GLA_QK_WIDTH), tok(GLA_QK_WIDTH), tok(GLA_WIDTH), tok(GLA_WIDTH), tok(LANES),
                  full(wd.shape), full(bdec.shape), full(gn.shape), st_spec, st_spec],
        out_specs=tok(GLA_WIDTH),
        out_shape=jax.ShapeDtypeStruct((B, L, GLA_WIDTH), BF16),
        scratch_shapes=[pltpu.VMEM((L, 2 * GLA_QK_WIDTH), F32),
                        pltpu.VMEM((L, GLA_WIDTH), F32),
                        pltpu.VMEM((2, L, GLA_QK_WIDTH), BF16),
                        pltpu.VMEM((2, 2, nchunk, GLA_DV, LANES), F32),
                        pltpu.VMEM((2, nchunk, GLA_DV, 2 * LANES), BF16),
                        pltpu.VMEM((2, L // SUPER, 2 * CH_PER, GLA_QK_WIDTH), F32),
                        pltpu.VMEM((4, GLA_DV, LANES), F32)],
        compiler_params=_cparams(("arbitrary",)),
        name="gla_bidirectional",
    )(gq, gk, gv, gg, lr, wd, bdec, gn, s_f, s_b)


def _outproj_kernel(attn_ref, gla_ref, x_ref, mod_ref, w_ref, g2_ref, wr_ref,
                    x1_ref, h2_ref, afft_ref):
    y = _dot(attn_ref[...], w_ref[0:ATTN_WIDTH, :]) + _dot(gla_ref[...], w_ref[ATTN_WIDTH:, :])
    x1 = x_ref[...] + mod_ref[2:3, :] * y
    x1_ref[...] = x1
    h2 = _rms_mod(x1, g2_ref[...], mod_ref[3:4, :], mod_ref[4:5, :]).astype(BF16)
    half = D_MODEL // 2
    hi = pltpu.bitcast(h2[:, 0:half].astype(F32), jnp.uint32)
    lo = pltpu.bitcast(h2[:, half:].astype(F32), jnp.uint32)
    h2_ref[:, 0:half] = pltpu.bitcast(hi | (lo >> 16), F32)
    logits = _dot_nt(wr_ref[...], h2)
    e = jnp.exp(logits - jnp.max(logits, axis=0, keepdims=True))
    afft = e / jnp.sum(e, axis=0, keepdims=True)
    afft_ref[...] = afft
    pad = jnp.zeros((LANES - N_EXPERTS, afft.shape[1]), F32)
    h2_ref[:, half:half + LANES] = jnp.concatenate([afft, pad], axis=0).T


def _output_projection(attn, gla, x, mod3, w_out, g2, w_router, tm):
    B, L, _ = x.shape
    full = lambda shape: pl.BlockSpec(shape, lambda b, i: (0,) * len(shape))
    tok = lambda w: pl.BlockSpec((None, tm, w), lambda b, i: (b, i, 0))
    return pl.pallas_call(
        _outproj_kernel,
        grid=(B, L // tm),
        in_specs=[tok(ATTN_WIDTH), tok(GLA_WIDTH), tok(D_MODEL),
                  pl.BlockSpec((None, 6, D_MODEL), lambda b, i: (b, 0, 0)),
                  full(w_out.shape), full(g2.shape), full(w_router.shape)],
        out_specs=[tok(D_MODEL), tok(ROW_WORDS),
                   pl.BlockSpec((None, N_EXPERTS, tm), lambda b, i: (b, 0, i))],
        out_shape=[jax.ShapeDtypeStruct((B, L, D_MODEL), F32),
                   jax.ShapeDtypeStruct((B, L, ROW_WORDS), F32),
                   jax.ShapeDtypeStruct((B, N_EXPERTS, L), F32)],
        compiler_params=_cparams(("arbitrary", "arbitrary")),
        name="output_projection_router",
    )(attn, gla, x, mod3, w_out, g2, w_router)


def _topk_kernel(afft_ref, post_ref, *, cap):
    nbatch, E, L = afft_ref.shape
    aff = afft_ref[...].reshape(nbatch * E, L)
    E = nbatch * E

    def search(i, thr):
        cand = thr | jnp.left_shift(jnp.int32(1), 30 - i)
        cnt = jnp.sum(jnp.where(aff >= pltpu.bitcast(cand, F32), 1.0, 0.0), axis=-1, keepdims=True)
        return jnp.where(cnt >= cap, cand, thr)

    thr_bits = lax.fori_loop(0, 31, search, jnp.zeros((E, 1), jnp.int32))
    thr = pltpu.bitcast(thr_bits, F32)
    above = aff > thr
    tie = aff == thr
    need = cap - jnp.sum(jnp.where(above, 1.0, 0.0), axis=-1, keepdims=True)

    upper = (lax.broadcasted_iota(jnp.int32, (LANES, LANES), 0)
             <= lax.broadcasted_iota(jnp.int32, (LANES, LANES), 1)).astype(BF16)

    def prefix(mask):
        parts = []
        run = jnp.zeros((E, 1), F32)
        for j in range(L // LANES):
            blk = jnp.where(mask[:, j * LANES:(j + 1) * LANES], 1.0, 0.0).astype(BF16)
            loc = _dot(blk, upper) + run
            parts.append(loc)
            run = loc[:, LANES - 1:LANES]
        return jnp.concatenate(parts, axis=1)

    tie_rank = prefix(tie)
    sel = above | (tie & (tie_rank <= need))
    slot = prefix(sel).astype(jnp.int32) - 1
    post_ref[...] = jnp.where(sel, slot, -1).reshape(post_ref.shape)


def _expert_choice(afft, cap, nbatch):
    B, E, L = afft.shape
    return pl.pallas_call(
        functools.partial(_topk_kernel, cap=cap),
        grid=(B // nbatch,),
        in_specs=[pl.BlockSpec((nbatch, E, L), lambda b: (b, 0, 0))],
        out_specs=pl.BlockSpec((nbatch, E, L), lambda b: (b, 0, 0)),
        out_shape=jax.ShapeDtypeStruct((B, E, L), jnp.int32),
        compiler_params=_cparams(("arbitrary",)),
        name="expert_choice_topk",
    )(afft)


SC_CORES = 2
SC_SUBCORES = 16
SC_LANES = 16
SC_WINDOW = 128


def _dispatch(rows, slots, cap, first_expert, n_experts):
    L = slots.shape[1]
    W = rows.shape[1]
    n_pair = slots.shape[0] // N_EXPERTS * n_experts
    per_worker = n_pair // (SC_CORES * SC_SUBCORES)
    n_win = cap // SC_WINDOW
    mesh = plsc.VectorSubcoreMesh(core_axis_name="c", subcore_axis_name="s",
                                  num_cores=SC_CORES, num_subcores=SC_SUBCORES)

    def body(rows_hbm, slots_hbm, out_hbm, slot_v, *scratch):
        idx_v, buf_v, sem = scratch[:n_win], scratch[n_win], scratch[n_win + 1]
        worker = lax.axis_index("s") * SC_CORES + lax.axis_index("c")

        @pl.loop(0, per_worker)
        def _(p):
            pair = worker * per_worker + p
            batch = pair // n_experts
            first_tok = batch * L
            pltpu.sync_copy(slots_hbm.at[batch * N_EXPERTS + first_expert + pair % n_experts], slot_v)

            @pl.loop(0, L // SC_LANES)
            def _(i):
                v = slot_v[pl.ds(i * SC_LANES, SC_LANES)]
                tok = lax.iota(jnp.int32, SC_LANES) + (i * SC_LANES + first_tok)
                for w in range(n_win):
                    in_win = (v >= w * SC_WINDOW) & (v < (w + 1) * SC_WINDOW)
                    plsc.store_scatter(idx_v[w], [v - w * SC_WINDOW], tok, mask=in_win)

            for w in range(n_win):
                pltpu.async_copy(rows_hbm.at[idx_v[w]], buf_v, sem).wait()
                pltpu.sync_copy(buf_v, out_hbm.at[pl.ds(pair * cap + w * SC_WINDOW, SC_WINDOW)])

    return pl.kernel(
        body,
        out_type=jax.ShapeDtypeStruct((n_pair * cap, W), rows.dtype),
        mesh=mesh,
        scratch_types=[pltpu.VMEM((L,), jnp.int32)]
        + [pltpu.VMEM((SC_WINDOW,), jnp.int32) for _ in range(n_win)]
        + [pltpu.VMEM((SC_WINDOW, W), rows.dtype), pltpu.SemaphoreType.DMA],
        compiler_params=pltpu.CompilerParams(needs_layout_passes=False),
        name="moe_dispatch_gather",
    )(rows, slots)


def _ffn_kernel(xs_ref, mod_ref, wg_ref, wu_ref, wd_ref, y_ref, wgb_ref, wub_ref, wdb_ref, *,
                first_expert):
    nbatch, cap, _ = xs_ref.shape
    d = wg_ref.shape[0]
    dw = d // 2

    @pl.when(pl.program_id(1) == 0)
    def _():
        wgb_ref[...] = wg_ref[...].astype(BF16)
        wub_ref[...] = wu_ref[...].astype(BF16)
        wdb_ref[...] = wd_ref[...].astype(BF16)

    words = pltpu.bitcast(xs_ref[:, :, 0:dw].reshape(nbatch * cap, dw), jnp.uint32)
    xs = jnp.concatenate([pltpu.bitcast(words & jnp.uint32(0xFFFF0000), F32).astype(BF16),
                          pltpu.bitcast(words << 16, F32).astype(BF16)], axis=1)
    f = wg_ref.shape[1]
    half = f // 2
    acc = None
    for j in range(2):
        cols = slice(j * half, (j + 1) * half)
        g = _dot(xs, wgb_ref[:, cols])
        u = _dot(xs, wub_ref[:, cols])
        hid = (g * jax.nn.sigmoid(g) * u).astype(BF16)
        part = _dot(hid, wdb_ref[cols, :])
        acc = part if acc is None else acc + part
    aff = xs_ref[:, :, dw:dw + LANES].reshape(nbatch * cap, LANES)
    lane = lax.broadcasted_iota(jnp.int32, aff.shape, 1)
    gate = jnp.sum(jnp.where(lane == first_expert + pl.program_id(0), aff, 0.0), axis=-1, keepdims=True)
    y = (acc * gate).reshape(nbatch, cap, d)
    for i in range(nbatch):
        yi = (y[i] * mod_ref[i, 5:6, :]).astype(BF16)
        hi = pltpu.bitcast(yi[:, 0:dw].astype(F32), jnp.uint32)
        lo = pltpu.bitcast(yi[:, dw:].astype(F32), jnp.uint32)
        y_ref[i] = pltpu.bitcast(hi | (lo >> 16), jnp.int32)


def _expert_ffn(xs, mod3, w_gate, w_up, w_down, first_expert, nbatch):
    B, E, cap, row_words = xs.shape
    d, f = w_gate.shape[1:]
    out_width, out_dtype = d // 2, jnp.int32
    tok = pl.BlockSpec((nbatch, None, cap, out_width), lambda e, b: (b, e, 0, 0))
    return pl.pallas_call(
        functools.partial(_ffn_kernel, first_expert=first_expert),
        grid=(E, B // nbatch),
        in_specs=[pl.BlockSpec((nbatch, None, cap, row_words), lambda e, b: (b, e, 0, 0)),
                  pl.BlockSpec((nbatch, 6, d), lambda e, b: (b, 0, 0)),
                  pl.BlockSpec((None, d, f), lambda e, b: (e + first_expert, 0, 0)),
                  pl.BlockSpec((None, d, f), lambda e, b: (e + first_expert, 0, 0)),
                  pl.BlockSpec((None, f, d), lambda e, b: (e + first_expert, 0, 0))],
        out_specs=tok,
        out_shape=jax.ShapeDtypeStruct((B, E, cap, out_width), out_dtype),
        scratch_shapes=[pltpu.VMEM((d, f), BF16), pltpu.VMEM((d, f), BF16), pltpu.VMEM((f, d), BF16)],
        compiler_params=_cparams(("arbitrary", "arbitrary")),
        name="expert_swiglu",
    )(xs, mod3, w_gate, w_up, w_down)


SC_ADD_ROWS = 32


def _combine_add(acc, y, slots, cap, first_expert, n_experts, after=()):
    L = slots.shape[1]
    half = y.shape[1]
    D = 2 * half
    assert slots.shape[0] // N_EXPERTS == SC_CORES * SC_SUBCORES
    n_win = cap // SC_ADD_ROWS
    mesh = plsc.VectorSubcoreMesh(core_axis_name="c", subcore_axis_name="s",
                                  num_cores=SC_CORES, num_subcores=SC_SUBCORES)

    def body(acc_hbm, y_hbm, slots_hbm, *rest):
        slot_v, *scratch = rest[len(after):]
        idx_v = scratch[:n_win]
        y_v = scratch[n_win:n_win + 2]
        o_v = scratch[n_win + 2:n_win + 4]
        sem_y, sem_g, sem_s = (scratch[n_win + 4 + 2 * k:n_win + 6 + 2 * k] for k in range(3))
        batch = lax.axis_index("s") * SC_CORES + lax.axis_index("c")

        @pl.loop(0, n_experts)
        def _(el):
            pltpu.sync_copy(slots_hbm.at[batch * N_EXPERTS + first_expert + el], slot_v)

            @pl.loop(0, L // SC_LANES)
            def _(i):
                v = slot_v[pl.ds(i * SC_LANES, SC_LANES)]
                tok = lax.iota(jnp.int32, SC_LANES) + (i * SC_LANES + batch * L)
                for w in range(n_win):
                    in_win = (v >= w * SC_ADD_ROWS) & (v < (w + 1) * SC_ADD_ROWS)
                    plsc.store_scatter(idx_v[w], [v - w * SC_ADD_ROWS], tok, mask=in_win)

            row0 = (batch * n_experts + el) * cap

            def fetch(w):
                b = w % 2
                return (pltpu.async_copy(y_hbm.at[pl.ds(row0 + w * SC_ADD_ROWS, SC_ADD_ROWS)], y_v[b], sem_y[b]),
                        pltpu.async_copy(acc_hbm.at[idx_v[w]], o_v[b], sem_g[b]))

            loads = fetch(0)
            stores = [None, None]
            for w in range(n_win):
                b = w % 2
                nxt = None
                if w + 1 < n_win:
                    if stores[1 - b] is not None:
                        stores[1 - b].wait()
                        stores[1 - b] = None
                    nxt = fetch(w + 1)
                loads[0].wait()
                loads[1].wait()

                @pl.loop(0, SC_ADD_ROWS)
                def _(r):
                    for c in range(half // SC_LANES):
                        words = y_v[b][r, pl.ds(c * SC_LANES, SC_LANES)]
                        hi = lax.bitcast_convert_type(words & jnp.int32(-65536), F32)
                        lo = lax.bitcast_convert_type(words << 16, F32)
                        plsc.addupdate(o_v[b].at[r, pl.ds(c * SC_LANES, SC_LANES)], hi)
                        plsc.addupdate(o_v[b].at[r, pl.ds(half + c * SC_LANES, SC_LANES)], lo)

                stores[b] = pltpu.async_copy(o_v[b], acc_hbm.at[idx_v[w]], sem_s[b])
                loads = nxt
            for st in stores:
                if st is not None:
                    st.wait()

    pl.kernel(
        body,
        out_type=(),
        mesh=mesh,
        scratch_types=[pltpu.VMEM((L,), jnp.int32)]
        + [pltpu.VMEM((SC_ADD_ROWS,), jnp.int32) for _ in range(n_win)]
        + [pltpu.VMEM((SC_ADD_ROWS, half), jnp.int32) for _ in range(2)]
        + [pltpu.VMEM((SC_ADD_ROWS, D), F32) for _ in range(2)]
        + [pltpu.SemaphoreType.DMA for _ in range(6)],
        compiler_params=pltpu.CompilerParams(needs_layout_passes=False),
        name="moe_combine_row_add",
    )(acc, y, slots, *after)


def _rope_tables(L):
    inv = ROPE_BASE ** (-jnp.arange(ROPE_FREQS, dtype=F32) / ROPE_FREQS)
    pos = jnp.arange(L)
    row = (pos // GRID_W).astype(F32)[:, None] * inv
    col = (pos % GRID_W).astype(F32)[:, None] * inv
    cos = jnp.concatenate([jnp.cos(row), jnp.cos(row), jnp.cos(col), jnp.cos(col)], axis=1)
    sin = jnp.concatenate([-jnp.sin(row), jnp.sin(row), -jnp.sin(col), jnp.sin(col)], axis=1)
    return jnp.tile(cos, (1, 2)), jnp.tile(sin, (1, 2))


def _head_mean_matrix(n):
    idx = np.arange(n) // HEAD_DIM
    return jnp.asarray((idx[:, None] == idx[None, :]).astype(np.float32) / HEAD_DIM, dtype=BF16)


def kernel(x, c, ctx, c_ctx, w_mod, b_mod, norm1_g, w_in, q_norm_g, k_norm_g, attn_sink,
           w_decay_fwd, b_decay_fwd, w_decay_bwd, b_decay_bwd, gla_norm_g, w_out, norm2_g,
           w_router, w_e_gate, w_e_up, w_e_down):
    B, L, D = x.shape
    cap = CAPACITY_FACTOR * L // N_EXPERTS
    layer = 0

    rows = ((B + 1 + 7) // 8) * 8
    cc = jnp.concatenate([c, c_ctx[None, :], jnp.zeros((rows - B - 1, D), F32)], axis=0)
    mod_all = _modulation(cc, w_mod[layer], b_mod[layer])
    mod3 = mod_all[:B].reshape(B, 6, D)
    modc = mod_all[B].reshape(6, D)

    w = w_in[layer]
    o = np.cumsum([0, ATTN_WIDTH, KV_WIDTH, KV_WIDTH, GLA_QK_WIDTH, GLA_QK_WIDTH,
                   GLA_WIDTH, GLA_WIDTH, GATE_RANK, GATE_RANK])
    w_lr = jnp.concatenate([w[:, o[7]:o[9]]] * 3 + [jnp.zeros((D, LANES - 6 * GATE_RANK), F32)], axis=1)
    head_order = np.arange(N_Q_HEADS).reshape(N_KV_HEADS, -1).T.reshape(-1)
    attn_perm = (head_order[:, None] * HEAD_DIM + np.arange(HEAD_DIM)[None, :]).reshape(-1)
    w_in_r = jnp.concatenate([w[:, attn_perm], w[:, o[1]:o[7]], w_lr], axis=1).astype(BF16)
    w_out_r = jnp.concatenate([w_out[layer][attn_perm], w_out[layer][ATTN_WIDTH:]], axis=0).astype(BF16)
    w_ctx = jnp.concatenate([w[:, o[1]:o[3]], w[:, o[4]:o[6]], w_lr], axis=1).astype(BF16)
    wd2 = jnp.zeros((2 * GATE_RANK, 2 * GLA_QK_WIDTH), F32)
    wd2 = wd2.at[0:GATE_RANK, 0:GLA_QK_WIDTH].set(w_decay_fwd[layer])
    wd2 = wd2.at[GATE_RANK:, GLA_QK_WIDTH:].set(w_decay_bwd[layer])
    wd_hi = wd2.astype(BF16)
    wd_lo = (wd2 - wd_hi.astype(F32)).astype(BF16)
    wd = jnp.concatenate([wd_hi, wd_hi, wd_lo,
                          jnp.zeros((LANES - 6 * GATE_RANK, 2 * GLA_QK_WIDTH), BF16)], axis=0)
    bdec = jnp.concatenate([b_decay_fwd[layer], b_decay_bwd[layer]])[None, :]
    g1 = norm1_g[layer][None, :]
    g2 = norm2_g[layer][None, :]
    qg = jnp.tile(q_norm_g[layer], N_Q_HEADS)[None, :]
    kg2 = jnp.tile(k_norm_g[layer], N_KV_HEADS)[None, :]
    gn = jnp.tile(gla_norm_g[layer], GLA_HEADS)[None, :]
    bd512 = _head_mean_matrix(ATTN_WIDTH)
    bd128 = _head_mean_matrix(KV_WIDTH)
    cos_t, sin_t = _rope_tables(L)
    w_router_t = w_router[layer].T.astype(BF16)

    kc, vc, s_f, s_b = _context_side(ctx, modc, g1, w_ctx, kg2, bd128, wd, bdec)
    q, k, v, gq, gk, gv, gg, lr = _input_projection(
        x, mod3, g1, w_in_r, qg, kg2, bd512, cos_t, sin_t, tm=1024)
    attn = _window_attention(attn_sink[layer], q, k, v, kc, vc, tq=512)
    gla = _gla(gq, gk, gv, gg, lr, wd, bdec, gn, s_f, s_b)
    x1, h2, afft = _output_projection(attn, gla, x, mod3, w_out_r, g2, w_router_t, tm=1024)
    post = _expert_choice(afft, cap, nbatch=4)
    n_groups = 4
    per_group = N_EXPERTS // n_groups
    rows = h2.reshape(B * L, ROW_WORDS)
    slots = post.reshape(B * N_EXPERTS, L)
    stream = jax.new_ref(x1.reshape(B * L, D))
    gathered = [_dispatch(rows, slots, cap, g * per_group, per_group) for g in range(n_groups)]
    for g in range(n_groups):
        xs = gathered[g].reshape(B, per_group, cap, ROW_WORDS)
        y = _expert_ffn(xs, mod3, w_e_gate[layer], w_e_up[layer], w_e_down[layer], g * per_group, nbatch=4)
        _combine_add(stream, y.reshape(B * per_group * cap, D // 2), slots, cap, g * per_group, per_group,
                     after=gathered[g + 1:])
    return jax.freeze(stream).reshape(B, L, D)
```

```python
import functools

import jax
import jax.numpy as jnp
import numpy as np
from jax import lax
from jax.experimental import pallas as pl
from jax.experimental.pallas import tpu as pltpu
from jax.experimental.pallas import tpu_sc as plsc

D_MODEL = 1024
GRID_W = 64
HEAD_DIM = 64
N_Q_HEADS = 8
N_KV_HEADS = 2
BLOCK = 128
ROPE_FREQS = 16
ROPE_BASE = 10000.0
GLA_HEADS = 4
GLA_DV = 128
GLA_DK = 64
GATE_RANK = 16
GATE_NORMALIZER = 16.0
CHUNK = 64
N_EXPERTS = 16
CAPACITY_FACTOR = 2
ATTN_WIDTH = 512
KV_WIDTH = 128
GLA_QK_WIDTH = 256
GLA_WIDTH = 512
EPS = 1e-6
NEG_INF = -1e30
LOG2E = 1.4426950408889634

LANES = 128
ROW_WORDS = D_MODEL // 2 + LANES
VMEM_LIMIT = 56 * 1024 * 1024

F32 = jnp.float32
BF16 = jnp.bfloat16
HI = lax.Precision.HIGHEST


def _cparams(sem):
    return pltpu.CompilerParams(dimension_semantics=sem, vmem_limit_bytes=VMEM_LIMIT)


def _dot(a, b):
    return jnp.dot(a, b, preferred_element_type=F32)


def _dot_hi(a, b):
    return jnp.dot(a, b, preferred_element_type=F32, precision=HI)


def _dot_nt(a, b):
    return lax.dot_general(a, b, (((1,), (1,)), ((), ())), preferred_element_type=F32)


def _dot_tn(a, b, precision=None):
    return lax.dot_general(a, b, (((0,), (0,)), ((), ())), preferred_element_type=F32,
                           precision=precision)


def _split2(t):
    hi = t.astype(BF16)
    lo = (t - hi.astype(F32)).astype(BF16)
    return hi, lo


def _rms_mod(t, g, shift, scale):
    y = t * lax.rsqrt(jnp.mean(t * t, axis=-1, keepdims=True) + EPS)
    return (y * g) * (1.0 + scale) + shift


def _log_decay(lr, wd3, bias):
    hi = lr.astype(BF16)
    lo = (lr - hi.astype(F32)).astype(BF16)
    lane = lax.broadcasted_iota(jnp.int32, lr.shape, 1)
    second = (lane >= 2 * GATE_RANK) & (lane < 4 * GATE_RANK)
    z = _dot(jnp.where(second, lo, hi), wd3) + bias
    return (jnp.minimum(z, 0.0) - jnp.log(1.0 + jnp.exp(-jnp.abs(z)))) * (1.0 / GATE_NORMALIZER)


def _lane_lo(shape):
    return (lax.broadcasted_iota(jnp.int32, shape, len(shape) - 1) % LANES) < HEAD_DIM


def _mod_kernel(c_ref, w_ref, b_ref, o_ref):
    c = c_ref[...]
    s = c * jax.nn.sigmoid(c)
    o_ref[...] = _dot_hi(s, w_ref[...]) + b_ref[...]


def _modulation(cc, w_mod, b_mod):
    m = cc.shape[0]
    n = w_mod.shape[1]
    tn = 1024
    return pl.pallas_call(
        _mod_kernel,
        grid=(n // tn,),
        in_specs=[pl.BlockSpec((m, D_MODEL), lambda j: (0, 0)),
                  pl.BlockSpec((D_MODEL, tn), lambda j: (0, j)),
                  pl.BlockSpec((1, tn), lambda j: (0, j))],
        out_specs=pl.BlockSpec((m, tn), lambda j: (0, j)),
        out_shape=jax.ShapeDtypeStruct((m, n), F32),
        compiler_params=_cparams(("arbitrary",)),
        name="adaln_mod",
    )(cc, w_mod, b_mod.reshape(1, n))


def _ctx_kernel(ctx_ref, mod_ref, g1_ref, w_ref, kg_ref, bd_ref, wd_ref, bdec_ref,
                kc_ref, vc_ref, sf_ref, sb_ref):
    n = ctx_ref.shape[0]
    h = _rms_mod(ctx_ref[...], g1_ref[...], mod_ref[0:1, :], mod_ref[1:2, :]).astype(BF16)
    pc = _dot(h, w_ref[...])
    ak = pc[:, 0:128]
    av = pc[:, 128:256]
    gk = pc[:, 256:512]
    gv = pc[:, 512:1024].astype(BF16)
    lr = pc[:, 1024:1152]
    sq_hi, sq_lo = _split2(ak * ak)
    ms = _dot(sq_hi, bd_ref[...]) + _dot(sq_lo, bd_ref[...])
    kn = ak * lax.rsqrt(ms + EPS) * kg_ref[...]
    kc_ref[...] = kn.astype(BF16)
    vc_ref[...] = av.astype(BF16)
    la = _log_decay(lr, wd_ref[...], bdec_ref[...])
    r = lax.broadcasted_iota(jnp.int32, (n, n), 0)
    cidx = lax.broadcasted_iota(jnp.int32, (n, n), 1)
    after = (cidx > r).astype(F32)
    before = (cidx < r).astype(F32)
    w_f = jnp.exp(_dot_hi(after, la[:, 0:256]))
    w_b = jnp.exp(_dot_hi(before, la[:, 256:512]))
    lo = _lane_lo((n, LANES))
    for w, out in ((w_f, sf_ref), (w_b, sb_ref)):
        kw = gk * w
        for c in range(2):
            kwc = kw[:, c * LANES:(c + 1) * LANES]
            k_lo = jnp.where(lo, kwc, 0.0).astype(BF16)
            k_hi = jnp.where(lo, 0.0, kwc).astype(BF16)
            v0 = gv[:, (2 * c) * GLA_DV:(2 * c + 1) * GLA_DV]
            v1 = gv[:, (2 * c + 1) * GLA_DV:(2 * c + 2) * GLA_DV]
            out[c] = _dot_tn(v0, k_lo) + _dot_tn(v1, k_hi)


def _context_side(ctx, modc, g1, w_ctx, kg2, bd128, wd, bdec):
    B, n, _ = ctx.shape
    full = lambda shape: pl.BlockSpec(shape, lambda b: (0,) * len(shape))
    kv_spec = pl.BlockSpec((None, n, KV_WIDTH), lambda b: (b, 0, 0))
    st_spec = pl.BlockSpec((None, 2, LANES, GLA_DV), lambda b: (b, 0, 0, 0))
    kv_shape = jax.ShapeDtypeStruct((B, n, KV_WIDTH), BF16)
    st_shape = jax.ShapeDtypeStruct((B, 2, LANES, GLA_DV), F32)
    return pl.pallas_call(
        _ctx_kernel,
        grid=(B,),
        in_specs=[pl.BlockSpec((None, n, D_MODEL), lambda b: (b, 0, 0)),
                  full(modc.shape), full(g1.shape), full(w_ctx.shape), full(kg2.shape),
                  full(bd128.shape), full(wd.shape), full(bdec.shape)],
        out_specs=[kv_spec, kv_spec, st_spec, st_spec],
        out_shape=[kv_shape, kv_shape, st_shape, st_shape],
        compiler_params=_cparams(("arbitrary",)),
        name="context_side",
    )(ctx, modc, g1, w_ctx, kg2, bd128, wd, bdec)


def _swap16(t):
    n = t.shape[1]
    first = (lax.broadcasted_iota(jnp.int32, t.shape, 1) % 32) < ROPE_FREQS
    return jnp.where(first, pltpu.roll(t, n - ROPE_FREQS, 1), pltpu.roll(t, ROPE_FREQS, 1))


def _inproj_kernel(x_ref, mod_ref, g1_ref, w_ref, qg_ref, kg_ref, bd_ref, cos_ref, sin_ref,
                   q_ref, k_ref, v_ref, gq_ref, gk_ref, gv_ref, gg_ref, lr_ref):
    h = _rms_mod(x_ref[...], g1_ref[...], mod_ref[0:1, :], mod_ref[1:2, :]).astype(BF16)
    cos = cos_ref[...]
    sin = sin_ref[...]

    def head_norm_rope(t, g, bd, reps):
        ms = _dot((t * t).astype(BF16), bd)
        tn = t * lax.rsqrt(ms + EPS) * g
        c = jnp.concatenate([cos] * reps, axis=1) if reps > 1 else cos
        s = jnp.concatenate([sin] * reps, axis=1) if reps > 1 else sin
        return tn * c + _swap16(tn) * s

    aq = _dot(h, w_ref[:, 0:512])
    q = head_norm_rope(aq, qg_ref[...], bd_ref[...], 4) * (HEAD_DIM ** -0.5 * LOG2E)
    q_ref[...] = q.astype(BF16)
    akv = _dot(h, w_ref[:, 512:768])
    k = head_norm_rope(akv[:, 0:128], kg_ref[...], bd_ref[0:128, 0:128], 1)
    k_ref[...] = k.astype(BF16)
    v_ref[...] = akv[:, 128:256].astype(BF16)
    gqk = _dot(h, w_ref[:, 768:1280])
    gq_ref[...] = (gqk[:, 0:256] * (GLA_DK ** -0.5)).astype(BF16)
    gk_ref[...] = gqk[:, 256:512].astype(BF16)
    gv_ref[...] = _dot(h, w_ref[:, 1280:1792]).astype(BF16)
    gg_ref[...] = _dot(h, w_ref[:, 1792:2304]).astype(BF16)
    lr_ref[...] = _dot(h, w_ref[:, 2304:2432])


def _input_projection(x, mod3, g1, w_in_r, qg, kg2, bd512, cos_t, sin_t, tm):
    B, L, _ = x.shape
    full = lambda shape: pl.BlockSpec(shape, lambda b, i: (0,) * len(shape))
    tok = lambda w: pl.BlockSpec((None, tm, w), lambda b, i: (b, i, 0))
    widths = (ATTN_WIDTH, KV_WIDTH, KV_WIDTH, GLA_QK_WIDTH, GLA_QK_WIDTH, GLA_WIDTH, GLA_WIDTH, LANES)
    dtypes = (BF16,) * 7 + (F32,)
    return pl.pallas_call(
        _inproj_kernel,
        grid=(B, L // tm),
        in_specs=[tok(D_MODEL),
                  pl.BlockSpec((None, 6, D_MODEL), lambda b, i: (b, 0, 0)),
                  full(g1.shape), full(w_in_r.shape), full(qg.shape), full(kg2.shape),
                  full(bd512.shape),
                  pl.BlockSpec((tm, LANES), lambda b, i: (i, 0)),
                  pl.BlockSpec((tm, LANES), lambda b, i: (i, 0))],
        out_specs=[tok(w) for w in widths],
        out_shape=[jax.ShapeDtypeStruct((B, L, w), dt) for w, dt in zip(widths, dtypes)],
        compiler_params=_cparams(("arbitrary", "arbitrary")),
        name="input_projection",
    )(x, mod3, g1, w_in_r, qg, kg2, bd512, cos_t, sin_t)


def _attn_kernel(sink_ref, q_ref, kp_ref, ko_ref, kn_ref, vp_ref, vo_ref, vn_ref,
                 kc_ref, vc_ref, o_ref):
    i = pl.program_id(1)
    ni = pl.num_programs(1)
    nsub = q_ref.shape[0] // BLOCK
    ncol = ATTN_WIDTH // LANES
    win = 3 * BLOCK
    ucol = 4
    half_rows = ucol * BLOCK
    k_win = jnp.concatenate([kp_ref[...], ko_ref[...], kn_ref[...]], axis=0)
    v_win = jnp.concatenate([vp_ref[...], vo_ref[...], vn_ref[...]], axis=0)
    k_ctx = kc_ref[...]
    lo_w = _lane_lo(v_win.shape)
    lo_c = _lane_lo(vc_ref.shape)
    lo_q = _lane_lo((BLOCK, LANES))
    zero = jnp.zeros((), BF16)
    one = jnp.ones((), BF16)
    v0_c, v0_w = jnp.where(lo_c, vc_ref[...], one), jnp.where(lo_w, v_win, one)
    v1_c, v1_w = jnp.where(lo_c, one, vc_ref[...]), jnp.where(lo_w, one, v_win)
    qi = lax.broadcasted_iota(jnp.int32, (half_rows, BLOCK), 0) % BLOCK
    kj = lax.broadcasted_iota(jnp.int32, (half_rows, BLOCK), 1)
    no_prev = jnp.where(i > 0, 0, BLOCK)
    no_next = jnp.where(i < ni - 1, 0, BLOCK)
    row_head = lax.broadcasted_iota(jnp.int32, (half_rows, 1), 0) // BLOCK
    lo_o = _lane_lo((half_rows, LANES))
    for t in range(nsub):
        rows = slice(t * BLOCK, (t + 1) * BLOCK)
        keys = slice(t * BLOCK, t * BLOCK + win)
        cols = [q_ref[rows, c * LANES:(c + 1) * LANES] for c in range(ncol)]
        prev_ok = kj >= qi + (no_prev if t == 0 else 0)
        next_ok = kj <= qi - (no_next if t == nsub - 1 else 0)
        for p in range(ncol // ucol):
            outs = []
            qs = jnp.concatenate([jnp.where(lo_q, qc, zero) for qc in cols[ucol * p:ucol * (p + 1)]]
                                 + [jnp.where(lo_q, zero, qc) for qc in cols[ucol * p:ucol * (p + 1)]], axis=0)
            s_c_all = _dot_nt(qs, k_ctx)
            s_w_all = _dot_nt(qs, k_win[keys])
            for g, (vv_c, vv_w) in enumerate(((v0_c, v0_w), (v1_c, v1_w))):
                head = g * ncol + ucol * p
                sink_g = jnp.full((half_rows, 1), sink_ref[head + ucol - 1], F32)
                for j in range(ucol - 2, -1, -1):
                    sink_g = jnp.where(row_head <= j, sink_ref[head + j], sink_g)
                sink_g = sink_g * LOG2E
                s_c = s_c_all[g * half_rows:(g + 1) * half_rows]
                s_w = s_w_all[g * half_rows:(g + 1) * half_rows]
                s_p = jnp.where(prev_ok, s_w[:, 0:BLOCK], NEG_INF)
                s_o = s_w[:, BLOCK:2 * BLOCK]
                s_n = jnp.where(next_ok, s_w[:, 2 * BLOCK:win], NEG_INF)
                m = jnp.maximum(jnp.maximum(jnp.max(s_c, axis=-1, keepdims=True),
                                            jnp.max(jnp.maximum(jnp.maximum(s_p, s_o), s_n),
                                                    axis=-1, keepdims=True)), sink_g)
                e_c = jnp.exp2(s_c - m).astype(BF16)
                e_w = jnp.concatenate([jnp.exp2(s_p - m), jnp.exp2(s_o - m), jnp.exp2(s_n - m)],
                                      axis=1).astype(BF16)
                acc = _dot(e_c, vv_c) + _dot(e_w, vv_w[keys])
                outs.append(acc / (pltpu.roll(acc, HEAD_DIM, 1) + jnp.exp2(sink_g - m)))
            o = jnp.where(lo_o, outs[0], outs[1]).astype(BF16)
            for j in range(ucol):
                c = ucol * p + j
                o_ref[rows, c * LANES:(c + 1) * LANES] = o[j * BLOCK:(j + 1) * BLOCK]


def _window_attention(sink, q, k, v, kc, vc, tq):
    B, L, _ = q.shape
    nb = L // BLOCK
    nsub = tq // BLOCK
    n_ctx = kc.shape[1]
    prev = pl.BlockSpec((None, BLOCK, KV_WIDTH), lambda b, n: (b, jnp.maximum(n * nsub - 1, 0), 0))
    own = pl.BlockSpec((None, tq, KV_WIDTH), lambda b, n: (b, n, 0))
    nxt = pl.BlockSpec((None, BLOCK, KV_WIDTH),
                       lambda b, n: (b, jnp.minimum((n + 1) * nsub, nb - 1), 0))
    cspec = pl.BlockSpec((None, n_ctx, KV_WIDTH), lambda b, n: (b, 0, 0))
    return pl.pallas_call(
        _attn_kernel,
        grid=(B, L // tq),
        in_specs=[pl.BlockSpec(memory_space=pltpu.SMEM),
                  pl.BlockSpec((None, tq, ATTN_WIDTH), lambda b, n: (b, n, 0)),
                  prev, own, nxt, prev, own, nxt, cspec, cspec],
        out_specs=pl.BlockSpec((None, tq, ATTN_WIDTH), lambda b, n: (b, n, 0)),
        out_shape=jax.ShapeDtypeStruct((B, L, ATTN_WIDTH), BF16),
        compiler_params=_cparams(("arbitrary", "arbitrary")),
        name="window_attention",
    )(sink, q, k, k, k, v, v, v, kc, vc)


SUPER = 256
CH_PER = SUPER // CHUNK
HALF = 128


def _dot_split(m, parts):
    return _dot(m, parts[0]) + _dot(m, parts[1])


def _gla_kernel(gq_ref, gk_ref, gv_ref, gg_ref, lr_ref, wd_ref, bdec_ref, gn_ref, sf_ref, sb_ref,
                o_ref, la_ref, oi_ref, qg_ref, kv_ref, sb16_ref, dec_ref, st_ref):
    L = gq_ref.shape[0]
    nsuper = L // SUPER
    nchunk = L // CHUNK
    half = CHUNK // 2
    la_ref[...] = _log_decay(lr_ref[...], wd_ref[...], bdec_ref[...])

    r = lax.broadcasted_iota(jnp.int32, (SUPER, SUPER), 0)
    cidx = lax.broadcasted_iota(jnp.int32, (SUPER, SUPER), 1)
    same = (r // CHUNK) == (cidx // CHUNK)
    pr = r % CHUNK
    pc = cidx % CHUNK
    one = jnp.float32(1.0)
    zero = jnp.float32(0.0)
    in_f = jnp.where(pc <= pr, one, zero)
    in_b = jnp.where(pc >= pr, one, zero)
    ref_f = jnp.where(pc < half, one, zero)
    ref_b = jnp.where(pc >= half, one, zero)
    m1_f = jnp.where(same, in_f - ref_f, zero).astype(BF16)
    m1_b = jnp.where(same, in_b - ref_b, zero).astype(BF16)
    rh = lax.broadcasted_iota(jnp.int32, (HALF, 2 * HALF), 0)
    ch = lax.broadcasted_iota(jnp.int32, (HALF, 2 * HALF), 1) % HALF
    same_h = (rh // CHUNK) == (ch // CHUNK)
    mask_f = jnp.where(same_h, jnp.where(ch % CHUNK <= rh % CHUNK, one, zero), zero) > 0.5
    mask_b = jnp.where(same_h, jnp.where(ch % CHUNK >= rh % CHUNK, one, zero), zero) > 0.5
    rr = lax.broadcasted_iota(jnp.int32, (2 * CH_PER, SUPER), 0)
    rc = lax.broadcasted_iota(jnp.int32, (2 * CH_PER, SUPER), 1)
    in_chunk = jnp.where((rc // CHUNK) == (rr % CH_PER), one, zero)
    first = jnp.where((rc % CHUNK) < half, 1, 0)
    is_tot = jnp.where(rr >= CH_PER, 1, 0)
    rs_f = (in_chunk * jnp.where(first != is_tot, one, zero)).astype(BF16)
    rs_b = (in_chunk * jnp.where(first == is_tot, one, zero)).astype(BF16)
    lo_h = _lane_lo((HALF, LANES))
    zero_blk = jnp.zeros((HALF, GLA_DV), BF16)

    def phase1(s, carry):
        r0 = pl.multiple_of(s * SUPER, SUPER)
        rows = pl.ds(r0, SUPER)
        q = gq_ref[rows, :].astype(F32)
        k = gk_ref[rows, :].astype(F32)
        qes, kes, kds = [], [], []
        for d, (m1, rs) in enumerate(((m1_f, rs_f), (m1_b, rs_b))):
            parts = _split2(la_ref[rows, d * GLA_QK_WIDTH:(d + 1) * GLA_QK_WIDTH])
            x1 = _dot_split(m1, parts)
            erow = jnp.exp(_dot_split(rs, parts))
            dec = erow[0:CH_PER] * erow[CH_PER:2 * CH_PER]
            dec_ref[d, s] = jnp.concatenate([dec, dec], axis=0)
            qe = q * jnp.exp(x1)
            ke = k * jnp.exp(-x1)
            qg_parts, kd_parts = [], []
            for j in range(CH_PER):
                rj = slice(j * CHUNK, (j + 1) * CHUNK)
                qg_parts.append(qe[rj] * erow[j:j + 1])
                kd_parts.append(ke[rj] * erow[CH_PER + j:CH_PER + j + 1])
            qg_ref[d, rows, :] = jnp.concatenate(qg_parts, axis=0).astype(BF16)
            qes.append(qe.astype(BF16))
            kes.append(ke)
            kds.append(jnp.concatenate(kd_parts, axis=0).astype(BF16))
        for c in range(2):
            cl = slice(c * LANES, (c + 1) * LANES)
            vpair = gv_ref[rows, 2 * c * GLA_DV:(2 * c + 2) * GLA_DV]
            for blk in range(SUPER // HALF):
                rb = slice(blk * HALF, (blk + 1) * HALF)
                vbd = jnp.concatenate(
                    [jnp.concatenate([vpair[rb, 0:GLA_DV], zero_blk], axis=1),
                     jnp.concatenate([zero_blk, vpair[rb, GLA_DV:2 * GLA_DV]], axis=1)], axis=0)
                o2 = None
                for d, mask in enumerate((mask_f, mask_b)):
                    ke_cb = kes[d][rb, cl]
                    ke_st = jnp.concatenate([jnp.where(lo_h, ke_cb, zero),
                                             jnp.where(lo_h, zero, ke_cb)], axis=0).astype(BF16)
                    a = _dot_nt(qes[d][rb, cl], ke_st)
                    o = _dot(jnp.where(mask, a, zero).astype(BF16), vbd)
                    o2 = o if o2 is None else o2 + o
                oi_ref[pl.ds(r0 + blk * HALF, HALF), 2 * c * GLA_DV:(2 * c + 2) * GLA_DV] = o2
            for d in range(2):
                for j in range(CH_PER):
                    rj = slice(j * CHUNK, (j + 1) * CHUNK)
                    t = _dot_tn(kds[d][rj, cl], vpair[rj])
                    kv = jnp.concatenate([t[0:GLA_DK, 0:GLA_DV], t[GLA_DK:, GLA_DV:]], axis=0)
                    kv_ref[d, c, s * CH_PER + j] = kv.T
        return carry

    lax.fori_loop(0, nsuper, phase1, 0)

    st_ref[0] = sf_ref[0]
    st_ref[1] = sf_ref[1]
    st_ref[2] = sb_ref[0]
    st_ref[3] = sb_ref[1]

    def phase2(n, carry):
        for d in range(2):
            idx = n if d == 0 else nchunk - 1 - n
            dec = dec_ref[d, idx // CH_PER, pl.ds(idx % CH_PER, 1), :]
            for c in range(2):
                st = st_ref[2 * d + c]
                sb16_ref[c, idx, :, d * LANES:(d + 1) * LANES] = st.astype(BF16)
                st_ref[2 * d + c] = dec[:, c * LANES:(c + 1) * LANES] * st + kv_ref[d, c, idx]
        return carry

    lax.fori_loop(0, nchunk, phase2, 0)

    lo64 = _lane_lo((CHUNK, 2 * LANES))
    zero_b = jnp.zeros((), BF16)

    def phase3(s, carry):
        r0 = pl.multiple_of(s * SUPER, SUPER)
        rows = pl.ds(r0, SUPER)
        inter = [[None] * CH_PER for _ in range(GLA_HEADS)]
        for j in range(CH_PER):
            rj = pl.ds(r0 + j * CHUNK, CHUNK)
            for c in range(2):
                qg_c = jnp.concatenate([qg_ref[d, rj, c * LANES:(c + 1) * LANES] for d in range(2)], axis=1)
                lhs = jnp.concatenate([jnp.where(lo64, qg_c, zero_b),
                                       jnp.where(lo64, zero_b, qg_c)], axis=0)
                t = _dot_nt(lhs, sb16_ref[c, s * CH_PER + j])
                for hh in range(2):
                    inter[2 * c + hh][j] = t[hh * CHUNK:(hh + 1) * CHUNK]
        for hd in range(GLA_HEADS):
            cl = slice(hd * GLA_DV, (hd + 1) * GLA_DV)
            o = oi_ref[rows, cl] + jnp.concatenate(inter[hd], axis=0)
            y = o * lax.rsqrt(jnp.mean(o * o, axis=-1, keepdims=True) + EPS) * gn_ref[:, cl]
            g = gg_ref[rows, cl].astype(F32)
            o_ref[rows, cl] = (y * (g * jax.nn.sigmoid(g))).astype(BF16)
        return carry

    lax.fori_loop(0, nsuper, phase3, 0)


def _gla(gq, gk, gv, gg, lr, wd, bdec, gn, s_f, s_b):
    B, L, _ = gq.shape
    nchunk = L // CHUNK
    full = lambda shape: pl.BlockSpec(shape, lambda b: (0,) * len(shape))
    tok = lambda w: pl.BlockSpec((None, L, w), lambda b: (b, 0, 0))
    st_spec = pl.BlockSpec((None, 2, LANES, GLA_DV), lambda b: (b, 0, 0, 0))
    return pl.pallas_call(
        _gla_kernel,
        grid=(B,),
        in_specs=[tok(GLA_QK_WIDTH), tok(GLA_QK_WIDTH), tok(GLA_WIDTH), tok(GLA_WIDTH), tok(LANES),
                  full(wd.shape), full(bdec.shape), full(gn.shape), st_spec, st_spec],
        out_specs=tok(GLA_WIDTH),
        out_shape=jax.ShapeDtypeStruct((B, L, GLA_WIDTH), BF16),
        scratch_shapes=[pltpu.VMEM((L, 2 * GLA_QK_WIDTH), F32),
                        pltpu.VMEM((L, GLA_WIDTH), F32),
                        pltpu.VMEM((2, L, GLA_QK_WIDTH), BF16),
                        pltpu.VMEM((2, 2, nchunk, GLA_DV, LANES), F32),
                        pltpu.VMEM((2, nchunk, GLA_DV, 2 * LANES), BF16),
                        pltpu.VMEM((2, L // SUPER, 2 * CH_PER, GLA_QK_WIDTH), F32),
                        pltpu.VMEM((4, GLA_DV, LANES), F32)],
        compiler_params=_cparams(("arbitrary",)),
        name="gla_bidirectional",
    )(gq, gk, gv, gg, lr, wd, bdec, gn, s_f, s_b)


def _outproj_kernel(attn_ref, gla_ref, x_ref, mod_ref, w_ref, g2_ref, wr_ref,
                    x1_ref, h2_ref, afft_ref):
    y = _dot(attn_ref[...], w_ref[0:ATTN_WIDTH, :]) + _dot(gla_ref[...], w_ref[ATTN_WIDTH:, :])
    x1 = x_ref[...] + mod_ref[2:3, :] * y
    x1_ref[...] = x1
    h2 = _rms_mod(x1, g2_ref[...], mod_ref[3:4, :], mod_ref[4:5, :]).astype(BF16)
    half = D_MODEL // 2
    hi = pltpu.bitcast(h2[:, 0:half].astype(F32), jnp.uint32)
    lo = pltpu.bitcast(h2[:, half:].astype(F32), jnp.uint32)
    h2_ref[:, 0:half] = pltpu.bitcast(hi | (lo >> 16), F32)
    logits = _dot_nt(wr_ref[...], h2)
    e = jnp.exp(logits - jnp.max(logits, axis=0, keepdims=True))
    afft = e / jnp.sum(e, axis=0, keepdims=True)
    afft_ref[...] = afft
    pad = jnp.zeros((LANES - N_EXPERTS, afft.shape[1]), F32)
    h2_ref[:, half:half + LANES] = jnp.concatenate([afft, pad], axis=0).T


def _output_projection(attn, gla, x, mod3, w_out, g2, w_router, tm):
    B, L, _ = x.shape
    full = lambda shape: pl.BlockSpec(shape, lambda b, i: (0,) * len(shape))
    tok = lambda w: pl.BlockSpec((None, tm, w), lambda b, i: (b, i, 0))
    return pl.pallas_call(
        _outproj_kernel,
        grid=(B, L // tm),
        in_specs=[tok(ATTN_WIDTH), tok(GLA_WIDTH), tok(D_MODEL),
                  pl.BlockSpec((None, 6, D_MODEL), lambda b, i: (b, 0, 0)),
                  full(w_out.shape), full(g2.shape), full(w_router.shape)],
        out_specs=[tok(D_MODEL), tok(ROW_WORDS),
                   pl.BlockSpec((None, N_EXPERTS, tm), lambda b, i: (b, 0, i))],
        out_shape=[jax.ShapeDtypeStruct((B, L, D_MODEL), F32),
                   jax.ShapeDtypeStruct((B, L, ROW_WORDS), F32),
                   jax.ShapeDtypeStruct((B, N_EXPERTS, L), F32)],
        compiler_params=_cparams(("arbitrary", "arbitrary")),
        name="output_projection_router",
    )(attn, gla, x, mod3, w_out, g2, w_router)


def _topk_kernel(afft_ref, post_ref, *, cap):
    nbatch, E, L = afft_ref.shape
    aff = afft_ref[...].reshape(nbatch * E, L)
    E = nbatch * E

    def search(i, thr):
        cand = thr | jnp.left_shift(jnp.int32(1), 30 - i)
        cnt = jnp.sum(jnp.where(aff >= pltpu.bitcast(cand, F32), 1.0, 0.0), axis=-1, keepdims=True)
        return jnp.where(cnt >= cap, cand, thr)

    thr_bits = lax.fori_loop(0, 31, search, jnp.zeros((E, 1), jnp.int32))
    thr = pltpu.bitcast(thr_bits, F32)
    above = aff > thr
    tie = aff == thr
    need = cap - jnp.sum(jnp.where(above, 1.0, 0.0), axis=-1, keepdims=True)

    upper = (lax.broadcasted_iota(jnp.int32, (LANES, LANES), 0)
             <= lax.broadcasted_iota(jnp.int32, (LANES, LANES), 1)).astype(BF16)

    def prefix(mask):
        parts = []
        run = jnp.zeros((E, 1), F32)
        for j in range(L // LANES):
            blk = jnp.where(mask[:, j * LANES:(j + 1) * LANES], 1.0, 0.0).astype(BF16)
            loc = _dot(blk, upper) + run
            parts.append(loc)
            run = loc[:, LANES - 1:LANES]
        return jnp.concatenate(parts, axis=1)

    tie_rank = prefix(tie)
    sel = above | (tie & (tie_rank <= need))
    slot = prefix(sel).astype(jnp.int32) - 1
    post_ref[...] = jnp.where(sel, slot, -1).reshape(post_ref.shape)


def _expert_choice(afft, cap, nbatch):
    B, E, L = afft.shape
    return pl.pallas_call(
        functools.partial(_topk_kernel, cap=cap),
        grid=(B // nbatch,),
        in_specs=[pl.BlockSpec((nbatch, E, L), lambda b: (b, 0, 0))],
        out_specs=pl.BlockSpec((nbatch, E, L), lambda b: (b, 0, 0)),
        out_shape=jax.ShapeDtypeStruct((B, E, L), jnp.int32),
        compiler_params=_cparams(("arbitrary",)),
        name="expert_choice_topk",
    )(afft)


SC_CORES = 2
SC_SUBCORES = 16
SC_LANES = 16
SC_WINDOW = 128


def _dispatch(rows, slots, cap, first_expert, n_experts):
    L = slots.shape[1]
    W = rows.shape[1]
    n_pair = slots.shape[0] // N_EXPERTS * n_experts
    per_worker = n_pair // (SC_CORES * SC_SUBCORES)
    n_win = cap // SC_WINDOW
    mesh = plsc.VectorSubcoreMesh(core_axis_name="c", subcore_axis_name="s",
                                  num_cores=SC_CORES, num_subcores=SC_SUBCORES)

    def body(rows_hbm, slots_hbm, out_hbm, slot_v, *scratch):
        idx_v, buf_v, sem = scratch[:n_win], scratch[n_win], scratch[n_win + 1]
        worker = lax.axis_index("s") * SC_CORES + lax.axis_index("c")

        @pl.loop(0, per_worker)
        def _(p):
            pair = worker * per_worker + p
            batch = pair // n_experts
            first_tok = batch * L
            pltpu.sync_copy(slots_hbm.at[batch * N_EXPERTS + first_expert + pair % n_experts], slot_v)

            @pl.loop(0, L // SC_LANES)
            def _(i):
                v = slot_v[pl.ds(i * SC_LANES, SC_LANES)]
                tok = lax.iota(jnp.int32, SC_LANES) + (i * SC_LANES + first_tok)
                for w in range(n_win):
                    in_win = (v >= w * SC_WINDOW) & (v < (w + 1) * SC_WINDOW)
                    plsc.store_scatter(idx_v[w], [v - w * SC_WINDOW], tok, mask=in_win)

            for w in range(n_win):
                pltpu.async_copy(rows_hbm.at[idx_v[w]], buf_v, sem).wait()
                pltpu.sync_copy(buf_v, out_hbm.at[pl.ds(pair * cap + w * SC_WINDOW, SC_WINDOW)])

    return pl.kernel(
        body,
        out_type=jax.ShapeDtypeStruct((n_pair * cap, W), rows.dtype),
        mesh=mesh,
        scratch_types=[pltpu.VMEM((L,), jnp.int32)]
        + [pltpu.VMEM((SC_WINDOW,), jnp.int32) for _ in range(n_win)]
        + [pltpu.VMEM((SC_WINDOW, W), rows.dtype), pltpu.SemaphoreType.DMA],
        compiler_params=pltpu.CompilerParams(needs_layout_passes=False),
        name="moe_dispatch_gather",
    )(rows, slots)


def _ffn_kernel(xs_ref, mod_ref, wg_ref, wu_ref, wd_ref, y_ref, wgb_ref, wub_ref, wdb_ref, *,
                first_expert):
    nbatch, cap, _ = xs_ref.shape
    d = wg_ref.shape[0]
    dw = d // 2

    @pl.when(pl.program_id(1) == 0)
    def _():
        wgb_ref[...] = wg_ref[...].astype(BF16)
        wub_ref[...] = wu_ref[...].astype(BF16)
        wdb_ref[...] = wd_ref[...].astype(BF16)

    words = pltpu.bitcast(xs_ref[:, :, 0:dw].reshape(nbatch * cap, dw), jnp.uint32)
    xs = jnp.concatenate([pltpu.bitcast(words & jnp.uint32(0xFFFF0000), F32).astype(BF16),
                          pltpu.bitcast(words << 16, F32).astype(BF16)], axis=1)
    f = wg_ref.shape[1]
    half = f // 2
    acc = None
    for j in range(2):
        cols = slice(j * half, (j + 1) * half)
        g = _dot(xs, wgb_ref[:, cols])
        u = _dot(xs, wub_ref[:, cols])
        hid = (g * jax.nn.sigmoid(g) * u).astype(BF16)
        part = _dot(hid, wdb_ref[cols, :])
        acc = part if acc is None else acc + part
    aff = xs_ref[:, :, dw:dw + LANES].reshape(nbatch * cap, LANES)
    lane = lax.broadcasted_iota(jnp.int32, aff.shape, 1)
    gate = jnp.sum(jnp.where(lane == first_expert + pl.program_id(0), aff, 0.0), axis=-1, keepdims=True)
    y = (acc * gate).reshape(nbatch, cap, d)
    for i in range(nbatch):
        yi = (y[i] * mod_ref[i, 5:6, :]).astype(BF16)
        hi = pltpu.bitcast(yi[:, 0:dw].astype(F32), jnp.uint32)
        lo = pltpu.bitcast(yi[:, dw:].astype(F32), jnp.uint32)
        y_ref[i] = pltpu.bitcast(hi | (lo >> 16), jnp.int32)


def _expert_ffn(xs, mod3, w_gate, w_up, w_down, first_expert, nbatch):
    B, E, cap, row_words = xs.shape
    d, f = w_gate.shape[1:]
    out_width, out_dtype = d // 2, jnp.int32
    tok = pl.BlockSpec((nbatch, None, cap, out_width), lambda e, b: (b, e, 0, 0))
    return pl.pallas_call(
        functools.partial(_ffn_kernel, first_expert=first_expert),
        grid=(E, B // nbatch),
        in_specs=[pl.BlockSpec((nbatch, None, cap, row_words), lambda e, b: (b, e, 0, 0)),
                  pl.BlockSpec((nbatch, 6, d), lambda e, b: (b, 0, 0)),
                  pl.BlockSpec((None, d, f), lambda e, b: (e + first_expert, 0, 0)),
                  pl.BlockSpec((None, d, f), lambda e, b: (e + first_expert, 0, 0)),
                  pl.BlockSpec((None, f, d), lambda e, b: (e + first_expert, 0, 0))],
        out_specs=tok,
        out_shape=jax.ShapeDtypeStruct((B, E, cap, out_width), out_dtype),
        scratch_shapes=[pltpu.VMEM((d, f), BF16), pltpu.VMEM((d, f), BF16), pltpu.VMEM((f, d), BF16)],
        compiler_params=_cparams(("arbitrary", "arbitrary")),
        name="expert_swiglu",
    )(xs, mod3, w_gate, w_up, w_down)


SC_ADD_ROWS = 32


def _combine_add(acc, y, slots, cap, first_expert, n_experts, after=()):
    L = slots.shape[1]
    half = y.shape[1]
    D = 2 * half
    assert slots.shape[0] // N_EXPERTS == SC_CORES * SC_SUBCORES
    n_win = cap // SC_ADD_ROWS
    mesh = plsc.VectorSubcoreMesh(core_axis_name="c", subcore_axis_name="s",
                                  num_cores=SC_CORES, num_subcores=SC_SUBCORES)

    def body(acc_hbm, y_hbm, slots_hbm, *rest):
        slot_v, *scratch = rest[len(after):]
        idx_v = scratch[:n_win]
        y_v = scratch[n_win:n_win + 2]
        o_v = scratch[n_win + 2:n_win + 4]
        sem_y, sem_g, sem_s = (scratch[n_win + 4 + 2 * k:n_win + 6 + 2 * k] for k in range(3))
        batch = lax.axis_index("s") * SC_CORES + lax.axis_index("c")

        @pl.loop(0, n_experts)
        def _(el):
            pltpu.sync_copy(slots_hbm.at[batch * N_EXPERTS + first_expert + el], slot_v)

            @pl.loop(0, L // SC_LANES)
            def _(i):
                v = slot_v[pl.ds(i * SC_LANES, SC_LANES)]
                tok = lax.iota(jnp.int32, SC_LANES) + (i * SC_LANES + batch * L)
                for w in range(n_win):
                    in_win = (v >= w * SC_ADD_ROWS) & (v < (w + 1) * SC_ADD_ROWS)
                    plsc.store_scatter(idx_v[w], [v - w * SC_ADD_ROWS], tok, mask=in_win)

            row0 = (batch * n_experts + el) * cap

            def fetch(w):
                b = w % 2
                return (pltpu.async_copy(y_hbm.at[pl.ds(row0 + w * SC_ADD_ROWS, SC_ADD_ROWS)], y_v[b], sem_y[b]),
                        pltpu.async_copy(acc_hbm.at[idx_v[w]], o_v[b], sem_g[b]))

            loads = fetch(0)
            stores = [None, None]
            for w in range(n_win):
                b = w % 2
                nxt = None
                if w + 1 < n_win:
                    if stores[1 - b] is not None:
                        stores[1 - b].wait()
                        stores[1 - b] = None
                    nxt = fetch(w + 1)
                loads[0].wait()
                loads[1].wait()

                @plsc.parallel_loop(0, SC_ADD_ROWS, unroll=2)
                def _(r):
                    for c in range(half // SC_LANES):
                        words = y_v[b][r, pl.ds(c * SC_LANES, SC_LANES)]
                        hi = lax.bitcast_convert_type(words & jnp.int32(-65536), F32)
                        lo = lax.bitcast_convert_type(words << 16, F32)
                        plsc.addupdate(o_v[b].at[r, pl.ds(c * SC_LANES, SC_LANES)], hi)
                        plsc.addupdate(o_v[b].at[r, pl.ds(half + c * SC_LANES, SC_LANES)], lo)

                stores[b] = pltpu.async_copy(o_v[b], acc_hbm.at[idx_v[w]], sem_s[b])
                loads = nxt
            for st in stores:
                if st is not None:
                    st.wait()

    pl.kernel(
        body,
        out_type=(),
        mesh=mesh,
        scratch_types=[pltpu.VMEM((L,), jnp.int32)]
        + [pltpu.VMEM((SC_ADD_ROWS,), jnp.int32) for _ in range(n_win)]
        + [pltpu.VMEM((SC_ADD_ROWS, half), jnp.int32) for _ in range(2)]
        + [pltpu.VMEM((SC_ADD_ROWS, D), F32) for _ in range(2)]
        + [pltpu.SemaphoreType.DMA for _ in range(6)],
        compiler_params=pltpu.CompilerParams(needs_layout_passes=False),
        name="moe_combine_row_add",
    )(acc, y, slots, *after)


def _rope_tables(L):
    inv = ROPE_BASE ** (-jnp.arange(ROPE_FREQS, dtype=F32) / ROPE_FREQS)
    pos = jnp.arange(L)
    row = (pos // GRID_W).astype(F32)[:, None] * inv
    col = (pos % GRID_W).astype(F32)[:, None] * inv
    cos = jnp.concatenate([jnp.cos(row), jnp.cos(row), jnp.cos(col), jnp.cos(col)], axis=1)
    sin = jnp.concatenate([-jnp.sin(row), jnp.sin(row), -jnp.sin(col), jnp.sin(col)], axis=1)
    return jnp.tile(cos, (1, 2)), jnp.tile(sin, (1, 2))


def _head_mean_matrix(n):
    idx = np.arange(n) // HEAD_DIM
    return jnp.asarray((idx[:, None] == idx[None, :]).astype(np.float32) / HEAD_DIM, dtype=BF16)


def kernel(x, c, ctx, c_ctx, w_mod, b_mod, norm1_g, w_in, q_norm_g, k_norm_g, attn_sink,
           w_decay_fwd, b_decay_fwd, w_decay_bwd, b_decay_bwd, gla_norm_g, w_out, norm2_g,
           w_router, w_e_gate, w_e_up, w_e_down):
    B, L, D = x.shape
    cap = CAPACITY_FACTOR * L // N_EXPERTS
    layer = 0

    rows = ((B + 1 + 7) // 8) * 8
    cc = jnp.concatenate([c, c_ctx[None, :], jnp.zeros((rows - B - 1, D), F32)], axis=0)
    mod_all = _modulation(cc, w_mod[layer], b_mod[layer])
    mod3 = mod_all[:B].reshape(B, 6, D)
    modc = mod_all[B].reshape(6, D)

    w = w_in[layer]
    o = np.cumsum([0, ATTN_WIDTH, KV_WIDTH, KV_WIDTH, GLA_QK_WIDTH, GLA_QK_WIDTH,
                   GLA_WIDTH, GLA_WIDTH, GATE_RANK, GATE_RANK])
    w_lr = jnp.concatenate([w[:, o[7]:o[9]]] * 3 + [jnp.zeros((D, LANES - 6 * GATE_RANK), F32)], axis=1)
    head_order = np.arange(N_Q_HEADS).reshape(N_KV_HEADS, -1).T.reshape(-1)
    attn_perm = (head_order[:, None] * HEAD_DIM + np.arange(HEAD_DIM)[None, :]).reshape(-1)
    w_in_r = jnp.concatenate([w[:, attn_perm], w[:, o[1]:o[7]], w_lr], axis=1).astype(BF16)
    w_out_r = jnp.concatenate([w_out[layer][attn_perm], w_out[layer][ATTN_WIDTH:]], axis=0).astype(BF16)
    w_ctx = jnp.concatenate([w[:, o[1]:o[3]], w[:, o[4]:o[6]], w_lr], axis=1).astype(BF16)
    wd2 = jnp.zeros((2 * GATE_RANK, 2 * GLA_QK_WIDTH), F32)
    wd2 = wd2.at[0:GATE_RANK, 0:GLA_QK_WIDTH].set(w_decay_fwd[layer])
    wd2 = wd2.at[GATE_RANK:, GLA_QK_WIDTH:].set(w_decay_bwd[layer])
    wd_hi = wd2.astype(BF16)
    wd_lo = (wd2 - wd_hi.astype(F32)).astype(BF16)
    wd = jnp.concatenate([wd_hi, wd_hi, wd_lo,
                          jnp.zeros((LANES - 6 * GATE_RANK, 2 * GLA_QK_WIDTH), BF16)], axis=0)
    bdec = jnp.concatenate([b_decay_fwd[layer], b_decay_bwd[layer]])[None, :]
    g1 = norm1_g[layer][None, :]
    g2 = norm2_g[layer][None, :]
    qg = jnp.tile(q_norm_g[layer], N_Q_HEADS)[None, :]
    kg2 = jnp.tile(k_norm_g[layer], N_KV_HEADS)[None, :]
    gn = jnp.tile(gla_norm_g[layer], GLA_HEADS)[None, :]
    bd512 = _head_mean_matrix(ATTN_WIDTH)
    bd128 = _head_mean_matrix(KV_WIDTH)
    cos_t, sin_t = _rope_tables(L)
    w_router_t = w_router[layer].T.astype(BF16)

    kc, vc, s_f, s_b = _context_side(ctx, modc, g1, w_ctx, kg2, bd128, wd, bdec)
    q, k, v, gq, gk, gv, gg, lr = _input_projection(
        x, mod3, g1, w_in_r, qg, kg2, bd512, cos_t, sin_t, tm=1024)
    attn = _window_attention(attn_sink[layer], q, k, v, kc, vc, tq=512)
    gla = _gla(gq, gk, gv, gg, lr, wd, bdec, gn, s_f, s_b)
    x1, h2, afft = _output_projection(attn, gla, x, mod3, w_out_r, g2, w_router_t, tm=1024)
    post = _expert_choice(afft, cap, nbatch=4)
    n_groups = 4
    per_group = N_EXPERTS // n_groups
    rows = h2.reshape(B * L, ROW_WORDS)
    slots = post.reshape(B * N_EXPERTS, L)
    stream = jax.new_ref(x1.reshape(B * L, D))
    gathered = [_dispatch(rows, slots, cap, g * per_group, per_group) for g in range(n_groups)]
    for g in range(n_groups):
        xs = gathered[g].reshape(B, per_group, cap, ROW_WORDS)
        y = _expert_ffn(xs, mod3, w_e_gate[layer], w_e_up[layer], w_e_down[layer], g * per_group, nbatch=4)
        _combine_add(stream, y.reshape(B * per_group * cap, D // 2), slots, cap, g * per_group, per_group,
                     after=gathered[g + 1:])
    return jax.freeze(stream).reshape(B, L, D)
```

```python
import functools

import jax
import jax.numpy as jnp
import numpy as np
from jax import lax
from jax.experimental import pallas as pl
from jax.experimental.pallas import tpu as pltpu
from jax.experimental.pallas import tpu_sc as plsc

D_MODEL = 1024
GRID_W = 64
HEAD_DIM = 64
N_Q_HEADS = 8
N_KV_HEADS = 2
BLOCK = 128
ROPE_FREQS = 16
ROPE_BASE = 10000.0
GLA_HEADS = 4
GLA_DV = 128
GLA_DK = 64
GATE_RANK = 16
GATE_NORMALIZER = 16.0
CHUNK = 64
N_EXPERTS = 16
CAPACITY_FACTOR = 2
ATTN_WIDTH = 512
KV_WIDTH = 128
GLA_QK_WIDTH = 256
GLA_WIDTH = 512
EPS = 1e-6
NEG_INF = -1e30
LOG2E = 1.4426950408889634

LANES = 128
ROW_WORDS = D_MODEL // 2 + LANES
VMEM_LIMIT = 56 * 1024 * 1024

F32 = jnp.float32
BF16 = jnp.bfloat16
HI = lax.Precision.HIGHEST


def _cparams(sem):
    return pltpu.CompilerParams(dimension_semantics=sem, vmem_limit_bytes=VMEM_LIMIT)


def _dot(a, b):
    return jnp.dot(a, b, preferred_element_type=F32)


def _dot_hi(a, b):
    return jnp.dot(a, b, preferred_element_type=F32, precision=HI)


def _dot_nt(a, b):
    return lax.dot_general(a, b, (((1,), (1,)), ((), ())), preferred_element_type=F32)


def _dot_tn(a, b, precision=None):
    return lax.dot_general(a, b, (((0,), (0,)), ((), ())), preferred_element_type=F32,
                           precision=precision)


def _split2(t):
    hi = t.astype(BF16)
    lo = (t - hi.astype(F32)).astype(BF16)
    return hi, lo


def _rms_mod(t, g, shift, scale):
    y = t * lax.rsqrt(jnp.mean(t * t, axis=-1, keepdims=True) + EPS)
    return (y * g) * (1.0 + scale) + shift


def _log_decay(lr, wd3, bias):
    hi = lr.astype(BF16)
    lo = (lr - hi.astype(F32)).astype(BF16)
    lane = lax.broadcasted_iota(jnp.int32, lr.shape, 1)
    second = (lane >= 2 * GATE_RANK) & (lane < 4 * GATE_RANK)
    z = _dot(jnp.where(second, lo, hi), wd3) + bias
    return (jnp.minimum(z, 0.0) - jnp.log(1.0 + jnp.exp(-jnp.abs(z)))) * (1.0 / GATE_NORMALIZER)


def _lane_lo(shape):
    return (lax.broadcasted_iota(jnp.int32, shape, len(shape) - 1) % LANES) < HEAD_DIM


def _mod_kernel(c_ref, w_ref, b_ref, o_ref):
    c = c_ref[...]
    s = c * jax.nn.sigmoid(c)
    o_ref[...] = _dot_hi(s, w_ref[...]) + b_ref[...]


def _modulation(cc, w_mod, b_mod):
    m = cc.shape[0]
    n = w_mod.shape[1]
    tn = 1024
    return pl.pallas_call(
        _mod_kernel,
        grid=(n // tn,),
        in_specs=[pl.BlockSpec((m, D_MODEL), lambda j: (0, 0)),
                  pl.BlockSpec((D_MODEL, tn), lambda j: (0, j)),
                  pl.BlockSpec((1, tn), lambda j: (0, j))],
        out_specs=pl.BlockSpec((m, tn), lambda j: (0, j)),
        out_shape=jax.ShapeDtypeStruct((m, n), F32),
        compiler_params=_cparams(("arbitrary",)),
        name="adaln_mod",
    )(cc, w_mod, b_mod.reshape(1, n))


def _ctx_kernel(ctx_ref, mod_ref, g1_ref, w_ref, kg_ref, bd_ref, wd_ref, bdec_ref,
                kc_ref, vc_ref, sf_ref, sb_ref):
    n = ctx_ref.shape[0]
    h = _rms_mod(ctx_ref[...], g1_ref[...], mod_ref[0:1, :], mod_ref[1:2, :]).astype(BF16)
    pc = _dot(h, w_ref[...])
    ak = pc[:, 0:128]
    av = pc[:, 128:256]
    gk = pc[:, 256:512]
    gv = pc[:, 512:1024].astype(BF16)
    lr = pc[:, 1024:1152]
    sq_hi, sq_lo = _split2(ak * ak)
    ms = _dot(sq_hi, bd_ref[...]) + _dot(sq_lo, bd_ref[...])
    kn = ak * lax.rsqrt(ms + EPS) * kg_ref[...]
    kc_ref[...] = kn.astype(BF16)
    vc_ref[...] = av.astype(BF16)
    la = _log_decay(lr, wd_ref[...], bdec_ref[...])
    r = lax.broadcasted_iota(jnp.int32, (n, n), 0)
    cidx = lax.broadcasted_iota(jnp.int32, (n, n), 1)
    after = (cidx > r).astype(F32)
    before = (cidx < r).astype(F32)
    w_f = jnp.exp(_dot_hi(after, la[:, 0:256]))
    w_b = jnp.exp(_dot_hi(before, la[:, 256:512]))
    lo = _lane_lo((n, LANES))
    for w, out in ((w_f, sf_ref), (w_b, sb_ref)):
        kw = gk * w
        for c in range(2):
            kwc = kw[:, c * LANES:(c + 1) * LANES]
            k_lo = jnp.where(lo, kwc, 0.0).astype(BF16)
            k_hi = jnp.where(lo, 0.0, kwc).astype(BF16)
            v0 = gv[:, (2 * c) * GLA_DV:(2 * c + 1) * GLA_DV]
            v1 = gv[:, (2 * c + 1) * GLA_DV:(2 * c + 2) * GLA_DV]
            out[c] = _dot_tn(v0, k_lo) + _dot_tn(v1, k_hi)


def _context_side(ctx, modc, g1, w_ctx, kg2, bd128, wd, bdec):
    B, n, _ = ctx.shape
    full = lambda shape: pl.BlockSpec(shape, lambda b: (0,) * len(shape))
    kv_spec = pl.BlockSpec((None, n, KV_WIDTH), lambda b: (b, 0, 0))
    st_spec = pl.BlockSpec((None, 2, LANES, GLA_DV), lambda b: (b, 0, 0, 0))
    kv_shape = jax.ShapeDtypeStruct((B, n, KV_WIDTH), BF16)
    st_shape = jax.ShapeDtypeStruct((B, 2, LANES, GLA_DV), F32)
    return pl.pallas_call(
        _ctx_kernel,
        grid=(B,),
        in_specs=[pl.BlockSpec((None, n, D_MODEL), lambda b: (b, 0, 0)),
                  full(modc.shape), full(g1.shape), full(w_ctx.shape), full(kg2.shape),
                  full(bd128.shape), full(wd.shape), full(bdec.shape)],
        out_specs=[kv_spec, kv_spec, st_spec, st_spec],
        out_shape=[kv_shape, kv_shape, st_shape, st_shape],
        compiler_params=_cparams(("arbitrary",)),
        name="context_side",
    )(ctx, modc, g1, w_ctx, kg2, bd128, wd, bdec)


def _swap16(t):
    n = t.shape[1]
    first = (lax.broadcasted_iota(jnp.int32, t.shape, 1) % 32) < ROPE_FREQS
    return jnp.where(first, pltpu.roll(t, n - ROPE_FREQS, 1), pltpu.roll(t, ROPE_FREQS, 1))


def _inproj_kernel(x_ref, mod_ref, g1_ref, w_ref, qg_ref, kg_ref, bd_ref, cos_ref, sin_ref,
                   q_ref, k_ref, v_ref, gq_ref, gk_ref, gv_ref, gg_ref, lr_ref):
    h = _rms_mod(x_ref[...], g1_ref[...], mod_ref[0:1, :], mod_ref[1:2, :]).astype(BF16)
    cos = cos_ref[...]
    sin = sin_ref[...]

    def head_norm_rope(t, g, bd, reps):
        ms = _dot((t * t).astype(BF16), bd)
        tn = t * lax.rsqrt(ms + EPS) * g
        c = jnp.concatenate([cos] * reps, axis=1) if reps > 1 else cos
        s = jnp.concatenate([sin] * reps, axis=1) if reps > 1 else sin
        return tn * c + _swap16(tn) * s

    aq = _dot(h, w_ref[:, 0:512])
    q = head_norm_rope(aq, qg_ref[...], bd_ref[...], 4) * (HEAD_DIM ** -0.5 * LOG2E)
    q_ref[...] = q.astype(BF16)
    akv = _dot(h, w_ref[:, 512:768])
    k = head_norm_rope(akv[:, 0:128], kg_ref[...], bd_ref[0:128, 0:128], 1)
    k_ref[...] = k.astype(BF16)
    v_ref[...] = akv[:, 128:256].astype(BF16)
    gqk = _dot(h, w_ref[:, 768:1280])
    gq_ref[...] = (gqk[:, 0:256] * (GLA_DK ** -0.5)).astype(BF16)
    gk_ref[...] = gqk[:, 256:512].astype(BF16)
    gv_ref[...] = _dot(h, w_ref[:, 1280:1792]).astype(BF16)
    gg_ref[...] = _dot(h, w_ref[:, 1792:2304]).astype(BF16)
    lr_ref[...] = _dot(h, w_ref[:, 2304:2432])


def _input_projection(x, mod3, g1, w_in_r, qg, kg2, bd512, cos_t, sin_t, tm):
    B, L, _ = x.shape
    full = lambda shape: pl.BlockSpec(shape, lambda b, i: (0,) * len(shape))
    tok = lambda w: pl.BlockSpec((None, tm, w), lambda b, i: (b, i, 0))
    widths = (ATTN_WIDTH, KV_WIDTH, KV_WIDTH, GLA_QK_WIDTH, GLA_QK_WIDTH, GLA_WIDTH, GLA_WIDTH, LANES)
    dtypes = (BF16,) * 7 + (F32,)
    return pl.pallas_call(
        _inproj_kernel,
        grid=(B, L // tm),
        in_specs=[tok(D_MODEL),
                  pl.BlockSpec((None, 6, D_MODEL), lambda b, i: (b, 0, 0)),
                  full(g1.shape), full(w_in_r.shape), full(qg.shape), full(kg2.shape),
                  full(bd512.shape),
                  pl.BlockSpec((tm, LANES), lambda b, i: (i, 0)),
                  pl.BlockSpec((tm, LANES), lambda b, i: (i, 0))],
        out_specs=[tok(w) for w in widths],
        out_shape=[jax.ShapeDtypeStruct((B, L, w), dt) for w, dt in zip(widths, dtypes)],
        compiler_params=_cparams(("arbitrary", "arbitrary")),
        name="input_projection",
    )(x, mod3, g1, w_in_r, qg, kg2, bd512, cos_t, sin_t)


def _attn_kernel(sink_ref, q_ref, kp_ref, ko_ref, kn_ref, vp_ref, vo_ref, vn_ref,
                 kc_ref, vc_ref, o_ref):
    i = pl.program_id(1)
    ni = pl.num_programs(1)
    nsub = q_ref.shape[0] // BLOCK
    ncol = ATTN_WIDTH // LANES
    win = 3 * BLOCK
    ucol = 4
    half_rows = ucol * BLOCK
    k_win = jnp.concatenate([kp_ref[...], ko_ref[...], kn_ref[...]], axis=0)
    v_win = jnp.concatenate([vp_ref[...], vo_ref[...], vn_ref[...]], axis=0)
    k_ctx = kc_ref[...]
    lo_w = _lane_lo(v_win.shape)
    lo_c = _lane_lo(vc_ref.shape)
    lo_q = _lane_lo((BLOCK, LANES))
    zero = jnp.zeros((), BF16)
    one = jnp.ones((), BF16)
    v0_c, v0_w = jnp.where(lo_c, vc_ref[...], one), jnp.where(lo_w, v_win, one)
    v1_c, v1_w = jnp.where(lo_c, one, vc_ref[...]), jnp.where(lo_w, one, v_win)
    qi = lax.broadcasted_iota(jnp.int32, (half_rows, BLOCK), 0) % BLOCK
    kj = lax.broadcasted_iota(jnp.int32, (half_rows, BLOCK), 1)
    no_prev = jnp.where(i > 0, 0, BLOCK)
    no_next = jnp.where(i < ni - 1, 0, BLOCK)
    row_head = lax.broadcasted_iota(jnp.int32, (half_rows, 1), 0) // BLOCK
    lo_o = _lane_lo((half_rows, LANES))
    for t in range(nsub):
        rows = slice(t * BLOCK, (t + 1) * BLOCK)
        keys = slice(t * BLOCK, t * BLOCK + win)
        cols = [q_ref[rows, c * LANES:(c + 1) * LANES] for c in range(ncol)]
        prev_ok = kj >= qi + (no_prev if t == 0 else 0)
        next_ok = kj <= qi - (no_next if t == nsub - 1 else 0)
        for p in range(ncol // ucol):
            outs = []
            qs = jnp.concatenate([jnp.where(lo_q, qc, zero) for qc in cols[ucol * p:ucol * (p + 1)]]
                                 + [jnp.where(lo_q, zero, qc) for qc in cols[ucol * p:ucol * (p + 1)]], axis=0)
            s_c_all = _dot_nt(qs, k_ctx)
            s_w_all = _dot_nt(qs, k_win[keys])
            for g, (vv_c, vv_w) in enumerate(((v0_c, v0_w), (v1_c, v1_w))):
                head = g * ncol + ucol * p
                sink_g = jnp.full((half_rows, 1), sink_ref[head + ucol - 1], F32)
                for j in range(ucol - 2, -1, -1):
                    sink_g = jnp.where(row_head <= j, sink_ref[head + j], sink_g)
                sink_g = sink_g * LOG2E
                s_c = s_c_all[g * half_rows:(g + 1) * half_rows]
                s_w = s_w_all[g * half_rows:(g + 1) * half_rows]
                s_p = jnp.where(prev_ok, s_w[:, 0:BLOCK], NEG_INF)
                s_o = s_w[:, BLOCK:2 * BLOCK]
                s_n = jnp.where(next_ok, s_w[:, 2 * BLOCK:win], NEG_INF)
                m = jnp.maximum(jnp.maximum(jnp.max(s_c, axis=-1, keepdims=True),
                                            jnp.max(jnp.maximum(jnp.maximum(s_p, s_o), s_n),
                                                    axis=-1, keepdims=True)), sink_g)
                e_c = jnp.exp2(s_c - m).astype(BF16)
                e_w = jnp.concatenate([jnp.exp2(s_p - m), jnp.exp2(s_o - m), jnp.exp2(s_n - m)],
                                      axis=1).astype(BF16)
                acc = _dot(e_c, vv_c) + _dot(e_w, vv_w[keys])
                outs.append(acc / (pltpu.roll(acc, HEAD_DIM, 1) + jnp.exp2(sink_g - m)))
            o = jnp.where(lo_o, outs[0], outs[1]).astype(BF16)
            for j in range(ucol):
                c = ucol * p + j
                o_ref[rows, c * LANES:(c + 1) * LANES] = o[j * BLOCK:(j + 1) * BLOCK]


def _window_attention(sink, q, k, v, kc, vc, tq):
    B, L, _ = q.shape
    nb = L // BLOCK
    nsub = tq // BLOCK
    n_ctx = kc.shape[1]
    prev = pl.BlockSpec((None, BLOCK, KV_WIDTH), lambda b, n: (b, jnp.maximum(n * nsub - 1, 0), 0))
    own = pl.BlockSpec((None, tq, KV_WIDTH), lambda b, n: (b, n, 0))
    nxt = pl.BlockSpec((None, BLOCK, KV_WIDTH),
                       lambda b, n: (b, jnp.minimum((n + 1) * nsub, nb - 1), 0))
    cspec = pl.BlockSpec((None, n_ctx, KV_WIDTH), lambda b, n: (b, 0, 0))
    return pl.pallas_call(
        _attn_kernel,
        grid=(B, L // tq),
        in_specs=[pl.BlockSpec(memory_space=pltpu.SMEM),
                  pl.BlockSpec((None, tq, ATTN_WIDTH), lambda b, n: (b, n, 0)),
                  prev, own, nxt, prev, own, nxt, cspec, cspec],
        out_specs=pl.BlockSpec((None, tq, ATTN_WIDTH), lambda b, n: (b, n, 0)),
        out_shape=jax.ShapeDtypeStruct((B, L, ATTN_WIDTH), BF16),
        compiler_params=_cparams(("arbitrary", "arbitrary")),
        name="window_attention",
    )(sink, q, k, k, k, v, v, v, kc, vc)


SUPER = 256
CH_PER = SUPER // CHUNK
HALF = 128


def _dot_split(m, parts):
    return _dot(m, parts[0]) + _dot(m, parts[1])


def _gla_kernel(gq_ref, gk_ref, gv_ref, gg_ref, lr_ref, wd_ref, bdec_ref, gn_ref, sf_ref, sb_ref,
                o_ref, la_ref, oi_ref, qg_ref, kv_ref, sb16_ref, dec_ref, st_ref):
    L = gq_ref.shape[0]
    nsuper = L // SUPER
    nchunk = L // CHUNK
    half = CHUNK // 2
    la_ref[...] = _log_decay(lr_ref[...], wd_ref[...], bdec_ref[...])

    r = lax.broadcasted_iota(jnp.int32, (SUPER, SUPER), 0)
    cidx = lax.broadcasted_iota(jnp.int32, (SUPER, SUPER), 1)
    same = (r // CHUNK) == (cidx // CHUNK)
    pr = r % CHUNK
    pc = cidx % CHUNK
    one = jnp.float32(1.0)
    zero = jnp.float32(0.0)
    in_f = jnp.where(pc <= pr, one, zero)
    in_b = jnp.where(pc >= pr, one, zero)
    ref_f = jnp.where(pc < half, one, zero)
    ref_b = jnp.where(pc >= half, one, zero)
    m1_f = jnp.where(same, in_f - ref_f, zero).astype(BF16)
    m1_b = jnp.where(same, in_b - ref_b, zero).astype(BF16)
    rh = lax.broadcasted_iota(jnp.int32, (HALF, 2 * HALF), 0)
    ch = lax.broadcasted_iota(jnp.int32, (HALF, 2 * HALF), 1) % HALF
    same_h = (rh // CHUNK) == (ch // CHUNK)
    mask_f = jnp.where(same_h, jnp.where(ch % CHUNK <= rh % CHUNK, one, zero), zero) > 0.5
    mask_b = jnp.where(same_h, jnp.where(ch % CHUNK >= rh % CHUNK, one, zero), zero) > 0.5
    rr = lax.broadcasted_iota(jnp.int32, (2 * CH_PER, SUPER), 0)
    rc = lax.broadcasted_iota(jnp.int32, (2 * CH_PER, SUPER), 1)
    in_chunk = jnp.where((rc // CHUNK) == (rr % CH_PER), one, zero)
    first = jnp.where((rc % CHUNK) < half, 1, 0)
    is_tot = jnp.where(rr >= CH_PER, 1, 0)
    rs_f = (in_chunk * jnp.where(first != is_tot, one, zero)).astype(BF16)
    rs_b = (in_chunk * jnp.where(first == is_tot, one, zero)).astype(BF16)
    lo_h = _lane_lo((HALF, LANES))
    zero_blk = jnp.zeros((HALF, GLA_DV), BF16)

    def phase1(s, carry):
        r0 = pl.multiple_of(s * SUPER, SUPER)
        rows = pl.ds(r0, SUPER)
        q = gq_ref[rows, :].astype(F32)
        k = gk_ref[rows, :].astype(F32)
        qes, kes, kds = [], [], []
        for d, (m1, rs) in enumerate(((m1_f, rs_f), (m1_b, rs_b))):
            parts = _split2(la_ref[rows, d * GLA_QK_WIDTH:(d + 1) * GLA_QK_WIDTH])
            x1 = _dot_split(m1, parts)
            erow = jnp.exp(_dot_split(rs, parts))
            dec = erow[0:CH_PER] * erow[CH_PER:2 * CH_PER]
            dec_ref[d, s] = jnp.concatenate([dec, dec], axis=0)
            qe = q * jnp.exp(x1)
            ke = k * jnp.exp(-x1)
            qg_parts, kd_parts = [], []
            for j in range(CH_PER):
                rj = slice(j * CHUNK, (j + 1) * CHUNK)
                qg_parts.append(qe[rj] * erow[j:j + 1])
                kd_parts.append(ke[rj] * erow[CH_PER + j:CH_PER + j + 1])
            qg_ref[d, rows, :] = jnp.concatenate(qg_parts, axis=0).astype(BF16)
            qes.append(qe.astype(BF16))
            kes.append(ke)
            kds.append(jnp.concatenate(kd_parts, axis=0).astype(BF16))
        for c in range(2):
            cl = slice(c * LANES, (c + 1) * LANES)
            vpair = gv_ref[rows, 2 * c * GLA_DV:(2 * c + 2) * GLA_DV]
            for blk in range(SUPER // HALF):
                rb = slice(blk * HALF, (blk + 1) * HALF)
                vbd = jnp.concatenate(
                    [jnp.concatenate([vpair[rb, 0:GLA_DV], zero_blk], axis=1),
                     jnp.concatenate([zero_blk, vpair[rb, GLA_DV:2 * GLA_DV]], axis=1)], axis=0)
                o2 = None
                for d, mask in enumerate((mask_f, mask_b)):
                    ke_cb = kes[d][rb, cl]
                    ke_st = jnp.concatenate([jnp.where(lo_h, ke_cb, zero),
                                             jnp.where(lo_h, zero, ke_cb)], axis=0).astype(BF16)
                    a = _dot_nt(qes[d][rb, cl], ke_st)
                    o = _dot(jnp.where(mask, a, zero).astype(BF16), vbd)
                    o2 = o if o2 is None else o2 + o
                oi_ref[pl.ds(r0 + blk * HALF, HALF), 2 * c * GLA_DV:(2 * c + 2) * GLA_DV] = o2
            for d in range(2):
                for j in range(CH_PER):
                    rj = slice(j * CHUNK, (j + 1) * CHUNK)
                    t = _dot_tn(kds[d][rj, cl], vpair[rj])
                    kv = jnp.concatenate([t[0:GLA_DK, 0:GLA_DV], t[GLA_DK:, GLA_DV:]], axis=0)
                    kv_ref[d, c, s * CH_PER + j] = kv.T
        return carry

    lax.fori_loop(0, nsuper, phase1, 0)

    st_ref[0] = sf_ref[0]
    st_ref[1] = sf_ref[1]
    st_ref[2] = sb_ref[0]
    st_ref[3] = sb_ref[1]

    def phase2(n, carry):
        for d in range(2):
            idx = n if d == 0 else nchunk - 1 - n
            dec = dec_ref[d, idx // CH_PER, pl.ds(idx % CH_PER, 1), :]
            for c in range(2):
                st = st_ref[2 * d + c]
                sb16_ref[c, idx, :, d * LANES:(d + 1) * LANES] = st.astype(BF16)
                st_ref[2 * d + c] = dec[:, c * LANES:(c + 1) * LANES] * st + kv_ref[d, c, idx]
        return carry

    lax.fori_loop(0, nchunk, phase2, 0)

    lo64 = _lane_lo((CHUNK, 2 * LANES))
    zero_b = jnp.zeros((), BF16)

    def phase3(s, carry):
        r0 = pl.multiple_of(s * SUPER, SUPER)
        rows = pl.ds(r0, SUPER)
        inter = [[None] * CH_PER for _ in range(GLA_HEADS)]
        for j in range(CH_PER):
            rj = pl.ds(r0 + j * CHUNK, CHUNK)
            for c in range(2):
                qg_c = jnp.concatenate([qg_ref[d, rj, c * LANES:(c + 1) * LANES] for d in range(2)], axis=1)
                lhs = jnp.concatenate([jnp.where(lo64, qg_c, zero_b),
                                       jnp.where(lo64, zero_b, qg_c)], axis=0)
                t = _dot_nt(lhs, sb16_ref[c, s * CH_PER + j])
                for hh in range(2):
                    inter[2 * c + hh][j] = t[hh * CHUNK:(hh + 1) * CHUNK]
        for hd in range(GLA_HEADS):
            cl = slice(hd * GLA_DV, (hd + 1) * GLA_DV)
            o = oi_ref[rows, cl] + jnp.concatenate(inter[hd], axis=0)
            y = o * lax.rsqrt(jnp.mean(o * o, axis=-1, keepdims=True) + EPS) * gn_ref[:, cl]
            g = gg_ref[rows, cl].astype(F32)
            o_ref[rows, cl] = (y * (g * jax.nn.sigmoid(g))).astype(BF16)
        return carry

    lax.fori_loop(0, nsuper, phase3, 0)


def _gla(gq, gk, gv, gg, lr, wd, bdec, gn, s_f, s_b):
    B, L, _ = gq.shape
    nchunk = L // CHUNK
    full = lambda shape: pl.BlockSpec(shape, lambda b: (0,) * len(shape))
    tok = lambda w: pl.BlockSpec((None, L, w), lambda b: (b, 0, 0))
    st_spec = pl.BlockSpec((None, 2, LANES, GLA_DV), lambda b: (b, 0, 0, 0))
    return pl.pallas_call(
        _gla_kernel,
        grid=(B,),
        in_specs=[tok(GLA_QK_WIDTH), tok(GLA_QK_WIDTH), tok(GLA_WIDTH), tok(GLA_WIDTH), tok(LANES),
                  full(wd.shape), full(bdec.shape), full(gn.shape), st_spec, st_spec],
        out_specs=tok(GLA_WIDTH),
        out_shape=jax.ShapeDtypeStruct((B, L, GLA_WIDTH), BF16),
        scratch_shapes=[pltpu.VMEM((L, 2 * GLA_QK_WIDTH), F32),
                        pltpu.VMEM((L, GLA_WIDTH), F32),
                        pltpu.VMEM((2, L, GLA_QK_WIDTH), BF16),
                        pltpu.VMEM((2, 2, nchunk, GLA_DV, LANES), F32),
                        pltpu.VMEM((2, nchunk, GLA_DV, 2 * LANES), BF16),
                        pltpu.VMEM((2, L // SUPER, 2 * CH_PER, GLA_QK_WIDTH), F32),
                        pltpu.VMEM((4, GLA_DV, LANES), F32)],
        compiler_params=_cparams(("arbitrary",)),
        name="gla_bidirectional",
    )(gq, gk, gv, gg, lr, wd, bdec, gn, s_f, s_b)


def _outproj_kernel(attn_ref, gla_ref, x_ref, mod_ref, w_ref, g2_ref, wr_ref,
                    x1_ref, h2_ref, afft_ref):
    y = _dot(attn_ref[...], w_ref[0:ATTN_WIDTH, :]) + _dot(gla_ref[...], w_ref[ATTN_WIDTH:, :])
    x1 = x_ref[...] + mod_ref[2:3, :] * y
    x1_ref[...] = x1
    h2 = _rms_mod(x1, g2_ref[...], mod_ref[3:4, :], mod_ref[4:5, :]).astype(BF16)
    half = D_MODEL // 2
    hi = pltpu.bitcast(h2[:, 0:half].astype(F32), jnp.uint32)
    lo = pltpu.bitcast(h2[:, half:].astype(F32), jnp.uint32)
    h2_ref[:, 0:half] = pltpu.bitcast(hi | (lo >> 16), F32)
    logits = _dot_nt(wr_ref[...], h2)
    e = jnp.exp(logits - jnp.max(logits, axis=0, keepdims=True))
    afft = e / jnp.sum(e, axis=0, keepdims=True)
    afft_ref[...] = afft
    pad = jnp.zeros((LANES - N_EXPERTS, afft.shape[1]), F32)
    h2_ref[:, half:half + LANES] = jnp.concatenate([afft, pad], axis=0).T


def _output_projection(attn, gla, x, mod3, w_out, g2, w_router, tm):
    B, L, _ = x.shape
    full = lambda shape: pl.BlockSpec(shape, lambda b, i: (0,) * len(shape))
    tok = lambda w: pl.BlockSpec((None, tm, w), lambda b, i: (b, i, 0))
    return pl.pallas_call(
        _outproj_kernel,
        grid=(B, L // tm),
        in_specs=[tok(ATTN_WIDTH), tok(GLA_WIDTH), tok(D_MODEL),
                  pl.BlockSpec((None, 6, D_MODEL), lambda b, i: (b, 0, 0)),
                  full(w_out.shape), full(g2.shape), full(w_router.shape)],
        out_specs=[tok(D_MODEL), tok(ROW_WORDS),
                   pl.BlockSpec((None, N_EXPERTS, tm), lambda b, i: (b, 0, i))],
        out_shape=[jax.ShapeDtypeStruct((B, L, D_MODEL), F32),
                   jax.ShapeDtypeStruct((B, L, ROW_WORDS), F32),
                   jax.ShapeDtypeStruct((B, N_EXPERTS, L), F32)],
        compiler_params=_cparams(("arbitrary", "arbitrary")),
        name="output_projection_router",
    )(attn, gla, x, mod3, w_out, g2, w_router)


def _topk_kernel(afft_ref, post_ref, *, cap):
    nbatch, E, L = afft_ref.shape
    aff = afft_ref[...].reshape(nbatch * E, L)
    E = nbatch * E

    def search(i, thr):
        cand = thr | jnp.left_shift(jnp.int32(1), 30 - i)
        cnt = jnp.sum(jnp.where(aff >= pltpu.bitcast(cand, F32), 1.0, 0.0), axis=-1, keepdims=True)
        return jnp.where(cnt >= cap, cand, thr)

    thr_bits = lax.fori_loop(0, 31, search, jnp.zeros((E, 1), jnp.int32))
    thr = pltpu.bitcast(thr_bits, F32)
    above = aff > thr
    tie = aff == thr
    need = cap - jnp.sum(jnp.where(above, 1.0, 0.0), axis=-1, keepdims=True)

    upper = (lax.broadcasted_iota(jnp.int32, (LANES, LANES), 0)
             <= lax.broadcasted_iota(jnp.int32, (LANES, LANES), 1)).astype(BF16)

    def prefix(mask):
        parts = []
        run = jnp.zeros((E, 1), F32)
        for j in range(L // LANES):
            blk = jnp.where(mask[:, j * LANES:(j + 1) * LANES], 1.0, 0.0).astype(BF16)
            loc = _dot(blk, upper) + run
            parts.append(loc)
            run = loc[:, LANES - 1:LANES]
        return jnp.concatenate(parts, axis=1)

    tie_rank = prefix(tie)
    sel = above | (tie & (tie_rank <= need))
    slot = prefix(sel).astype(jnp.int32) - 1
    post_ref[...] = jnp.where(sel, slot, -1).reshape(post_ref.shape)


def _expert_choice(afft, cap, nbatch):
    B, E, L = afft.shape
    return pl.pallas_call(
        functools.partial(_topk_kernel, cap=cap),
        grid=(B // nbatch,),
        in_specs=[pl.BlockSpec((nbatch, E, L), lambda b: (b, 0, 0))],
        out_specs=pl.BlockSpec((nbatch, E, L), lambda b: (b, 0, 0)),
        out_shape=jax.ShapeDtypeStruct((B, E, L), jnp.int32),
        compiler_params=_cparams(("arbitrary",)),
        name="expert_choice_topk",
    )(afft)


SC_CORES = 2
SC_SUBCORES = 16
SC_LANES = 16
SC_WINDOW = 128


def _dispatch(rows, slots, cap, first_expert, n_experts):
    L = slots.shape[1]
    W = rows.shape[1]
    n_pair = slots.shape[0] // N_EXPERTS * n_experts
    per_worker = n_pair // (SC_CORES * SC_SUBCORES)
    n_win = cap // SC_WINDOW
    mesh = plsc.VectorSubcoreMesh(core_axis_name="c", subcore_axis_name="s",
                                  num_cores=SC_CORES, num_subcores=SC_SUBCORES)

    def body(rows_hbm, slots_hbm, out_hbm, slot_v, *scratch):
        idx_v, buf_v, sem = scratch[:n_win], scratch[n_win], scratch[n_win + 1]
        worker = lax.axis_index("s") * SC_CORES + lax.axis_index("c")

        @pl.loop(0, per_worker)
        def _(p):
            pair = worker * per_worker + p
            batch = pair // n_experts
            first_tok = batch * L
            pltpu.sync_copy(slots_hbm.at[batch * N_EXPERTS + first_expert + pair % n_experts], slot_v)

            @pl.loop(0, L // SC_LANES)
            def _(i):
                v = slot_v[pl.ds(i * SC_LANES, SC_LANES)]
                tok = lax.iota(jnp.int32, SC_LANES) + (i * SC_LANES + first_tok)
                for w in range(n_win):
                    in_win = (v >= w * SC_WINDOW) & (v < (w + 1) * SC_WINDOW)
                    plsc.store_scatter(idx_v[w], [v - w * SC_WINDOW], tok, mask=in_win)

            for w in range(n_win):
                pltpu.async_copy(rows_hbm.at[idx_v[w]], buf_v, sem).wait()
                pltpu.sync_copy(buf_v, out_hbm.at[pl.ds(pair * cap + w * SC_WINDOW, SC_WINDOW)])

    return pl.kernel(
        body,
        out_type=jax.ShapeDtypeStruct((n_pair * cap, W), rows.dtype),
        mesh=mesh,
        scratch_types=[pltpu.VMEM((L,), jnp.int32)]
        + [pltpu.VMEM((SC_WINDOW,), jnp.int32) for _ in range(n_win)]
        + [pltpu.VMEM((SC_WINDOW, W), rows.dtype), pltpu.SemaphoreType.DMA],
        compiler_params=pltpu.CompilerParams(needs_layout_passes=False),
        name="moe_dispatch_gather",
    )(rows, slots)


def _ffn_kernel(xs_ref, mod_ref, wg_ref, wu_ref, wd_ref, y_ref, wgb_ref, wub_ref, wdb_ref, *,
                first_expert):
    nbatch, cap, _ = xs_ref.shape
    d = wg_ref.shape[0]
    dw = d // 2

    @pl.when(pl.program_id(1) == 0)
    def _():
        wgb_ref[...] = wg_ref[...].astype(BF16)
        wub_ref[...] = wu_ref[...].astype(BF16)
        wdb_ref[...] = wd_ref[...].astype(BF16)

    words = pltpu.bitcast(xs_ref[:, :, 0:dw].reshape(nbatch * cap, dw), jnp.uint32)
    xs = jnp.concatenate([pltpu.bitcast(words & jnp.uint32(0xFFFF0000), F32).astype(BF16),
                          pltpu.bitcast(words << 16, F32).astype(BF16)], axis=1)
    f = wg_ref.shape[1]
    half = f // 2
    acc = None
    for j in range(2):
        cols = slice(j * half, (j + 1) * half)
        g = _dot(xs, wgb_ref[:, cols])
        u = _dot(xs, wub_ref[:, cols])
        hid = (g * jax.nn.sigmoid(g) * u).astype(BF16)
        part = _dot(hid, wdb_ref[cols, :])
        acc = part if acc is None else acc + part
    aff = xs_ref[:, :, dw:dw + LANES].reshape(nbatch * cap, LANES)
    lane = lax.broadcasted_iota(jnp.int32, aff.shape, 1)
    gate = jnp.sum(jnp.where(lane == first_expert + pl.program_id(0), aff, 0.0), axis=-1, keepdims=True)
    y = (acc * gate).reshape(nbatch, cap, d)
    for i in range(nbatch):
        yi = (y[i] * mod_ref[i, 5:6, :]).astype(BF16)
        hi = pltpu.bitcast(yi[:, 0:dw].astype(F32), jnp.uint32)
        lo = pltpu.bitcast(yi[:, dw:].astype(F32), jnp.uint32)
        y_ref[i] = pltpu.bitcast(hi | (lo >> 16), jnp.int32)


def _expert_ffn(xs, mod3, w_gate, w_up, w_down, first_expert, nbatch):
    B, E, cap, row_words = xs.shape
    d, f = w_gate.shape[1:]
    out_width, out_dtype = d // 2, jnp.int32
    tok = pl.BlockSpec((nbatch, None, cap, out_width), lambda e, b: (b, e, 0, 0))
    return pl.pallas_call(
        functools.partial(_ffn_kernel, first_expert=first_expert),
        grid=(E, B // nbatch),
        in_specs=[pl.BlockSpec((nbatch, None, cap, row_words), lambda e, b: (b, e, 0, 0)),
                  pl.BlockSpec((nbatch, 6, d), lambda e, b: (b, 0, 0)),
                  pl.BlockSpec((None, d, f), lambda e, b: (e + first_expert, 0, 0)),
                  pl.BlockSpec((None, d, f), lambda e, b: (e + first_expert, 0, 0)),
                  pl.BlockSpec((None, f, d), lambda e, b: (e + first_expert, 0, 0))],
        out_specs=tok,
        out_shape=jax.ShapeDtypeStruct((B, E, cap, out_width), out_dtype),
        scratch_shapes=[pltpu.VMEM((d, f), BF16), pltpu.VMEM((d, f), BF16), pltpu.VMEM((f, d), BF16)],
        compiler_params=_cparams(("arbitrary", "arbitrary")),
        name="expert_swiglu",
    )(xs, mod3, w_gate, w_up, w_down)


SC_ADD_ROWS = 32


def _combine_add(acc, y, slots, cap, first_expert, n_experts, after=()):
    L = slots.shape[1]
    half = y.shape[1]
    D = 2 * half
    assert slots.shape[0] // N_EXPERTS == SC_CORES * SC_SUBCORES
    n_win = cap // SC_ADD_ROWS
    mesh = plsc.VectorSubcoreMesh(core_axis_name="c", subcore_axis_name="s",
                                  num_cores=SC_CORES, num_subcores=SC_SUBCORES)

    def body(acc_hbm, y_hbm, slots_hbm, *rest):
        slot_v, *scratch = rest[len(after):]
        idx_v = scratch[:n_win]
        y_v = scratch[n_win:n_win + 2]
        o_v = scratch[n_win + 2:n_win + 4]
        sem_y, sem_g, sem_s = (scratch[n_win + 4 + 2 * k:n_win + 6 + 2 * k] for k in range(3))
        batch = lax.axis_index("s") * SC_CORES + lax.axis_index("c")

        @pl.loop(0, n_experts)
        def _(el):
            pltpu.sync_copy(slots_hbm.at[batch * N_EXPERTS + first_expert + el], slot_v)

            @pl.loop(0, L // SC_LANES)
            def _(i):
                v = slot_v[pl.ds(i * SC_LANES, SC_LANES)]
                tok = lax.iota(jnp.int32, SC_LANES) + (i * SC_LANES + batch * L)
                for w in range(n_win):
                    in_win = (v >= w * SC_ADD_ROWS) & (v < (w + 1) * SC_ADD_ROWS)
                    plsc.store_scatter(idx_v[w], [v - w * SC_ADD_ROWS], tok, mask=in_win)

            row0 = (batch * n_experts + el) * cap

            def fetch(w):
                b = w % 2
                return (pltpu.async_copy(y_hbm.at[pl.ds(row0 + w * SC_ADD_ROWS, SC_ADD_ROWS)], y_v[b], sem_y[b]),
                        pltpu.async_copy(acc_hbm.at[idx_v[w]], o_v[b], sem_g[b]))

            loads = fetch(0)
            stores = [None, None]
            for w in range(n_win):
                b = w % 2
                nxt = None
                if w + 1 < n_win:
                    if stores[1 - b] is not None:
                        stores[1 - b].wait()
                        stores[1 - b] = None
                    nxt = fetch(w + 1)
                loads[0].wait()
                loads[1].wait()

                @plsc.parallel_loop(0, SC_ADD_ROWS, unroll=2)
                def _(r):
                    for c in range(half // SC_LANES):
                        words = y_v[b][r, pl.ds(c * SC_LANES, SC_LANES)]
                        hi = lax.bitcast_convert_type(words & jnp.int32(-65536), F32)
                        lo = lax.bitcast_convert_type(words << 16, F32)
                        plsc.addupdate(o_v[b].at[r, pl.ds(c * SC_LANES, SC_LANES)], hi)
                        plsc.addupdate(o_v[b].at[r, pl.ds(half + c * SC_LANES, SC_LANES)], lo)

                stores[b] = pltpu.async_copy(o_v[b], acc_hbm.at[idx_v[w]], sem_s[b])
                loads = nxt
            for st in stores:
                if st is not None:
                    st.wait()

    pl.kernel(
        body,
        out_type=(),
        mesh=mesh,
        scratch_types=[pltpu.VMEM((L,), jnp.int32)]
        + [pltpu.VMEM((SC_ADD_ROWS,), jnp.int32) for _ in range(n_win)]
        + [pltpu.VMEM((SC_ADD_ROWS, half), jnp.int32) for _ in range(2)]
        + [pltpu.VMEM((SC_ADD_ROWS, D), F32) for _ in range(2)]
        + [pltpu.SemaphoreType.DMA for _ in range(6)],
        compiler_params=pltpu.CompilerParams(needs_layout_passes=False),
        name="moe_combine_row_add",
    )(acc, y, slots, *after)


def _rope_tables(L):
    inv = ROPE_BASE ** (-jnp.arange(ROPE_FREQS, dtype=F32) / ROPE_FREQS)
    pos = jnp.arange(L)
    row = (pos // GRID_W).astype(F32)[:, None] * inv
    col = (pos % GRID_W).astype(F32)[:, None] * inv
    cos = jnp.concatenate([jnp.cos(row), jnp.cos(row), jnp.cos(col), jnp.cos(col)], axis=1)
    sin = jnp.concatenate([-jnp.sin(row), jnp.sin(row), -jnp.sin(col), jnp.sin(col)], axis=1)
    return jnp.tile(cos, (1, 2)), jnp.tile(sin, (1, 2))


def _head_mean_matrix(n):
    idx = np.arange(n) // HEAD_DIM
    return jnp.asarray((idx[:, None] == idx[None, :]).astype(np.float32) / HEAD_DIM, dtype=BF16)


def kernel(x, c, ctx, c_ctx, w_mod, b_mod, norm1_g, w_in, q_norm_g, k_norm_g, attn_sink,
           w_decay_fwd, b_decay_fwd, w_decay_bwd, b_decay_bwd, gla_norm_g, w_out, norm2_g,
           w_router, w_e_gate, w_e_up, w_e_down):
    B, L, D = x.shape
    cap = CAPACITY_FACTOR * L // N_EXPERTS
    layer = 0

    rows = ((B + 1 + 7) // 8) * 8
    cc = jnp.concatenate([c, c_ctx[None, :], jnp.zeros((rows - B - 1, D), F32)], axis=0)
    mod_all = _modulation(cc, w_mod[layer], b_mod[layer])
    mod3 = mod_all[:B].reshape(B, 6, D)
    modc = mod_all[B].reshape(6, D)

    w = w_in[layer]
    o = np.cumsum([0, ATTN_WIDTH, KV_WIDTH, KV_WIDTH, GLA_QK_WIDTH, GLA_QK_WIDTH,
                   GLA_WIDTH, GLA_WIDTH, GATE_RANK, GATE_RANK])
    w_lr = jnp.concatenate([w[:, o[7]:o[9]]] * 3 + [jnp.zeros((D, LANES - 6 * GATE_RANK), F32)], axis=1)
    head_order = np.arange(N_Q_HEADS).reshape(N_KV_HEADS, -1).T.reshape(-1)
    attn_perm = (head_order[:, None] * HEAD_DIM + np.arange(HEAD_DIM)[None, :]).reshape(-1)
    w_in_r = jnp.concatenate([w[:, attn_perm], w[:, o[1]:o[7]], w_lr], axis=1).astype(BF16)
    w_out_r = jnp.concatenate([w_out[layer][attn_perm], w_out[layer][ATTN_WIDTH:]], axis=0).astype(BF16)
    w_ctx = jnp.concatenate([w[:, o[1]:o[3]], w[:, o[4]:o[6]], w_lr], axis=1).astype(BF16)
    wd2 = jnp.zeros((2 * GATE_RANK, 2 * GLA_QK_WIDTH), F32)
    wd2 = wd2.at[0:GATE_RANK, 0:GLA_QK_WIDTH].set(w_decay_fwd[layer])
    wd2 = wd2.at[GATE_RANK:, GLA_QK_WIDTH:].set(w_decay_bwd[layer])
    wd_hi = wd2.astype(BF16)
    wd_lo = (wd2 - wd_hi.astype(F32)).astype(BF16)
    wd = jnp.concatenate([wd_hi, wd_hi, wd_lo,
                          jnp.zeros((LANES - 6 * GATE_RANK, 2 * GLA_QK_WIDTH), BF16)], axis=0)
    bdec = jnp.concatenate([b_decay_fwd[layer], b_decay_bwd[layer]])[None, :]
    g1 = norm1_g[layer][None, :]
    g2 = norm2_g[layer][None, :]
    qg = jnp.tile(q_norm_g[layer], N_Q_HEADS)[None, :]
    kg2 = jnp.tile(k_norm_g[layer], N_KV_HEADS)[None, :]
    gn = jnp.tile(gla_norm_g[layer], GLA_HEADS)[None, :]
    bd512 = _head_mean_matrix(ATTN_WIDTH)
    bd128 = _head_mean_matrix(KV_WIDTH)
    cos_t, sin_t = _rope_tables(L)
    w_router_t = w_router[layer].T.astype(BF16)

    kc, vc, s_f, s_b = _context_side(ctx, modc, g1, w_ctx, kg2, bd128, wd, bdec)
    q, k, v, gq, gk, gv, gg, lr = _input_projection(
        x, mod3, g1, w_in_r, qg, kg2, bd512, cos_t, sin_t, tm=1024)
    attn = _window_attention(attn_sink[layer], q, k, v, kc, vc, tq=512)
    gla = _gla(gq, gk, gv, gg, lr, wd, bdec, gn, s_f, s_b)
    x1, h2, afft = _output_projection(attn, gla, x, mod3, w_out_r, g2, w_router_t, tm=1024)
    post = _expert_choice(afft, cap, nbatch=4)
    group_sizes = (4, 4, 4, 2, 2)
    assert sum(group_sizes) == N_EXPERTS
    firsts = [sum(group_sizes[:g]) for g in range(len(group_sizes))]
    rows = h2.reshape(B * L, ROW_WORDS)
    slots = post.reshape(B * N_EXPERTS, L)
    stream = jax.new_ref(x1.reshape(B * L, D))
    gathered = [_dispatch(rows, slots, cap, first, size) for first, size in zip(firsts, group_sizes)]
    for g, (first, size) in enumerate(zip(firsts, group_sizes)):
        xs = gathered[g].reshape(B, size, cap, ROW_WORDS)
        y = _expert_ffn(xs, mod3, w_e_gate[layer], w_e_up[layer], w_e_down[layer], first, nbatch=4)
        _combine_add(stream, y.reshape(B * size * cap, D // 2), slots, cap, first, size,
                     after=gathered[g + 1:])
    return jax.freeze(stream).reshape(B, L, D)
```

```python
import functools

import jax
import jax.numpy as jnp
import numpy as np
from jax import lax
from jax.experimental import pallas as pl
from jax.experimental.pallas import tpu as pltpu
from jax.experimental.pallas import tpu_sc as plsc

D_MODEL = 1024
GRID_W = 64
HEAD_DIM = 64
N_Q_HEADS = 8
N_KV_HEADS = 2
BLOCK = 128
ROPE_FREQS = 16
ROPE_BASE = 10000.0
GLA_HEADS = 4
GLA_DV = 128
GLA_DK = 64
GATE_RANK = 16
GATE_NORMALIZER = 16.0
CHUNK = 64
N_EXPERTS = 16
CAPACITY_FACTOR = 2
ATTN_WIDTH = 512
KV_WIDTH = 128
GLA_QK_WIDTH = 256
GLA_WIDTH = 512
EPS = 1e-6
NEG_INF = -1e30
LOG2E = 1.4426950408889634

LANES = 128
ROW_WORDS = D_MODEL // 2 + LANES
VMEM_LIMIT = 56 * 1024 * 1024

F32 = jnp.float32
BF16 = jnp.bfloat16
HI = lax.Precision.HIGHEST


def _cparams(sem):
    return pltpu.CompilerParams(dimension_semantics=sem, vmem_limit_bytes=VMEM_LIMIT)


def _dot(a, b):
    return jnp.dot(a, b, preferred_element_type=F32)


def _dot_hi(a, b):
    return jnp.dot(a, b, preferred_element_type=F32, precision=HI)


def _dot_nt(a, b):
    return lax.dot_general(a, b, (((1,), (1,)), ((), ())), preferred_element_type=F32)


def _dot_tn(a, b, precision=None):
    return lax.dot_general(a, b, (((0,), (0,)), ((), ())), preferred_element_type=F32,
                           precision=precision)


def _split2(t):
    hi = t.astype(BF16)
    lo = (t - hi.astype(F32)).astype(BF16)
    return hi, lo


def _rms_mod(t, g, shift, scale):
    y = t * lax.rsqrt(jnp.mean(t * t, axis=-1, keepdims=True) + EPS)
    return (y * g) * (1.0 + scale) + shift


def _log_decay(lr, wd3, bias):
    hi = lr.astype(BF16)
    lo = (lr - hi.astype(F32)).astype(BF16)
    lane = lax.broadcasted_iota(jnp.int32, lr.shape, 1)
    second = (lane >= 2 * GATE_RANK) & (lane < 4 * GATE_RANK)
    z = _dot(jnp.where(second, lo, hi), wd3) + bias
    return (jnp.minimum(z, 0.0) - jnp.log(1.0 + jnp.exp(-jnp.abs(z)))) * (1.0 / GATE_NORMALIZER)


def _lane_lo(shape):
    return (lax.broadcasted_iota(jnp.int32, shape, len(shape) - 1) % LANES) < HEAD_DIM


def _mod_kernel(c_ref, w_ref, b_ref, o_ref):
    c = c_ref[...]
    s = c * jax.nn.sigmoid(c)
    o_ref[...] = _dot_hi(s, w_ref[...]) + b_ref[...]


def _modulation(cc, w_mod, b_mod):
    m = cc.shape[0]
    n = w_mod.shape[1]
    tn = 1024
    return pl.pallas_call(
        _mod_kernel,
        grid=(n // tn,),
        in_specs=[pl.BlockSpec((m, D_MODEL), lambda j: (0, 0)),
                  pl.BlockSpec((D_MODEL, tn), lambda j: (0, j)),
                  pl.BlockSpec((1, tn), lambda j: (0, j))],
        out_specs=pl.BlockSpec((m, tn), lambda j: (0, j)),
        out_shape=jax.ShapeDtypeStruct((m, n), F32),
        compiler_params=_cparams(("arbitrary",)),
        name="adaln_mod",
    )(cc, w_mod, b_mod.reshape(1, n))


def _ctx_kernel(ctx_ref, mod_ref, g1_ref, w_ref, kg_ref, bd_ref, wd_ref, bdec_ref,
                kc_ref, vc_ref, sf_ref, sb_ref):
    n = ctx_ref.shape[0]
    h = _rms_mod(ctx_ref[...], g1_ref[...], mod_ref[0:1, :], mod_ref[1:2, :]).astype(BF16)
    pc = _dot(h, w_ref[...])
    ak = pc[:, 0:128]
    av = pc[:, 128:256]
    gk = pc[:, 256:512]
    gv = pc[:, 512:1024].astype(BF16)
    lr = pc[:, 1024:1152]
    sq_hi, sq_lo = _split2(ak * ak)
    ms = _dot(sq_hi, bd_ref[...]) + _dot(sq_lo, bd_ref[...])
    kn = ak * lax.rsqrt(ms + EPS) * kg_ref[...]
    kc_ref[...] = kn.astype(BF16)
    vc_ref[...] = av.astype(BF16)
    la = _log_decay(lr, wd_ref[...], bdec_ref[...])
    r = lax.broadcasted_iota(jnp.int32, (n, n), 0)
    cidx = lax.broadcasted_iota(jnp.int32, (n, n), 1)
    after = (cidx > r).astype(F32)
    before = (cidx < r).astype(F32)
    w_f = jnp.exp(_dot_hi(after, la[:, 0:256]))
    w_b = jnp.exp(_dot_hi(before, la[:, 256:512]))
    lo = _lane_lo((n, LANES))
    for w, out in ((w_f, sf_ref), (w_b, sb_ref)):
        kw = gk * w
        for c in range(2):
            kwc = kw[:, c * LANES:(c + 1) * LANES]
            k_lo = jnp.where(lo, kwc, 0.0).astype(BF16)
            k_hi = jnp.where(lo, 0.0, kwc).astype(BF16)
            v0 = gv[:, (2 * c) * GLA_DV:(2 * c + 1) * GLA_DV]
            v1 = gv[:, (2 * c + 1) * GLA_DV:(2 * c + 2) * GLA_DV]
            out[c] = _dot_tn(v0, k_lo) + _dot_tn(v1, k_hi)


def _context_side(ctx, modc, g1, w_ctx, kg2, bd128, wd, bdec):
    B, n, _ = ctx.shape
    full = lambda shape: pl.BlockSpec(shape, lambda b: (0,) * len(shape))
    kv_spec = pl.BlockSpec((None, n, KV_WIDTH), lambda b: (b, 0, 0))
    st_spec = pl.BlockSpec((None, 2, LANES, GLA_DV), lambda b: (b, 0, 0, 0))
    kv_shape = jax.ShapeDtypeStruct((B, n, KV_WIDTH), BF16)
    st_shape = jax.ShapeDtypeStruct((B, 2, LANES, GLA_DV), F32)
    return pl.pallas_call(
        _ctx_kernel,
        grid=(B,),
        in_specs=[pl.BlockSpec((None, n, D_MODEL), lambda b: (b, 0, 0)),
                  full(modc.shape), full(g1.shape), full(w_ctx.shape), full(kg2.shape),
                  full(bd128.shape), full(wd.shape), full(bdec.shape)],
        out_specs=[kv_spec, kv_spec, st_spec, st_spec],
        out_shape=[kv_shape, kv_shape, st_shape, st_shape],
        compiler_params=_cparams(("arbitrary",)),
        name="context_side",
    )(ctx, modc, g1, w_ctx, kg2, bd128, wd, bdec)


def _swap16(t):
    n = t.shape[1]
    first = (lax.broadcasted_iota(jnp.int32, t.shape, 1) % 32) < ROPE_FREQS
    return jnp.where(first, pltpu.roll(t, n - ROPE_FREQS, 1), pltpu.roll(t, ROPE_FREQS, 1))


def _inproj_kernel(x_ref, mod_ref, g1_ref, w_ref, qg_ref, kg_ref, bd_ref, cos_ref, sin_ref,
                   q_ref, k_ref, v_ref, gq_ref, gk_ref, gv_ref, gg_ref, lr_ref):
    h = _rms_mod(x_ref[...], g1_ref[...], mod_ref[0:1, :], mod_ref[1:2, :]).astype(BF16)
    cos = cos_ref[...]
    sin = sin_ref[...]

    def head_norm_rope(t, g, bd, reps):
        ms = _dot((t * t).astype(BF16), bd)
        tn = t * lax.rsqrt(ms + EPS) * g
        c = jnp.concatenate([cos] * reps, axis=1) if reps > 1 else cos
        s = jnp.concatenate([sin] * reps, axis=1) if reps > 1 else sin
        return tn * c + _swap16(tn) * s

    aq = _dot(h, w_ref[:, 0:512])
    q = head_norm_rope(aq, qg_ref[...], bd_ref[...], 4) * (HEAD_DIM ** -0.5 * LOG2E)
    q_ref[...] = q.astype(BF16)
    akv = _dot(h, w_ref[:, 512:768])
    k = head_norm_rope(akv[:, 0:128], kg_ref[...], bd_ref[0:128, 0:128], 1)
    k_ref[...] = k.astype(BF16)
    v_ref[...] = akv[:, 128:256].astype(BF16)
    gqk = _dot(h, w_ref[:, 768:1280])
    gq_ref[...] = (gqk[:, 0:256] * (GLA_DK ** -0.5)).astype(BF16)
    gk_ref[...] = gqk[:, 256:512].astype(BF16)
    gv_ref[...] = _dot(h, w_ref[:, 1280:1792]).astype(BF16)
    gg_ref[...] = _dot(h, w_ref[:, 1792:2304]).astype(BF16)
    lr_ref[...] = _dot(h, w_ref[:, 2304:2432])


def _input_projection(x, mod3, g1, w_in_r, qg, kg2, bd512, cos_t, sin_t, tm):
    B, L, _ = x.shape
    full = lambda shape: pl.BlockSpec(shape, lambda b, i: (0,) * len(shape))
    tok = lambda w: pl.BlockSpec((None, tm, w), lambda b, i: (b, i, 0))
    widths = (ATTN_WIDTH, KV_WIDTH, KV_WIDTH, GLA_QK_WIDTH, GLA_QK_WIDTH, GLA_WIDTH, GLA_WIDTH, LANES)
    dtypes = (BF16,) * 7 + (F32,)
    return pl.pallas_call(
        _inproj_kernel,
        grid=(B, L // tm),
        in_specs=[tok(D_MODEL),
                  pl.BlockSpec((None, 6, D_MODEL), lambda b, i: (b, 0, 0)),
                  full(g1.shape), full(w_in_r.shape), full(qg.shape), full(kg2.shape),
                  full(bd512.shape),
                  pl.BlockSpec((tm, LANES), lambda b, i: (i, 0)),
                  pl.BlockSpec((tm, LANES), lambda b, i: (i, 0))],
        out_specs=[tok(w) for w in widths],
        out_shape=[jax.ShapeDtypeStruct((B, L, w), dt) for w, dt in zip(widths, dtypes)],
        compiler_params=_cparams(("arbitrary", "arbitrary")),
        name="input_projection",
    )(x, mod3, g1, w_in_r, qg, kg2, bd512, cos_t, sin_t)


def _attn_kernel(sink_ref, q_ref, kp_ref, ko_ref, kn_ref, vp_ref, vo_ref, vn_ref,
                 kc_ref, vc_ref, o_ref):
    i = pl.program_id(1)
    ni = pl.num_programs(1)
    nsub = q_ref.shape[0] // BLOCK
    ncol = ATTN_WIDTH // LANES
    win = 3 * BLOCK
    ucol = 4
    half_rows = ucol * BLOCK
    k_win = jnp.concatenate([kp_ref[...], ko_ref[...], kn_ref[...]], axis=0)
    v_win = jnp.concatenate([vp_ref[...], vo_ref[...], vn_ref[...]], axis=0)
    k_ctx = kc_ref[...]
    lo_w = _lane_lo(v_win.shape)
    lo_c = _lane_lo(vc_ref.shape)
    lo_q = _lane_lo((BLOCK, LANES))
    zero = jnp.zeros((), BF16)
    one = jnp.ones((), BF16)
    v0_c, v0_w = jnp.where(lo_c, vc_ref[...], one), jnp.where(lo_w, v_win, one)
    v1_c, v1_w = jnp.where(lo_c, one, vc_ref[...]), jnp.where(lo_w, one, v_win)
    qi = lax.broadcasted_iota(jnp.int32, (half_rows, BLOCK), 0) % BLOCK
    kj = lax.broadcasted_iota(jnp.int32, (half_rows, BLOCK), 1)
    no_prev = jnp.where(i > 0, 0, BLOCK)
    no_next = jnp.where(i < ni - 1, 0, BLOCK)
    row_head = lax.broadcasted_iota(jnp.int32, (half_rows, 1), 0) // BLOCK
    lo_o = _lane_lo((half_rows, LANES))
    for t in range(nsub):
        rows = slice(t * BLOCK, (t + 1) * BLOCK)
        keys = slice(t * BLOCK, t * BLOCK + win)
        cols = [q_ref[rows, c * LANES:(c + 1) * LANES] for c in range(ncol)]
        prev_ok = kj >= qi + (no_prev if t == 0 else 0)
        next_ok = kj <= qi - (no_next if t == nsub - 1 else 0)
        for p in range(ncol // ucol):
            outs = []
            qs = jnp.concatenate([jnp.where(lo_q, qc, zero) for qc in cols[ucol * p:ucol * (p + 1)]]
                                 + [jnp.where(lo_q, zero, qc) for qc in cols[ucol * p:ucol * (p + 1)]], axis=0)
            s_c_all = _dot_nt(qs, k_ctx)
            s_w_all = _dot_nt(qs, k_win[keys])
            for g, (vv_c, vv_w) in enumerate(((v0_c, v0_w), (v1_c, v1_w))):
                head = g * ncol + ucol * p
                sink_g = jnp.full((half_rows, 1), sink_ref[head + ucol - 1], F32)
                for j in range(ucol - 2, -1, -1):
                    sink_g = jnp.where(row_head <= j, sink_ref[head + j], sink_g)
                sink_g = sink_g * LOG2E
                s_c = s_c_all[g * half_rows:(g + 1) * half_rows]
                s_w = s_w_all[g * half_rows:(g + 1) * half_rows]
                s_p = jnp.where(prev_ok, s_w[:, 0:BLOCK], NEG_INF)
                s_o = s_w[:, BLOCK:2 * BLOCK]
                s_n = jnp.where(next_ok, s_w[:, 2 * BLOCK:win], NEG_INF)
                m = jnp.maximum(jnp.maximum(jnp.max(s_c, axis=-1, keepdims=True),
                                            jnp.max(jnp.maximum(jnp.maximum(s_p, s_o), s_n),
                                                    axis=-1, keepdims=True)), sink_g)
                e_c = jnp.exp2(s_c - m).astype(BF16)
                e_w = jnp.concatenate([jnp.exp2(s_p - m), jnp.exp2(s_o - m), jnp.exp2(s_n - m)],
                                      axis=1).astype(BF16)
                acc = _dot(e_c, vv_c) + _dot(e_w, vv_w[keys])
                outs.append(acc / (pltpu.roll(acc, HEAD_DIM, 1) + jnp.exp2(sink_g - m)))
            o = jnp.where(lo_o, outs[0], outs[1]).astype(BF16)
            for j in range(ucol):
                c = ucol * p + j
                o_ref[rows, c * LANES:(c + 1) * LANES] = o[j * BLOCK:(j + 1) * BLOCK]


def _window_attention(sink, q, k, v, kc, vc, tq):
    B, L, _ = q.shape
    nb = L // BLOCK
    nsub = tq // BLOCK
    n_ctx = kc.shape[1]
    prev = pl.BlockSpec((None, BLOCK, KV_WIDTH), lambda b, n: (b, jnp.maximum(n * nsub - 1, 0), 0))
    own = pl.BlockSpec((None, tq, KV_WIDTH), lambda b, n: (b, n, 0))
    nxt = pl.BlockSpec((None, BLOCK, KV_WIDTH),
                       lambda b, n: (b, jnp.minimum((n + 1) * nsub, nb - 1), 0))
    cspec = pl.BlockSpec((None, n_ctx, KV_WIDTH), lambda b, n: (b, 0, 0))
    return pl.pallas_call(
        _attn_kernel,
        grid=(B, L // tq),
        in_specs=[pl.BlockSpec(memory_space=pltpu.SMEM),
                  pl.BlockSpec((None, tq, ATTN_WIDTH), lambda b, n: (b, n, 0)),
                  prev, own, nxt, prev, own, nxt, cspec, cspec],
        out_specs=pl.BlockSpec((None, tq, ATTN_WIDTH), lambda b, n: (b, n, 0)),
        out_shape=jax.ShapeDtypeStruct((B, L, ATTN_WIDTH), BF16),
        compiler_params=_cparams(("arbitrary", "arbitrary")),
        name="window_attention",
    )(sink, q, k, k, k, v, v, v, kc, vc)


SUPER = 256
CH_PER = SUPER // CHUNK
HALF = 128


def _dot_split(m, parts):
    return _dot(m, parts[0]) + _dot(m, parts[1])


def _gla_kernel(gq_ref, gk_ref, gv_ref, gg_ref, lr_ref, wd_ref, bdec_ref, gn_ref, sf_ref, sb_ref,
                o_ref, la_ref, oi_ref, qg_ref, kv_ref, sb16_ref, dec_ref, st_ref):
    L = gq_ref.shape[0]
    nsuper = L // SUPER
    nchunk = L // CHUNK
    half = CHUNK // 2
    la_ref[...] = _log_decay(lr_ref[...], wd_ref[...], bdec_ref[...])

    r = lax.broadcasted_iota(jnp.int32, (SUPER, SUPER), 0)
    cidx = lax.broadcasted_iota(jnp.int32, (SUPER, SUPER), 1)
    same = (r // CHUNK) == (cidx // CHUNK)
    pr = r % CHUNK
    pc = cidx % CHUNK
    one = jnp.float32(1.0)
    zero = jnp.float32(0.0)
    in_f = jnp.where(pc <= pr, one, zero)
    in_b = jnp.where(pc >= pr, one, zero)
    ref_f = jnp.where(pc < half, one, zero)
    ref_b = jnp.where(pc >= half, one, zero)
    m1_f = jnp.where(same, in_f - ref_f, zero).astype(BF16)
    m1_b = jnp.where(same, in_b - ref_b, zero).astype(BF16)
    rh = lax.broadcasted_iota(jnp.int32, (HALF, 2 * HALF), 0)
    ch = lax.broadcasted_iota(jnp.int32, (HALF, 2 * HALF), 1) % HALF
    same_h = (rh // CHUNK) == (ch // CHUNK)
    mask_f = jnp.where(same_h, jnp.where(ch % CHUNK <= rh % CHUNK, one, zero), zero) > 0.5
    mask_b = jnp.where(same_h, jnp.where(ch % CHUNK >= rh % CHUNK, one, zero), zero) > 0.5
    rr = lax.broadcasted_iota(jnp.int32, (2 * CH_PER, SUPER), 0)
    rc = lax.broadcasted_iota(jnp.int32, (2 * CH_PER, SUPER), 1)
    in_chunk = jnp.where((rc // CHUNK) == (rr % CH_PER), one, zero)
    first = jnp.where((rc % CHUNK) < half, 1, 0)
    is_tot = jnp.where(rr >= CH_PER, 1, 0)
    rs_f = (in_chunk * jnp.where(first != is_tot, one, zero)).astype(BF16)
    rs_b = (in_chunk * jnp.where(first == is_tot, one, zero)).astype(BF16)
    lo_h = _lane_lo((HALF, LANES))
    zero_blk = jnp.zeros((HALF, GLA_DV), BF16)

    def phase1(s, carry):
        r0 = pl.multiple_of(s * SUPER, SUPER)
        rows = pl.ds(r0, SUPER)
        q = gq_ref[rows, :].astype(F32)
        k = gk_ref[rows, :].astype(F32)
        qes, kes, kds = [], [], []
        for d, (m1, rs) in enumerate(((m1_f, rs_f), (m1_b, rs_b))):
            parts = _split2(la_ref[rows, d * GLA_QK_WIDTH:(d + 1) * GLA_QK_WIDTH])
            x1 = _dot_split(m1, parts)
            erow = jnp.exp(_dot_split(rs, parts))
            dec = erow[0:CH_PER] * erow[CH_PER:2 * CH_PER]
            dec_ref[d, s] = jnp.concatenate([dec, dec], axis=0)
            qe = q * jnp.exp(x1)
            ke = k * jnp.exp(-x1)
            qg_parts, kd_parts = [], []
            for j in range(CH_PER):
                rj = slice(j * CHUNK, (j + 1) * CHUNK)
                qg_parts.append(qe[rj] * erow[j:j + 1])
                kd_parts.append(ke[rj] * erow[CH_PER + j:CH_PER + j + 1])
            qg_ref[d, rows, :] = jnp.concatenate(qg_parts, axis=0).astype(BF16)
            qes.append(qe.astype(BF16))
            kes.append(ke)
            kds.append(jnp.concatenate(kd_parts, axis=0).astype(BF16))
        for c in range(2):
            cl = slice(c * LANES, (c + 1) * LANES)
            vpair = gv_ref[rows, 2 * c * GLA_DV:(2 * c + 2) * GLA_DV]
            for blk in range(SUPER // HALF):
                rb = slice(blk * HALF, (blk + 1) * HALF)
                vbd = jnp.concatenate(
                    [jnp.concatenate([vpair[rb, 0:GLA_DV], zero_blk], axis=1),
                     jnp.concatenate([zero_blk, vpair[rb, GLA_DV:2 * GLA_DV]], axis=1)], axis=0)
                o2 = None
                for d, mask in enumerate((mask_f, mask_b)):
                    ke_cb = kes[d][rb, cl]
                    ke_st = jnp.concatenate([jnp.where(lo_h, ke_cb, zero),
                                             jnp.where(lo_h, zero, ke_cb)], axis=0).astype(BF16)
                    a = _dot_nt(qes[d][rb, cl], ke_st)
                    o = _dot(jnp.where(mask, a, zero).astype(BF16), vbd)
                    o2 = o if o2 is None else o2 + o
                oi_ref[pl.ds(r0 + blk * HALF, HALF), 2 * c * GLA_DV:(2 * c + 2) * GLA_DV] = o2
            for d in range(2):
                for j in range(CH_PER):
                    rj = slice(j * CHUNK, (j + 1) * CHUNK)
                    t = _dot_tn(kds[d][rj, cl], vpair[rj])
                    kv = jnp.concatenate([t[0:GLA_DK, 0:GLA_DV], t[GLA_DK:, GLA_DV:]], axis=0)
                    kv_ref[d, c, s * CH_PER + j] = kv.T
        return carry

    lax.fori_loop(0, nsuper, phase1, 0)

    st_ref[0] = sf_ref[0]
    st_ref[1] = sf_ref[1]
    st_ref[2] = sb_ref[0]
    st_ref[3] = sb_ref[1]

    def phase2(n, carry):
        for d in range(2):
            idx = n if d == 0 else nchunk - 1 - n
            dec = dec_ref[d, idx // CH_PER, pl.ds(idx % CH_PER, 1), :]
            for c in range(2):
                st = st_ref[2 * d + c]
                sb16_ref[c, idx, :, d * LANES:(d + 1) * LANES] = st.astype(BF16)
                st_ref[2 * d + c] = dec[:, c * LANES:(c + 1) * LANES] * st + kv_ref[d, c, idx]
        return carry

    lax.fori_loop(0, nchunk, phase2, 0)

    lo64 = _lane_lo((CHUNK, 2 * LANES))
    zero_b = jnp.zeros((), BF16)

    def phase3(s, carry):
        r0 = pl.multiple_of(s * SUPER, SUPER)
        rows = pl.ds(r0, SUPER)
        inter = [[None] * CH_PER for _ in range(GLA_HEADS)]
        for j in range(CH_PER):
            rj = pl.ds(r0 + j * CHUNK, CHUNK)
            for c in range(2):
                qg_c = jnp.concatenate([qg_ref[d, rj, c * LANES:(c + 1) * LANES] for d in range(2)], axis=1)
                lhs = jnp.concatenate([jnp.where(lo64, qg_c, zero_b),
                                       jnp.where(lo64, zero_b, qg_c)], axis=0)
                t = _dot_nt(lhs, sb16_ref[c, s * CH_PER + j])
                for hh in range(2):
                    inter[2 * c + hh][j] = t[hh * CHUNK:(hh + 1) * CHUNK]
        for hd in range(GLA_HEADS):
            cl = slice(hd * GLA_DV, (hd + 1) * GLA_DV)
            o = oi_ref[rows, cl] + jnp.concatenate(inter[hd], axis=0)
            y = o * lax.rsqrt(jnp.mean(o * o, axis=-1, keepdims=True) + EPS) * gn_ref[:, cl]
            g = gg_ref[rows, cl].astype(F32)
            o_ref[rows, cl] = (y * (g * jax.nn.sigmoid(g))).astype(BF16)
        return carry

    lax.fori_loop(0, nsuper, phase3, 0)


def _gla(gq, gk, gv, gg, lr, wd, bdec, gn, s_f, s_b):
    B, L, _ = gq.shape
    nchunk = L // CHUNK
    full = lambda shape: pl.BlockSpec(shape, lambda b: (0,) * len(shape))
    tok = lambda w: pl.BlockSpec((None, L, w), lambda b: (b, 0, 0))
    st_spec = pl.BlockSpec((None, 2, LANES, GLA_DV), lambda b: (b, 0, 0, 0))
    return pl.pallas_call(
        _gla_kernel,
        grid=(B,),
        in_specs=[tok(GLA_QK_WIDTH), tok(GLA_QK_WIDTH), tok(GLA_WIDTH), tok(GLA_WIDTH), tok(LANES),
                  full(wd.shape), full(bdec.shape), full(gn.shape), st_spec, st_spec],
        out_specs=tok(GLA_WIDTH),
        out_shape=jax.ShapeDtypeStruct((B, L, GLA_WIDTH), BF16),
        scratch_shapes=[pltpu.VMEM((L, 2 * GLA_QK_WIDTH), F32),
                        pltpu.VMEM((L, GLA_WIDTH), F32),
                        pltpu.VMEM((2, L, GLA_QK_WIDTH), BF16),
                        pltpu.VMEM((2, 2, nchunk, GLA_DV, LANES), F32),
                        pltpu.VMEM((2, nchunk, GLA_DV, 2 * LANES), BF16),
                        pltpu.VMEM((2, L // SUPER, 2 * CH_PER, GLA_QK_WIDTH), F32),
                        pltpu.VMEM((4, GLA_DV, LANES), F32)],
        compiler_params=_cparams(("arbitrary",)),
        name="gla_bidirectional",
    )(gq, gk, gv, gg, lr, wd, bdec, gn, s_f, s_b)


def _outproj_kernel(attn_ref, gla_ref, x_ref, mod_ref, w_ref, g2_ref, wr_ref,
                    x1_ref, h2_ref, afft_ref):
    y = _dot(attn_ref[...], w_ref[0:ATTN_WIDTH, :]) + _dot(gla_ref[...], w_ref[ATTN_WIDTH:, :])
    x1 = x_ref[...] + mod_ref[2:3, :] * y
    x1_ref[...] = x1
    h2 = _rms_mod(x1, g2_ref[...], mod_ref[3:4, :], mod_ref[4:5, :]).astype(BF16)
    half = D_MODEL // 2
    hi = pltpu.bitcast(h2[:, 0:half].astype(F32), jnp.uint32)
    lo = pltpu.bitcast(h2[:, half:].astype(F32), jnp.uint32)
    h2_ref[:, 0:half] = pltpu.bitcast(hi | (lo >> 16), F32)
    logits = _dot_nt(wr_ref[...], h2)
    e = jnp.exp(logits - jnp.max(logits, axis=0, keepdims=True))
    afft = e / jnp.sum(e, axis=0, keepdims=True)
    afft_ref[...] = afft
    pad = jnp.zeros((LANES - N_EXPERTS, afft.shape[1]), F32)
    h2_ref[:, half:half + LANES] = jnp.concatenate([afft, pad], axis=0).T


def _output_projection(attn, gla, x, mod3, w_out, g2, w_router, tm):
    B, L, _ = x.shape
    full = lambda shape: pl.BlockSpec(shape, lambda b, i: (0,) * len(shape))
    tok = lambda w: pl.BlockSpec((None, tm, w), lambda b, i: (b, i, 0))
    return pl.pallas_call(
        _outproj_kernel,
        grid=(B, L // tm),
        in_specs=[tok(ATTN_WIDTH), tok(GLA_WIDTH), tok(D_MODEL),
                  pl.BlockSpec((None, 6, D_MODEL), lambda b, i: (b, 0, 0)),
                  full(w_out.shape), full(g2.shape), full(w_router.shape)],
        out_specs=[tok(D_MODEL), tok(ROW_WORDS),
                   pl.BlockSpec((None, N_EXPERTS, tm), lambda b, i: (b, 0, i))],
        out_shape=[jax.ShapeDtypeStruct((B, L, D_MODEL), F32),
                   jax.ShapeDtypeStruct((B, L, ROW_WORDS), F32),
                   jax.ShapeDtypeStruct((B, N_EXPERTS, L), F32)],
        compiler_params=_cparams(("arbitrary", "arbitrary")),
        name="output_projection_router",
    )(attn, gla, x, mod3, w_out, g2, w_router)


def _topk_kernel(afft_ref, post_ref, *, cap):
    nbatch, E, L = afft_ref.shape
    aff = afft_ref[...].reshape(nbatch * E, L)
    E = nbatch * E

    def search(i, thr):
        cand = thr | jnp.left_shift(jnp.int32(1), 30 - i)
        cnt = jnp.sum(jnp.where(aff >= pltpu.bitcast(cand, F32), 1.0, 0.0), axis=-1, keepdims=True)
        return jnp.where(cnt >= cap, cand, thr)

    thr_bits = lax.fori_loop(0, 31, search, jnp.zeros((E, 1), jnp.int32))
    thr = pltpu.bitcast(thr_bits, F32)
    above = aff > thr
    tie = aff == thr
    need = cap - jnp.sum(jnp.where(above, 1.0, 0.0), axis=-1, keepdims=True)

    upper = (lax.broadcasted_iota(jnp.int32, (LANES, LANES), 0)
             <= lax.broadcasted_iota(jnp.int32, (LANES, LANES), 1)).astype(BF16)

    def prefix(mask):
        parts = []
        run = jnp.zeros((E, 1), F32)
        for j in range(L // LANES):
            blk = jnp.where(mask[:, j * LANES:(j + 1) * LANES], 1.0, 0.0).astype(BF16)
            loc = _dot(blk, upper) + run
            parts.append(loc)
            run = loc[:, LANES - 1:LANES]
        return jnp.concatenate(parts, axis=1)

    tie_rank = prefix(tie)
    sel = above | (tie & (tie_rank <= need))
    slot = prefix(sel).astype(jnp.int32) - 1
    post_ref[...] = jnp.where(sel, slot, -1).reshape(post_ref.shape)


def _expert_choice(afft, cap, nbatch):
    B, E, L = afft.shape
    return pl.pallas_call(
        functools.partial(_topk_kernel, cap=cap),
        grid=(B // nbatch,),
        in_specs=[pl.BlockSpec((nbatch, E, L), lambda b: (b, 0, 0))],
        out_specs=pl.BlockSpec((nbatch, E, L), lambda b: (b, 0, 0)),
        out_shape=jax.ShapeDtypeStruct((B, E, L), jnp.int32),
        compiler_params=_cparams(("arbitrary",)),
        name="expert_choice_topk",
    )(afft)


SC_CORES = 2
SC_SUBCORES = 16
SC_LANES = 16
SC_WINDOW = 128


def _dispatch(rows, slots, cap, first_expert, n_experts):
    L = slots.shape[1]
    W = rows.shape[1]
    n_pair = slots.shape[0] // N_EXPERTS * n_experts
    per_worker = n_pair // (SC_CORES * SC_SUBCORES)
    n_win = cap // SC_WINDOW
    mesh = plsc.VectorSubcoreMesh(core_axis_name="c", subcore_axis_name="s",
                                  num_cores=SC_CORES, num_subcores=SC_SUBCORES)

    def body(rows_hbm, slots_hbm, out_hbm, slot_v, *scratch):
        idx_v, buf_v, sem = scratch[:n_win], scratch[n_win], scratch[n_win + 1]
        worker = lax.axis_index("s") * SC_CORES + lax.axis_index("c")

        @pl.loop(0, per_worker)
        def _(p):
            pair = worker * per_worker + p
            batch = pair // n_experts
            first_tok = batch * L
            pltpu.sync_copy(slots_hbm.at[batch * N_EXPERTS + first_expert + pair % n_experts], slot_v)

            @pl.loop(0, L // SC_LANES)
            def _(i):
                v = slot_v[pl.ds(i * SC_LANES, SC_LANES)]
                tok = lax.iota(jnp.int32, SC_LANES) + (i * SC_LANES + first_tok)
                for w in range(n_win):
                    in_win = (v >= w * SC_WINDOW) & (v < (w + 1) * SC_WINDOW)
                    plsc.store_scatter(idx_v[w], [v - w * SC_WINDOW], tok, mask=in_win)

            for w in range(n_win):
                pltpu.async_copy(rows_hbm.at[idx_v[w]], buf_v, sem).wait()
                pltpu.sync_copy(buf_v, out_hbm.at[pl.ds(pair * cap + w * SC_WINDOW, SC_WINDOW)])

    return pl.kernel(
        body,
        out_type=jax.ShapeDtypeStruct((n_pair * cap, W), rows.dtype),
        mesh=mesh,
        scratch_types=[pltpu.VMEM((L,), jnp.int32)]
        + [pltpu.VMEM((SC_WINDOW,), jnp.int32) for _ in range(n_win)]
        + [pltpu.VMEM((SC_WINDOW, W), rows.dtype), pltpu.SemaphoreType.DMA],
        compiler_params=pltpu.CompilerParams(needs_layout_passes=False),
        name="moe_dispatch_gather",
    )(rows, slots)


def _ffn_kernel(xs_ref, mod_ref, wg_ref, wu_ref, wd_ref, y_ref, wgb_ref, wub_ref, wdb_ref, *,
                first_expert):
    nbatch, cap, _ = xs_ref.shape
    d = wg_ref.shape[0]
    dw = d // 2

    @pl.when(pl.program_id(1) == 0)
    def _():
        wgb_ref[...] = wg_ref[...].astype(BF16)
        wub_ref[...] = wu_ref[...].astype(BF16)
        wdb_ref[...] = wd_ref[...].astype(BF16)

    words = pltpu.bitcast(xs_ref[:, :, 0:dw].reshape(nbatch * cap, dw), jnp.uint32)
    xs = jnp.concatenate([pltpu.bitcast(words & jnp.uint32(0xFFFF0000), F32).astype(BF16),
                          pltpu.bitcast(words << 16, F32).astype(BF16)], axis=1)
    f = wg_ref.shape[1]
    half = f // 2
    acc = None
    for j in range(2):
        cols = slice(j * half, (j + 1) * half)
        g = _dot(xs, wgb_ref[:, cols])
        u = _dot(xs, wub_ref[:, cols])
        hid = (g * jax.nn.sigmoid(g) * u).astype(BF16)
        part = _dot(hid, wdb_ref[cols, :])
        acc = part if acc is None else acc + part
    aff = xs_ref[:, :, dw:dw + LANES].reshape(nbatch * cap, LANES)
    lane = lax.broadcasted_iota(jnp.int32, aff.shape, 1)
    gate = jnp.sum(jnp.where(lane == first_expert + pl.program_id(0), aff, 0.0), axis=-1, keepdims=True)
    y = (acc * gate).reshape(nbatch, cap, d)
    for i in range(nbatch):
        yi = (y[i] * mod_ref[i, 5:6, :]).astype(BF16)
        hi = pltpu.bitcast(yi[:, 0:dw].astype(F32), jnp.uint32)
        lo = pltpu.bitcast(yi[:, dw:].astype(F32), jnp.uint32)
        y_ref[i] = pltpu.bitcast(hi | (lo >> 16), jnp.int32)


def _expert_ffn(xs, mod3, w_gate, w_up, w_down, first_expert, nbatch):
    B, E, cap, row_words = xs.shape
    d, f = w_gate.shape[1:]
    out_width, out_dtype = d // 2, jnp.int32
    tok = pl.BlockSpec((nbatch, None, cap, out_width), lambda e, b: (b, e, 0, 0))
    return pl.pallas_call(
        functools.partial(_ffn_kernel, first_expert=first_expert),
        grid=(E, B // nbatch),
        in_specs=[pl.BlockSpec((nbatch, None, cap, row_words), lambda e, b: (b, e, 0, 0)),
                  pl.BlockSpec((nbatch, 6, d), lambda e, b: (b, 0, 0)),
                  pl.BlockSpec((None, d, f), lambda e, b: (e + first_expert, 0, 0)),
                  pl.BlockSpec((None, d, f), lambda e, b: (e + first_expert, 0, 0)),
                  pl.BlockSpec((None, f, d), lambda e, b: (e + first_expert, 0, 0))],
        out_specs=tok,
        out_shape=jax.ShapeDtypeStruct((B, E, cap, out_width), out_dtype),
        scratch_shapes=[pltpu.VMEM((d, f), BF16), pltpu.VMEM((d, f), BF16), pltpu.VMEM((f, d), BF16)],
        compiler_params=_cparams(("arbitrary", "arbitrary")),
        name="expert_swiglu",
    )(xs, mod3, w_gate, w_up, w_down)


SC_ADD_ROWS = 32


def _combine_add(acc, y, slots, cap, first_expert, n_experts, after=()):
    L = slots.shape[1]
    half = y.shape[1]
    D = 2 * half
    assert slots.shape[0] // N_EXPERTS == SC_CORES * SC_SUBCORES
    n_win = cap // SC_ADD_ROWS
    mesh = plsc.VectorSubcoreMesh(core_axis_name="c", subcore_axis_name="s",
                                  num_cores=SC_CORES, num_subcores=SC_SUBCORES)

    def body(acc_hbm, y_hbm, slots_hbm, *rest):
        slot_v, *scratch = rest[len(after):]
        idx_v = scratch[:n_win]
        y_v = scratch[n_win:n_win + 2]
        o_v = scratch[n_win + 2:n_win + 4]
        sem_y, sem_g, sem_s = (scratch[n_win + 4 + 2 * k:n_win + 6 + 2 * k] for k in range(3))
        batch = lax.axis_index("s") * SC_CORES + lax.axis_index("c")

        @pl.loop(0, n_experts)
        def _(el):
            pltpu.sync_copy(slots_hbm.at[batch * N_EXPERTS + first_expert + el], slot_v)

            @pl.loop(0, L // SC_LANES)
            def _(i):
                v = slot_v[pl.ds(i * SC_LANES, SC_LANES)]
                tok = lax.iota(jnp.int32, SC_LANES) + (i * SC_LANES + batch * L)
                for w in range(n_win):
                    in_win = (v >= w * SC_ADD_ROWS) & (v < (w + 1) * SC_ADD_ROWS)
                    plsc.store_scatter(idx_v[w], [v - w * SC_ADD_ROWS], tok, mask=in_win)

            row0 = (batch * n_experts + el) * cap

            def fetch(w):
                b = w % 2
                return (pltpu.async_copy(y_hbm.at[pl.ds(row0 + w * SC_ADD_ROWS, SC_ADD_ROWS)], y_v[b], sem_y[b]),
                        pltpu.async_copy(acc_hbm.at[idx_v[w]], o_v[b], sem_g[b]))

            loads = fetch(0)
            stores = [None, None]
            for w in range(n_win):
                b = w % 2
                nxt = None
                if w + 1 < n_win:
                    if stores[1 - b] is not None:
                        stores[1 - b].wait()
                        stores[1 - b] = None
                    nxt = fetch(w + 1)
                loads[0].wait()
                loads[1].wait()

                @plsc.parallel_loop(0, SC_ADD_ROWS, unroll=2)
                def _(r):
                    for c in range(half // SC_LANES):
                        words = y_v[b][r, pl.ds(c * SC_LANES, SC_LANES)]
                        hi = lax.bitcast_convert_type(words & jnp.int32(-65536), F32)
                        lo = lax.bitcast_convert_type(words << 16, F32)
                        plsc.addupdate(o_v[b].at[r, pl.ds(c * SC_LANES, SC_LANES)], hi)
                        plsc.addupdate(o_v[b].at[r, pl.ds(half + c * SC_LANES, SC_LANES)], lo)

                stores[b] = pltpu.async_copy(o_v[b], acc_hbm.at[idx_v[w]], sem_s[b])
                loads = nxt
            for st in stores:
                if st is not None:
                    st.wait()

    pl.kernel(
        body,
        out_type=(),
        mesh=mesh,
        scratch_types=[pltpu.VMEM((L,), jnp.int32)]
        + [pltpu.VMEM((SC_ADD_ROWS,), jnp.int32) for _ in range(n_win)]
        + [pltpu.VMEM((SC_ADD_ROWS, half), jnp.int32) for _ in range(2)]
        + [pltpu.VMEM((SC_ADD_ROWS, D), F32) for _ in range(2)]
        + [pltpu.SemaphoreType.DMA for _ in range(6)],
        compiler_params=pltpu.CompilerParams(needs_layout_passes=False),
        name="moe_combine_row_add",
    )(acc, y, slots, *after)


def _rope_tables(L):
    inv = ROPE_BASE ** (-jnp.arange(ROPE_FREQS, dtype=F32) / ROPE_FREQS)
    pos = jnp.arange(L)
    row = (pos // GRID_W).astype(F32)[:, None] * inv
    col = (pos % GRID_W).astype(F32)[:, None] * inv
    cos = jnp.concatenate([jnp.cos(row), jnp.cos(row), jnp.cos(col), jnp.cos(col)], axis=1)
    sin = jnp.concatenate([-jnp.sin(row), jnp.sin(row), -jnp.sin(col), jnp.sin(col)], axis=1)
    return jnp.tile(cos, (1, 2)), jnp.tile(sin, (1, 2))


def _head_mean_matrix(n):
    idx = np.arange(n) // HEAD_DIM
    return jnp.asarray((idx[:, None] == idx[None, :]).astype(np.float32) / HEAD_DIM, dtype=BF16)


def kernel(x, c, ctx, c_ctx, w_mod, b_mod, norm1_g, w_in, q_norm_g, k_norm_g, attn_sink,
           w_decay_fwd, b_decay_fwd, w_decay_bwd, b_decay_bwd, gla_norm_g, w_out, norm2_g,
           w_router, w_e_gate, w_e_up, w_e_down):
    B, L, D = x.shape
    cap = CAPACITY_FACTOR * L // N_EXPERTS
    layer = 0

    rows = ((B + 1 + 7) // 8) * 8
    cc = jnp.concatenate([c, c_ctx[None, :], jnp.zeros((rows - B - 1, D), F32)], axis=0)
    mod_all = _modulation(cc, w_mod[layer], b_mod[layer])
    mod3 = mod_all[:B].reshape(B, 6, D)
    modc = mod_all[B].reshape(6, D)

    w = w_in[layer]
    o = np.cumsum([0, ATTN_WIDTH, KV_WIDTH, KV_WIDTH, GLA_QK_WIDTH, GLA_QK_WIDTH,
                   GLA_WIDTH, GLA_WIDTH, GATE_RANK, GATE_RANK])
    w_lr = jnp.concatenate([w[:, o[7]:o[9]]] * 3 + [jnp.zeros((D, LANES - 6 * GATE_RANK), F32)], axis=1)
    head_order = np.arange(N_Q_HEADS).reshape(N_KV_HEADS, -1).T.reshape(-1)
    attn_perm = (head_order[:, None] * HEAD_DIM + np.arange(HEAD_DIM)[None, :]).reshape(-1)
    w_in_r = jnp.concatenate([w[:, attn_perm], w[:, o[1]:o[7]], w_lr], axis=1).astype(BF16)
    w_out_r = jnp.concatenate([w_out[layer][attn_perm], w_out[layer][ATTN_WIDTH:]], axis=0).astype(BF16)
    w_ctx = jnp.concatenate([w[:, o[1]:o[3]], w[:, o[4]:o[6]], w_lr], axis=1).astype(BF16)
    wd2 = jnp.zeros((2 * GATE_RANK, 2 * GLA_QK_WIDTH), F32)
    wd2 = wd2.at[0:GATE_RANK, 0:GLA_QK_WIDTH].set(w_decay_fwd[layer])
    wd2 = wd2.at[GATE_RANK:, GLA_QK_WIDTH:].set(w_decay_bwd[layer])
    wd_hi = wd2.astype(BF16)
    wd_lo = (wd2 - wd_hi.astype(F32)).astype(BF16)
    wd = jnp.concatenate([wd_hi, wd_hi, wd_lo,
                          jnp.zeros((LANES - 6 * GATE_RANK, 2 * GLA_QK_WIDTH), BF16)], axis=0)
    bdec = jnp.concatenate([b_decay_fwd[layer], b_decay_bwd[layer]])[None, :]
    g1 = norm1_g[layer][None, :]
    g2 = norm2_g[layer][None, :]
    qg = jnp.tile(q_norm_g[layer], N_Q_HEADS)[None, :]
    kg2 = jnp.tile(k_norm_g[layer], N_KV_HEADS)[None, :]
    gn = jnp.tile(gla_norm_g[layer], GLA_HEADS)[None, :]
    bd512 = _head_mean_matrix(ATTN_WIDTH)
    bd128 = _head_mean_matrix(KV_WIDTH)
    cos_t, sin_t = _rope_tables(L)
    w_router_t = w_router[layer].T.astype(BF16)

    kc, vc, s_f, s_b = _context_side(ctx, modc, g1, w_ctx, kg2, bd128, wd, bdec)
    q, k, v, gq, gk, gv, gg, lr = _input_projection(
        x, mod3, g1, w_in_r, qg, kg2, bd512, cos_t, sin_t, tm=1024)
    attn = _window_attention(attn_sink[layer], q, k, v, kc, vc, tq=512)
    gla = _gla(gq, gk, gv, gg, lr, wd, bdec, gn, s_f, s_b)
    x1, h2, afft = _output_projection(attn, gla, x, mod3, w_out_r, g2, w_router_t, tm=1024)
    post = _expert_choice(afft, cap, nbatch=4)
    group_sizes = (4, 4, 4, 2, 1, 1)
    assert sum(group_sizes) == N_EXPERTS
    firsts = [sum(group_sizes[:g]) for g in range(len(group_sizes))]
    rows = h2.reshape(B * L, ROW_WORDS)
    slots = post.reshape(B * N_EXPERTS, L)
    stream = jax.new_ref(x1.reshape(B * L, D))
    gathered = [_dispatch(rows, slots, cap, first, size) for first, size in zip(firsts, group_sizes)]
    for g, (first, size) in enumerate(zip(firsts, group_sizes)):
        xs = gathered[g].reshape(B, size, cap, ROW_WORDS)
        y = _expert_ffn(xs, mod3, w_e_gate[layer], w_e_up[layer], w_e_down[layer], first, nbatch=4)
        _combine_add(stream, y.reshape(B * size * cap, D // 2), slots, cap, first, size,
                     after=gathered[g + 1:])
    return jax.freeze(stream).reshape(B, L, D)
```
